```python
import math
import jax, jax.numpy as jnp
from jax import lax
import numpy as np

D_MODEL = 1024
BATCH = 8
SEQ = 8192
DEPTH = 2

CHUNK = 64
N_META = 16
Q_BLOCK = 128
N_A_LAYERS = DEPTH // 2
N_B_LAYERS = DEPTH - N_A_LAYERS
D_FF = 2816
FFN_RES = 0.5
RMS_EPS = 1e-6
SSM_EXPAND = 2
D_INNER = SSM_EXPAND * D_MODEL
SSM_HEADDIM = 64
SSM_HEADS = D_INNER // SSM_HEADDIM
SSM_GROUPS = 8
SSM_HPG = SSM_HEADS // SSM_GROUPS
D_STATE = 128
D_CONV = 4
CONV_DIM = D_INNER + 2 * SSM_GROUPS * D_STATE
IN_PROJ_DIM = D_INNER + CONV_DIM + SSM_HEADS
SB_HEADS = 16
SB_HEAD_DIM = D_MODEL // SB_HEADS

kernel_name = "yoco_mamba2_stickbreaking_macaron"


def rms_norm(x, g):
    x32 = x.astype(jnp.float32)
    y = x32 * lax.rsqrt(jnp.mean(x32 * x32, axis=-1, keepdims=True) + RMS_EPS)
    return (y * g.astype(jnp.float32)).astype(x.dtype)


def head_rms(t, g):
    t32 = t.astype(jnp.float32)
    return t32 * lax.rsqrt(jnp.mean(t32 * t32, axis=-1, keepdims=True) + RMS_EPS) * g.astype(jnp.float32)


def swiglu(x, w1, w3, w2):
    return (jax.nn.silu(x @ w1) * (x @ w3)) @ w2


def causal_depthwise_conv(u, w, b):
    y = lax.conv_general_dilated(u, w[:, None, :].astype(u.dtype), window_strides=(1,),
                                 padding=[(w.shape[0] - 1, 0)],
                                 dimension_numbers=('NWC', 'WIO', 'NWC'),
                                 feature_group_count=u.shape[-1])
    return y + b.astype(u.dtype)


def ssd_chunk_scan(xs, dt, A, bm, cm):
    b, L = xs.shape[:2]
    nc = L // CHUNK

    def to_chunks(t):
        return jnp.moveaxis(t.reshape((b, nc, CHUNK) + t.shape[2:]), 1, 0)

    causal = jnp.tril(jnp.ones((CHUNK, CHUNK), dtype=bool))

    def step(state, inp):
        xc, dtc, bc, cc = inp
        cum = jnp.cumsum(dtc * A, axis=1)
        seg = cum[:, :, None] - cum[:, None, :]
        decay = jnp.exp(jnp.where(causal[None, :, :, None, None], seg, -jnp.inf))
        cb = jnp.einsum('btgn,bsgn->btsg', cc, bc)
        m = cb[..., None] * decay * dtc[:, None]
        y_diag = jnp.einsum('btsgh,bsghp->btghp', m, xc)
        y_off = jnp.einsum('btgn,bghpn->btghp', cc, state) * jnp.exp(cum)[..., None]
        w_end = jnp.exp(cum[:, -1:] - cum) * dtc
        new_state = (state * jnp.exp(cum[:, -1])[..., None, None]
                     + jnp.einsum('bsgn,bsgh,bsghp->bghpn', bc, w_end, xc))
        return new_state, y_diag + y_off

    state0 = jnp.zeros((b, SSM_GROUPS, SSM_HPG, SSM_HEADDIM, D_STATE), jnp.float32)
    _, ys = lax.scan(step, state0, (to_chunks(xs), to_chunks(dt), to_chunks(bm), to_chunks(cm)))
    return jnp.moveaxis(ys, 0, 1).reshape(xs.shape)


def mamba2_mixer(u, in_proj, conv_w, conv_b, dt_bias, a_log, d_skip, norm_g, out_proj):
    b, L, _ = u.shape
    f32 = jnp.float32
    zxbcdt = u @ in_proj
    z, xbc, dt_raw = jnp.split(zxbcdt, [D_INNER, D_INNER + CONV_DIM], axis=-1)
    xbc = jax.nn.silu(causal_depthwise_conv(xbc, conv_w, conv_b))
    xs, bm, cm = jnp.split(xbc, [D_INNER, D_INNER + SSM_GROUPS * D_STATE], axis=-1)
    dt = jax.nn.softplus(dt_raw.astype(f32) + dt_bias.astype(f32)).reshape(b, L, SSM_GROUPS, SSM_HPG)
    A = -jnp.exp(a_log.astype(f32)).reshape(SSM_GROUPS, SSM_HPG)
    xs = xs.astype(f32).reshape(b, L, SSM_GROUPS, SSM_HPG, SSM_HEADDIM)
    bm = bm.astype(f32).reshape(b, L, SSM_GROUPS, D_STATE)
    cm = cm.astype(f32).reshape(b, L, SSM_GROUPS, D_STATE)
    front = (-N_META) % CHUNK
    back = (-(L + front)) % CHUNK

    def pad(t):
        return jnp.pad(t, [(0, 0), (front, back)] + [(0, 0)] * (t.ndim - 2))

    y = ssd_chunk_scan(pad(xs), pad(dt), A, pad(bm), pad(cm))[:, front:front + L]
    y = y + d_skip.astype(f32).reshape(SSM_GROUPS, SSM_HPG)[..., None] * xs
    y = y.reshape(b, L, D_INNER) * jax.nn.silu(z.astype(f32))
    yg = y.reshape(b, L, SSM_GROUPS, D_INNER // SSM_GROUPS)
    yg = yg * lax.rsqrt(jnp.mean(yg * yg, axis=-1, keepdims=True) + RMS_EPS)
    y = (yg.reshape(b, L, D_INNER) * norm_g.astype(f32)).astype(u.dtype)
    return y @ out_proj


def stick_breaking_attention(q, k, v):
    L = q.shape[1]
    edges = [0] + list(range(N_META, L, Q_BLOCK)) + [L]
    scale = SB_HEAD_DIM ** -0.5
    outs = []
    for q0, q1 in zip(edges[:-1], edges[1:]):
        z = jnp.einsum('bqhd,bkhd->bhqk', q[:, q0:q1], k[:, :q1]) * scale
        strict = jnp.arange(q1)[None, :] < jnp.arange(q0, q1)[:, None]
        log_keep = jnp.where(strict, jax.nn.log_sigmoid(-z), 0.0)
        log_w = jax.nn.log_sigmoid(z) + lax.cumsum(log_keep, axis=3, reverse=True) - log_keep
        w = jnp.where(strict, jnp.exp(log_w), 0.0)
        outs.append(jnp.einsum('bhqk,bkhd->bqhd', w, v[:, :q1]))
    return jnp.concatenate(outs, axis=1)


def stick_breaking_mixer(u, w_q, q_norm_g, k_shared, v_shared, w_o):
    b, L, _ = u.shape
    q = head_rms((u @ w_q).reshape(b, L, SB_HEADS, SB_HEAD_DIM), q_norm_g)
    o = stick_breaking_attention(q, k_shared, v_shared)
    return o.reshape(b, L, D_MODEL).astype(u.dtype) @ w_o


def _fwd_setup_inputs(seed: int = 0) -> dict:
    key = jax.random.key(seed)
    ks = jax.random.split(key, 24)
    f32 = jnp.float32

    def nrm(k, shape, fan_in):
        return jax.random.normal(k, shape, f32) * fan_in ** -0.5

    def gain(k, shape):
        return 1.0 + 0.02 * jax.random.normal(k, shape, f32)

    dt0 = jnp.exp(jax.random.uniform(ks[9], (N_A_LAYERS, SSM_HEADS), f32, math.log(1e-3), math.log(1e-1)))
    dt_bias = dt0 + jnp.log(-jnp.expm1(-dt0))
    return {
        "x": jax.random.normal(ks[0], (BATCH, SEQ, D_MODEL), f32),
        "meta_tokens": jax.random.normal(ks[1], (N_META, D_MODEL), f32),
        "norm_g": gain(ks[2], (DEPTH, 3, D_MODEL)),
        "ffn_w1": nrm(ks[3], (DEPTH, 2, D_MODEL, D_FF), D_MODEL),
        "ffn_w3": nrm(ks[4], (DEPTH, 2, D_MODEL, D_FF), D_MODEL),
        "ffn_w2": nrm(ks[5], (DEPTH, 2, D_FF, D_MODEL), D_FF),
        "ssm_in_proj": nrm(ks[6], (N_A_LAYERS, D_MODEL, IN_PROJ_DIM), D_MODEL),
        "ssm_conv_w": nrm(ks[7], (N_A_LAYERS, D_CONV, CONV_DIM), D_CONV),
        "ssm_conv_b": 0.02 * jax.random.normal(ks[8], (N_A_LAYERS, CONV_DIM), f32),
        "ssm_dt_bias": dt_bias,
        "ssm_a_log": jnp.log(jax.random.uniform(ks[10], (N_A_LAYERS, SSM_HEADS), f32, 1.0, 16.0)),
        "ssm_d": 1.0 + 0.1 * jax.random.normal(ks[11], (N_A_LAYERS, SSM_HEADS), f32),
        "ssm_norm_g": gain(ks[12], (N_A_LAYERS, D_INNER)),
        "ssm_out_proj": nrm(ks[13], (N_A_LAYERS, D_INNER, D_MODEL), D_INNER),
        "kv_norm_g": gain(ks[14], (D_MODEL,)),
        "w_k": nrm(ks[15], (D_MODEL, D_MODEL), D_MODEL),
        "k_norm_g": gain(ks[16], (SB_HEAD_DIM,)),
        "w_v": nrm(ks[17], (D_MODEL, D_MODEL), D_MODEL),
        "sb_w_q": nrm(ks[18], (N_B_LAYERS, D_MODEL, D_MODEL), D_MODEL),
        "sb_q_norm_g": gain(ks[19], (N_B_LAYERS, SB_HEAD_DIM)),
        "sb_w_o": nrm(ks[20], (N_B_LAYERS, D_MODEL, D_MODEL), D_MODEL),
    }


def _fwd_reference(x, meta_tokens, norm_g, ffn_w1, ffn_w3, ffn_w2, ssm_in_proj, ssm_conv_w, ssm_conv_b,
              ssm_dt_bias, ssm_a_log, ssm_d, ssm_norm_g, ssm_out_proj, kv_norm_g, w_k, k_norm_g, w_v,
              sb_w_q, sb_q_norm_g, sb_w_o):
    b = x.shape[0]
    meta = jnp.broadcast_to(meta_tokens[None].astype(x.dtype), (b, N_META, D_MODEL))
    h = jnp.concatenate([meta, x], axis=1)
    L = h.shape[1]
    k_shared = None
    v_shared = None
    for i in range(DEPTH):
        h = h + FFN_RES * swiglu(rms_norm(h, norm_g[i, 0]), ffn_w1[i, 0], ffn_w3[i, 0], ffn_w2[i, 0])
        u = rms_norm(h, norm_g[i, 1])
        if i < N_A_LAYERS:
            h = h + mamba2_mixer(u, ssm_in_proj[i], ssm_conv_w[i], ssm_conv_b[i], ssm_dt_bias[i],
                                 ssm_a_log[i], ssm_d[i], ssm_norm_g[i], ssm_out_proj[i])
        else:
            j = i - N_A_LAYERS
            h = h + stick_breaking_mixer(u, sb_w_q[j], sb_q_norm_g[j], k_shared, v_shared, sb_w_o[j])
        h = h + FFN_RES * swiglu(rms_norm(h, norm_g[i, 2]), ffn_w1[i, 1], ffn_w3[i, 1], ffn_w2[i, 1])
        if i == N_A_LAYERS - 1:
            kv_in = rms_norm(h, kv_norm_g)
            k_shared = head_rms((kv_in @ w_k).reshape(b, L, SB_HEADS, SB_HEAD_DIM), k_norm_g)
            v_shared = (kv_in @ w_v).reshape(b, L, SB_HEADS, SB_HEAD_DIM).astype(jnp.float32)
    return h[:, N_META:]


import jax as _jax
import jax.numpy as _jnp

TWIN_FORMAT = 'train_step'
FWD_PARAMS = ['x', 'meta_tokens', 'norm_g', 'ffn_w1', 'ffn_w3', 'ffn_w2', 'ssm_in_proj', 'ssm_conv_w', 'ssm_conv_b', 'ssm_dt_bias', 'ssm_a_log', 'ssm_d', 'ssm_norm_g', 'ssm_out_proj', 'kv_norm_g', 'w_k', 'k_norm_g', 'w_v', 'sb_w_q', 'sb_q_norm_g', 'sb_w_o']
TWIN_WEIGHTS = ['meta_tokens', 'norm_g', 'ffn_w1', 'ffn_w3', 'ffn_w2', 'ssm_in_proj', 'ssm_conv_w', 'ssm_conv_b', 'ssm_dt_bias', 'ssm_a_log', 'ssm_d', 'ssm_norm_g', 'ssm_out_proj', 'kv_norm_g', 'w_k', 'k_norm_g', 'w_v', 'sb_w_q', 'sb_q_norm_g', 'sb_w_o']
TWIN_DIFF_INPUT = 'x'
TWIN_INPUTS = ['x', 'meta_tokens', 'norm_g', 'ffn_w1', 'ffn_w3', 'ffn_w2', 'ssm_in_proj', 'ssm_conv_w', 'ssm_conv_b', 'ssm_dt_bias', 'ssm_a_log', 'ssm_d', 'ssm_norm_g', 'ssm_out_proj', 'kv_norm_g', 'w_k', 'k_norm_g', 'w_v', 'sb_w_q', 'sb_q_norm_g', 'sb_w_o', 'loss_target', 'm_meta_tokens', 'm_norm_g', 'm_ffn_w1', 'm_ffn_w3', 'm_ffn_w2', 'm_ssm_in_proj', 'm_ssm_conv_w', 'm_ssm_conv_b', 'm_ssm_dt_bias', 'm_ssm_a_log', 'm_ssm_d', 'm_ssm_norm_g', 'm_ssm_out_proj', 'm_kv_norm_g', 'm_w_k', 'm_k_norm_g', 'm_w_v', 'm_sb_w_q', 'm_sb_q_norm_g', 'm_sb_w_o', 'v_meta_tokens', 'v_norm_g', 'v_ffn_w1', 'v_ffn_w3', 'v_ffn_w2', 'v_ssm_in_proj', 'v_ssm_conv_w', 'v_ssm_conv_b', 'v_ssm_dt_bias', 'v_ssm_a_log', 'v_ssm_d', 'v_ssm_norm_g', 'v_ssm_out_proj', 'v_kv_norm_g', 'v_w_k', 'v_k_norm_g', 'v_w_v', 'v_sb_w_q', 'v_sb_q_norm_g', 'v_sb_w_o']
TWIN_OUTPUTS = ['loss', 'grad_x', 'grad_meta_tokens', 'grad_norm_g', 'grad_ffn_w1', 'grad_ffn_w3', 'grad_ffn_w2', 'grad_ssm_in_proj', 'grad_ssm_conv_w', 'grad_ssm_conv_b', 'grad_ssm_dt_bias', 'grad_ssm_a_log', 'grad_ssm_d', 'grad_ssm_norm_g', 'grad_ssm_out_proj', 'grad_kv_norm_g', 'grad_w_k', 'grad_k_norm_g', 'grad_w_v', 'grad_sb_w_q', 'grad_sb_q_norm_g', 'grad_sb_w_o', 'delta_meta_tokens', 'delta_norm_g', 'delta_ffn_w1', 'delta_ffn_w3', 'delta_ffn_w2', 'delta_ssm_in_proj', 'delta_ssm_conv_w', 'delta_ssm_conv_b', 'delta_ssm_dt_bias', 'delta_ssm_a_log', 'delta_ssm_d', 'delta_ssm_norm_g', 'delta_ssm_out_proj', 'delta_kv_norm_g', 'delta_w_k', 'delta_k_norm_g', 'delta_w_v', 'delta_sb_w_q', 'delta_sb_q_norm_g', 'delta_sb_w_o', 'new_m_meta_tokens', 'new_m_norm_g', 'new_m_ffn_w1', 'new_m_ffn_w3', 'new_m_ffn_w2', 'new_m_ssm_in_proj', 'new_m_ssm_conv_w', 'new_m_ssm_conv_b', 'new_m_ssm_dt_bias', 'new_m_ssm_a_log', 'new_m_ssm_d', 'new_m_ssm_norm_g', 'new_m_ssm_out_proj', 'new_m_kv_norm_g', 'new_m_w_k', 'new_m_k_norm_g', 'new_m_w_v', 'new_m_sb_w_q', 'new_m_sb_q_norm_g', 'new_m_sb_w_o', 'new_v_meta_tokens', 'new_v_norm_g', 'new_v_ffn_w1', 'new_v_ffn_w3', 'new_v_ffn_w2', 'new_v_ssm_in_proj', 'new_v_ssm_conv_w', 'new_v_ssm_conv_b', 'new_v_ssm_dt_bias', 'new_v_ssm_a_log', 'new_v_ssm_d', 'new_v_ssm_norm_g', 'new_v_ssm_out_proj', 'new_v_kv_norm_g', 'new_v_w_k', 'new_v_k_norm_g', 'new_v_w_v', 'new_v_sb_w_q', 'new_v_sb_q_norm_g', 'new_v_sb_w_o']
TWIN_LEAF_KINDS = {'loss': 'loss', 'grad_x': 'grad_x', 'grad_meta_tokens': 'grad_w', 'grad_norm_g': 'grad_w', 'grad_ffn_w1': 'grad_w', 'grad_ffn_w3': 'grad_w', 'grad_ffn_w2': 'grad_w', 'grad_ssm_in_proj': 'grad_w', 'grad_ssm_conv_w': 'grad_w', 'grad_ssm_conv_b': 'grad_w', 'grad_ssm_dt_bias': 'grad_w', 'grad_ssm_a_log': 'grad_w', 'grad_ssm_d': 'grad_w', 'grad_ssm_norm_g': 'grad_w', 'grad_ssm_out_proj': 'grad_w', 'grad_kv_norm_g': 'grad_w', 'grad_w_k': 'grad_w', 'grad_k_norm_g': 'grad_w', 'grad_w_v': 'grad_w', 'grad_sb_w_q': 'grad_w', 'grad_sb_q_norm_g': 'grad_w', 'grad_sb_w_o': 'grad_w', 'delta_meta_tokens': 'delta_w', 'delta_norm_g': 'delta_w', 'delta_ffn_w1': 'delta_w', 'delta_ffn_w3': 'delta_w', 'delta_ffn_w2': 'delta_w', 'delta_ssm_in_proj': 'delta_w', 'delta_ssm_conv_w': 'delta_w', 'delta_ssm_conv_b': 'delta_w', 'delta_ssm_dt_bias': 'delta_w', 'delta_ssm_a_log': 'delta_w', 'delta_ssm_d': 'delta_w', 'delta_ssm_norm_g': 'delta_w', 'delta_ssm_out_proj': 'delta_w', 'delta_kv_norm_g': 'delta_w', 'delta_w_k': 'delta_w', 'delta_k_norm_g': 'delta_w', 'delta_w_v': 'delta_w', 'delta_sb_w_q': 'delta_w', 'delta_sb_q_norm_g': 'delta_w', 'delta_sb_w_o': 'delta_w', 'new_m_meta_tokens': 'new_m', 'new_m_norm_g': 'new_m', 'new_m_ffn_w1': 'new_m', 'new_m_ffn_w3': 'new_m', 'new_m_ffn_w2': 'new_m', 'new_m_ssm_in_proj': 'new_m', 'new_m_ssm_conv_w': 'new_m', 'new_m_ssm_conv_b': 'new_m', 'new_m_ssm_dt_bias': 'new_m', 'new_m_ssm_a_log': 'new_m', 'new_m_ssm_d': 'new_m', 'new_m_ssm_norm_g': 'new_m', 'new_m_ssm_out_proj': 'new_m', 'new_m_kv_norm_g': 'new_m', 'new_m_w_k': 'new_m', 'new_m_k_norm_g': 'new_m', 'new_m_w_v': 'new_m', 'new_m_sb_w_q': 'new_m', 'new_m_sb_q_norm_g': 'new_m', 'new_m_sb_w_o': 'new_m', 'new_v_meta_tokens': 'new_v', 'new_v_norm_g': 'new_v', 'new_v_ffn_w1': 'new_v', 'new_v_ffn_w3': 'new_v', 'new_v_ffn_w2': 'new_v', 'new_v_ssm_in_proj': 'new_v', 'new_v_ssm_conv_w': 'new_v', 'new_v_ssm_conv_b': 'new_v', 'new_v_ssm_dt_bias': 'new_v', 'new_v_ssm_a_log': 'new_v', 'new_v_ssm_d': 'new_v', 'new_v_ssm_norm_g': 'new_v', 'new_v_ssm_out_proj': 'new_v', 'new_v_kv_norm_g': 'new_v', 'new_v_w_k': 'new_v', 'new_v_k_norm_g': 'new_v', 'new_v_w_v': 'new_v', 'new_v_sb_w_q': 'new_v', 'new_v_sb_q_norm_g': 'new_v', 'new_v_sb_w_o': 'new_v'}


def _forward(args):
    return _fwd_reference(*[args[k] for k in FWD_PARAMS])


def _output_shape():
    out = _jax.eval_shape(lambda: _forward(_fwd_setup_inputs(0)))
    return out.shape, out.dtype

N_MICROBATCH = 1
ADAM_LR = 0.001
ADAM_B1 = 0.9
ADAM_B2 = 0.999
ADAM_EPS = 1e-08
ADAM_WD = 0.01
ADAM_STEP = 10
PER_EXAMPLE_BATCH_AXIS = {'x': 0, 'loss_target': 0}
SHARED_INPUTS = []
_WEIGHT_DTYPES = {'meta_tokens': _jnp.float32, 'norm_g': _jnp.float32, 'ffn_w1': _jnp.float32, 'ffn_w3': _jnp.float32, 'ffn_w2': _jnp.float32, 'ssm_in_proj': _jnp.float32, 'ssm_conv_w': _jnp.float32, 'ssm_conv_b': _jnp.float32, 'ssm_dt_bias': _jnp.float32, 'ssm_a_log': _jnp.float32, 'ssm_d': _jnp.float32, 'ssm_norm_g': _jnp.float32, 'ssm_out_proj': _jnp.float32, 'kv_norm_g': _jnp.float32, 'w_k': _jnp.float32, 'k_norm_g': _jnp.float32, 'w_v': _jnp.float32, 'sb_w_q': _jnp.float32, 'sb_q_norm_g': _jnp.float32, 'sb_w_o': _jnp.float32}
MOMENT_SCALE = {'meta_tokens': 2.498106e-02, 'norm_g': 9.311497e+00, 'ffn_w1': 1.760345e-01, 'ffn_w3': 1.841576e-01, 'ffn_w2': 3.037675e-01, 'ssm_in_proj': 4.448264e-01, 'ssm_conv_w': 9.505396e-01, 'ssm_conv_b': 4.158009e+00, 'ssm_dt_bias': 1.045672e+00, 'ssm_a_log': 1.185878e+01, 'ssm_d': 1.414129e+01, 'ssm_norm_g': 4.283741e+01, 'ssm_out_proj': 3.372282e+00, 'kv_norm_g': 2.740471e+01, 'w_k': 4.008048e-01, 'k_norm_g': 6.217735e+01, 'w_v': 1.752183e+00, 'sb_w_q': 4.052054e-01, 'sb_q_norm_g': 6.222802e+01, 'sb_w_o': 1.552026e+00}


def _to_microbatches(a, axis):
    t = _jnp.moveaxis(a, axis, 0)
    t = t.reshape((N_MICROBATCH, t.shape[0] // N_MICROBATCH) + t.shape[1:])
    return _jnp.moveaxis(t, 1, axis + 1)


def setup_inputs(seed: int = 0) -> dict:
    inp = _fwd_setup_inputs(seed)
    key = _jax.random.fold_in(_jax.random.key(seed), 7919)
    shape, _ = _output_shape()
    out = dict(inp)
    out["loss_target"] = _jax.random.normal(_jax.random.fold_in(key, 0), shape, _jnp.float32)
    for i, name in enumerate(TWIN_WEIGHTS):
        w = inp[name].astype(_jnp.float32)
        if MOMENT_SCALE is None:
            s = _jnp.sqrt(_jnp.mean(_jnp.square(w)) + 1e-30)
        else:
            s = MOMENT_SCALE[name]
        km, kv = _jax.random.split(_jax.random.fold_in(key, i + 1))
        out[name] = w
        out["m_" + name] = s * _jax.random.normal(km, w.shape, _jnp.float32)
        out["v_" + name] = (s * s) * _jax.random.uniform(kv, w.shape, _jnp.float32, 0.5, 1.5)
    if N_MICROBATCH > 1:
        for name, axis in PER_EXAMPLE_BATCH_AXIS.items():
            out[name] = _to_microbatches(out[name], axis)
    return {'x': out['x'], 'meta_tokens': out['meta_tokens'], 'norm_g': out['norm_g'], 'ffn_w1': out['ffn_w1'], 'ffn_w3': out['ffn_w3'], 'ffn_w2': out['ffn_w2'], 'ssm_in_proj': out['ssm_in_proj'], 'ssm_conv_w': out['ssm_conv_w'], 'ssm_conv_b': out['ssm_conv_b'], 'ssm_dt_bias': out['ssm_dt_bias'], 'ssm_a_log': out['ssm_a_log'], 'ssm_d': out['ssm_d'], 'ssm_norm_g': out['ssm_norm_g'], 'ssm_out_proj': out['ssm_out_proj'], 'kv_norm_g': out['kv_norm_g'], 'w_k': out['w_k'], 'k_norm_g': out['k_norm_g'], 'w_v': out['w_v'], 'sb_w_q': out['sb_w_q'], 'sb_q_norm_g': out['sb_q_norm_g'], 'sb_w_o': out['sb_w_o'], 'loss_target': out['loss_target'], 'm_meta_tokens': out['m_meta_tokens'], 'm_norm_g': out['m_norm_g'], 'm_ffn_w1': out['m_ffn_w1'], 'm_ffn_w3': out['m_ffn_w3'], 'm_ffn_w2': out['m_ffn_w2'], 'm_ssm_in_proj': out['m_ssm_in_proj'], 'm_ssm_conv_w': out['m_ssm_conv_w'], 'm_ssm_conv_b': out['m_ssm_conv_b'], 'm_ssm_dt_bias': out['m_ssm_dt_bias'], 'm_ssm_a_log': out['m_ssm_a_log'], 'm_ssm_d': out['m_ssm_d'], 'm_ssm_norm_g': out['m_ssm_norm_g'], 'm_ssm_out_proj': out['m_ssm_out_proj'], 'm_kv_norm_g': out['m_kv_norm_g'], 'm_w_k': out['m_w_k'], 'm_k_norm_g': out['m_k_norm_g'], 'm_w_v': out['m_w_v'], 'm_sb_w_q': out['m_sb_w_q'], 'm_sb_q_norm_g': out['m_sb_q_norm_g'], 'm_sb_w_o': out['m_sb_w_o'], 'v_meta_tokens': out['v_meta_tokens'], 'v_norm_g': out['v_norm_g'], 'v_ffn_w1': out['v_ffn_w1'], 'v_ffn_w3': out['v_ffn_w3'], 'v_ffn_w2': out['v_ffn_w2'], 'v_ssm_in_proj': out['v_ssm_in_proj'], 'v_ssm_conv_w': out['v_ssm_conv_w'], 'v_ssm_conv_b': out['v_ssm_conv_b'], 'v_ssm_dt_bias': out['v_ssm_dt_bias'], 'v_ssm_a_log': out['v_ssm_a_log'], 'v_ssm_d': out['v_ssm_d'], 'v_ssm_norm_g': out['v_ssm_norm_g'], 'v_ssm_out_proj': out['v_ssm_out_proj'], 'v_kv_norm_g': out['v_kv_norm_g'], 'v_w_k': out['v_w_k'], 'v_k_norm_g': out['v_k_norm_g'], 'v_w_v': out['v_w_v'], 'v_sb_w_q': out['v_sb_w_q'], 'v_sb_q_norm_g': out['v_sb_q_norm_g'], 'v_sb_w_o': out['v_sb_w_o']}


def _loss(weights, diff, rest, loss_target):
    with _jax.named_scope("forward"):
        args = {**rest, TWIN_DIFF_INPUT: diff, **{k: w.astype(_WEIGHT_DTYPES[k]) for k, w in weights.items()}}
        y = _forward(args)
    with _jax.named_scope("loss_head"):
        err = _jnp.square(y.astype(_jnp.float32) - loss_target)
        return 0.5 * _jnp.sum(_jnp.mean(err, axis=-1)) if err.ndim else 0.5 * err


def _adamw(w, g, m, v):
    m = ADAM_B1 * m + (1.0 - ADAM_B1) * g
    v = ADAM_B2 * v + (1.0 - ADAM_B2) * _jnp.square(g)
    m_hat = m / (1.0 - ADAM_B1 ** ADAM_STEP)
    v_hat = v / (1.0 - ADAM_B2 ** ADAM_STEP)
    delta = -ADAM_LR * (m_hat / (_jnp.sqrt(v_hat) + ADAM_EPS) + ADAM_WD * w)
    return delta, m, v


def reference(x, meta_tokens, norm_g, ffn_w1, ffn_w3, ffn_w2, ssm_in_proj, ssm_conv_w, ssm_conv_b, ssm_dt_bias, ssm_a_log, ssm_d, ssm_norm_g, ssm_out_proj, kv_norm_g, w_k, k_norm_g, w_v, sb_w_q, sb_q_norm_g, sb_w_o, loss_target, m_meta_tokens, m_norm_g, m_ffn_w1, m_ffn_w3, m_ffn_w2, m_ssm_in_proj, m_ssm_conv_w, m_ssm_conv_b, m_ssm_dt_bias, m_ssm_a_log, m_ssm_d, m_ssm_norm_g, m_ssm_out_proj, m_kv_norm_g, m_w_k, m_k_norm_g, m_w_v, m_sb_w_q, m_sb_q_norm_g, m_sb_w_o, v_meta_tokens, v_norm_g, v_ffn_w1, v_ffn_w3, v_ffn_w2, v_ssm_in_proj, v_ssm_conv_w, v_ssm_conv_b, v_ssm_dt_bias, v_ssm_a_log, v_ssm_d, v_ssm_norm_g, v_ssm_out_proj, v_kv_norm_g, v_w_k, v_k_norm_g, v_w_v, v_sb_w_q, v_sb_q_norm_g, v_sb_w_o):
    given = dict(x=x, meta_tokens=meta_tokens, norm_g=norm_g, ffn_w1=ffn_w1, ffn_w3=ffn_w3, ffn_w2=ffn_w2, ssm_in_proj=ssm_in_proj, ssm_conv_w=ssm_conv_w, ssm_conv_b=ssm_conv_b, ssm_dt_bias=ssm_dt_bias, ssm_a_log=ssm_a_log, ssm_d=ssm_d, ssm_norm_g=ssm_norm_g, ssm_out_proj=ssm_out_proj, kv_norm_g=kv_norm_g, w_k=w_k, k_norm_g=k_norm_g, w_v=w_v, sb_w_q=sb_w_q, sb_q_norm_g=sb_q_norm_g, sb_w_o=sb_w_o, loss_target=loss_target, m_meta_tokens=m_meta_tokens, m_norm_g=m_norm_g, m_ffn_w1=m_ffn_w1, m_ffn_w3=m_ffn_w3, m_ffn_w2=m_ffn_w2, m_ssm_in_proj=m_ssm_in_proj, m_ssm_conv_w=m_ssm_conv_w, m_ssm_conv_b=m_ssm_conv_b, m_ssm_dt_bias=m_ssm_dt_bias, m_ssm_a_log=m_ssm_a_log, m_ssm_d=m_ssm_d, m_ssm_norm_g=m_ssm_norm_g, m_ssm_out_proj=m_ssm_out_proj, m_kv_norm_g=m_kv_norm_g, m_w_k=m_w_k, m_k_norm_g=m_k_norm_g, m_w_v=m_w_v, m_sb_w_q=m_sb_w_q, m_sb_q_norm_g=m_sb_q_norm_g, m_sb_w_o=m_sb_w_o, v_meta_tokens=v_meta_tokens, v_norm_g=v_norm_g, v_ffn_w1=v_ffn_w1, v_ffn_w3=v_ffn_w3, v_ffn_w2=v_ffn_w2, v_ssm_in_proj=v_ssm_in_proj, v_ssm_conv_w=v_ssm_conv_w, v_ssm_conv_b=v_ssm_conv_b, v_ssm_dt_bias=v_ssm_dt_bias, v_ssm_a_log=v_ssm_a_log, v_ssm_d=v_ssm_d, v_ssm_norm_g=v_ssm_norm_g, v_ssm_out_proj=v_ssm_out_proj, v_kv_norm_g=v_kv_norm_g, v_w_k=v_w_k, v_k_norm_g=v_k_norm_g, v_w_v=v_w_v, v_sb_w_q=v_sb_w_q, v_sb_q_norm_g=v_sb_q_norm_g, v_sb_w_o=v_sb_w_o)
    weights = {n: given[n] for n in TWIN_WEIGHTS}
    shared = {n: given[n] for n in SHARED_INPUTS}
    per_example = {n: given[n] for n in ['x']}
    grad_fn = _jax.value_and_grad(_loss, argnums=(0, 1))

    def one_microbatch(ex, loss_target):
        ex = dict(ex)
        diff = ex.pop(TWIN_DIFF_INPUT)
        return grad_fn(weights, diff, {**shared, **ex}, loss_target)

    if N_MICROBATCH == 1:
        loss, (grad_w, grad_x) = one_microbatch(per_example, given["loss_target"])
    else:
        def body(carry, xs):
            loss_sum, grad_sum = carry
            l_k, (gw_k, gx_k) = one_microbatch(xs[0], xs[1])
            with _jax.named_scope("update"):
                return (loss_sum + l_k, _jax.tree.map(_jnp.add, grad_sum, gw_k)), gx_k

        init = (_jnp.zeros((), _jnp.float32), _jax.tree.map(_jnp.zeros_like, weights))
        (loss, grad_w), grad_x = _jax.lax.scan(body, init, (per_example, given["loss_target"]))
    with _jax.named_scope("update"):
        delta_w, new_m, new_v = {}, {}, {}
        for n in TWIN_WEIGHTS:
            delta_w[n], new_m[n], new_v[n] = _adamw(weights[n], grad_w[n], given["m_" + n], given["v_" + n])
    return (loss, grad_x, *[grad_w[n] for n in TWIN_WEIGHTS], *[delta_w[n] for n in TWIN_WEIGHTS],
            *[new_m[n] for n in TWIN_WEIGHTS], *[new_v[n] for n in TWIN_WEIGHTS])
```

```python
import functools
import math

import jax
import jax.numpy as jnp
from jax import lax
from jax.experimental import pallas as pl
from jax.experimental.pallas import tpu as pltpu

F32 = jnp.float32
BF16 = jnp.bfloat16
RMS_EPS = 1e-6
N_META = 16
HEAD_DIM = 64
SSM_GROUPS = 8
D_STATE = 128
D_CONV = 4
FFN_RES = 0.5
ADAM_LR, ADAM_B1, ADAM_B2, ADAM_EPS, ADAM_WD, ADAM_STEP = 0.001, 0.9, 0.999, 1e-08, 0.01, 10
N_DEV = 8
SSD_CHUNK = 128
ATT_BLOCK = 256
ROW_ALIGN = 768
VMEM_LIMIT_V7X = 48 * 1024 * 1024
MESH = pl.DeviceIdType.MESH
LANES = 128


def _cp(*sem):
    return pltpu.CompilerParams(dimension_semantics=sem if sem else None, vmem_limit_bytes=VMEM_LIMIT_V7X)


def _tile(n, cands):
    for c in cands:
        if n % c == 0:
            return c
    return n


def _softplus(x):
    return jnp.maximum(x, 0.0) + jnp.log(1.0 + jnp.exp(-jnp.abs(x)))


def _sigmoid(x):
    return 1.0 / (1.0 + jnp.exp(-x))


def _split3(x):
    hi = x.astype(BF16)
    r1 = x - hi.astype(F32)
    mid = r1.astype(BF16)
    lo = (r1 - mid.astype(F32)).astype(BF16)
    return hi, mid, lo


def _dot(a, b):
    return jnp.dot(a, b, preferred_element_type=F32)


def _dot_nt(a, b):
    return lax.dot_general(a, b, (((1,), (1,)), ((), ())), preferred_element_type=F32)


def _dot_tn(a, b):
    return lax.dot_general(a, b, (((0,), (0,)), ((), ())), preferred_element_type=F32)


def _dot3_left(t_bf16, x):
    hi, mid, lo = _split3(x)
    return _dot(t_bf16, hi) + _dot(t_bf16, mid) + _dot(t_bf16, lo)


def _dot3_right(x, t_bf16):
    hi, mid, lo = _split3(x)
    return _dot(hi, t_bf16) + _dot(mid, t_bf16) + _dot(lo, t_bf16)


def _matmul(pairs, *, name, out_dtype=F32, trans_a=False, res=None, alpha=1.0, tm=None, tn=None, tk=None):
    a0, b0 = pairs[0]
    if trans_a:
        K, M = a0.shape
    else:
        M, K = a0.shape
    N = b0.shape[1]
    tm = tm or _tile(M, (768, 512, 384, 256, 128))
    tn = tn or _tile(N, (512, 1408, 384, 256, 128))
    tk = tk or _tile(K, (1024, 1408, 768, 512, 256, 128))
    nk = K // tk
    npair = len(pairs)
    has_res = res is not None

    def body(*refs):
        o_ref, acc = refs[-2], refs[-1]
        k = pl.program_id(2)

        @pl.when(k == 0)
        def _():
            acc[...] = jnp.zeros_like(acc)

        part = None
        for p in range(npair):
            a = refs[2 * p][...].astype(BF16)
            b = refs[2 * p + 1][...].astype(BF16)
            d = _dot_tn(a, b) if trans_a else _dot(a, b)
            part = d if part is None else part + d
        acc[...] += part

        @pl.when(k == nk - 1)
        def _():
            v = acc[...]
            if alpha != 1.0:
                v = v * alpha
            if has_res:
                v = refs[2 * npair][...] + v
            o_ref[...] = v.astype(out_dtype)

    if trans_a:
        a_spec = pl.BlockSpec((tk, tm), lambda i, j, k: (k, i))
    else:
        a_spec = pl.BlockSpec((tm, tk), lambda i, j, k: (i, k))
    b_spec = pl.BlockSpec((tk, tn), lambda i, j, k: (k, j))
    o_spec = pl.BlockSpec((tm, tn), lambda i, j, k: (i, j))
    in_specs, args = [], []
    for a, b in pairs:
        in_specs += [a_spec, b_spec]
        args += [a, b]
    if has_res:
        in_specs.append(o_spec)
        args.append(res)
    return pl.pallas_call(
        body, name=name, grid=(M // tm, N // tn, nk), in_specs=in_specs, out_specs=o_spec,
        out_shape=jax.ShapeDtypeStruct((M, N), out_dtype),
        scratch_shapes=[pltpu.VMEM((tm, tn), F32)],
        compiler_params=_cp("parallel", "parallel", "arbitrary"),
    )(*args)


def _rms_fwd(h, g, *, name, scale=1.0):
    R, D = h.shape
    tr = _tile(R, (2048, 1024, 768, 512, 256, 128)) if D <= 128 else _tile(R, (384, 256, 128))

    def body(h_ref, g_ref, o_ref):
        x = h_ref[...]
        r = lax.rsqrt(jnp.mean(x * x, axis=1, keepdims=True) + RMS_EPS)
        y = x * r * g_ref[...]
        if scale != 1.0:
            y = y * scale
        o_ref[...] = y.astype(BF16)

    return pl.pallas_call(
        body, name=name, grid=(R // tr,),
        in_specs=[pl.BlockSpec((tr, D), lambda i: (i, 0)), pl.BlockSpec((1, D), lambda i: (0, 0))],
        out_specs=pl.BlockSpec((tr, D), lambda i: (i, 0)),
        out_shape=jax.ShapeDtypeStruct((R, D), BF16), compiler_params=_cp("parallel"),
    )(h, g.reshape(1, D))


def _rms_bwd(h, g, dn, res=None, *, name, alpha=1.0):
    R, D = h.shape
    tr = _tile(R, (2048, 1024, 768, 512, 256, 128)) if D <= 128 else _tile(R, (384, 256, 128))
    has_res = res is not None

    def body(*refs):
        h_ref, g_ref, dn_ref = refs[:3]
        dh_ref, dg_ref = refs[-2], refs[-1]
        i = pl.program_id(0)

        @pl.when(i == 0)
        def _():
            dg_ref[...] = jnp.zeros_like(dg_ref)

        x = h_ref[...]
        r = lax.rsqrt(jnp.mean(x * x, axis=1, keepdims=True) + RMS_EPS)
        xh = x * r
        d = dn_ref[...].astype(F32)
        if alpha != 1.0:
            d = d * alpha
        dng = d * g_ref[...]
        m = jnp.mean(dng * xh, axis=1, keepdims=True)
        dh = r * (dng - xh * m)
        if has_res:
            dh = dh + refs[3][...]
        dh_ref[...] = dh
        dg_ref[...] += jnp.sum(d * xh, axis=0, keepdims=True)

    row = pl.BlockSpec((tr, D), lambda i: (i, 0))
    vec = pl.BlockSpec((1, D), lambda i: (0, 0))
    in_specs = [row, vec, row] + ([row] if has_res else [])
    args = [h, g.reshape(1, D), dn] + ([res] if has_res else [])
    return pl.pallas_call(
        body, name=name, grid=(R // tr,), in_specs=in_specs, out_specs=[row, vec],
        out_shape=[jax.ShapeDtypeStruct((R, D), F32), jax.ShapeDtypeStruct((1, D), F32)],
        compiler_params=_cp("arbitrary"),
    )(*args)


def _ffn_up(n, w1, w3, *, name):
    M, K = n.shape
    N = w1.shape[1]
    tm = _tile(M, (384, 256, 128))
    tn = _tile(N, (1408, 512, 256, 128))

    def body(n_ref, w1_ref, w3_ref, a_ref, b_ref, s_ref):
        x = n_ref[...]
        a = _dot(x, w1_ref[...])
        b = _dot(x, w3_ref[...])
        a_ref[...] = a.astype(BF16)
        b_ref[...] = b.astype(BF16)
        s_ref[...] = (a * _sigmoid(a) * b).astype(BF16)

    o_spec = pl.BlockSpec((tm, tn), lambda j, i: (i, j))
    w_spec = pl.BlockSpec((K, tn), lambda j, i: (0, j))
    sh = jax.ShapeDtypeStruct((M, N), BF16)
    return pl.pallas_call(
        body, name=name, grid=(N // tn, M // tm),
        in_specs=[pl.BlockSpec((tm, K), lambda j, i: (i, 0)), w_spec, w_spec],
        out_specs=[o_spec, o_spec, o_spec], out_shape=[sh, sh, sh],
        compiler_params=_cp("parallel", "parallel"),
    )(n, w1, w3)


def _ffn_mid_bwd(dh, w2t, a, b, *, name):
    M, K = dh.shape
    N = w2t.shape[1]
    tm = _tile(M, (384, 256, 128))
    tn = _tile(N, (1408, 512, 256, 128))

    def body(dh_ref, w_ref, a_ref, b_ref, da_ref, db_ref):
        ds = _dot(dh_ref[...].astype(BF16), w_ref[...]) * FFN_RES
        av = a_ref[...].astype(F32)
        bv = b_ref[...].astype(F32)
        sg = _sigmoid(av)
        da_ref[...] = (ds * bv * sg * (1.0 + av * (1.0 - sg))).astype(BF16)
        db_ref[...] = (ds * av * sg).astype(BF16)

    o_spec = pl.BlockSpec((tm, tn), lambda j, i: (i, j))
    sh = jax.ShapeDtypeStruct((M, N), BF16)
    return pl.pallas_call(
        body, name=name, grid=(N // tn, M // tm),
        in_specs=[pl.BlockSpec((tm, K), lambda j, i: (i, 0)), pl.BlockSpec((K, tn), lambda j, i: (0, j)), o_spec, o_spec],
        out_specs=[o_spec, o_spec], out_shape=[sh, sh], compiler_params=_cp("parallel", "parallel"),
    )(dh, w2t, a, b)


def _conv_pre(xx, w_ref, b_ref, tr):
    acc = None
    for k in range(D_CONV):
        sh = D_CONV - 1 - k
        v = (pltpu.roll(xx, sh, 0) if sh else xx)[8:8 + tr]
        t = w_ref[k:k + 1, :] * v
        acc = t if acc is None else acc + t
    return acc + b_ref[...]


def _conv_fwd(zx, w, b, col_off, *, name):
    LP = zx.shape[0]
    C = w.shape[1]
    tr = _tile(LP, (256, 128))
    tc = _tile(C, (512, 256, 128))
    co = col_off // tc

    def body(cur_ref, prev_ref, w_ref, b_ref, o_ref):
        i = pl.program_id(0)
        prev = jnp.where(i == 0, 0.0, prev_ref[...])
        pre = _conv_pre(jnp.concatenate([prev, cur_ref[...]], axis=0), w_ref, b_ref, tr)
        o_ref[...] = pre * _sigmoid(pre)

    return pl.pallas_call(
        body, name=name, grid=(LP // tr, C // tc),
        in_specs=[pl.BlockSpec((tr, tc), lambda i, j: (i, j + co)),
                  pl.BlockSpec((8, tc), lambda i, j: (jnp.maximum(i * (tr // 8) - 1, 0), j + co)),
                  pl.BlockSpec((D_CONV, tc), lambda i, j: (0, j)), pl.BlockSpec((1, tc), lambda i, j: (0, j))],
        out_specs=pl.BlockSpec((tr, tc), lambda i, j: (i, j)),
        out_shape=jax.ShapeDtypeStruct((LP, C), F32), compiler_params=_cp("parallel", "parallel"),
    )(zx, zx, w, b.reshape(1, C))


def _conv_bwd_g(zx, w, b, dact, col_off, *, name):
    LP = zx.shape[0]
    C = w.shape[1]
    tr = _tile(LP, (256, 128))
    tc = _tile(C, (512, 256, 128))
    co = col_off // tc

    def body(cur_ref, prev_ref, w_ref, b_ref, d_ref, g_ref, dw_ref, db_ref):
        i = pl.program_id(1)

        @pl.when(i == 0)
        def _():
            dw_ref[...] = jnp.zeros_like(dw_ref)
            db_ref[...] = jnp.zeros_like(db_ref)

        prev = jnp.where(i == 0, 0.0, prev_ref[...])
        xx = jnp.concatenate([prev, cur_ref[...]], axis=0)
        pre = _conv_pre(xx, w_ref, b_ref, tr)
        sg = _sigmoid(pre)
        g = d_ref[...] * sg * (1.0 + pre * (1.0 - sg))
        g_ref[...] = g
        db_ref[...] += jnp.sum(g, axis=0, keepdims=True)
        rows = []
        for k in range(D_CONV):
            sh = D_CONV - 1 - k
            v = (pltpu.roll(xx, sh, 0) if sh else xx)[8:8 + tr]
            rows.append(jnp.sum(g * v, axis=0, keepdims=True))
        rows.append(jnp.zeros((8 - D_CONV, tc), F32))
        dw_ref[...] += jnp.concatenate(rows, axis=0)

    return pl.pallas_call(
        body, name=name, grid=(C // tc, LP // tr),
        in_specs=[pl.BlockSpec((tr, tc), lambda j, i: (i, j + co)),
                  pl.BlockSpec((8, tc), lambda j, i: (jnp.maximum(i * (tr // 8) - 1, 0), j + co)),
                  pl.BlockSpec((D_CONV, tc), lambda j, i: (0, j)), pl.BlockSpec((1, tc), lambda j, i: (0, j)),
                  pl.BlockSpec((tr, tc), lambda j, i: (i, j))],
        out_specs=[pl.BlockSpec((tr, tc), lambda j, i: (i, j)), pl.BlockSpec((8, tc), lambda j, i: (0, j)),
                   pl.BlockSpec((1, tc), lambda j, i: (0, j))],
        out_shape=[jax.ShapeDtypeStruct((LP, C), F32), jax.ShapeDtypeStruct((8, C), F32), jax.ShapeDtypeStruct((1, C), F32)],
        compiler_params=_cp("parallel", "arbitrary"),
    )(zx, zx, w, b.reshape(1, C), dact)


def _conv_bwd_u(g, w, *, name):
    LP, C = g.shape
    tr = _tile(LP, (256, 128))
    tc = _tile(C, (512, 256, 128))
    nb = LP // tr

    def body(cur_ref, nxt_ref, w_ref, o_ref):
        i = pl.program_id(0)
        nxt = jnp.where(i == nb - 1, 0.0, nxt_ref[...])
        xx = jnp.concatenate([cur_ref[...], nxt], axis=0)
        acc = None
        for k in range(D_CONV):
            sh = D_CONV - 1 - k
            v = (pltpu.roll(xx, tr + 8 - sh, 0) if sh else xx)[:tr]
            t = w_ref[k:k + 1, :] * v
            acc = t if acc is None else acc + t
        o_ref[...] = acc

    return pl.pallas_call(
        body, name=name, grid=(nb, C // tc),
        in_specs=[pl.BlockSpec((tr, tc), lambda i, j: (i, j)),
                  pl.BlockSpec((8, tc), lambda i, j: (jnp.minimum((i + 1) * (tr // 8), LP // 8 - 1), j)),
                  pl.BlockSpec((D_CONV, tc), lambda i, j: (0, j))],
        out_specs=pl.BlockSpec((tr, tc), lambda i, j: (i, j)),
        out_shape=jax.ShapeDtypeStruct((LP, C), F32), compiler_params=_cp("parallel", "parallel"),
    )(g, g, w)


def _ssd_prelude(dtr_ref, dtrt_ref, brow_ref, bcol_ref, alrow_ref, alcol_ref, Q):
    ii = lax.broadcasted_iota(jnp.int32, (Q, Q), 0)
    jj = lax.broadcasted_iota(jnp.int32, (Q, Q), 1)
    tril = ii >= jj
    dt_col = _softplus(dtr_ref[...] + brow_ref[...])
    a_row_p = -jnp.exp(alrow_ref[...])
    dt_row = _softplus(dtrt_ref[...] + bcol_ref[...])
    a_col_p = -jnp.exp(alcol_ref[...])
    cum_col = _dot3_left(tril.astype(BF16), dt_col * a_row_p)
    cum_row = _dot3_right(dt_row * a_col_p, (ii <= jj).astype(BF16))
    return ii, jj, tril, dt_col, dt_row, a_row_p, cum_col, cum_row


def _col_of(mat, lane_idx, h):
    return jnp.sum(jnp.where(lane_idx == h, mat, 0.0), axis=1, keepdims=True)


def _ssd_fwd(xbc, dtr, dtrt, brow, bcol, alrow, alcol, dvec, *, name):
    LP = xbc.shape[0]
    Q = SSD_CHUNK
    nc = LP // Q
    G = SSM_GROUPS
    DI = dvec.shape[1]
    gw = DI // G
    hpg = gw // HEAD_DIM
    H = G * hpg
    boff, coff = DI, DI + G * D_STATE

    def body(xbc_ref, dtr_ref, dtrt_ref, brow_ref, bcol_ref, alrow_ref, alcol_ref, dvec_ref, y_ref, st_ref, state):
        c = pl.program_id(0)

        @pl.when(c == 0)
        def _():
            state[...] = jnp.zeros_like(state)

        st_ref[...] = state[...]
        ii, jj, tril, dt_col, dt_row, _, cum_col, cum_row = _ssd_prelude(
            dtr_ref, dtrt_ref, brow_ref, bcol_ref, alrow_ref, alcol_ref, Q)
        lane_h = lax.broadcasted_iota(jnp.int32, (Q, 128), 1)
        lane_g = lax.broadcasted_iota(jnp.int32, (Q, gw), 1) // HEAD_DIM
        for g in range(G):
            xg = xbc_ref[:, g * gw:(g + 1) * gw]
            bb = xbc_ref[:, boff + g * D_STATE: boff + (g + 1) * D_STATE].astype(BF16)
            cb = xbc_ref[:, coff + g * D_STATE: coff + (g + 1) * D_STATE].astype(BF16)
            gm = _dot_nt(cb, bb)
            sg = state[g]
            yoff = _dot(cb, sg.astype(BF16))
            ydiag = jnp.zeros((Q, gw), F32)
            esc = jnp.zeros((Q, gw), F32)
            wsc = jnp.zeros((Q, gw), F32)
            lam = jnp.zeros((1, gw), F32)
            for j in range(hpg):
                h = g * hpg + j
                ccol = _col_of(cum_col, lane_h, h)
                dcol = _col_of(dt_col, lane_h, h)
                seg = ccol - cum_row[h:h + 1, :]
                decay = jnp.exp(jnp.where(tril, seg, -jnp.inf))
                mh = gm * decay * dt_row[h:h + 1, :]
                hm = lane_g == j
                ydiag = ydiag + _dot(mh.astype(BF16), jnp.where(hm, xg, 0.0).astype(BF16))
                tot = ccol[Q - 1:Q, :]
                esc = jnp.where(hm, jnp.exp(ccol), esc)
                wsc = jnp.where(hm, jnp.exp(tot - ccol) * dcol, wsc)
                lam = jnp.where(hm[0:1], jnp.exp(tot), lam)
            y_ref[:, g * gw:(g + 1) * gw] = ydiag + yoff * esc + dvec_ref[:, g * gw:(g + 1) * gw] * xg
            state[g] = sg * lam + _dot_tn(bb, (xg * wsc).astype(BF16))

    W = xbc.shape[1]
    full = lambda shape: pl.BlockSpec(shape, lambda c: (0,) * len(shape))
    return pl.pallas_call(
        body, name=name, grid=(nc,),
        in_specs=[pl.BlockSpec((Q, W), lambda c: (c, 0)), pl.BlockSpec((Q, 128), lambda c: (c, 0)),
                  pl.BlockSpec((H, Q), lambda c: (0, c)), full((1, 128)), full((H, 1)), full((1, 128)), full((H, 1)),
                  full((1, DI))],
        out_specs=[pl.BlockSpec((Q, DI), lambda c: (c, 0)), pl.BlockSpec((None, G, D_STATE, gw), lambda c: (c, 0, 0, 0))],
        out_shape=[jax.ShapeDtypeStruct((LP, DI), F32), jax.ShapeDtypeStruct((nc, G, D_STATE, gw), F32)],
        scratch_shapes=[pltpu.VMEM((G, D_STATE, gw), F32)],
        compiler_params=_cp("arbitrary"),
    )(xbc, dtr, dtrt, brow, bcol, alrow, alcol, dvec)


def _ssd_bwd(xbc, dtr, dtrt, brow, bcol, alrow, alcol, dvec, dy, states, *, name):
    LP = xbc.shape[0]
    Q = SSD_CHUNK
    nc = LP // Q
    G = SSM_GROUPS
    DI = dvec.shape[1]
    gw = DI // G
    hpg = gw // HEAD_DIM
    H = G * hpg
    boff, coff = DI, DI + G * D_STATE
    W = xbc.shape[1]

    def body(xbc_ref, dtr_ref, dtrt_ref, brow_ref, bcol_ref, alrow_ref, alcol_ref, dvec_ref, dy_ref, st_ref,
             dxbc_ref, ddtr_ref, dbias_ref, dalog_ref, ddvec_ref, dstate):
        c = pl.program_id(0)

        @pl.when(c == 0)
        def _():
            dstate[...] = jnp.zeros_like(dstate)
            dbias_ref[...] = jnp.zeros_like(dbias_ref)
            dalog_ref[...] = jnp.zeros_like(dalog_ref)
            ddvec_ref[...] = jnp.zeros_like(ddvec_ref)

        ii, jj, tril, dt_col, dt_row, a_row_p, cum_col, cum_row = _ssd_prelude(
            dtr_ref, dtrt_ref, brow_ref, bcol_ref, alrow_ref, alcol_ref, Q)
        eye = ii == jj
        lane_h = lax.broadcasted_iota(jnp.int32, (Q, 128), 1)
        row_h = lax.broadcasted_iota(jnp.int32, (Q, 128), 0)
        lane_g = lax.broadcasted_iota(jnp.int32, (Q, gw), 1) // HEAD_DIM
        lane_s = lax.broadcasted_iota(jnp.int32, (D_STATE, gw), 1) // HEAD_DIM
        dcum_mat = jnp.zeros((Q, 128), F32)
        ddt_mat = jnp.zeros((Q, 128), F32)
        dtot_row = jnp.zeros((1, 128), F32)
        for g in range(G):
            xg = xbc_ref[:, g * gw:(g + 1) * gw]
            dyg = dy_ref[:, g * gw:(g + 1) * gw]
            bb = xbc_ref[:, boff + g * D_STATE: boff + (g + 1) * D_STATE].astype(BF16)
            cb = xbc_ref[:, coff + g * D_STATE: coff + (g + 1) * D_STATE].astype(BF16)
            sg = st_ref[g]
            dsg = dstate[g]
            sb = sg.astype(BF16)
            dsb = dsg.astype(BF16)
            xb = xg.astype(BF16)
            gm = _dot_nt(cb, bb)
            cs = _dot(cb, sb)
            bds = _dot(bb, dsb)
            dxg = dvec_ref[:, g * gw:(g + 1) * gw] * dyg
            dgm = jnp.zeros((Q, Q), F32)
            esc = jnp.zeros((Q, gw), F32)
            wsc = jnp.zeros((Q, gw), F32)
            lam = jnp.zeros((1, gw), F32)
            dycs = dyg * cs
            xbds = xg * bds
            dss = dsg * sg
            for j in range(hpg):
                h = g * hpg + j
                ccol = _col_of(cum_col, lane_h, h)
                dcol = _col_of(dt_col, lane_h, h)
                drow = dt_row[h:h + 1, :]
                seg = ccol - cum_row[h:h + 1, :]
                decay = jnp.exp(jnp.where(tril, seg, -jnp.inf))
                hm = lane_g == j
                dyh = jnp.where(hm, dyg, 0.0).astype(BF16)
                gl = gm * decay
                mh = gl * drow
                dmf = _dot_nt(dyh, xb)
                dxg = dxg + _dot_tn(mh.astype(BF16), dyh)
                dgm = dgm + dmf * decay * drow
                n_p = dmf * gl
                n_m = n_p * drow
                rowsum_n = jnp.sum(n_m, axis=1, keepdims=True)
                colsum_n = jnp.sum(jnp.where(eye, jnp.sum(n_m, axis=0, keepdims=True), 0.0), axis=1, keepdims=True)
                colsum_np = jnp.sum(jnp.where(eye, jnp.sum(n_p, axis=0, keepdims=True), 0.0), axis=1, keepdims=True)
                tot = ccol[Q - 1:Q, :]
                e = jnp.exp(ccol)
                wexp = jnp.exp(tot - ccol)
                wcol = wexp * dcol
                lamh = jnp.exp(tot)
                yoff_t = jnp.sum(jnp.where(hm, dycs, 0.0), axis=1, keepdims=True) * e
                e_s = jnp.sum(jnp.where(hm, xbds, 0.0), axis=1, keepdims=True)
                ew = e_s * wcol
                dtot = jnp.sum(ew, axis=0, keepdims=True) + lamh * jnp.sum(
                    jnp.sum(jnp.where(lane_s == j, dss, 0.0), axis=1, keepdims=True), axis=0, keepdims=True)
                dcum_h = rowsum_n + yoff_t - colsum_n - ew
                ddt_h = colsum_np + e_s * wexp
                onehot = lane_h == h
                dcum_mat = jnp.where(onehot, dcum_h, dcum_mat)
                ddt_mat = jnp.where(onehot, ddt_h, ddt_mat)
                dtot_row = jnp.where(onehot[0:1], dtot, dtot_row)
                esc = jnp.where(hm, e, esc)
                wsc = jnp.where(hm, wcol, wsc)
                lam = jnp.where(hm[0:1], lamh, lam)
            dgb = dgm.astype(BF16)
            dye = (dyg * esc).astype(BF16)
            xw = (xg * wsc).astype(BF16)
            dxbc_ref[:, g * gw:(g + 1) * gw] = dxg + bds * wsc
            dxbc_ref[:, boff + g * D_STATE: boff + (g + 1) * D_STATE] = _dot_tn(dgb, cb) + _dot_nt(xw, dsb)
            dxbc_ref[:, coff + g * D_STATE: coff + (g + 1) * D_STATE] = _dot(dgb, bb) + _dot_nt(dye, sb)
            dstate[g] = dsg * lam + _dot_tn(cb, dye)
            ddvec_ref[:, g * gw:(g + 1) * gw] += jnp.sum(dyg * xg, axis=0, keepdims=True)
        dcum_mat = dcum_mat + jnp.where(row_h == Q - 1, dtot_row, 0.0)
        da = _dot3_left((ii <= jj).astype(BF16), dcum_mat)
        ddt = ddt_mat + da * a_row_p
        dalog_ref[...] += jnp.sum(da * dt_col, axis=0, keepdims=True) * a_row_p
        ddtr = ddt * _sigmoid(dtr_ref[...] + brow_ref[...])
        ddtr_ref[...] = ddtr
        dbias_ref[...] += jnp.sum(ddtr, axis=0, keepdims=True)

    full = lambda shape: pl.BlockSpec(shape, lambda c: (0,) * len(shape))
    rc = lambda c: nc - 1 - c
    return pl.pallas_call(
        body, name=name, grid=(nc,),
        in_specs=[pl.BlockSpec((Q, W), lambda c: (rc(c), 0)), pl.BlockSpec((Q, 128), lambda c: (rc(c), 0)),
                  pl.BlockSpec((H, Q), lambda c: (0, rc(c))), full((1, 128)), full((H, 1)), full((1, 128)), full((H, 1)),
                  full((1, DI)), pl.BlockSpec((Q, DI), lambda c: (rc(c), 0)),
                  pl.BlockSpec((None, G, D_STATE, gw), lambda c: (rc(c), 0, 0, 0))],
        out_specs=[pl.BlockSpec((Q, W), lambda c: (rc(c), 0)), pl.BlockSpec((Q, 128), lambda c: (rc(c), 0)),
                   full((1, 128)), full((1, 128)), full((1, DI))],
        out_shape=[jax.ShapeDtypeStruct((LP, W), F32), jax.ShapeDtypeStruct((LP, 128), F32),
                   jax.ShapeDtypeStruct((1, 128), F32), jax.ShapeDtypeStruct((1, 128), F32),
                   jax.ShapeDtypeStruct((1, DI), F32)],
        scratch_shapes=[pltpu.VMEM((G, D_STATE, gw), F32)],
        compiler_params=_cp("arbitrary"),
    )(xbc, dtr, dtrt, brow, bcol, alrow, alcol, dvec, dy, states)


def _gate_fwd(y, zx, g, *, name):
    LP, DI = y.shape
    gw = DI // SSM_GROUPS
    tr = _tile(LP, (256, 128))

    def body(y_ref, z_ref, g_ref, o_ref):
        for k in range(SSM_GROUPS):
            sl = slice(k * gw, (k + 1) * gw)
            z = z_ref[:, sl]
            t = y_ref[:, sl] * (z * _sigmoid(z))
            r = lax.rsqrt(jnp.mean(t * t, axis=1, keepdims=True) + RMS_EPS)
            o_ref[:, sl] = (t * r * g_ref[:, sl]).astype(BF16)

    row = pl.BlockSpec((tr, DI), lambda i: (i, 0))
    return pl.pallas_call(
        body, name=name, grid=(LP // tr,), in_specs=[row, row, pl.BlockSpec((1, DI), lambda i: (0, 0))],
        out_specs=row, out_shape=jax.ShapeDtypeStruct((LP, DI), BF16), compiler_params=_cp("parallel"),
    )(y, zx, g)


def _gate_bwd(y, zx, g, dyn, *, name):
    LP, DI = y.shape
    gw = DI // SSM_GROUPS
    tr = _tile(LP, (256, 128))

    def body(y_ref, z_ref, g_ref, d_ref, dy_ref, dz_ref, dg_ref):
        i = pl.program_id(0)

        @pl.when(i == 0)
        def _():
            dg_ref[...] = jnp.zeros_like(dg_ref)

        for k in range(SSM_GROUPS):
            sl = slice(k * gw, (k + 1) * gw)
            z = z_ref[:, sl]
            yv = y_ref[:, sl]
            sg = _sigmoid(z)
            sz = z * sg
            t = yv * sz
            r = lax.rsqrt(jnp.mean(t * t, axis=1, keepdims=True) + RMS_EPS)
            th = t * r
            d = d_ref[:, sl]
            dtn = d * g_ref[:, sl]
            dt_ = r * (dtn - th * jnp.mean(dtn * th, axis=1, keepdims=True))
            dg_ref[:, sl] += jnp.sum(d * th, axis=0, keepdims=True)
            dy_ref[:, sl] = dt_ * sz
            dz_ref[:, sl] = dt_ * yv * sg * (1.0 + z * (1.0 - sg))

    row = pl.BlockSpec((tr, DI), lambda i: (i, 0))
    vec = pl.BlockSpec((1, DI), lambda i: (0, 0))
    return pl.pallas_call(
        body, name=name, grid=(LP // tr,), in_specs=[row, row, vec, row], out_specs=[row, row, vec],
        out_shape=[jax.ShapeDtypeStruct((LP, DI), F32), jax.ShapeDtypeStruct((LP, DI), F32),
                   jax.ShapeDtypeStruct((1, DI), F32)],
        compiler_params=_cp("arbitrary"),
    )(y, zx, g, dyn)


def _sb_tile(q, k_blk, valid, lower, c):
    z = _dot_nt(q, k_blk)
    sp = _softplus(z)
    lk = jnp.where(valid, -sp, 0.0)
    rin = _dot3_right(lk, lower)
    w = jnp.where(valid, jnp.exp(z + rin + c), 0.0)
    return z, sp, lk, w


def _sb_fwd(q, k, v, *, name):
    H, LP, dh = q.shape
    T = ATT_BLOCK
    nq = LP // T
    assert nq <= LANES

    def body(q_ref, k_ref, v_ref, o_ref, c_ref):
        i = pl.program_id(1)
        ii = lax.broadcasted_iota(jnp.int32, (T, T), 0)
        jj = lax.broadcasted_iota(jnp.int32, (T, T), 1)
        lane = lax.broadcasted_iota(jnp.int32, (T, LANES), 1)
        lower = (ii >= jj).astype(BF16)
        qv = q_ref[...]
        c_ref[...] = jnp.zeros_like(c_ref)

        def step(it, carry):
            c, acc = carry
            kb = i - it
            ks = pl.multiple_of(kb * T, T)
            valid = (jj + kb * T) < (ii + i * T)
            _, _, lk, w = _sb_tile(qv, k_ref[pl.ds(ks, T), :], valid, lower, c)
            acc = acc + _dot(w.astype(BF16), v_ref[pl.ds(ks, T), :])
            c_ref[...] = jnp.where(lane == kb, c, c_ref[...])
            return c + jnp.sum(lk, axis=1, keepdims=True), acc

        _, acc = lax.fori_loop(0, i + 1, step, (jnp.zeros((T, 1), F32), jnp.zeros((T, dh), F32)))
        o_ref[...] = acc

    blk = pl.BlockSpec((None, T, dh), lambda h, i: (h, i, 0))
    cblk = pl.BlockSpec((None, T, LANES), lambda h, i: (h, i, 0))
    whole = pl.BlockSpec((None, LP, dh), lambda h, i: (h, 0, 0))
    return pl.pallas_call(
        body, name=name, grid=(H, nq), in_specs=[blk, whole, whole], out_specs=[blk, cblk],
        out_shape=[jax.ShapeDtypeStruct((H, LP, dh), F32), jax.ShapeDtypeStruct((H, LP, LANES), F32)],
        compiler_params=_cp("parallel", "parallel"),
    )(q, k, v)


def _sb_bwd(q, k, v, cmat, do, *, name):
    H, LP, dh = q.shape
    T = ATT_BLOCK
    nq = LP // T

    def body(q_ref, k_ref, v_ref, c_ref, do_ref, dq_ref, dk_ref, dv_ref):
        i = pl.program_id(1)

        @pl.when(i == 0)
        def _():
            dk_ref[...] = jnp.zeros_like(dk_ref)
            dv_ref[...] = jnp.zeros_like(dv_ref)

        ii = lax.broadcasted_iota(jnp.int32, (T, T), 0)
        jj = lax.broadcasted_iota(jnp.int32, (T, T), 1)
        lane = lax.broadcasted_iota(jnp.int32, (T, LANES), 1)
        lower = (ii >= jj).astype(BF16)
        upper = (ii <= jj).astype(BF16)
        qv = q_ref[...]
        dob = do_ref[...].astype(BF16)
        cm = c_ref[...]

        def step(kb, carry):
            cg, dq = carry
            ks = pl.multiple_of(kb * T, T)
            valid = (jj + kb * T) < (ii + i * T)
            k_blk = k_ref[pl.ds(ks, T), :]
            c = jnp.sum(jnp.where(lane == kb, cm, 0.0), axis=1, keepdims=True)
            z, sp, _, w = _sb_tile(qv, k_blk, valid, lower, c)
            gw_ = w * _dot_nt(dob, v_ref[pl.ds(ks, T), :])
            gin = _dot3_right(gw_, upper)
            dz = jnp.where(valid, gw_ - jnp.exp(z - sp) * (cg + gin), 0.0).astype(BF16)
            dq = dq + _dot(dz, k_blk)
            dk_ref[pl.ds(ks, T), :] += _dot_tn(dz, qv)
            dv_ref[pl.ds(ks, T), :] += _dot_tn(w.astype(BF16), dob)
            return cg + jnp.sum(gw_, axis=1, keepdims=True), dq

        _, dq = lax.fori_loop(0, i + 1, step, (jnp.zeros((T, 1), F32), jnp.zeros((T, dh), F32)))
        dq_ref[...] = dq

    blk = pl.BlockSpec((None, T, dh), lambda h, i: (h, i, 0))
    cblk = pl.BlockSpec((None, T, LANES), lambda h, i: (h, i, 0))
    whole = pl.BlockSpec((None, LP, dh), lambda h, i: (h, 0, 0))
    sh = jax.ShapeDtypeStruct((H, LP, dh), F32)
    return pl.pallas_call(
        body, name=name, grid=(H, nq), in_specs=[blk, whole, whole, cblk, blk], out_specs=[blk, whole, whole],
        out_shape=[sh, sh, sh], compiler_params=_cp("parallel", "arbitrary"),
    )(q, k, v, cmat, do)


def _loss_head(h, tgt, seq, *, name):
    LP, D = h.shape
    tr = _tile(LP, (384, 256, 128))

    def body(h_ref, t_ref, dh_ref, l_ref):
        i = pl.program_id(0)

        @pl.when(i == 0)
        def _():
            l_ref[...] = jnp.zeros_like(l_ref)

        row = lax.broadcasted_iota(jnp.int32, (tr, D), 0) + i * tr
        e = jnp.where((row >= N_META) & (row < N_META + seq), h_ref[...] - t_ref[...], 0.0)
        dh_ref[...] = e * (1.0 / D)
        l_ref[...] += jnp.sum(e * e, axis=0, keepdims=True) * (0.5 / D)

    row = pl.BlockSpec((tr, D), lambda i: (i, 0))
    return pl.pallas_call(
        body, name=name, grid=(LP // tr,), in_specs=[row, row], out_specs=[row, pl.BlockSpec((1, D), lambda i: (0, 0))],
        out_shape=[jax.ShapeDtypeStruct((LP, D), F32), jax.ShapeDtypeStruct((1, D), F32)],
        compiler_params=_cp("arbitrary"),
    )(h, tgt)


def _adamw(w, g, m, v, *, name):
    shape = w.shape
    C = shape[-1]
    R = math.prod(shape) // C
    tr = _tile(R, (512, 256, 128, 64, 32, 16, 8))
    c1 = 1.0 / (1.0 - ADAM_B1 ** ADAM_STEP)
    c2 = 1.0 / (1.0 - ADAM_B2 ** ADAM_STEP)

    def body(w_ref, g_ref, m_ref, v_ref, d_ref, nm_ref, nv_ref):
        gv = g_ref[...]
        nm = ADAM_B1 * m_ref[...] + (1.0 - ADAM_B1) * gv
        nv = ADAM_B2 * v_ref[...] + (1.0 - ADAM_B2) * (gv * gv)
        d_ref[...] = -ADAM_LR * ((nm * c1) / (jnp.sqrt(nv * c2) + ADAM_EPS) + ADAM_WD * w_ref[...])
        nm_ref[...] = nm
        nv_ref[...] = nv

    blk = pl.BlockSpec((tr, C), lambda i: (i, 0))
    sh = jax.ShapeDtypeStruct((R, C), F32)
    d, nm, nv = pl.pallas_call(
        body, name=name, grid=(R // tr,), in_specs=[blk] * 4, out_specs=[blk] * 3, out_shape=[sh] * 3,
        compiler_params=_cp("parallel"),
    )(w.reshape(R, C), g.reshape(R, C), m.reshape(R, C), v.reshape(R, C))
    return d.reshape(shape), nm.reshape(shape), nv.reshape(shape)


def _sum_rows(buf, *, name):
    n, R, C = buf.shape
    tr = _tile(R, (512, 256, 128, 64, 32, 16, 8))

    def body(b_ref, o_ref):
        acc = b_ref[0]
        for k in range(1, n):
            acc = acc + b_ref[k]
        o_ref[...] = acc

    return pl.pallas_call(
        body, name=name, grid=(R // tr,), in_specs=[pl.BlockSpec((n, tr, C), lambda i: (0, i, 0))],
        out_specs=pl.BlockSpec((tr, C), lambda i: (i, 0)), out_shape=jax.ShapeDtypeStruct((R, C), buf.dtype),
        compiler_params=_cp("parallel"),
    )(buf)


def _mesh_pos():
    x, y, c = lax.axis_index("x"), lax.axis_index("y"), lax.axis_index("c")
    return x, y, c, 4 * x + 2 * y + c


def _peer(x, y, c, f):
    px, py, pc = (x + ((f >> 2) & 1)) % 2, (y + ((f >> 1) & 1)) % 2, (c + (f & 1)) % 2
    return (px, py, pc), 4 * px + 2 * py + pc


def _exchange(src, *, scatter, name):
    shape = src.shape[1:] if scatter else src.shape

    def body(src_ref, dst_ref, send_sems, recv_sems, local_sem):
        x, y, c, me = _mesh_pos()
        own = pltpu.make_async_copy(src_ref.at[me] if scatter else src_ref, dst_ref.at[me], local_sem)
        own.start()
        sends, recvs = [], []
        for f in range(1, N_DEV):
            peer, pid = _peer(x, y, c, f)
            sends.append(pltpu.make_async_remote_copy(
                src_ref=src_ref.at[pid] if scatter else src_ref, dst_ref=dst_ref.at[me],
                send_sem=send_sems.at[f - 1], recv_sem=recv_sems.at[f - 1], device_id=peer, device_id_type=MESH))
            recvs.append(pltpu.make_async_remote_copy(
                src_ref=src_ref.at[pid] if scatter else src_ref, dst_ref=dst_ref.at[pid],
                send_sem=send_sems.at[f - 1], recv_sem=recv_sems.at[f - 1], device_id=peer, device_id_type=MESH))
        for cp in sends:
            cp.start()
        for snd, rcv in zip(sends, recvs):
            snd.wait_send()
            rcv.wait_recv()
        own.wait()

    return pl.pallas_call(
        body, name=name, in_specs=[pl.BlockSpec(memory_space=pl.ANY)], out_specs=pl.BlockSpec(memory_space=pl.ANY),
        out_shape=jax.ShapeDtypeStruct((N_DEV,) + tuple(shape), src.dtype),
        scratch_shapes=[pltpu.SemaphoreType.DMA((N_DEV - 1,)), pltpu.SemaphoreType.DMA((N_DEV - 1,)),
                        pltpu.SemaphoreType.DMA],
        compiler_params=pltpu.CompilerParams(has_side_effects=True),
    )(src)


def _heads(t, H):
    LP = t.shape[0]
    return t.reshape(LP, H, HEAD_DIM).transpose(1, 0, 2)


def _unheads(t):
    H, LP, dh = t.shape
    return t.transpose(1, 0, 2).reshape(LP, H * dh)


def _ffn_fwd(h, g, w1, w3, w2, tag):
    n = _rms_fwd(h, g, name=f"ffn_norm_{tag}")
    a, b, s = _ffn_up(n, w1, w3, name=f"ffn_up_{tag}")
    h2 = _matmul([(s, w2)], res=h, alpha=FFN_RES, name=f"ffn_down_{tag}")
    return h2, (h, n, a, b, s)


def _ffn_bwd(dh, saved, g, w1, w3, w2, tag):
    h, n, a, b, s = saved
    da, db = _ffn_mid_bwd(dh, w2.T, a, b, name=f"ffn_mid_bwd_{tag}")
    dw2 = _matmul([(s, dh)], trans_a=True, alpha=FFN_RES, name=f"ffn_dw2_{tag}")
    dn = _matmul([(da, w1.T), (db, w3.T)], name=f"ffn_dn_{tag}")
    dw1 = _matmul([(n, da)], trans_a=True, name=f"ffn_dw1_{tag}")
    dw3 = _matmul([(n, db)], trans_a=True, name=f"ffn_dw3_{tag}")
    dh_in, dg = _rms_bwd(h, g, dn, res=dh, name=f"ffn_norm_bwd_{tag}")
    return dh_in, dg, dw1, dw3, dw2


def _local_step(x, tgt, W):
    seq, D = x.shape
    L = N_META + seq
    LP = -(-L // ROW_ALIGN) * ROW_ALIGN
    pad = LP - L
    H_sb = D // HEAD_DIM
    DI = W["ssm_norm_g"].shape[-1]
    H_ssm = DI // HEAD_DIM
    CONV = DI + 2 * SSM_GROUPS * D_STATE
    ZX = DI + CONV

    h0 = jnp.concatenate([W["meta_tokens"], x, jnp.zeros((pad, D), F32)], axis=0)
    tgt_p = jnp.pad(tgt, ((N_META, pad), (0, 0)))
    ng = W["norm_g"]

    h1, sv_f00 = _ffn_fwd(h0, ng[0, 0], W["ffn_w1"][0, 0], W["ffn_w3"][0, 0], W["ffn_w2"][0, 0], "00")
    u0 = _rms_fwd(h1, ng[0, 1], name="ssm_norm")
    w_in = W["ssm_in_proj"][0]
    w_zx = w_in[:, :ZX]
    w_dt = jnp.pad(w_in[:, ZX:], ((0, 0), (0, 128 - H_ssm)))
    zx = _matmul([(u0, w_zx)], name="ssm_in_zx")
    dtr = _matmul([(u0, w_dt)], name="ssm_in_dt")
    conv_w, conv_b = W["ssm_conv_w"][0], W["ssm_conv_b"][0]
    xbc = _conv_fwd(zx, conv_w, conv_b, DI, name="ssm_conv")
    dtrt = dtr[:, :H_ssm].T
    padh = lambda t: jnp.pad(t.reshape(1, H_ssm), ((0, 0), (0, 128 - H_ssm)))
    brow, bcol = padh(W["ssm_dt_bias"][0]), W["ssm_dt_bias"][0].reshape(H_ssm, 1)
    alrow, alcol = padh(W["ssm_a_log"][0]), W["ssm_a_log"][0].reshape(H_ssm, 1)
    dvec = jnp.repeat(W["ssm_d"][0], HEAD_DIM).reshape(1, DI)
    ssm_args = (xbc, dtr, dtrt, brow, bcol, alrow, alcol, dvec)
    y, states = _ssd_fwd(*ssm_args, name="ssd_fwd")
    sng = W["ssm_norm_g"].reshape(1, DI)
    yn = _gate_fwd(y, zx, sng, name="ssm_gate")
    w_out = W["ssm_out_proj"][0]
    h2 = _matmul([(yn, w_out)], res=h1, name="ssm_out")
    h3, sv_f01 = _ffn_fwd(h2, ng[0, 2], W["ffn_w1"][0, 1], W["ffn_w3"][0, 1], W["ffn_w2"][0, 1], "01")

    kv_in = _rms_fwd(h3, W["kv_norm_g"], name="kv_norm")
    kraw = _heads(_matmul([(kv_in, W["w_k"])], name="kv_k"), H_sb)
    vh = _heads(_matmul([(kv_in, W["w_v"])], out_dtype=BF16, name="kv_v"), H_sb)
    kh = _rms_fwd(kraw.reshape(H_sb * LP, HEAD_DIM), W["k_norm_g"], name="k_headnorm").reshape(H_sb, LP, HEAD_DIM)

    h4, sv_f10 = _ffn_fwd(h3, ng[1, 0], W["ffn_w1"][1, 0], W["ffn_w3"][1, 0], W["ffn_w2"][1, 0], "10")
    u1 = _rms_fwd(h4, ng[1, 1], name="sb_norm")
    qraw = _heads(_matmul([(u1, W["sb_w_q"][0])], name="sb_q"), H_sb)
    scale = HEAD_DIM ** -0.5
    qh = _rms_fwd(qraw.reshape(H_sb * LP, HEAD_DIM), W["sb_q_norm_g"][0], scale=scale,
                  name="q_headnorm").reshape(H_sb, LP, HEAD_DIM)
    o, cmat = _sb_fwd(qh, kh, vh, name="sb_fwd")
    o_flat = _unheads(o)
    h5 = _matmul([(o_flat, W["sb_w_o"][0])], res=h4, name="sb_out")
    h6, sv_f11 = _ffn_fwd(h5, ng[1, 2], W["ffn_w1"][1, 1], W["ffn_w3"][1, 1], W["ffn_w2"][1, 1], "11")

    dh, lvec = _loss_head(h6, tgt_p, seq, name="loss_head")
    loss = jnp.sum(lvec)

    G = {}
    dng = [[None] * 3 for _ in range(2)]
    dw1 = [[None] * 2 for _ in range(2)]
    dw3 = [[None] * 2 for _ in range(2)]
    dw2 = [[None] * 2 for _ in range(2)]

    dh, dng[1][2], dw1[1][1], dw3[1][1], dw2[1][1] = _ffn_bwd(
        dh, sv_f11, ng[1, 2], W["ffn_w1"][1, 1], W["ffn_w3"][1, 1], W["ffn_w2"][1, 1], "11")
    G["sb_w_o"] = _matmul([(o_flat, dh)], trans_a=True, name="sb_dwo")[None]
    do = _heads(_matmul([(dh, W["sb_w_o"][0].T)], name="sb_do"), H_sb)
    dq, dk, dv = _sb_bwd(qh, kh, vh, cmat, do, name="sb_bwd")
    dqraw, dqg = _rms_bwd(qraw.reshape(H_sb * LP, HEAD_DIM), W["sb_q_norm_g"][0], dq.reshape(H_sb * LP, HEAD_DIM),
                          alpha=scale, name="q_headnorm_bwd")
    G["sb_q_norm_g"] = dqg
    dqraw = _unheads(dqraw.reshape(H_sb, LP, HEAD_DIM))
    G["sb_w_q"] = _matmul([(u1, dqraw)], trans_a=True, name="sb_dwq")[None]
    du1 = _matmul([(dqraw, W["sb_w_q"][0].T)], name="sb_du")
    dh, dng[1][1] = _rms_bwd(h4, ng[1, 1], du1, res=dh, name="sb_norm_bwd")
    dh, dng[1][0], dw1[1][0], dw3[1][0], dw2[1][0] = _ffn_bwd(
        dh, sv_f10, ng[1, 0], W["ffn_w1"][1, 0], W["ffn_w3"][1, 0], W["ffn_w2"][1, 0], "10")

    dkraw, dkg = _rms_bwd(kraw.reshape(H_sb * LP, HEAD_DIM), W["k_norm_g"], dk.reshape(H_sb * LP, HEAD_DIM),
                          name="k_headnorm_bwd")
    G["k_norm_g"] = dkg.reshape(-1)
    dkraw = _unheads(dkraw.reshape(H_sb, LP, HEAD_DIM))
    dvf = _unheads(dv)
    G["w_k"] = _matmul([(kv_in, dkraw)], trans_a=True, name="kv_dwk")
    G["w_v"] = _matmul([(kv_in, dvf)], trans_a=True, name="kv_dwv")
    dkv = _matmul([(dkraw, W["w_k"].T), (dvf, W["w_v"].T)], name="kv_din")
    dh, dkvg = _rms_bwd(h3, W["kv_norm_g"], dkv, res=dh, name="kv_norm_bwd")
    G["kv_norm_g"] = dkvg.reshape(-1)

    dh, dng[0][2], dw1[0][1], dw3[0][1], dw2[0][1] = _ffn_bwd(
        dh, sv_f01, ng[0, 2], W["ffn_w1"][0, 1], W["ffn_w3"][0, 1], W["ffn_w2"][0, 1], "01")
    G["ssm_out_proj"] = _matmul([(yn, dh)], trans_a=True, name="ssm_dwout")[None]
    dyn = _matmul([(dh, w_out.T)], name="ssm_dyn")
    dy, dz, dsng = _gate_bwd(y, zx, sng, dyn, name="ssm_gate_bwd")
    G["ssm_norm_g"] = dsng
    dxbc, ddtr, dbias, dalog, ddvec = _ssd_bwd(*ssm_args, dy, states, name="ssd_bwd")
    G["ssm_dt_bias"] = dbias[:, :H_ssm]
    G["ssm_a_log"] = dalog[:, :H_ssm]
    G["ssm_d"] = jnp.sum(ddvec.reshape(H_ssm, HEAD_DIM), axis=1).reshape(1, H_ssm)
    gpre, dcw, dcb = _conv_bwd_g(zx, conv_w, conv_b, dxbc, DI, name="ssm_conv_bwd_g")
    G["ssm_conv_w"] = dcw[:D_CONV][None]
    G["ssm_conv_b"] = dcb
    dxbc_pre = _conv_bwd_u(gpre, conv_w, name="ssm_conv_bwd_u")
    G["ssm_in_proj"] = jnp.concatenate([
        _matmul([(u0, dz)], trans_a=True, name="ssm_dwin_z"),
        _matmul([(u0, dxbc_pre)], trans_a=True, name="ssm_dwin_x"),
        _matmul([(u0, ddtr)], trans_a=True, name="ssm_dwin_dt")[:, :H_ssm]], axis=1)[None]
    du0 = _matmul([(dz, w_zx[:, :DI].T)], name="ssm_du_z")
    du0 = _matmul([(dxbc_pre, w_zx[:, DI:].T)], res=du0, name="ssm_du_x")
    du0 = _matmul([(ddtr, w_dt.T)], res=du0, name="ssm_du_dt")
    dh, dng[0][1] = _rms_bwd(h1, ng[0, 1], du0, res=dh, name="ssm_norm_bwd")
    dh, dng[0][0], dw1[0][0], dw3[0][0], dw2[0][0] = _ffn_bwd(
        dh, sv_f00, ng[0, 0], W["ffn_w1"][0, 0], W["ffn_w3"][0, 0], W["ffn_w2"][0, 0], "00")

    G["norm_g"] = jnp.stack([jnp.concatenate(r, axis=0) for r in dng])
    stack2 = lambda t: jnp.stack([jnp.stack(r) for r in t])
    G["ffn_w1"], G["ffn_w3"], G["ffn_w2"] = stack2(dw1), stack2(dw3), stack2(dw2)
    G["meta_tokens"] = dh[:N_META]
    return loss, dh[N_META:L], G


WEIGHTS = ['meta_tokens', 'norm_g', 'ffn_w1', 'ffn_w3', 'ffn_w2', 'ssm_in_proj', 'ssm_conv_w', 'ssm_conv_b',
           'ssm_dt_bias', 'ssm_a_log', 'ssm_d', 'ssm_norm_g', 'ssm_out_proj', 'kv_norm_g', 'w_k', 'k_norm_g', 'w_v',
           'sb_w_q', 'sb_q_norm_g', 'sb_w_o']
SHARD_AXIS = {'meta_tokens': 1, 'norm_g': 2, 'ffn_w1': 3, 'ffn_w3': 3, 'ffn_w2': 2, 'ssm_in_proj': 2, 'ssm_conv_w': 2,
              'ssm_conv_b': 1, 'ssm_dt_bias': None, 'ssm_a_log': None, 'ssm_d': None, 'ssm_norm_g': 1,
              'ssm_out_proj': 1, 'kv_norm_g': None, 'w_k': 0, 'k_norm_g': None, 'w_v': 0, 'sb_w_q': 1,
              'sb_q_norm_g': None, 'sb_w_o': 1}
MATMUL_WEIGHTS = ('ffn_w1', 'ffn_w3', 'ffn_w2', 'ssm_in_proj', 'ssm_out_proj', 'w_k', 'w_v', 'sb_w_q', 'sb_w_o')
PACK_ROWS = 16


def _pack(arrs, dtype):
    flat = jnp.concatenate([a.reshape(-1).astype(dtype) for a in arrs])
    n = flat.shape[0]
    npad = -(-n // (LANES * PACK_ROWS)) * (LANES * PACK_ROWS)
    return jnp.pad(flat, (0, npad - n)).reshape(npad // LANES, LANES)


def _unpack_gathered(buf, names, shard_shapes, dtype):
    flat = buf.reshape(N_DEV, -1)
    out, off = {}, 0
    for n in names:
        shp = shard_shapes[n]
        size = math.prod(shp)
        t = flat[:, off:off + size].reshape((N_DEV,) + tuple(shp))
        off += size
        ax = SHARD_AXIS[n]
        t = jnp.moveaxis(t, 0, ax)
        full = shp[:ax] + (N_DEV * shp[ax],) + shp[ax + 1:]
        out[n] = t.reshape(full).astype(dtype)
    return out


def _to_shards(g, ax):
    shp = g.shape
    t = g.reshape(shp[:ax] + (N_DEV, shp[ax] // N_DEV) + shp[ax + 1:])
    return jnp.moveaxis(t, ax, 0).reshape(N_DEV, -1)


def kernel(x, meta_tokens, norm_g, ffn_w1, ffn_w3, ffn_w2, ssm_in_proj, ssm_conv_w, ssm_conv_b, ssm_dt_bias, ssm_a_log, ssm_d, ssm_norm_g, ssm_out_proj, kv_norm_g, w_k, k_norm_g, w_v, sb_w_q, sb_q_norm_g, sb_w_o, loss_target, m_meta_tokens, m_norm_g, m_ffn_w1, m_ffn_w3, m_ffn_w2, m_ssm_in_proj, m_ssm_conv_w, m_ssm_conv_b, m_ssm_dt_bias, m_ssm_a_log, m_ssm_d, m_ssm_norm_g, m_ssm_out_proj, m_kv_norm_g, m_w_k, m_k_norm_g, m_w_v, m_sb_w_q, m_sb_q_norm_g, m_sb_w_o, v_meta_tokens, v_norm_g, v_ffn_w1, v_ffn_w3, v_ffn_w2, v_ssm_in_proj, v_ssm_conv_w, v_ssm_conv_b, v_ssm_dt_bias, v_ssm_a_log, v_ssm_d, v_ssm_norm_g, v_ssm_out_proj, v_kv_norm_g, v_w_k, v_k_norm_g, v_w_v, v_sb_w_q, v_sb_q_norm_g, v_sb_w_o):
    shard = dict(meta_tokens=meta_tokens, norm_g=norm_g, ffn_w1=ffn_w1, ffn_w3=ffn_w3, ffn_w2=ffn_w2,
                 ssm_in_proj=ssm_in_proj, ssm_conv_w=ssm_conv_w, ssm_conv_b=ssm_conv_b, ssm_dt_bias=ssm_dt_bias,
                 ssm_a_log=ssm_a_log, ssm_d=ssm_d, ssm_norm_g=ssm_norm_g, ssm_out_proj=ssm_out_proj,
                 kv_norm_g=kv_norm_g, w_k=w_k, k_norm_g=k_norm_g, w_v=w_v, sb_w_q=sb_w_q, sb_q_norm_g=sb_q_norm_g,
                 sb_w_o=sb_w_o)
    mom_m = dict(zip(WEIGHTS, (m_meta_tokens, m_norm_g, m_ffn_w1, m_ffn_w3, m_ffn_w2, m_ssm_in_proj, m_ssm_conv_w,
                               m_ssm_conv_b, m_ssm_dt_bias, m_ssm_a_log, m_ssm_d, m_ssm_norm_g, m_ssm_out_proj,
                               m_kv_norm_g, m_w_k, m_k_norm_g, m_w_v, m_sb_w_q, m_sb_q_norm_g, m_sb_w_o)))
    mom_v = dict(zip(WEIGHTS, (v_meta_tokens, v_norm_g, v_ffn_w1, v_ffn_w3, v_ffn_w2, v_ssm_in_proj, v_ssm_conv_w,
                               v_ssm_conv_b, v_ssm_dt_bias, v_ssm_a_log, v_ssm_d, v_ssm_norm_g, v_ssm_out_proj,
                               v_kv_norm_g, v_w_k, v_k_norm_g, v_w_v, v_sb_w_q, v_sb_q_norm_g, v_sb_w_o)))
    sharded = [n for n in WEIGHTS if SHARD_AXIS[n] is not None]
    replicated = [n for n in WEIGHTS if SHARD_AXIS[n] is None]
    big = [n for n in sharded if n in MATMUL_WEIGHTS]
    small = [n for n in sharded if n not in MATMUL_WEIGHTS]
    shapes = {n: tuple(shard[n].shape) for n in WEIGHTS}

    gathered_big = _exchange(_pack([shard[n] for n in big], BF16), scatter=False, name="gather_matmul_weights")
    gathered_small = _exchange(_pack([shard[n] for n in small], F32), scatter=False, name="gather_gains")
    W = dict(_unpack_gathered(gathered_big, big, shapes, BF16))
    W.update(_unpack_gathered(gathered_small, small, shapes, F32))
    for n in replicated:
        W[n] = shard[n]

    loss, dx, G = _local_step(x[0], loss_target[0], W)
    loss = lax.psum(loss, ("x", "y", "c"))

    send = jnp.concatenate([_to_shards(G[n], SHARD_AXIS[n]) for n in sharded], axis=1)
    n_el = send.shape[1]
    npad = -(-n_el // (LANES * PACK_ROWS)) * (LANES * PACK_ROWS)
    send = jnp.pad(send, ((0, 0), (0, npad - n_el))).reshape(N_DEV, npad // LANES, LANES)
    summed = _sum_rows(_exchange(send, scatter=True, name="scatter_grads"), name="sum_grads").reshape(-1)
    rep = _pack([G[n] for n in replicated], F32)
    rep_sum = _sum_rows(_exchange(rep, scatter=False, name="gather_small_grads"), name="sum_small_grads").reshape(-1)

    grads, off = {}, 0
    for n in sharded:
        size = math.prod(shapes[n])
        grads[n] = summed[off:off + size].reshape(shapes[n])
        off += size
    off = 0
    for n in replicated:
        size = math.prod(shapes[n])
        grads[n] = rep_sum[off:off + size].reshape(shapes[n])
        off += size

    delta, new_m, new_v = {}, {}, {}
    for n in WEIGHTS:
        w2 = shard[n].reshape(1, -1) if shard[n].ndim == 1 else shard[n]
        r2 = lambda t: t.reshape(w2.shape)
        d, nm, nv = _adamw(w2, r2(grads[n]), r2(mom_m[n]), r2(mom_v[n]), name=f"adamw_{n}")
        delta[n], new_m[n], new_v[n] = (t.reshape(shapes[n]) for t in (d, nm, nv))

    return (loss, dx[None], *[grads[n] for n in WEIGHTS], *[delta[n] for n in WEIGHTS],
            *[new_m[n] for n in WEIGHTS], *[new_v[n] for n in WEIGHTS])
```

```python
import functools
import math

import jax
import jax.numpy as jnp
from jax import lax
from jax.experimental import pallas as pl
from jax.experimental.pallas import tpu as pltpu

F32 = jnp.float32
BF16 = jnp.bfloat16
RMS_EPS = 1e-6
N_META = 16
HEAD_DIM = 64
SSM_GROUPS = 8
D_STATE = 128
D_CONV = 4
FFN_RES = 0.5
ADAM_LR, ADAM_B1, ADAM_B2, ADAM_EPS, ADAM_WD, ADAM_STEP = 0.001, 0.9, 0.999, 1e-08, 0.01, 10
N_DEV = 8
SSD_CHUNK = 128
ATT_BLOCK = 256
ROW_ALIGN = 768
VMEM_LIMIT_V7X = 48 * 1024 * 1024
MESH = pl.DeviceIdType.MESH
LANES = 128


def _cp(*sem):
    return pltpu.CompilerParams(dimension_semantics=sem if sem else None, vmem_limit_bytes=VMEM_LIMIT_V7X)


def _tile(n, cands):
    for c in cands:
        if n % c == 0:
            return c
    return n


def _softplus(x):
    return jnp.maximum(x, 0.0) + jnp.log(1.0 + jnp.exp(-jnp.abs(x)))


def _sigmoid(x):
    return 1.0 / (1.0 + jnp.exp(-x))


def _split3(x):
    hi = x.astype(BF16)
    r1 = x - hi.astype(F32)
    mid = r1.astype(BF16)
    lo = (r1 - mid.astype(F32)).astype(BF16)
    return hi, mid, lo


def _dot(a, b):
    return jnp.dot(a, b, preferred_element_type=F32)


def _dot_nt(a, b):
    return lax.dot_general(a, b, (((1,), (1,)), ((), ())), preferred_element_type=F32)


def _dot_tn(a, b):
    return lax.dot_general(a, b, (((0,), (0,)), ((), ())), preferred_element_type=F32)


def _dot3_left(t_bf16, x):
    hi, mid, lo = _split3(x)
    return _dot(t_bf16, hi) + _dot(t_bf16, mid) + _dot(t_bf16, lo)


def _dot3_right(x, t_bf16):
    hi, mid, lo = _split3(x)
    return _dot(hi, t_bf16) + _dot(mid, t_bf16) + _dot(lo, t_bf16)


def _matmul(pairs, *, name, out_dtype=F32, trans_a=False, res=None, alpha=1.0, shards=None, tm=None, tn=None, tk=None):
    a0, b0 = pairs[0]
    if trans_a:
        K, M = a0.shape
    else:
        M, K = a0.shape
    N = b0.shape[1]
    tm = tm or _tile(M, (768, 512, 1408, 384, 256, 128))
    tn = tn or (N if shards else _tile(N, (512, 1408, 384, 256, 128)))
    tk = tk or _tile(K, (1024, 1408, 768, 512, 256, 128))
    nk = K // tk
    npair = len(pairs)
    has_res = res is not None
    cs = N // shards if shards else None

    def body(*refs):
        o_ref, acc = refs[-2], refs[-1]
        k = pl.program_id(2)

        @pl.when(k == 0)
        def _():
            acc[...] = jnp.zeros_like(acc)

        part = None
        for p in range(npair):
            a = refs[2 * p][...].astype(BF16)
            b = refs[2 * p + 1][...].astype(BF16)
            d = _dot_tn(a, b) if trans_a else _dot(a, b)
            part = d if part is None else part + d
        acc[...] += part

        @pl.when(k == nk - 1)
        def _():
            if shards:
                for d in range(shards):
                    v = acc[:, d * cs:(d + 1) * cs]
                    o_ref[d] = (v * alpha if alpha != 1.0 else v).astype(out_dtype)
                return
            v = acc[...]
            if alpha != 1.0:
                v = v * alpha
            if has_res:
                v = refs[2 * npair][...] + v
            o_ref[...] = v.astype(out_dtype)

    if trans_a:
        a_spec = pl.BlockSpec((tk, tm), lambda i, j, k: (k, i))
    else:
        a_spec = pl.BlockSpec((tm, tk), lambda i, j, k: (i, k))
    b_spec = pl.BlockSpec((tk, tn), lambda i, j, k: (k, j))
    if shards:
        assert not has_res and tn == N
        o_spec = pl.BlockSpec((shards, tm, cs), lambda i, j, k: (0, i, 0))
        out_shape = jax.ShapeDtypeStruct((shards, M, cs), out_dtype)
    else:
        o_spec = pl.BlockSpec((tm, tn), lambda i, j, k: (i, j))
        out_shape = jax.ShapeDtypeStruct((M, N), out_dtype)
    in_specs, args = [], []
    for a, b in pairs:
        in_specs += [a_spec, b_spec]
        args += [a, b]
    if has_res:
        in_specs.append(o_spec)
        args.append(res)
    return pl.pallas_call(
        body, name=name, grid=(M // tm, N // tn, nk), in_specs=in_specs, out_specs=o_spec,
        out_shape=out_shape,
        scratch_shapes=[pltpu.VMEM((tm, tn), F32)],
        compiler_params=_cp("parallel", "parallel", "arbitrary"),
    )(*args)


def _rms_fwd(h, g, *, name, scale=1.0):
    R, D = h.shape
    tr = _tile(R, (2048, 1024, 768, 512, 256, 128)) if D <= 128 else _tile(R, (384, 256, 128))

    def body(h_ref, g_ref, o_ref):
        x = h_ref[...]
        r = lax.rsqrt(jnp.mean(x * x, axis=1, keepdims=True) + RMS_EPS)
        y = x * r * g_ref[...]
        if scale != 1.0:
            y = y * scale
        o_ref[...] = y.astype(BF16)

    return pl.pallas_call(
        body, name=name, grid=(R // tr,),
        in_specs=[pl.BlockSpec((tr, D), lambda i: (i, 0)), pl.BlockSpec((1, D), lambda i: (0, 0))],
        out_specs=pl.BlockSpec((tr, D), lambda i: (i, 0)),
        out_shape=jax.ShapeDtypeStruct((R, D), BF16), compiler_params=_cp("parallel"),
    )(h, g.reshape(1, D))


def _rms_bwd(h, g, dn, res=None, *, name, alpha=1.0):
    R, D = h.shape
    tr = _tile(R, (2048, 1024, 768, 512, 256, 128)) if D <= 128 else _tile(R, (384, 256, 128))
    has_res = res is not None

    def body(*refs):
        h_ref, g_ref, dn_ref = refs[:3]
        dh_ref, dg_ref = refs[-2], refs[-1]
        i = pl.program_id(0)

        @pl.when(i == 0)
        def _():
            dg_ref[...] = jnp.zeros_like(dg_ref)

        x = h_ref[...]
        r = lax.rsqrt(jnp.mean(x * x, axis=1, keepdims=True) + RMS_EPS)
        xh = x * r
        d = dn_ref[...].astype(F32)
        if alpha != 1.0:
            d = d * alpha
        dng = d * g_ref[...]
        m = jnp.mean(dng * xh, axis=1, keepdims=True)
        dh = r * (dng - xh * m)
        if has_res:
            dh = dh + refs[3][...]
        dh_ref[...] = dh
        dg_ref[...] += jnp.sum(d * xh, axis=0, keepdims=True)

    row = pl.BlockSpec((tr, D), lambda i: (i, 0))
    vec = pl.BlockSpec((1, D), lambda i: (0, 0))
    in_specs = [row, vec, row] + ([row] if has_res else [])
    args = [h, g.reshape(1, D), dn] + ([res] if has_res else [])
    return pl.pallas_call(
        body, name=name, grid=(R // tr,), in_specs=in_specs, out_specs=[row, vec],
        out_shape=[jax.ShapeDtypeStruct((R, D), F32), jax.ShapeDtypeStruct((1, D), F32)],
        compiler_params=_cp("arbitrary"),
    )(*args)


def _ffn_up(n, w1, w3, *, name):
    M, K = n.shape
    N = w1.shape[1]
    tm = _tile(M, (384, 256, 128))
    tn = _tile(N, (1408, 512, 256, 128))

    def body(n_ref, w1_ref, w3_ref, a_ref, b_ref, s_ref):
        x = n_ref[...]
        a = _dot(x, w1_ref[...])
        b = _dot(x, w3_ref[...])
        a_ref[...] = a.astype(BF16)
        b_ref[...] = b.astype(BF16)
        s_ref[...] = (a * _sigmoid(a) * b).astype(BF16)

    o_spec = pl.BlockSpec((tm, tn), lambda j, i: (i, j))
    w_spec = pl.BlockSpec((K, tn), lambda j, i: (0, j))
    sh = jax.ShapeDtypeStruct((M, N), BF16)
    return pl.pallas_call(
        body, name=name, grid=(N // tn, M // tm),
        in_specs=[pl.BlockSpec((tm, K), lambda j, i: (i, 0)), w_spec, w_spec],
        out_specs=[o_spec, o_spec, o_spec], out_shape=[sh, sh, sh],
        compiler_params=_cp("parallel", "parallel"),
    )(n, w1, w3)


def _ffn_mid_bwd(dh, w2t, a, b, *, name):
    M, K = dh.shape
    N = w2t.shape[1]
    tm = _tile(M, (384, 256, 128))
    tn = _tile(N, (1408, 512, 256, 128))

    def body(dh_ref, w_ref, a_ref, b_ref, da_ref, db_ref):
        ds = _dot(dh_ref[...].astype(BF16), w_ref[...]) * FFN_RES
        av = a_ref[...].astype(F32)
        bv = b_ref[...].astype(F32)
        sg = _sigmoid(av)
        da_ref[...] = (ds * bv * sg * (1.0 + av * (1.0 - sg))).astype(BF16)
        db_ref[...] = (ds * av * sg).astype(BF16)

    o_spec = pl.BlockSpec((tm, tn), lambda j, i: (i, j))
    sh = jax.ShapeDtypeStruct((M, N), BF16)
    return pl.pallas_call(
        body, name=name, grid=(N // tn, M // tm),
        in_specs=[pl.BlockSpec((tm, K), lambda j, i: (i, 0)), pl.BlockSpec((K, tn), lambda j, i: (0, j)), o_spec, o_spec],
        out_specs=[o_spec, o_spec], out_shape=[sh, sh], compiler_params=_cp("parallel", "parallel"),
    )(dh, w2t, a, b)


def _conv_pre(xx, w_ref, b_ref, tr):
    acc = None
    for k in range(D_CONV):
        sh = D_CONV - 1 - k
        v = (pltpu.roll(xx, sh, 0) if sh else xx)[8:8 + tr]
        t = w_ref[k:k + 1, :] * v
        acc = t if acc is None else acc + t
    return acc + b_ref[...]


def _conv_fwd(zx, w, b, col_off, *, name):
    LP = zx.shape[0]
    C = w.shape[1]
    tr = _tile(LP, (256, 128))
    tc = _tile(C, (512, 256, 128))
    co = col_off // tc

    def body(cur_ref, prev_ref, w_ref, b_ref, o_ref):
        i = pl.program_id(0)
        prev = jnp.where(i == 0, 0.0, prev_ref[...])
        pre = _conv_pre(jnp.concatenate([prev, cur_ref[...]], axis=0), w_ref, b_ref, tr)
        o_ref[...] = pre * _sigmoid(pre)

    return pl.pallas_call(
        body, name=name, grid=(LP // tr, C // tc),
        in_specs=[pl.BlockSpec((tr, tc), lambda i, j: (i, j + co)),
                  pl.BlockSpec((8, tc), lambda i, j: (jnp.maximum(i * (tr // 8) - 1, 0), j + co)),
                  pl.BlockSpec((D_CONV, tc), lambda i, j: (0, j)), pl.BlockSpec((1, tc), lambda i, j: (0, j))],
        out_specs=pl.BlockSpec((tr, tc), lambda i, j: (i, j)),
        out_shape=jax.ShapeDtypeStruct((LP, C), F32), compiler_params=_cp("parallel", "parallel"),
    )(zx, zx, w, b.reshape(1, C))


def _conv_bwd_g(zx, w, b, dact, col_off, *, name):
    LP = zx.shape[0]
    C = w.shape[1]
    tr = _tile(LP, (256, 128))
    tc = _tile(C, (512, 256, 128))
    co = col_off // tc

    def body(cur_ref, prev_ref, w_ref, b_ref, d_ref, g_ref, dw_ref, db_ref):
        i = pl.program_id(1)

        @pl.when(i == 0)
        def _():
            dw_ref[...] = jnp.zeros_like(dw_ref)
            db_ref[...] = jnp.zeros_like(db_ref)

        prev = jnp.where(i == 0, 0.0, prev_ref[...])
        xx = jnp.concatenate([prev, cur_ref[...]], axis=0)
        pre = _conv_pre(xx, w_ref, b_ref, tr)
        sg = _sigmoid(pre)
        g = d_ref[...] * sg * (1.0 + pre * (1.0 - sg))
        g_ref[...] = g
        db_ref[...] += jnp.sum(g, axis=0, keepdims=True)
        rows = []
        for k in range(D_CONV):
            sh = D_CONV - 1 - k
            v = (pltpu.roll(xx, sh, 0) if sh else xx)[8:8 + tr]
            rows.append(jnp.sum(g * v, axis=0, keepdims=True))
        rows.append(jnp.zeros((8 - D_CONV, tc), F32))
        dw_ref[...] += jnp.concatenate(rows, axis=0)

    return pl.pallas_call(
        body, name=name, grid=(C // tc, LP // tr),
        in_specs=[pl.BlockSpec((tr, tc), lambda j, i: (i, j + co)),
                  pl.BlockSpec((8, tc), lambda j, i: (jnp.maximum(i * (tr // 8) - 1, 0), j + co)),
                  pl.BlockSpec((D_CONV, tc), lambda j, i: (0, j)), pl.BlockSpec((1, tc), lambda j, i: (0, j)),
                  pl.BlockSpec((tr, tc), lambda j, i: (i, j))],
        out_specs=[pl.BlockSpec((tr, tc), lambda j, i: (i, j)), pl.BlockSpec((8, tc), lambda j, i: (0, j)),
                   pl.BlockSpec((1, tc), lambda j, i: (0, j))],
        out_shape=[jax.ShapeDtypeStruct((LP, C), F32), jax.ShapeDtypeStruct((8, C), F32), jax.ShapeDtypeStruct((1, C), F32)],
        compiler_params=_cp("parallel", "arbitrary"),
    )(zx, zx, w, b.reshape(1, C), dact)


def _conv_bwd_u(g, w, *, name):
    LP, C = g.shape
    tr = _tile(LP, (256, 128))
    tc = _tile(C, (512, 256, 128))
    nb = LP // tr

    def body(cur_ref, nxt_ref, w_ref, o_ref):
        i = pl.program_id(0)
        nxt = jnp.where(i == nb - 1, 0.0, nxt_ref[...])
        xx = jnp.concatenate([cur_ref[...], nxt], axis=0)
        acc = None
        for k in range(D_CONV):
            sh = D_CONV - 1 - k
            v = (pltpu.roll(xx, tr + 8 - sh, 0) if sh else xx)[:tr]
            t = w_ref[k:k + 1, :] * v
            acc = t if acc is None else acc + t
        o_ref[...] = acc

    return pl.pallas_call(
        body, name=name, grid=(nb, C // tc),
        in_specs=[pl.BlockSpec((tr, tc), lambda i, j: (i, j)),
                  pl.BlockSpec((8, tc), lambda i, j: (jnp.minimum((i + 1) * (tr // 8), LP // 8 - 1), j)),
                  pl.BlockSpec((D_CONV, tc), lambda i, j: (0, j))],
        out_specs=pl.BlockSpec((tr, tc), lambda i, j: (i, j)),
        out_shape=jax.ShapeDtypeStruct((LP, C), F32), compiler_params=_cp("parallel", "parallel"),
    )(g, g, w)


def _ssd_prelude(dtr_ref, dtrt_ref, brow_ref, bcol_ref, alrow_ref, alcol_ref, Q):
    ii = lax.broadcasted_iota(jnp.int32, (Q, Q), 0)
    jj = lax.broadcasted_iota(jnp.int32, (Q, Q), 1)
    tril = ii >= jj
    dt_col = _softplus(dtr_ref[...] + brow_ref[...])
    a_row_p = -jnp.exp(alrow_ref[...])
    dt_row = _softplus(dtrt_ref[...] + bcol_ref[...])
    a_col_p = -jnp.exp(alcol_ref[...])
    cum_col = _dot3_left(tril.astype(BF16), dt_col * a_row_p)
    cum_row = _dot3_right(dt_row * a_col_p, (ii <= jj).astype(BF16))
    return ii, jj, tril, dt_col, dt_row, a_row_p, cum_col, cum_row


def _col_of(mat, lane_idx, h):
    return jnp.sum(jnp.where(lane_idx == h, mat, 0.0), axis=1, keepdims=True)


def _ssd_fwd(xbc, dtr, dtrt, brow, bcol, alrow, alcol, dvec, *, name):
    LP = xbc.shape[0]
    Q = SSD_CHUNK
    nc = LP // Q
    G = SSM_GROUPS
    DI = dvec.shape[1]
    gw = DI // G
    hpg = gw // HEAD_DIM
    H = G * hpg
    boff, coff = DI, DI + G * D_STATE

    def body(xbc_ref, dtr_ref, dtrt_ref, brow_ref, bcol_ref, alrow_ref, alcol_ref, dvec_ref, y_ref, st_ref, state):
        c = pl.program_id(0)

        @pl.when(c == 0)
        def _():
            state[...] = jnp.zeros_like(state)

        st_ref[...] = state[...]
        ii, jj, tril, dt_col, dt_row, _, cum_col, cum_row = _ssd_prelude(
            dtr_ref, dtrt_ref, brow_ref, bcol_ref, alrow_ref, alcol_ref, Q)
        lane_h = lax.broadcasted_iota(jnp.int32, (Q, 128), 1)
        lane_g = lax.broadcasted_iota(jnp.int32, (Q, gw), 1) // HEAD_DIM
        for g in range(G):
            xg = xbc_ref[:, g * gw:(g + 1) * gw]
            bb = xbc_ref[:, boff + g * D_STATE: boff + (g + 1) * D_STATE].astype(BF16)
            cb = xbc_ref[:, coff + g * D_STATE: coff + (g + 1) * D_STATE].astype(BF16)
            gm = _dot_nt(cb, bb)
            sg = state[g]
            yoff = _dot(cb, sg.astype(BF16))
            ydiag = jnp.zeros((Q, gw), F32)
            esc = jnp.zeros((Q, gw), F32)
            wsc = jnp.zeros((Q, gw), F32)
            lam = jnp.zeros((1, gw), F32)
            for j in range(hpg):
                h = g * hpg + j
                ccol = _col_of(cum_col, lane_h, h)
                dcol = _col_of(dt_col, lane_h, h)
                seg = ccol - cum_row[h:h + 1, :]
                decay = jnp.exp(jnp.where(tril, seg, -jnp.inf))
                mh = gm * decay * dt_row[h:h + 1, :]
                hm = lane_g == j
                ydiag = ydiag + _dot(mh.astype(BF16), jnp.where(hm, xg, 0.0).astype(BF16))
                tot = ccol[Q - 1:Q, :]
                esc = jnp.where(hm, jnp.exp(ccol), esc)
                wsc = jnp.where(hm, jnp.exp(tot - ccol) * dcol, wsc)
                lam = jnp.where(hm[0:1], jnp.exp(tot), lam)
            y_ref[:, g * gw:(g + 1) * gw] = ydiag + yoff * esc + dvec_ref[:, g * gw:(g + 1) * gw] * xg
            state[g] = sg * lam + _dot_tn(bb, (xg * wsc).astype(BF16))

    W = xbc.shape[1]
    full = lambda shape: pl.BlockSpec(shape, lambda c: (0,) * len(shape))
    return pl.pallas_call(
        body, name=name, grid=(nc,),
        in_specs=[pl.BlockSpec((Q, W), lambda c: (c, 0)), pl.BlockSpec((Q, 128), lambda c: (c, 0)),
                  pl.BlockSpec((H, Q), lambda c: (0, c)), full((1, 128)), full((H, 1)), full((1, 128)), full((H, 1)),
                  full((1, DI))],
        out_specs=[pl.BlockSpec((Q, DI), lambda c: (c, 0)), pl.BlockSpec((None, G, D_STATE, gw), lambda c: (c, 0, 0, 0))],
        out_shape=[jax.ShapeDtypeStruct((LP, DI), F32), jax.ShapeDtypeStruct((nc, G, D_STATE, gw), F32)],
        scratch_shapes=[pltpu.VMEM((G, D_STATE, gw), F32)],
        compiler_params=_cp("arbitrary"),
    )(xbc, dtr, dtrt, brow, bcol, alrow, alcol, dvec)


def _ssd_bwd(xbc, dtr, dtrt, brow, bcol, alrow, alcol, dvec, dy, states, *, name):
    LP = xbc.shape[0]
    Q = SSD_CHUNK
    nc = LP // Q
    G = SSM_GROUPS
    DI = dvec.shape[1]
    gw = DI // G
    hpg = gw // HEAD_DIM
    H = G * hpg
    boff, coff = DI, DI + G * D_STATE
    W = xbc.shape[1]

    def body(xbc_ref, dtr_ref, dtrt_ref, brow_ref, bcol_ref, alrow_ref, alcol_ref, dvec_ref, dy_ref, st_ref,
             dxbc_ref, ddtr_ref, dbias_ref, dalog_ref, ddvec_ref, dstate):
        c = pl.program_id(0)

        @pl.when(c == 0)
        def _():
            dstate[...] = jnp.zeros_like(dstate)
            dbias_ref[...] = jnp.zeros_like(dbias_ref)
            dalog_ref[...] = jnp.zeros_like(dalog_ref)
            ddvec_ref[...] = jnp.zeros_like(ddvec_ref)

        ii, jj, tril, dt_col, dt_row, a_row_p, cum_col, cum_row = _ssd_prelude(
            dtr_ref, dtrt_ref, brow_ref, bcol_ref, alrow_ref, alcol_ref, Q)
        eye = ii == jj
        lane_h = lax.broadcasted_iota(jnp.int32, (Q, 128), 1)
        row_h = lax.broadcasted_iota(jnp.int32, (Q, 128), 0)
        lane_g = lax.broadcasted_iota(jnp.int32, (Q, gw), 1) // HEAD_DIM
        lane_s = lax.broadcasted_iota(jnp.int32, (D_STATE, gw), 1) // HEAD_DIM
        dcum_mat = jnp.zeros((Q, 128), F32)
        ddt_mat = jnp.zeros((Q, 128), F32)
        dtot_row = jnp.zeros((1, 128), F32)
        for g in range(G):
            xg = xbc_ref[:, g * gw:(g + 1) * gw]
            dyg = dy_ref[:, g * gw:(g + 1) * gw]
            bb = xbc_ref[:, boff + g * D_STATE: boff + (g + 1) * D_STATE].astype(BF16)
            cb = xbc_ref[:, coff + g * D_STATE: coff + (g + 1) * D_STATE].astype(BF16)
            sg = st_ref[g]
            dsg = dstate[g]
            sb = sg.astype(BF16)
            dsb = dsg.astype(BF16)
            xb = xg.astype(BF16)
            gm = _dot_nt(cb, bb)
            cs = _dot(cb, sb)
            bds = _dot(bb, dsb)
            dxg = dvec_ref[:, g * gw:(g + 1) * gw] * dyg
            dgm = jnp.zeros((Q, Q), F32)
            esc = jnp.zeros((Q, gw), F32)
            wsc = jnp.zeros((Q, gw), F32)
            lam = jnp.zeros((1, gw), F32)
            dycs = dyg * cs
            xbds = xg * bds
            dss = dsg * sg
            for j in range(hpg):
                h = g * hpg + j
                ccol = _col_of(cum_col, lane_h, h)
                dcol = _col_of(dt_col, lane_h, h)
                drow = dt_row[h:h + 1, :]
                seg = ccol - cum_row[h:h + 1, :]
                decay = jnp.exp(jnp.where(tril, seg, -jnp.inf))
                hm = lane_g == j
                dyh = jnp.where(hm, dyg, 0.0).astype(BF16)
                gl = gm * decay
                mh = gl * drow
                dmf = _dot_nt(dyh, xb)
                dxg = dxg + _dot_tn(mh.astype(BF16), dyh)
                dgm = dgm + dmf * decay * drow
                n_p = dmf * gl
                n_m = n_p * drow
                rowsum_n = jnp.sum(n_m, axis=1, keepdims=True)
                colsum_n = jnp.sum(jnp.where(eye, jnp.sum(n_m, axis=0, keepdims=True), 0.0), axis=1, keepdims=True)
                colsum_np = jnp.sum(jnp.where(eye, jnp.sum(n_p, axis=0, keepdims=True), 0.0), axis=1, keepdims=True)
                tot = ccol[Q - 1:Q, :]
                e = jnp.exp(ccol)
                wexp = jnp.exp(tot - ccol)
                wcol = wexp * dcol
                lamh = jnp.exp(tot)
                yoff_t = jnp.sum(jnp.where(hm, dycs, 0.0), axis=1, keepdims=True) * e
                e_s = jnp.sum(jnp.where(hm, xbds, 0.0), axis=1, keepdims=True)
                ew = e_s * wcol
                dtot = jnp.sum(ew, axis=0, keepdims=True) + lamh * jnp.sum(
                    jnp.sum(jnp.where(lane_s == j, dss, 0.0), axis=1, keepdims=True), axis=0, keepdims=True)
                dcum_h = rowsum_n + yoff_t - colsum_n - ew
                ddt_h = colsum_np + e_s * wexp
                onehot = lane_h == h
                dcum_mat = jnp.where(onehot, dcum_h, dcum_mat)
                ddt_mat = jnp.where(onehot, ddt_h, ddt_mat)
                dtot_row = jnp.where(onehot[0:1], dtot, dtot_row)
                esc = jnp.where(hm, e, esc)
                wsc = jnp.where(hm, wcol, wsc)
                lam = jnp.where(hm[0:1], lamh, lam)
            dgb = dgm.astype(BF16)
            dye = (dyg * esc).astype(BF16)
            xw = (xg * wsc).astype(BF16)
            dxbc_ref[:, g * gw:(g + 1) * gw] = dxg + bds * wsc
            dxbc_ref[:, boff + g * D_STATE: boff + (g + 1) * D_STATE] = _dot_tn(dgb, cb) + _dot_nt(xw, dsb)
            dxbc_ref[:, coff + g * D_STATE: coff + (g + 1) * D_STATE] = _dot(dgb, bb) + _dot_nt(dye, sb)
            dstate[g] = dsg * lam + _dot_tn(cb, dye)
            ddvec_ref[:, g * gw:(g + 1) * gw] += jnp.sum(dyg * xg, axis=0, keepdims=True)
        dcum_mat = dcum_mat + jnp.where(row_h == Q - 1, dtot_row, 0.0)
        da = _dot3_left((ii <= jj).astype(BF16), dcum_mat)
        ddt = ddt_mat + da * a_row_p
        dalog_ref[...] += jnp.sum(da * dt_col, axis=0, keepdims=True) * a_row_p
        ddtr = ddt * _sigmoid(dtr_ref[...] + brow_ref[...])
        ddtr_ref[...] = ddtr
        dbias_ref[...] += jnp.sum(ddtr, axis=0, keepdims=True)

    full = lambda shape: pl.BlockSpec(shape, lambda c: (0,) * len(shape))
    rc = lambda c: nc - 1 - c
    return pl.pallas_call(
        body, name=name, grid=(nc,),
        in_specs=[pl.BlockSpec((Q, W), lambda c: (rc(c), 0)), pl.BlockSpec((Q, 128), lambda c: (rc(c), 0)),
                  pl.BlockSpec((H, Q), lambda c: (0, rc(c))), full((1, 128)), full((H, 1)), full((1, 128)), full((H, 1)),
                  full((1, DI)), pl.BlockSpec((Q, DI), lambda c: (rc(c), 0)),
                  pl.BlockSpec((None, G, D_STATE, gw), lambda c: (rc(c), 0, 0, 0))],
        out_specs=[pl.BlockSpec((Q, W), lambda c: (rc(c), 0)), pl.BlockSpec((Q, 128), lambda c: (rc(c), 0)),
                   full((1, 128)), full((1, 128)), full((1, DI))],
        out_shape=[jax.ShapeDtypeStruct((LP, W), F32), jax.ShapeDtypeStruct((LP, 128), F32),
                   jax.ShapeDtypeStruct((1, 128), F32), jax.ShapeDtypeStruct((1, 128), F32),
                   jax.ShapeDtypeStruct((1, DI), F32)],
        scratch_shapes=[pltpu.VMEM((G, D_STATE, gw), F32)],
        compiler_params=_cp("arbitrary"),
    )(xbc, dtr, dtrt, brow, bcol, alrow, alcol, dvec, dy, states)


def _gate_fwd(y, zx, g, *, name):
    LP, DI = y.shape
    gw = DI // SSM_GROUPS
    tr = _tile(LP, (256, 128))

    def body(y_ref, z_ref, g_ref, o_ref):
        for k in range(SSM_GROUPS):
            sl = slice(k * gw, (k + 1) * gw)
            z = z_ref[:, sl]
            t = y_ref[:, sl] * (z * _sigmoid(z))
            r = lax.rsqrt(jnp.mean(t * t, axis=1, keepdims=True) + RMS_EPS)
            o_ref[:, sl] = (t * r * g_ref[:, sl]).astype(BF16)

    row = pl.BlockSpec((tr, DI), lambda i: (i, 0))
    return pl.pallas_call(
        body, name=name, grid=(LP // tr,), in_specs=[row, row, pl.BlockSpec((1, DI), lambda i: (0, 0))],
        out_specs=row, out_shape=jax.ShapeDtypeStruct((LP, DI), BF16), compiler_params=_cp("parallel"),
    )(y, zx, g)


def _gate_bwd(y, zx, g, dyn, *, name):
    LP, DI = y.shape
    gw = DI // SSM_GROUPS
    tr = _tile(LP, (256, 128))

    def body(y_ref, z_ref, g_ref, d_ref, dy_ref, dz_ref, dg_ref):
        i = pl.program_id(0)

        @pl.when(i == 0)
        def _():
            dg_ref[...] = jnp.zeros_like(dg_ref)

        for k in range(SSM_GROUPS):
            sl = slice(k * gw, (k + 1) * gw)
            z = z_ref[:, sl]
            yv = y_ref[:, sl]
            sg = _sigmoid(z)
            sz = z * sg
            t = yv * sz
            r = lax.rsqrt(jnp.mean(t * t, axis=1, keepdims=True) + RMS_EPS)
            th = t * r
            d = d_ref[:, sl]
            dtn = d * g_ref[:, sl]
            dt_ = r * (dtn - th * jnp.mean(dtn * th, axis=1, keepdims=True))
            dg_ref[:, sl] += jnp.sum(d * th, axis=0, keepdims=True)
            dy_ref[:, sl] = dt_ * sz
            dz_ref[:, sl] = dt_ * yv * sg * (1.0 + z * (1.0 - sg))

    row = pl.BlockSpec((tr, DI), lambda i: (i, 0))
    vec = pl.BlockSpec((1, DI), lambda i: (0, 0))
    return pl.pallas_call(
        body, name=name, grid=(LP // tr,), in_specs=[row, row, vec, row], out_specs=[row, row, vec],
        out_shape=[jax.ShapeDtypeStruct((LP, DI), F32), jax.ShapeDtypeStruct((LP, DI), F32),
                   jax.ShapeDtypeStruct((1, DI), F32)],
        compiler_params=_cp("arbitrary"),
    )(y, zx, g, dyn)


def _dot2_right(x, t_bf16):
    hi = x.astype(BF16)
    lo = (x - hi.astype(F32)).astype(BF16)
    return _dot(hi, t_bf16) + _dot(lo, t_bf16)


def _sb_tile(q, k_blk, lower, valid=None):
    z = _dot_nt(q, k_blk)
    sp = _softplus(z)
    lk = -sp if valid is None else jnp.where(valid, -sp, 0.0)
    return z, sp, lk, z + _dot2_right(lk, lower)


def _sb_weights(zr, c, valid=None):
    w = jnp.exp(zr + c)
    return w if valid is None else jnp.where(valid, w, 0.0)


def _sb_fwd(q, k, v, *, name):
    H, LP, dh = q.shape
    T = ATT_BLOCK
    nq = LP // T
    assert nq <= LANES

    def body(q_ref, k_ref, v_ref, o_ref, c_ref):
        i = pl.program_id(1)
        ii = lax.broadcasted_iota(jnp.int32, (T, T), 0)
        jj = lax.broadcasted_iota(jnp.int32, (T, T), 1)
        lane = lax.broadcasted_iota(jnp.int32, (T, LANES), 1)
        lower = (ii >= jj).astype(BF16)
        qv = q_ref[...]

        def kv(kb):
            ks = pl.multiple_of(kb * T, T)
            return k_ref[pl.ds(ks, T), :], v_ref[pl.ds(ks, T), :]

        kd, vd = kv(i)
        diag = jj < ii
        _, _, lk, zr = _sb_tile(qv, kd, lower, diag)
        acc = _dot(_sb_weights(zr, 0.0, diag).astype(BF16), vd)
        c = jnp.sum(lk, axis=1, keepdims=True)
        c_ref[...] = jnp.zeros_like(c_ref)

        def pair(p, carry):
            c, acc = carry
            kb1 = i - 1 - 2 * p
            k1, v1 = kv(kb1)
            k2, v2 = kv(kb1 - 1)
            _, _, lk1, zr1 = _sb_tile(qv, k1, lower)
            _, _, lk2, zr2 = _sb_tile(qv, k2, lower)
            c1 = c + jnp.sum(lk1, axis=1, keepdims=True)
            acc = acc + _dot(_sb_weights(zr1, c).astype(BF16), v1) + _dot(_sb_weights(zr2, c1).astype(BF16), v2)
            c_ref[...] = jnp.where(lane == kb1, c, jnp.where(lane == kb1 - 1, c1, c_ref[...]))
            return c1 + jnp.sum(lk2, axis=1, keepdims=True), acc

        def single(_, carry):
            c, acc = carry
            k1, v1 = kv(0)
            _, _, lk1, zr1 = _sb_tile(qv, k1, lower)
            acc = acc + _dot(_sb_weights(zr1, c).astype(BF16), v1)
            c_ref[...] = jnp.where(lane == 0, c, c_ref[...])
            return c + jnp.sum(lk1, axis=1, keepdims=True), acc

        carry = lax.fori_loop(0, i // 2, pair, (c, acc))
        _, acc = lax.fori_loop(0, i % 2, single, carry)
        o_ref[...] = acc

    blk = pl.BlockSpec((None, T, dh), lambda h, i: (h, i, 0))
    cblk = pl.BlockSpec((None, T, LANES), lambda h, i: (h, i, 0))
    whole = pl.BlockSpec((None, LP, dh), lambda h, i: (h, 0, 0))
    return pl.pallas_call(
        body, name=name, grid=(H, nq), in_specs=[blk, whole, whole], out_specs=[blk, cblk],
        out_shape=[jax.ShapeDtypeStruct((H, LP, dh), F32), jax.ShapeDtypeStruct((H, LP, LANES), F32)],
        compiler_params=_cp("parallel", "parallel"),
    )(q, k, v)


def _sb_bwd(q, k, v, cmat, do, *, name):
    H, LP, dh = q.shape
    T = ATT_BLOCK
    nq = LP // T

    def body(q_ref, k_ref, v_ref, c_ref, do_ref, dq_ref, dk_ref, dv_ref):
        i = pl.program_id(1)

        @pl.when(i == 0)
        def _():
            dk_ref[...] = jnp.zeros_like(dk_ref)
            dv_ref[...] = jnp.zeros_like(dv_ref)

        ii = lax.broadcasted_iota(jnp.int32, (T, T), 0)
        jj = lax.broadcasted_iota(jnp.int32, (T, T), 1)
        lane = lax.broadcasted_iota(jnp.int32, (T, LANES), 1)
        lower = (ii >= jj).astype(BF16)
        upper = (ii <= jj).astype(BF16)
        qv = q_ref[...]
        dob = do_ref[...].astype(BF16)
        cm = c_ref[...]

        def tile(kb, valid=None):
            ks = pl.multiple_of(kb * T, T)
            k_blk = k_ref[pl.ds(ks, T), :]
            c = jnp.sum(jnp.where(lane == kb, cm, 0.0), axis=1, keepdims=True)
            z, sp, _, zr = _sb_tile(qv, k_blk, lower, valid)
            w = _sb_weights(zr, c, valid)
            gw_ = w * _dot_nt(dob, v_ref[pl.ds(ks, T), :])
            gin = _dot2_right(gw_, upper)
            return ks, k_blk, w, gw_, gin, jnp.exp(z - sp)

        def finish(t, cg, dq, valid=None):
            ks, k_blk, w, gw_, gin, sig = t
            dz = gw_ - sig * (cg + gin)
            if valid is not None:
                dz = jnp.where(valid, dz, 0.0)
            dz = dz.astype(BF16)
            dk_ref[pl.ds(ks, T), :] += _dot_tn(dz, qv)
            dv_ref[pl.ds(ks, T), :] += _dot_tn(w.astype(BF16), dob)
            return cg + jnp.sum(gw_, axis=1, keepdims=True), dq + _dot(dz, k_blk)

        def pair(p, carry):
            cg, dq = carry
            t1 = tile(2 * p)
            t2 = tile(2 * p + 1)
            cg, dq = finish(t1, cg, dq)
            return finish(t2, cg, dq)

        def single(_, carry):
            cg, dq = carry
            return finish(tile(i - 1), cg, dq)

        carry = lax.fori_loop(0, i // 2, pair, (jnp.zeros((T, 1), F32), jnp.zeros((T, dh), F32)))
        cg, dq = lax.fori_loop(0, i % 2, single, carry)
        diag = jj < ii
        _, dq = finish(tile(i, diag), cg, dq, diag)
        dq_ref[...] = dq

    blk = pl.BlockSpec((None, T, dh), lambda h, i: (h, i, 0))
    cblk = pl.BlockSpec((None, T, LANES), lambda h, i: (h, i, 0))
    whole = pl.BlockSpec((None, LP, dh), lambda h, i: (h, 0, 0))
    sh = jax.ShapeDtypeStruct((H, LP, dh), F32)
    return pl.pallas_call(
        body, name=name, grid=(H, nq), in_specs=[blk, whole, whole, cblk, blk], out_specs=[blk, whole, whole],
        out_shape=[sh, sh, sh], compiler_params=_cp("parallel", "arbitrary"),
    )(q, k, v, cmat, do)


def _loss_head(h, tgt, seq, *, name):
    LP, D = h.shape
    tr = _tile(LP, (384, 256, 128))

    def body(h_ref, t_ref, dh_ref, l_ref):
        i = pl.program_id(0)

        @pl.when(i == 0)
        def _():
            l_ref[...] = jnp.zeros_like(l_ref)

        row = lax.broadcasted_iota(jnp.int32, (tr, D), 0) + i * tr
        e = jnp.where((row >= N_META) & (row < N_META + seq), h_ref[...] - t_ref[...], 0.0)
        dh_ref[...] = e * (1.0 / D)
        l_ref[...] += jnp.sum(e * e, axis=0, keepdims=True) * (0.5 / D)

    row = pl.BlockSpec((tr, D), lambda i: (i, 0))
    return pl.pallas_call(
        body, name=name, grid=(LP // tr,), in_specs=[row, row], out_specs=[row, pl.BlockSpec((1, D), lambda i: (0, 0))],
        out_shape=[jax.ShapeDtypeStruct((LP, D), F32), jax.ShapeDtypeStruct((1, D), F32)],
        compiler_params=_cp("arbitrary"),
    )(h, tgt)


def _adamw(w, g, m, v, *, name):
    shape = w.shape
    C = shape[-1]
    R = math.prod(shape) // C
    tr = _tile(R, (512, 256, 128, 64, 32, 16, 8))
    c1 = 1.0 / (1.0 - ADAM_B1 ** ADAM_STEP)
    c2 = 1.0 / (1.0 - ADAM_B2 ** ADAM_STEP)

    def body(w_ref, g_ref, m_ref, v_ref, d_ref, nm_ref, nv_ref):
        gv = g_ref[...]
        nm = ADAM_B1 * m_ref[...] + (1.0 - ADAM_B1) * gv
        nv = ADAM_B2 * v_ref[...] + (1.0 - ADAM_B2) * (gv * gv)
        d_ref[...] = -ADAM_LR * ((nm * c1) / (jnp.sqrt(nv * c2) + ADAM_EPS) + ADAM_WD * w_ref[...])
        nm_ref[...] = nm
        nv_ref[...] = nv

    blk = pl.BlockSpec((tr, C), lambda i: (i, 0))
    sh = jax.ShapeDtypeStruct((R, C), F32)
    d, nm, nv = pl.pallas_call(
        body, name=name, grid=(R // tr,), in_specs=[blk] * 4, out_specs=[blk] * 3, out_shape=[sh] * 3,
        compiler_params=_cp("parallel"),
    )(w.reshape(R, C), g.reshape(R, C), m.reshape(R, C), v.reshape(R, C))
    return d.reshape(shape), nm.reshape(shape), nv.reshape(shape)


def _sum_rows(buf, *, name):
    n, R, C = buf.shape
    tr = _tile(R, (512, 256, 128, 64, 32, 16, 8))

    def body(b_ref, o_ref):
        acc = b_ref[0].astype(F32)
        for k in range(1, n):
            acc = acc + b_ref[k].astype(F32)
        o_ref[...] = acc

    return pl.pallas_call(
        body, name=name, grid=(R // tr,), in_specs=[pl.BlockSpec((n, tr, C), lambda i: (0, i, 0))],
        out_specs=pl.BlockSpec((tr, C), lambda i: (i, 0)), out_shape=jax.ShapeDtypeStruct((R, C), F32),
        compiler_params=_cp("parallel"),
    )(buf)


def _interleave(buf, *, name):
    n, R, C = buf.shape
    tr = _tile(R, (256, 128, 64, 32, 16))

    def body(b_ref, o_ref):
        for d in range(n):
            o_ref[:, d * C:(d + 1) * C] = b_ref[d]

    return pl.pallas_call(
        body, name=name, grid=(R // tr,), in_specs=[pl.BlockSpec((n, tr, C), lambda i: (0, i, 0))],
        out_specs=pl.BlockSpec((tr, n * C), lambda i: (i, 0)), out_shape=jax.ShapeDtypeStruct((R, n * C), buf.dtype),
        compiler_params=_cp("parallel"),
    )(buf)


def _deinterleave(x, *, out_dtype, name):
    R, NC = x.shape
    C = NC // N_DEV
    tr = _tile(R, (256, 128, 64, 32, 16))

    def body(x_ref, o_ref):
        for d in range(N_DEV):
            o_ref[d] = x_ref[:, d * C:(d + 1) * C].astype(out_dtype)

    return pl.pallas_call(
        body, name=name, grid=(R // tr,), in_specs=[pl.BlockSpec((tr, NC), lambda i: (i, 0))],
        out_specs=pl.BlockSpec((N_DEV, tr, C), lambda i: (0, i, 0)),
        out_shape=jax.ShapeDtypeStruct((N_DEV, R, C), out_dtype), compiler_params=_cp("parallel"),
    )(x)


def _row_gather(buf, off, n, *, name):
    nd, R, C = buf.shape
    assert off % n == 0

    def body(b_ref, o_ref):
        o_ref[...] = b_ref[...]

    return pl.pallas_call(
        body, name=name, grid=(nd,), in_specs=[pl.BlockSpec((None, n, C), lambda d: (d, off // n, 0))],
        out_specs=pl.BlockSpec((n, C), lambda d: (d, 0)), out_shape=jax.ShapeDtypeStruct((nd * n, C), buf.dtype),
        compiler_params=_cp("parallel"),
    )(buf)


def _mesh_pos():
    x, y, c = lax.axis_index("x"), lax.axis_index("y"), lax.axis_index("c")
    return x, y, c, 4 * x + 2 * y + c


def _peer(x, y, c, f):
    px, py, pc = (x + ((f >> 2) & 1)) % 2, (y + ((f >> 1) & 1)) % 2, (c + (f & 1)) % 2
    return (px, py, pc), 4 * px + 2 * py + pc


def _exchange(src, *, scatter, name):
    shape = src.shape[1:] if scatter else src.shape

    def body(src_ref, dst_ref, send_sems, recv_sems, local_sem):
        x, y, c, me = _mesh_pos()
        own = pltpu.make_async_copy(src_ref.at[me] if scatter else src_ref, dst_ref.at[me], local_sem)
        own.start()
        sends, recvs = [], []
        for f in range(1, N_DEV):
            peer, pid = _peer(x, y, c, f)
            sends.append(pltpu.make_async_remote_copy(
                src_ref=src_ref.at[pid] if scatter else src_ref, dst_ref=dst_ref.at[me],
                send_sem=send_sems.at[f - 1], recv_sem=recv_sems.at[f - 1], device_id=peer, device_id_type=MESH))
            recvs.append(pltpu.make_async_remote_copy(
                src_ref=src_ref.at[pid] if scatter else src_ref, dst_ref=dst_ref.at[pid],
                send_sem=send_sems.at[f - 1], recv_sem=recv_sems.at[f - 1], device_id=peer, device_id_type=MESH))
        for cp in sends:
            cp.start()
        for snd, rcv in zip(sends, recvs):
            snd.wait_send()
            rcv.wait_recv()
        own.wait()

    return pl.pallas_call(
        body, name=name, in_specs=[pl.BlockSpec(memory_space=pl.ANY)], out_specs=pl.BlockSpec(memory_space=pl.ANY),
        out_shape=jax.ShapeDtypeStruct((N_DEV,) + tuple(shape), src.dtype),
        scratch_shapes=[pltpu.SemaphoreType.DMA((N_DEV - 1,)), pltpu.SemaphoreType.DMA((N_DEV - 1,)),
                        pltpu.SemaphoreType.DMA],
        compiler_params=pltpu.CompilerParams(has_side_effects=True),
    )(src)


def _heads(t, H):
    LP = t.shape[0]
    return t.reshape(LP, H, HEAD_DIM).transpose(1, 0, 2)


def _unheads(t):
    H, LP, dh = t.shape
    return t.transpose(1, 0, 2).reshape(LP, H * dh)


def _ffn_fwd(h, g, w1, w3, w2, tag):
    n = _rms_fwd(h, g, name=f"ffn_norm_{tag}")
    a, b, s = _ffn_up(n, w1, w3, name=f"ffn_up_{tag}")
    h2 = _matmul([(s, w2)], res=h, alpha=FFN_RES, name=f"ffn_down_{tag}")
    return h2, (h, n, a, b, s)


def _ffn_bwd(dh, saved, g, w1, w3, w2, tag):
    h, n, a, b, s = saved
    da, db = _ffn_mid_bwd(dh, w2.T, a, b, name=f"ffn_mid_bwd_{tag}")
    dw2 = _matmul([(s, dh)], trans_a=True, alpha=FFN_RES, out_dtype=BF16, name=f"ffn_dw2_{tag}")
    dn = _matmul([(da, w1.T), (db, w3.T)], name=f"ffn_dn_{tag}")
    dw1 = _matmul([(n, da)], trans_a=True, shards=N_DEV, out_dtype=BF16, name=f"ffn_dw1_{tag}")
    dw3 = _matmul([(n, db)], trans_a=True, shards=N_DEV, out_dtype=BF16, name=f"ffn_dw3_{tag}")
    dh_in, dg = _rms_bwd(h, g, dn, res=dh, name=f"ffn_norm_bwd_{tag}")
    return dh_in, dg, dw1, dw3, dw2


def _local_step(x, tgt, W):
    seq, D = x.shape
    L = N_META + seq
    LP = -(-L // ROW_ALIGN) * ROW_ALIGN
    pad = LP - L
    H_sb = D // HEAD_DIM
    DI = W["ssm_norm_g"].shape[-1]
    H_ssm = DI // HEAD_DIM
    CONV = DI + 2 * SSM_GROUPS * D_STATE
    ZX = DI + CONV

    h0 = jnp.concatenate([W["meta_tokens"], x, jnp.zeros((pad, D), F32)], axis=0)
    tgt_p = jnp.pad(tgt, ((N_META, pad), (0, 0)))
    ng = W["norm_g"]

    h1, sv_f00 = _ffn_fwd(h0, ng[0, 0], W["ffn_w1"][0, 0], W["ffn_w3"][0, 0], W["ffn_w2"][0, 0], "00")
    u0 = _rms_fwd(h1, ng[0, 1], name="ssm_norm")
    w_in = W["ssm_in_proj"][0]
    w_zx = w_in[:, :ZX]
    w_dt = jnp.pad(w_in[:, ZX:], ((0, 0), (0, 128 - H_ssm)))
    zx = _matmul([(u0, w_zx)], name="ssm_in_zx")
    dtr = _matmul([(u0, w_dt)], name="ssm_in_dt")
    conv_w, conv_b = W["ssm_conv_w"][0], W["ssm_conv_b"][0]
    xbc = _conv_fwd(zx, conv_w, conv_b, DI, name="ssm_conv")
    dtrt = dtr[:, :H_ssm].T
    padh = lambda t: jnp.pad(t.reshape(1, H_ssm), ((0, 0), (0, 128 - H_ssm)))
    brow, bcol = padh(W["ssm_dt_bias"][0]), W["ssm_dt_bias"][0].reshape(H_ssm, 1)
    alrow, alcol = padh(W["ssm_a_log"][0]), W["ssm_a_log"][0].reshape(H_ssm, 1)
    dvec = jnp.repeat(W["ssm_d"][0], HEAD_DIM).reshape(1, DI)
    ssm_args = (xbc, dtr, dtrt, brow, bcol, alrow, alcol, dvec)
    y, states = _ssd_fwd(*ssm_args, name="ssd_fwd")
    sng = W["ssm_norm_g"].reshape(1, DI)
    yn = _gate_fwd(y, zx, sng, name="ssm_gate")
    w_out = W["ssm_out_proj"][0]
    h2 = _matmul([(yn, w_out)], res=h1, name="ssm_out")
    h3, sv_f01 = _ffn_fwd(h2, ng[0, 2], W["ffn_w1"][0, 1], W["ffn_w3"][0, 1], W["ffn_w2"][0, 1], "01")

    kv_in = _rms_fwd(h3, W["kv_norm_g"], name="kv_norm")
    kraw = _heads(_matmul([(kv_in, W["w_k"])], name="kv_k"), H_sb)
    vh = _heads(_matmul([(kv_in, W["w_v"])], out_dtype=BF16, name="kv_v"), H_sb)
    kh = _rms_fwd(kraw.reshape(H_sb * LP, HEAD_DIM), W["k_norm_g"], name="k_headnorm").reshape(H_sb, LP, HEAD_DIM)

    h4, sv_f10 = _ffn_fwd(h3, ng[1, 0], W["ffn_w1"][1, 0], W["ffn_w3"][1, 0], W["ffn_w2"][1, 0], "10")
    u1 = _rms_fwd(h4, ng[1, 1], name="sb_norm")
    qraw = _heads(_matmul([(u1, W["sb_w_q"][0])], name="sb_q"), H_sb)
    scale = HEAD_DIM ** -0.5
    qh = _rms_fwd(qraw.reshape(H_sb * LP, HEAD_DIM), W["sb_q_norm_g"][0], scale=scale,
                  name="q_headnorm").reshape(H_sb, LP, HEAD_DIM)
    o, cmat = _sb_fwd(qh, kh, vh, name="sb_fwd")
    o_flat = _unheads(o)
    h5 = _matmul([(o_flat, W["sb_w_o"][0])], res=h4, name="sb_out")
    h6, sv_f11 = _ffn_fwd(h5, ng[1, 2], W["ffn_w1"][1, 1], W["ffn_w3"][1, 1], W["ffn_w2"][1, 1], "11")

    dh, lvec = _loss_head(h6, tgt_p, seq, name="loss_head")
    loss = jnp.sum(lvec)

    G = {}
    dng = [[None] * 3 for _ in range(2)]
    dw1 = [[None] * 2 for _ in range(2)]
    dw3 = [[None] * 2 for _ in range(2)]
    dw2 = [[None] * 2 for _ in range(2)]

    dh, dng[1][2], dw1[1][1], dw3[1][1], dw2[1][1] = _ffn_bwd(
        dh, sv_f11, ng[1, 2], W["ffn_w1"][1, 1], W["ffn_w3"][1, 1], W["ffn_w2"][1, 1], "11")
    shard_rows = lambda t: t.reshape(N_DEV, t.shape[0] // N_DEV, t.shape[1])
    g_wo = shard_rows(_matmul([(o_flat, dh)], trans_a=True, out_dtype=BF16, name="sb_dwo"))
    do = _heads(_matmul([(dh, W["sb_w_o"][0].T)], name="sb_do"), H_sb)
    dq, dk, dv = _sb_bwd(qh, kh, vh, cmat, do, name="sb_bwd")
    dqraw, dqg = _rms_bwd(qraw.reshape(H_sb * LP, HEAD_DIM), W["sb_q_norm_g"][0], dq.reshape(H_sb * LP, HEAD_DIM),
                          alpha=scale, name="q_headnorm_bwd")
    G["sb_q_norm_g"] = dqg
    dqraw = _unheads(dqraw.reshape(H_sb, LP, HEAD_DIM))
    g_wq = shard_rows(_matmul([(u1, dqraw)], trans_a=True, out_dtype=BF16, name="sb_dwq"))
    du1 = _matmul([(dqraw, W["sb_w_q"][0].T)], name="sb_du")
    dh, dng[1][1] = _rms_bwd(h4, ng[1, 1], du1, res=dh, name="sb_norm_bwd")
    dh, dng[1][0], dw1[1][0], dw3[1][0], dw2[1][0] = _ffn_bwd(
        dh, sv_f10, ng[1, 0], W["ffn_w1"][1, 0], W["ffn_w3"][1, 0], W["ffn_w2"][1, 0], "10")

    dkraw, dkg = _rms_bwd(kraw.reshape(H_sb * LP, HEAD_DIM), W["k_norm_g"], dk.reshape(H_sb * LP, HEAD_DIM),
                          name="k_headnorm_bwd")
    G["k_norm_g"] = dkg.reshape(-1)
    dkraw = _unheads(dkraw.reshape(H_sb, LP, HEAD_DIM))
    dvf = _unheads(dv)
    g_wk = shard_rows(_matmul([(kv_in, dkraw)], trans_a=True, out_dtype=BF16, name="kv_dwk"))
    g_wv = shard_rows(_matmul([(kv_in, dvf)], trans_a=True, out_dtype=BF16, name="kv_dwv"))
    dkv = _matmul([(dkraw, W["w_k"].T), (dvf, W["w_v"].T)], name="kv_din")
    dh, dkvg = _rms_bwd(h3, W["kv_norm_g"], dkv, res=dh, name="kv_norm_bwd")
    G["kv_norm_g"] = dkvg.reshape(-1)

    dh, dng[0][2], dw1[0][1], dw3[0][1], dw2[0][1] = _ffn_bwd(
        dh, sv_f01, ng[0, 2], W["ffn_w1"][0, 1], W["ffn_w3"][0, 1], W["ffn_w2"][0, 1], "01")
    g_wout = shard_rows(_matmul([(yn, dh)], trans_a=True, out_dtype=BF16, name="ssm_dwout"))
    dyn = _matmul([(dh, w_out.T)], name="ssm_dyn")
    dy, dz, dsng = _gate_bwd(y, zx, sng, dyn, name="ssm_gate_bwd")
    G["ssm_norm_g"] = dsng
    dxbc, ddtr, dbias, dalog, ddvec = _ssd_bwd(*ssm_args, dy, states, name="ssd_bwd")
    G["ssm_dt_bias"] = dbias[:, :H_ssm]
    G["ssm_a_log"] = dalog[:, :H_ssm]
    G["ssm_d"] = jnp.sum(ddvec.reshape(H_ssm, HEAD_DIM), axis=1).reshape(1, H_ssm)
    gpre, dcw, dcb = _conv_bwd_g(zx, conv_w, conv_b, dxbc, DI, name="ssm_conv_bwd_g")
    G["ssm_conv_w"] = dcw[:D_CONV][None]
    G["ssm_conv_b"] = dcb
    dxbc_pre = _conv_bwd_u(gpre, conv_w, name="ssm_conv_bwd_u")
    g_win = _deinterleave(jnp.concatenate([
        _matmul([(u0, dz)], trans_a=True, out_dtype=BF16, name="ssm_dwin_z"),
        _matmul([(u0, dxbc_pre)], trans_a=True, out_dtype=BF16, name="ssm_dwin_x"),
        _matmul([(u0, ddtr)], trans_a=True, out_dtype=BF16, name="ssm_dwin_dt")[:, :H_ssm]], axis=1),
        out_dtype=BF16, name="ssm_dwin_shards")
    du0 = _matmul([(dz, w_zx[:, :DI].T)], name="ssm_du_z")
    du0 = _matmul([(dxbc_pre, w_zx[:, DI:].T)], res=du0, name="ssm_du_x")
    du0 = _matmul([(ddtr, w_dt.T)], res=du0, name="ssm_du_dt")
    dh, dng[0][1] = _rms_bwd(h1, ng[0, 1], du0, res=dh, name="ssm_norm_bwd")
    dh, dng[0][0], dw1[0][0], dw3[0][0], dw2[0][0] = _ffn_bwd(
        dh, sv_f00, ng[0, 0], W["ffn_w1"][0, 0], W["ffn_w3"][0, 0], W["ffn_w2"][0, 0], "00")

    G["norm_g"] = jnp.stack([jnp.concatenate(r, axis=0) for r in dng])
    G["meta_tokens"] = dh[:N_META]
    flat = lambda t: [t[l][s] for l in range(2) for s in range(2)]
    groups = dict(
        cols=jnp.concatenate(flat(dw1) + flat(dw3), axis=1),
        rows=jnp.concatenate([g_wout, g_wk, g_wv, g_wq, g_wo], axis=1),
        w2=jnp.concatenate([shard_rows(t) for t in flat(dw2)], axis=1),
        win=g_win)
    return loss, dh[N_META:L], G, groups


WEIGHTS = ['meta_tokens', 'norm_g', 'ffn_w1', 'ffn_w3', 'ffn_w2', 'ssm_in_proj', 'ssm_conv_w', 'ssm_conv_b',
           'ssm_dt_bias', 'ssm_a_log', 'ssm_d', 'ssm_norm_g', 'ssm_out_proj', 'kv_norm_g', 'w_k', 'k_norm_g', 'w_v',
           'sb_w_q', 'sb_q_norm_g', 'sb_w_o']
SHARD_AXIS = {'meta_tokens': 1, 'norm_g': 2, 'ffn_w1': 3, 'ffn_w3': 3, 'ffn_w2': 2, 'ssm_in_proj': 2, 'ssm_conv_w': 2,
              'ssm_conv_b': 1, 'ssm_dt_bias': None, 'ssm_a_log': None, 'ssm_d': None, 'ssm_norm_g': 1,
              'ssm_out_proj': 1, 'kv_norm_g': None, 'w_k': 0, 'k_norm_g': None, 'w_v': 0, 'sb_w_q': 1,
              'sb_q_norm_g': None, 'sb_w_o': 1}
MATMUL_WEIGHTS = ('ffn_w1', 'ffn_w3', 'ffn_w2', 'ssm_in_proj', 'ssm_out_proj', 'w_k', 'w_v', 'sb_w_q', 'sb_w_o')
PACK_ROWS = 16


def _pack(arrs, dtype):
    flat = jnp.concatenate([a.reshape(-1).astype(dtype) for a in arrs])
    n = flat.shape[0]
    npad = -(-n // (LANES * PACK_ROWS)) * (LANES * PACK_ROWS)
    return jnp.pad(flat, (0, npad - n)).reshape(npad // LANES, LANES)


def _unpack_gathered(buf, names, shard_shapes, dtype):
    flat = buf.reshape(N_DEV, -1)
    out, off = {}, 0
    for n in names:
        shp = shard_shapes[n]
        size = math.prod(shp)
        t = flat[:, off:off + size].reshape((N_DEV,) + tuple(shp))
        off += size
        ax = SHARD_AXIS[n]
        t = jnp.moveaxis(t, 0, ax)
        full = shp[:ax] + (N_DEV * shp[ax],) + shp[ax + 1:]
        out[n] = t.reshape(full).astype(dtype)
    return out


def _weight_groups(w1, w3, w2, win, wout, wk, wv, wq, wo):
    return dict(
        cols=jnp.concatenate([w1.reshape(-1, w1.shape[-1]), w3.reshape(-1, w3.shape[-1])], axis=0),
        rows=jnp.concatenate([wout[0], wk, wv, wq[0], wo[0]], axis=0),
        w2=w2.reshape(-1, w2.shape[-1]),
        win=win[0])


def _to_shards(g, ax):
    shp = g.shape
    t = g.reshape(shp[:ax] + (N_DEV, shp[ax] // N_DEV) + shp[ax + 1:])
    return jnp.moveaxis(t, ax, 0).reshape(N_DEV, -1)


def kernel(x, meta_tokens, norm_g, ffn_w1, ffn_w3, ffn_w2, ssm_in_proj, ssm_conv_w, ssm_conv_b, ssm_dt_bias, ssm_a_log, ssm_d, ssm_norm_g, ssm_out_proj, kv_norm_g, w_k, k_norm_g, w_v, sb_w_q, sb_q_norm_g, sb_w_o, loss_target, m_meta_tokens, m_norm_g, m_ffn_w1, m_ffn_w3, m_ffn_w2, m_ssm_in_proj, m_ssm_conv_w, m_ssm_conv_b, m_ssm_dt_bias, m_ssm_a_log, m_ssm_d, m_ssm_norm_g, m_ssm_out_proj, m_kv_norm_g, m_w_k, m_k_norm_g, m_w_v, m_sb_w_q, m_sb_q_norm_g, m_sb_w_o, v_meta_tokens, v_norm_g, v_ffn_w1, v_ffn_w3, v_ffn_w2, v_ssm_in_proj, v_ssm_conv_w, v_ssm_conv_b, v_ssm_dt_bias, v_ssm_a_log, v_ssm_d, v_ssm_norm_g, v_ssm_out_proj, v_kv_norm_g, v_w_k, v_k_norm_g, v_w_v, v_sb_w_q, v_sb_q_norm_g, v_sb_w_o):
    shard = dict(meta_tokens=meta_tokens, norm_g=norm_g, ffn_w1=ffn_w1, ffn_w3=ffn_w3, ffn_w2=ffn_w2,
                 ssm_in_proj=ssm_in_proj, ssm_conv_w=ssm_conv_w, ssm_conv_b=ssm_conv_b, ssm_dt_bias=ssm_dt_bias,
                 ssm_a_log=ssm_a_log, ssm_d=ssm_d, ssm_norm_g=ssm_norm_g, ssm_out_proj=ssm_out_proj,
                 kv_norm_g=kv_norm_g, w_k=w_k, k_norm_g=k_norm_g, w_v=w_v, sb_w_q=sb_w_q, sb_q_norm_g=sb_q_norm_g,
                 sb_w_o=sb_w_o)
    mom_m = dict(zip(WEIGHTS, (m_meta_tokens, m_norm_g, m_ffn_w1, m_ffn_w3, m_ffn_w2, m_ssm_in_proj, m_ssm_conv_w,
                               m_ssm_conv_b, m_ssm_dt_bias, m_ssm_a_log, m_ssm_d, m_ssm_norm_g, m_ssm_out_proj,
                               m_kv_norm_g, m_w_k, m_k_norm_g, m_w_v, m_sb_w_q, m_sb_q_norm_g, m_sb_w_o)))
    mom_v = dict(zip(WEIGHTS, (v_meta_tokens, v_norm_g, v_ffn_w1, v_ffn_w3, v_ffn_w2, v_ssm_in_proj, v_ssm_conv_w,
                               v_ssm_conv_b, v_ssm_dt_bias, v_ssm_a_log, v_ssm_d, v_ssm_norm_g, v_ssm_out_proj,
                               v_kv_norm_g, v_w_k, v_k_norm_g, v_w_v, v_sb_w_q, v_sb_q_norm_g, v_sb_w_o)))
    sharded = [n for n in WEIGHTS if SHARD_AXIS[n] is not None]
    replicated = [n for n in WEIGHTS if SHARD_AXIS[n] is None]
    small = [n for n in sharded if n not in MATMUL_WEIGHTS]
    shapes = {n: tuple(shard[n].shape) for n in WEIGHTS}
    D = x.shape[-1]
    F8 = ffn_w1.shape[-1]
    DI8 = ssm_out_proj.shape[1]
    D8 = w_k.shape[0]

    bf = lambda t: t.astype(BF16)
    src = _weight_groups(bf(ffn_w1), bf(ffn_w3), bf(ffn_w2), bf(ssm_in_proj), bf(ssm_out_proj), bf(w_k), bf(w_v),
                         bf(sb_w_q), bf(sb_w_o))
    got = {k: _exchange(v, scatter=False, name=f"gather_{k}") for k, v in src.items()}
    gathered_small = _exchange(_pack([shard[n] for n in small], F32), scatter=False, name="gather_gains")
    W = dict(_unpack_gathered(gathered_small, small, shapes, F32))
    for n in replicated:
        W[n] = shard[n]
    w13 = _interleave(got["cols"], name="weights_cols")
    W["ffn_w1"] = w13[:4 * D].reshape(2, 2, D, N_DEV * F8)
    W["ffn_w3"] = w13[4 * D:].reshape(2, 2, D, N_DEV * F8)
    W["ffn_w2"] = jnp.stack([_row_gather(got["w2"], m * F8, F8, name=f"weights_w2_{m}")
                             for m in range(4)]).reshape(2, 2, N_DEV * F8, D)
    W["ssm_in_proj"] = _interleave(got["win"], name="weights_win")[None]
    W["ssm_out_proj"] = _row_gather(got["rows"], 0, DI8, name="weights_wout")[None]
    W["w_k"] = _row_gather(got["rows"], DI8, D8, name="weights_wk")
    W["w_v"] = _row_gather(got["rows"], DI8 + D8, D8, name="weights_wv")
    W["sb_w_q"] = _row_gather(got["rows"], DI8 + 2 * D8, D8, name="weights_wq")[None]
    W["sb_w_o"] = _row_gather(got["rows"], DI8 + 3 * D8, D8, name="weights_wo")[None]

    loss, dx, G, groups = _local_step(x[0], loss_target[0], W)
    loss = lax.psum(loss, ("x", "y", "c"))

    summed = {k: _sum_rows(_exchange(v, scatter=True, name=f"scatter_{k}"), name=f"sum_{k}") for k, v in groups.items()}
    send = jnp.concatenate([_to_shards(G[n], SHARD_AXIS[n]) for n in small], axis=1)
    n_el = send.shape[1]
    npad = -(-n_el // (LANES * PACK_ROWS)) * (LANES * PACK_ROWS)
    send = jnp.pad(send, ((0, 0), (0, npad - n_el))).reshape(N_DEV, npad // LANES, LANES)
    small_sum = _sum_rows(_exchange(send, scatter=True, name="scatter_gains"), name="sum_gains").reshape(-1)
    rep = _pack([G[n] for n in replicated], F32)
    rep_sum = _sum_rows(_exchange(rep, scatter=False, name="gather_small_grads"), name="sum_small_grads").reshape(-1)

    grads, off = {}, 0
    for n in small:
        size = math.prod(shapes[n])
        grads[n] = small_sum[off:off + size].reshape(shapes[n])
        off += size
    off = 0
    for n in replicated:
        size = math.prod(shapes[n])
        grads[n] = rep_sum[off:off + size].reshape(shapes[n])
        off += size
    grads["ffn_w1"] = summed["cols"][:4 * D].reshape(shapes["ffn_w1"])
    grads["ffn_w3"] = summed["cols"][4 * D:].reshape(shapes["ffn_w3"])
    grads["ffn_w2"] = summed["w2"].reshape(shapes["ffn_w2"])
    grads["ssm_in_proj"] = summed["win"][None]
    rows = summed["rows"]
    grads["ssm_out_proj"] = rows[:DI8][None]
    grads["w_k"] = rows[DI8:DI8 + D8]
    grads["w_v"] = rows[DI8 + D8:DI8 + 2 * D8]
    grads["sb_w_q"] = rows[DI8 + 2 * D8:DI8 + 3 * D8][None]
    grads["sb_w_o"] = rows[DI8 + 3 * D8:][None]

    delta, new_m, new_v = {}, {}, {}
    for n in WEIGHTS:
        w2 = shard[n].reshape(1, -1) if shard[n].ndim == 1 else shard[n]
        r2 = lambda t: t.reshape(w2.shape)
        d, nm, nv = _adamw(w2, r2(grads[n]), r2(mom_m[n]), r2(mom_v[n]), name=f"adamw_{n}")
        delta[n], new_m[n], new_v[n] = (t.reshape(shapes[n]) for t in (d, nm, nv))

    return (loss, dx[None], *[grads[n] for n in WEIGHTS], *[delta[n] for n in WEIGHTS],
            *[new_m[n] for n in WEIGHTS], *[new_v[n] for n in WEIGHTS])
```

```python
import functools
import math

import jax
import jax.numpy as jnp
from jax import lax
from jax.experimental import pallas as pl
from jax.experimental.pallas import tpu as pltpu

F32 = jnp.float32
BF16 = jnp.bfloat16
RMS_EPS = 1e-6
N_META = 16
HEAD_DIM = 64
SSM_GROUPS = 8
D_STATE = 128
D_CONV = 4
FFN_RES = 0.5
ADAM_LR, ADAM_B1, ADAM_B2, ADAM_EPS, ADAM_WD, ADAM_STEP = 0.001, 0.9, 0.999, 1e-08, 0.01, 10
N_DEV = 8
SSD_CHUNK = 128
ATT_BLOCK = 256
ROW_ALIGN = 768
VMEM_LIMIT_V7X = 48 * 1024 * 1024
MESH = pl.DeviceIdType.MESH
LANES = 128


def _cp(*sem):
    return pltpu.CompilerParams(dimension_semantics=sem if sem else None, vmem_limit_bytes=VMEM_LIMIT_V7X)


def _tile(n, cands):
    for c in cands:
        if n % c == 0:
            return c
    return n


def _softplus(x):
    return jnp.maximum(x, 0.0) + jnp.log(1.0 + jnp.exp(-jnp.abs(x)))


def _sigmoid(x):
    return 1.0 / (1.0 + jnp.exp(-x))


def _split3(x):
    hi = x.astype(BF16)
    r1 = x - hi.astype(F32)
    mid = r1.astype(BF16)
    lo = (r1 - mid.astype(F32)).astype(BF16)
    return hi, mid, lo


def _dot(a, b):
    return jnp.dot(a, b, preferred_element_type=F32)


def _dot_nt(a, b):
    return lax.dot_general(a, b, (((1,), (1,)), ((), ())), preferred_element_type=F32)


def _dot_tn(a, b):
    return lax.dot_general(a, b, (((0,), (0,)), ((), ())), preferred_element_type=F32)


def _dot3_left(t_bf16, x):
    hi, mid, lo = _split3(x)
    return _dot(t_bf16, hi) + _dot(t_bf16, mid) + _dot(t_bf16, lo)


def _dot3_right(x, t_bf16):
    hi, mid, lo = _split3(x)
    return _dot(hi, t_bf16) + _dot(mid, t_bf16) + _dot(lo, t_bf16)


def _matmul(pairs, *, name, out_dtype=F32, trans_a=False, res=None, alpha=1.0, shards=None, tm=None, tn=None, tk=None):
    a0, b0 = pairs[0]
    if trans_a:
        K, M = a0.shape
    else:
        M, K = a0.shape
    N = b0.shape[1]
    tm = tm or _tile(M, (768, 512, 1408, 384, 256, 128))
    tn = tn or (N if shards else _tile(N, (512, 1408, 384, 256, 128)))
    tk = tk or _tile(K, (1024, 1408, 768, 512, 256, 128))
    nk = K // tk
    npair = len(pairs)
    has_res = res is not None
    cs = N // shards if shards else None

    def body(*refs):
        o_ref, acc = refs[-2], refs[-1]
        k = pl.program_id(2)

        @pl.when(k == 0)
        def _():
            acc[...] = jnp.zeros_like(acc)

        part = None
        for p in range(npair):
            a = refs[2 * p][...].astype(BF16)
            b = refs[2 * p + 1][...].astype(BF16)
            d = _dot_tn(a, b) if trans_a else _dot(a, b)
            part = d if part is None else part + d
        acc[...] += part

        @pl.when(k == nk - 1)
        def _():
            if shards:
                for d in range(shards):
                    v = acc[:, d * cs:(d + 1) * cs]
                    o_ref[d] = (v * alpha if alpha != 1.0 else v).astype(out_dtype)
                return
            v = acc[...]
            if alpha != 1.0:
                v = v * alpha
            if has_res:
                v = refs[2 * npair][...] + v
            o_ref[...] = v.astype(out_dtype)

    if trans_a:
        a_spec = pl.BlockSpec((tk, tm), lambda i, j, k: (k, i))
    else:
        a_spec = pl.BlockSpec((tm, tk), lambda i, j, k: (i, k))
    b_spec = pl.BlockSpec((tk, tn), lambda i, j, k: (k, j))
    if shards:
        assert not has_res and tn == N
        o_spec = pl.BlockSpec((shards, tm, cs), lambda i, j, k: (0, i, 0))
        out_shape = jax.ShapeDtypeStruct((shards, M, cs), out_dtype)
    else:
        o_spec = pl.BlockSpec((tm, tn), lambda i, j, k: (i, j))
        out_shape = jax.ShapeDtypeStruct((M, N), out_dtype)
    in_specs, args = [], []
    for a, b in pairs:
        in_specs += [a_spec, b_spec]
        args += [a, b]
    if has_res:
        in_specs.append(o_spec)
        args.append(res)
    return pl.pallas_call(
        body, name=name, grid=(M // tm, N // tn, nk), in_specs=in_specs, out_specs=o_spec,
        out_shape=out_shape,
        scratch_shapes=[pltpu.VMEM((tm, tn), F32)],
        compiler_params=_cp("parallel", "parallel", "arbitrary"),
    )(*args)


def _rms_fwd(h, g, *, name, scale=1.0):
    R, D = h.shape
    tr = _tile(R, (2048, 1024, 768, 512, 256, 128)) if D <= 128 else _tile(R, (384, 256, 128))

    def body(h_ref, g_ref, o_ref):
        x = h_ref[...]
        r = lax.rsqrt(jnp.mean(x * x, axis=1, keepdims=True) + RMS_EPS)
        y = x * r * g_ref[...]
        if scale != 1.0:
            y = y * scale
        o_ref[...] = y.astype(BF16)

    return pl.pallas_call(
        body, name=name, grid=(R // tr,),
        in_specs=[pl.BlockSpec((tr, D), lambda i: (i, 0)), pl.BlockSpec((1, D), lambda i: (0, 0))],
        out_specs=pl.BlockSpec((tr, D), lambda i: (i, 0)),
        out_shape=jax.ShapeDtypeStruct((R, D), BF16), compiler_params=_cp("parallel"),
    )(h, g.reshape(1, D))


def _rms_bwd(h, g, dn, res=None, *, name, alpha=1.0):
    R, D = h.shape
    tr = _tile(R, (2048, 1024, 768, 512, 256, 128)) if D <= 128 else _tile(R, (384, 256, 128))
    has_res = res is not None

    def body(*refs):
        h_ref, g_ref, dn_ref = refs[:3]
        dh_ref, dg_ref = refs[-2], refs[-1]
        i = pl.program_id(0)

        @pl.when(i == 0)
        def _():
            dg_ref[...] = jnp.zeros_like(dg_ref)

        x = h_ref[...]
        r = lax.rsqrt(jnp.mean(x * x, axis=1, keepdims=True) + RMS_EPS)
        xh = x * r
        d = dn_ref[...].astype(F32)
        if alpha != 1.0:
            d = d * alpha
        dng = d * g_ref[...]
        m = jnp.mean(dng * xh, axis=1, keepdims=True)
        dh = r * (dng - xh * m)
        if has_res:
            dh = dh + refs[3][...]
        dh_ref[...] = dh
        dg_ref[...] += jnp.sum(d * xh, axis=0, keepdims=True)

    row = pl.BlockSpec((tr, D), lambda i: (i, 0))
    vec = pl.BlockSpec((1, D), lambda i: (0, 0))
    in_specs = [row, vec, row] + ([row] if has_res else [])
    args = [h, g.reshape(1, D), dn] + ([res] if has_res else [])
    return pl.pallas_call(
        body, name=name, grid=(R // tr,), in_specs=in_specs, out_specs=[row, vec],
        out_shape=[jax.ShapeDtypeStruct((R, D), F32), jax.ShapeDtypeStruct((1, D), F32)],
        compiler_params=_cp("arbitrary"),
    )(*args)


def _ffn_up(n, w1, w3, *, name):
    M, K = n.shape
    N = w1.shape[1]
    tm = _tile(M, (384, 256, 128))
    tn = _tile(N, (1408, 512, 256, 128))

    def body(n_ref, w1_ref, w3_ref, a_ref, b_ref, s_ref):
        x = n_ref[...]
        a = _dot(x, w1_ref[...])
        b = _dot(x, w3_ref[...])
        a_ref[...] = a.astype(BF16)
        b_ref[...] = b.astype(BF16)
        s_ref[...] = (a * _sigmoid(a) * b).astype(BF16)

    o_spec = pl.BlockSpec((tm, tn), lambda j, i: (i, j))
    w_spec = pl.BlockSpec((K, tn), lambda j, i: (0, j))
    sh = jax.ShapeDtypeStruct((M, N), BF16)
    return pl.pallas_call(
        body, name=name, grid=(N // tn, M // tm),
        in_specs=[pl.BlockSpec((tm, K), lambda j, i: (i, 0)), w_spec, w_spec],
        out_specs=[o_spec, o_spec, o_spec], out_shape=[sh, sh, sh],
        compiler_params=_cp("parallel", "parallel"),
    )(n, w1, w3)


def _ffn_mid_bwd(dh, w2t, a, b, *, name):
    M, K = dh.shape
    N = w2t.shape[1]
    tm = _tile(M, (384, 256, 128))
    tn = _tile(N, (1408, 512, 256, 128))

    def body(dh_ref, w_ref, a_ref, b_ref, da_ref, db_ref):
        ds = _dot(dh_ref[...].astype(BF16), w_ref[...]) * FFN_RES
        av = a_ref[...].astype(F32)
        bv = b_ref[...].astype(F32)
        sg = _sigmoid(av)
        da_ref[...] = (ds * bv * sg * (1.0 + av * (1.0 - sg))).astype(BF16)
        db_ref[...] = (ds * av * sg).astype(BF16)

    o_spec = pl.BlockSpec((tm, tn), lambda j, i: (i, j))
    sh = jax.ShapeDtypeStruct((M, N), BF16)
    return pl.pallas_call(
        body, name=name, grid=(N // tn, M // tm),
        in_specs=[pl.BlockSpec((tm, K), lambda j, i: (i, 0)), pl.BlockSpec((K, tn), lambda j, i: (0, j)), o_spec, o_spec],
        out_specs=[o_spec, o_spec], out_shape=[sh, sh], compiler_params=_cp("parallel", "parallel"),
    )(dh, w2t, a, b)


def _conv_pre(xx, w_ref, b_ref, tr):
    acc = None
    for k in range(D_CONV):
        sh = D_CONV - 1 - k
        v = (pltpu.roll(xx, sh, 0) if sh else xx)[8:8 + tr]
        t = w_ref[k:k + 1, :] * v
        acc = t if acc is None else acc + t
    return acc + b_ref[...]


def _conv_fwd(zx, w, b, col_off, *, name):
    LP = zx.shape[0]
    C = w.shape[1]
    tr = _tile(LP, (256, 128))
    tc = _tile(C, (512, 256, 128))
    co = col_off // tc

    def body(cur_ref, prev_ref, w_ref, b_ref, o_ref):
        i = pl.program_id(0)
        prev = jnp.where(i == 0, 0.0, prev_ref[...])
        pre = _conv_pre(jnp.concatenate([prev, cur_ref[...]], axis=0), w_ref, b_ref, tr)
        o_ref[...] = pre * _sigmoid(pre)

    return pl.pallas_call(
        body, name=name, grid=(LP // tr, C // tc),
        in_specs=[pl.BlockSpec((tr, tc), lambda i, j: (i, j + co)),
                  pl.BlockSpec((8, tc), lambda i, j: (jnp.maximum(i * (tr // 8) - 1, 0), j + co)),
                  pl.BlockSpec((D_CONV, tc), lambda i, j: (0, j)), pl.BlockSpec((1, tc), lambda i, j: (0, j))],
        out_specs=pl.BlockSpec((tr, tc), lambda i, j: (i, j)),
        out_shape=jax.ShapeDtypeStruct((LP, C), F32), compiler_params=_cp("parallel", "parallel"),
    )(zx, zx, w, b.reshape(1, C))


def _conv_bwd_g(zx, w, b, dact, col_off, *, name):
    LP = zx.shape[0]
    C = w.shape[1]
    tr = _tile(LP, (256, 128))
    tc = _tile(C, (512, 256, 128))
    co = col_off // tc

    def body(cur_ref, prev_ref, w_ref, b_ref, d_ref, g_ref, dw_ref, db_ref):
        i = pl.program_id(1)

        @pl.when(i == 0)
        def _():
            dw_ref[...] = jnp.zeros_like(dw_ref)
            db_ref[...] = jnp.zeros_like(db_ref)

        prev = jnp.where(i == 0, 0.0, prev_ref[...])
        xx = jnp.concatenate([prev, cur_ref[...]], axis=0)
        pre = _conv_pre(xx, w_ref, b_ref, tr)
        sg = _sigmoid(pre)
        g = d_ref[...] * sg * (1.0 + pre * (1.0 - sg))
        g_ref[...] = g
        db_ref[...] += jnp.sum(g, axis=0, keepdims=True)
        rows = []
        for k in range(D_CONV):
            sh = D_CONV - 1 - k
            v = (pltpu.roll(xx, sh, 0) if sh else xx)[8:8 + tr]
            rows.append(jnp.sum(g * v, axis=0, keepdims=True))
        rows.append(jnp.zeros((8 - D_CONV, tc), F32))
        dw_ref[...] += jnp.concatenate(rows, axis=0)

    return pl.pallas_call(
        body, name=name, grid=(C // tc, LP // tr),
        in_specs=[pl.BlockSpec((tr, tc), lambda j, i: (i, j + co)),
                  pl.BlockSpec((8, tc), lambda j, i: (jnp.maximum(i * (tr // 8) - 1, 0), j + co)),
                  pl.BlockSpec((D_CONV, tc), lambda j, i: (0, j)), pl.BlockSpec((1, tc), lambda j, i: (0, j)),
                  pl.BlockSpec((tr, tc), lambda j, i: (i, j))],
        out_specs=[pl.BlockSpec((tr, tc), lambda j, i: (i, j)), pl.BlockSpec((8, tc), lambda j, i: (0, j)),
                   pl.BlockSpec((1, tc), lambda j, i: (0, j))],
        out_shape=[jax.ShapeDtypeStruct((LP, C), F32), jax.ShapeDtypeStruct((8, C), F32), jax.ShapeDtypeStruct((1, C), F32)],
        compiler_params=_cp("parallel", "arbitrary"),
    )(zx, zx, w, b.reshape(1, C), dact)


def _conv_bwd_u(g, w, *, name):
    LP, C = g.shape
    tr = _tile(LP, (256, 128))
    tc = _tile(C, (512, 256, 128))
    nb = LP // tr

    def body(cur_ref, nxt_ref, w_ref, o_ref):
        i = pl.program_id(0)
        nxt = jnp.where(i == nb - 1, 0.0, nxt_ref[...])
        xx = jnp.concatenate([cur_ref[...], nxt], axis=0)
        acc = None
        for k in range(D_CONV):
            sh = D_CONV - 1 - k
            v = (pltpu.roll(xx, tr + 8 - sh, 0) if sh else xx)[:tr]
            t = w_ref[k:k + 1, :] * v
            acc = t if acc is None else acc + t
        o_ref[...] = acc

    return pl.pallas_call(
        body, name=name, grid=(nb, C // tc),
        in_specs=[pl.BlockSpec((tr, tc), lambda i, j: (i, j)),
                  pl.BlockSpec((8, tc), lambda i, j: (jnp.minimum((i + 1) * (tr // 8), LP // 8 - 1), j)),
                  pl.BlockSpec((D_CONV, tc), lambda i, j: (0, j))],
        out_specs=pl.BlockSpec((tr, tc), lambda i, j: (i, j)),
        out_shape=jax.ShapeDtypeStruct((LP, C), F32), compiler_params=_cp("parallel", "parallel"),
    )(g, g, w)


def _ssd_prelude(dtr_ref, dtrt_ref, brow_ref, bcol_ref, alrow_ref, alcol_ref, Q):
    ii = lax.broadcasted_iota(jnp.int32, (Q, Q), 0)
    jj = lax.broadcasted_iota(jnp.int32, (Q, Q), 1)
    tril = ii >= jj
    dt_col = _softplus(dtr_ref[...] + brow_ref[...])
    a_row_p = -jnp.exp(alrow_ref[...])
    dt_row = _softplus(dtrt_ref[...] + bcol_ref[...])
    a_col_p = -jnp.exp(alcol_ref[...])
    cum_col = _dot3_left(tril.astype(BF16), dt_col * a_row_p)
    cum_row = _dot3_right(dt_row * a_col_p, (ii <= jj).astype(BF16))
    return ii, jj, tril, dt_col, dt_row, a_row_p, cum_col, cum_row


def _col_of(mat, lane_idx, h):
    return jnp.sum(jnp.where(lane_idx == h, mat, 0.0), axis=1, keepdims=True)


def _ssd_fwd(xbc, dtr, dtrt, brow, bcol, alrow, alcol, dvec, *, name):
    LP = xbc.shape[0]
    Q = SSD_CHUNK
    nc = LP // Q
    G = SSM_GROUPS
    DI = dvec.shape[1]
    gw = DI // G
    hpg = gw // HEAD_DIM
    H = G * hpg
    boff, coff = DI, DI + G * D_STATE

    def body(xbc_ref, dtr_ref, dtrt_ref, brow_ref, bcol_ref, alrow_ref, alcol_ref, dvec_ref, y_ref, st_ref, state):
        c = pl.program_id(0)

        @pl.when(c == 0)
        def _():
            state[...] = jnp.zeros_like(state)

        st_ref[...] = state[...]
        ii, jj, tril, dt_col, dt_row, _, cum_col, cum_row = _ssd_prelude(
            dtr_ref, dtrt_ref, brow_ref, bcol_ref, alrow_ref, alcol_ref, Q)
        lane_h = lax.broadcasted_iota(jnp.int32, (Q, 128), 1)
        lane_g = lax.broadcasted_iota(jnp.int32, (Q, gw), 1) // HEAD_DIM
        for g in range(G):
            xg = xbc_ref[:, g * gw:(g + 1) * gw]
            bb = xbc_ref[:, boff + g * D_STATE: boff + (g + 1) * D_STATE].astype(BF16)
            cb = xbc_ref[:, coff + g * D_STATE: coff + (g + 1) * D_STATE].astype(BF16)
            gm = _dot_nt(cb, bb)
            sg = state[g]
            yoff = _dot(cb, sg.astype(BF16))
            ydiag = jnp.zeros((Q, gw), F32)
            esc = jnp.zeros((Q, gw), F32)
            wsc = jnp.zeros((Q, gw), F32)
            lam = jnp.zeros((1, gw), F32)
            for j in range(hpg):
                h = g * hpg + j
                ccol = _col_of(cum_col, lane_h, h)
                dcol = _col_of(dt_col, lane_h, h)
                seg = ccol - cum_row[h:h + 1, :]
                decay = jnp.exp(jnp.where(tril, seg, -jnp.inf))
                mh = gm * decay * dt_row[h:h + 1, :]
                hm = lane_g == j
                ydiag = ydiag + _dot(mh.astype(BF16), jnp.where(hm, xg, 0.0).astype(BF16))
                tot = ccol[Q - 1:Q, :]
                esc = jnp.where(hm, jnp.exp(ccol), esc)
                wsc = jnp.where(hm, jnp.exp(tot - ccol) * dcol, wsc)
                lam = jnp.where(hm[0:1], jnp.exp(tot), lam)
            y_ref[:, g * gw:(g + 1) * gw] = ydiag + yoff * esc + dvec_ref[:, g * gw:(g + 1) * gw] * xg
            state[g] = sg * lam + _dot_tn(bb, (xg * wsc).astype(BF16))

    W = xbc.shape[1]
    full = lambda shape: pl.BlockSpec(shape, lambda c: (0,) * len(shape))
    return pl.pallas_call(
        body, name=name, grid=(nc,),
        in_specs=[pl.BlockSpec((Q, W), lambda c: (c, 0)), pl.BlockSpec((Q, 128), lambda c: (c, 0)),
                  pl.BlockSpec((H, Q), lambda c: (0, c)), full((1, 128)), full((H, 1)), full((1, 128)), full((H, 1)),
                  full((1, DI))],
        out_specs=[pl.BlockSpec((Q, DI), lambda c: (c, 0)), pl.BlockSpec((None, G, D_STATE, gw), lambda c: (c, 0, 0, 0))],
        out_shape=[jax.ShapeDtypeStruct((LP, DI), F32), jax.ShapeDtypeStruct((nc, G, D_STATE, gw), F32)],
        scratch_shapes=[pltpu.VMEM((G, D_STATE, gw), F32)],
        compiler_params=_cp("arbitrary"),
    )(xbc, dtr, dtrt, brow, bcol, alrow, alcol, dvec)


def _ssd_bwd(xbc, dtr, dtrt, brow, bcol, alrow, alcol, dvec, dy, states, *, name):
    LP = xbc.shape[0]
    Q = SSD_CHUNK
    nc = LP // Q
    G = SSM_GROUPS
    DI = dvec.shape[1]
    gw = DI // G
    hpg = gw // HEAD_DIM
    H = G * hpg
    boff, coff = DI, DI + G * D_STATE
    W = xbc.shape[1]

    def body(xbc_ref, dtr_ref, dtrt_ref, brow_ref, bcol_ref, alrow_ref, alcol_ref, dvec_ref, dy_ref, st_ref,
             dxbc_ref, ddtr_ref, dbias_ref, dalog_ref, ddvec_ref, dstate):
        c = pl.program_id(0)

        @pl.when(c == 0)
        def _():
            dstate[...] = jnp.zeros_like(dstate)
            dbias_ref[...] = jnp.zeros_like(dbias_ref)
            dalog_ref[...] = jnp.zeros_like(dalog_ref)
            ddvec_ref[...] = jnp.zeros_like(ddvec_ref)

        ii, jj, tril, dt_col, dt_row, a_row_p, cum_col, cum_row = _ssd_prelude(
            dtr_ref, dtrt_ref, brow_ref, bcol_ref, alrow_ref, alcol_ref, Q)
        eye = ii == jj
        lane_h = lax.broadcasted_iota(jnp.int32, (Q, 128), 1)
        row_h = lax.broadcasted_iota(jnp.int32, (Q, 128), 0)
        lane_g = lax.broadcasted_iota(jnp.int32, (Q, gw), 1) // HEAD_DIM
        lane_s = lax.broadcasted_iota(jnp.int32, (D_STATE, gw), 1) // HEAD_DIM
        dcum_mat = jnp.zeros((Q, 128), F32)
        ddt_mat = jnp.zeros((Q, 128), F32)
        dtot_row = jnp.zeros((1, 128), F32)
        for g in range(G):
            xg = xbc_ref[:, g * gw:(g + 1) * gw]
            dyg = dy_ref[:, g * gw:(g + 1) * gw]
            bb = xbc_ref[:, boff + g * D_STATE: boff + (g + 1) * D_STATE].astype(BF16)
            cb = xbc_ref[:, coff + g * D_STATE: coff + (g + 1) * D_STATE].astype(BF16)
            sg = st_ref[g]
            dsg = dstate[g]
            sb = sg.astype(BF16)
            dsb = dsg.astype(BF16)
            xb = xg.astype(BF16)
            gm = _dot_nt(cb, bb)
            cs = _dot(cb, sb)
            bds = _dot(bb, dsb)
            dxg = dvec_ref[:, g * gw:(g + 1) * gw] * dyg
            dgm = jnp.zeros((Q, Q), F32)
            esc = jnp.zeros((Q, gw), F32)
            wsc = jnp.zeros((Q, gw), F32)
            lam = jnp.zeros((1, gw), F32)
            dycs = dyg * cs
            xbds = xg * bds
            dss = dsg * sg
            for j in range(hpg):
                h = g * hpg + j
                ccol = _col_of(cum_col, lane_h, h)
                dcol = _col_of(dt_col, lane_h, h)
                drow = dt_row[h:h + 1, :]
                seg = ccol - cum_row[h:h + 1, :]
                decay = jnp.exp(jnp.where(tril, seg, -jnp.inf))
                hm = lane_g == j
                dyh = jnp.where(hm, dyg, 0.0).astype(BF16)
                gl = gm * decay
                mh = gl * drow
                dmf = _dot_nt(dyh, xb)
                dxg = dxg + _dot_tn(mh.astype(BF16), dyh)
                dgm = dgm + dmf * decay * drow
                n_p = dmf * gl
                n_m = n_p * drow
                rowsum_n = jnp.sum(n_m, axis=1, keepdims=True)
                colsum_n = jnp.sum(jnp.where(eye, jnp.sum(n_m, axis=0, keepdims=True), 0.0), axis=1, keepdims=True)
                colsum_np = jnp.sum(jnp.where(eye, jnp.sum(n_p, axis=0, keepdims=True), 0.0), axis=1, keepdims=True)
                tot = ccol[Q - 1:Q, :]
                e = jnp.exp(ccol)
                wexp = jnp.exp(tot - ccol)
                wcol = wexp * dcol
                lamh = jnp.exp(tot)
                yoff_t = jnp.sum(jnp.where(hm, dycs, 0.0), axis=1, keepdims=True) * e
                e_s = jnp.sum(jnp.where(hm, xbds, 0.0), axis=1, keepdims=True)
                ew = e_s * wcol
                dtot = jnp.sum(ew, axis=0, keepdims=True) + lamh * jnp.sum(
                    jnp.sum(jnp.where(lane_s == j, dss, 0.0), axis=1, keepdims=True), axis=0, keepdims=True)
                dcum_h = rowsum_n + yoff_t - colsum_n - ew
                ddt_h = colsum_np + e_s * wexp
                onehot = lane_h == h
                dcum_mat = jnp.where(onehot, dcum_h, dcum_mat)
                ddt_mat = jnp.where(onehot, ddt_h, ddt_mat)
                dtot_row = jnp.where(onehot[0:1], dtot, dtot_row)
                esc = jnp.where(hm, e, esc)
                wsc = jnp.where(hm, wcol, wsc)
                lam = jnp.where(hm[0:1], lamh, lam)
            dgb = dgm.astype(BF16)
            dye = (dyg * esc).astype(BF16)
            xw = (xg * wsc).astype(BF16)
            dxbc_ref[:, g * gw:(g + 1) * gw] = dxg + bds * wsc
            dxbc_ref[:, boff + g * D_STATE: boff + (g + 1) * D_STATE] = _dot_tn(dgb, cb) + _dot_nt(xw, dsb)
            dxbc_ref[:, coff + g * D_STATE: coff + (g + 1) * D_STATE] = _dot(dgb, bb) + _dot_nt(dye, sb)
            dstate[g] = dsg * lam + _dot_tn(cb, dye)
            ddvec_ref[:, g * gw:(g + 1) * gw] += jnp.sum(dyg * xg, axis=0, keepdims=True)
        dcum_mat = dcum_mat + jnp.where(row_h == Q - 1, dtot_row, 0.0)
        da = _dot3_left((ii <= jj).astype(BF16), dcum_mat)
        ddt = ddt_mat + da * a_row_p
        dalog_ref[...] += jnp.sum(da * dt_col, axis=0, keepdims=True) * a_row_p
        ddtr = ddt * _sigmoid(dtr_ref[...] + brow_ref[...])
        ddtr_ref[...] = ddtr
        dbias_ref[...] += jnp.sum(ddtr, axis=0, keepdims=True)

    full = lambda shape: pl.BlockSpec(shape, lambda c: (0,) * len(shape))
    rc = lambda c: nc - 1 - c
    return pl.pallas_call(
        body, name=name, grid=(nc,),
        in_specs=[pl.BlockSpec((Q, W), lambda c: (rc(c), 0)), pl.BlockSpec((Q, 128), lambda c: (rc(c), 0)),
                  pl.BlockSpec((H, Q), lambda c: (0, rc(c))), full((1, 128)), full((H, 1)), full((1, 128)), full((H, 1)),
                  full((1, DI)), pl.BlockSpec((Q, DI), lambda c: (rc(c), 0)),
                  pl.BlockSpec((None, G, D_STATE, gw), lambda c: (rc(c), 0, 0, 0))],
        out_specs=[pl.BlockSpec((Q, W), lambda c: (rc(c), 0)), pl.BlockSpec((Q, 128), lambda c: (rc(c), 0)),
                   full((1, 128)), full((1, 128)), full((1, DI))],
        out_shape=[jax.ShapeDtypeStruct((LP, W), F32), jax.ShapeDtypeStruct((LP, 128), F32),
                   jax.ShapeDtypeStruct((1, 128), F32), jax.ShapeDtypeStruct((1, 128), F32),
                   jax.ShapeDtypeStruct((1, DI), F32)],
        scratch_shapes=[pltpu.VMEM((G, D_STATE, gw), F32)],
        compiler_params=_cp("arbitrary"),
    )(xbc, dtr, dtrt, brow, bcol, alrow, alcol, dvec, dy, states)


def _gate_fwd(y, zx, g, *, name):
    LP, DI = y.shape
    gw = DI // SSM_GROUPS
    tr = _tile(LP, (256, 128))

    def body(y_ref, z_ref, g_ref, o_ref):
        for k in range(SSM_GROUPS):
            sl = slice(k * gw, (k + 1) * gw)
            z = z_ref[:, sl]
            t = y_ref[:, sl] * (z * _sigmoid(z))
            r = lax.rsqrt(jnp.mean(t * t, axis=1, keepdims=True) + RMS_EPS)
            o_ref[:, sl] = (t * r * g_ref[:, sl]).astype(BF16)

    row = pl.BlockSpec((tr, DI), lambda i: (i, 0))
    return pl.pallas_call(
        body, name=name, grid=(LP // tr,), in_specs=[row, row, pl.BlockSpec((1, DI), lambda i: (0, 0))],
        out_specs=row, out_shape=jax.ShapeDtypeStruct((LP, DI), BF16), compiler_params=_cp("parallel"),
    )(y, zx, g)


def _gate_bwd(y, zx, g, dyn, *, name):
    LP, DI = y.shape
    gw = DI // SSM_GROUPS
    tr = _tile(LP, (256, 128))

    def body(y_ref, z_ref, g_ref, d_ref, dy_ref, dz_ref, dg_ref):
        i = pl.program_id(0)

        @pl.when(i == 0)
        def _():
            dg_ref[...] = jnp.zeros_like(dg_ref)

        for k in range(SSM_GROUPS):
            sl = slice(k * gw, (k + 1) * gw)
            z = z_ref[:, sl]
            yv = y_ref[:, sl]
            sg = _sigmoid(z)
            sz = z * sg
            t = yv * sz
            r = lax.rsqrt(jnp.mean(t * t, axis=1, keepdims=True) + RMS_EPS)
            th = t * r
            d = d_ref[:, sl]
            dtn = d * g_ref[:, sl]
            dt_ = r * (dtn - th * jnp.mean(dtn * th, axis=1, keepdims=True))
            dg_ref[:, sl] += jnp.sum(d * th, axis=0, keepdims=True)
            dy_ref[:, sl] = dt_ * sz
            dz_ref[:, sl] = dt_ * yv * sg * (1.0 + z * (1.0 - sg))

    row = pl.BlockSpec((tr, DI), lambda i: (i, 0))
    vec = pl.BlockSpec((1, DI), lambda i: (0, 0))
    return pl.pallas_call(
        body, name=name, grid=(LP // tr,), in_specs=[row, row, vec, row], out_specs=[row, row, vec],
        out_shape=[jax.ShapeDtypeStruct((LP, DI), F32), jax.ShapeDtypeStruct((LP, DI), F32),
                   jax.ShapeDtypeStruct((1, DI), F32)],
        compiler_params=_cp("arbitrary"),
    )(y, zx, g, dyn)


EXP_ZERO = -104.0
LOG2E = 1.4426950408889634


def _dot2_right(x, t2_bf16):
    hi = x.astype(BF16)
    lo = (x - hi.astype(F32)).astype(BF16)
    return _dot(jnp.concatenate([hi, lo], axis=1), t2_bf16)


def _tri2(T, upper):
    r = lax.broadcasted_iota(jnp.int32, (2 * T, T), 0) % T
    c = lax.broadcasted_iota(jnp.int32, (2 * T, T), 1)
    return (r <= c if upper else r >= c).astype(BF16)


def _sb_tile(q, k_blk, lower2, valid=None):
    z = _dot_nt(q, k_blk)
    sp = jnp.maximum(z, 0.0) + jnp.log(1.0 + jnp.exp2(jnp.abs(z) * (-LOG2E)))
    if valid is not None:
        sp = jnp.where(valid, sp, 0.0)
    return z, sp, z - _dot2_right(sp, lower2)


def _sb_weights(zr, c, valid=None):
    w = jnp.exp(zr + c)
    return w if valid is None else jnp.where(valid, w, 0.0)


def _sb_fwd(q, k, v, zmax, *, name):
    H, LP, dh = q.shape
    T = ATT_BLOCK
    nq = LP // T
    assert nq < LANES

    def body(q_ref, k_ref, v_ref, zb_ref, o_ref, c_ref):
        i = pl.program_id(1)
        ii = lax.broadcasted_iota(jnp.int32, (T, T), 0)
        jj = lax.broadcasted_iota(jnp.int32, (T, T), 1)
        lane = lax.broadcasted_iota(jnp.int32, (T, LANES), 1)
        lower2 = _tri2(T, upper=False)
        qv = q_ref[...]
        zb = zb_ref[0:1, 0:1]

        def kv(kb):
            ks = pl.multiple_of(kb * T, T)
            return k_ref[pl.ds(ks, T), :], v_ref[pl.ds(ks, T), :]

        kd, vd = kv(i)
        diag = jj < ii
        _, sp, zr = _sb_tile(qv, kd, lower2, diag)
        acc = _dot(_sb_weights(zr, 0.0, diag).astype(BF16), vd)
        c = -jnp.sum(sp, axis=1, keepdims=True)
        c_ref[...] = jnp.zeros_like(c_ref)

        def alive(c):
            return jnp.max(c + zb) > EXP_ZERO

        def cond(carry):
            kb, _, _, live = carry
            return (kb >= 0) & live

        def step(carry):
            kb, c, acc, _ = carry
            kt, vt = kv(kb)
            _, sp, zr = _sb_tile(qv, kt, lower2)
            acc = acc + _dot(_sb_weights(zr, c).astype(BF16), vt)
            c_ref[...] = jnp.where(lane == kb, c, c_ref[...])
            c = c - jnp.sum(sp, axis=1, keepdims=True)
            return kb - 1, c, acc, alive(c)

        kb, _, acc, _ = lax.while_loop(cond, step, (i - 1, c, acc, alive(c)))
        o_ref[...] = acc
        c_ref[...] = jnp.where(lane == LANES - 1, (kb + 1).astype(F32), c_ref[...])

    blk = pl.BlockSpec((None, T, dh), lambda h, i: (h, i, 0))
    cblk = pl.BlockSpec((None, T, LANES), lambda h, i: (h, i, 0))
    whole = pl.BlockSpec((None, LP, dh), lambda h, i: (h, 0, 0))
    return pl.pallas_call(
        body, name=name, grid=(H, nq), in_specs=[blk, whole, whole, pl.BlockSpec((1, LANES), lambda h, i: (0, 0))],
        out_specs=[blk, cblk],
        out_shape=[jax.ShapeDtypeStruct((H, LP, dh), F32), jax.ShapeDtypeStruct((H, LP, LANES), F32)],
        compiler_params=_cp("parallel", "parallel"),
    )(q, k, v, zmax)


def _sb_bwd(kstart, q, k, v, cmat, do, *, name):
    H, LP, dh = q.shape
    T = ATT_BLOCK
    nq = LP // T

    def body(ks_ref, q_ref, k_ref, v_ref, c_ref, do_ref, dq_ref, dk_ref, dv_ref):
        h = pl.program_id(0)
        i = pl.program_id(1)

        @pl.when(i == 0)
        def _():
            dk_ref[...] = jnp.zeros_like(dk_ref)
            dv_ref[...] = jnp.zeros_like(dv_ref)

        ii = lax.broadcasted_iota(jnp.int32, (T, T), 0)
        jj = lax.broadcasted_iota(jnp.int32, (T, T), 1)
        lane = lax.broadcasted_iota(jnp.int32, (T, LANES), 1)
        lower2 = _tri2(T, upper=False)
        upper2 = _tri2(T, upper=True)
        qv = q_ref[...]
        dob = do_ref[...].astype(BF16)
        cm = c_ref[...]

        def tile(kb, cg, dq, valid=None):
            ks = pl.multiple_of(kb * T, T)
            k_blk = k_ref[pl.ds(ks, T), :]
            c = jnp.sum(jnp.where(lane == kb, cm, 0.0), axis=1, keepdims=True)
            z, sp, zr = _sb_tile(qv, k_blk, lower2, valid)
            w = _sb_weights(zr, c, valid)
            gw_ = w * _dot_nt(dob, v_ref[pl.ds(ks, T), :])
            gin = _dot2_right(gw_, upper2)
            dz = gw_ - jnp.exp(z - sp) * (cg + gin)
            if valid is not None:
                dz = jnp.where(valid, dz, 0.0)
            dz = dz.astype(BF16)
            dk_ref[pl.ds(ks, T), :] += _dot_tn(dz, qv)
            dv_ref[pl.ds(ks, T), :] += _dot_tn(w.astype(BF16), dob)
            return cg + jnp.sum(gw_, axis=1, keepdims=True), dq + _dot(dz, k_blk)

        carry = lax.fori_loop(ks_ref[h, i], i, lambda kb, cr: tile(kb, *cr),
                              (jnp.zeros((T, 1), F32), jnp.zeros((T, dh), F32)))
        _, dq = tile(i, *carry, valid=jj < ii)
        dq_ref[...] = dq

    blk = pl.BlockSpec((None, T, dh), lambda h, i, ks: (h, i, 0))
    cblk = pl.BlockSpec((None, T, LANES), lambda h, i, ks: (h, i, 0))
    whole = pl.BlockSpec((None, LP, dh), lambda h, i, ks: (h, 0, 0))
    sh = jax.ShapeDtypeStruct((H, LP, dh), F32)
    return pl.pallas_call(
        body, name=name,
        grid_spec=pltpu.PrefetchScalarGridSpec(
            num_scalar_prefetch=1, grid=(H, nq), in_specs=[blk, whole, whole, cblk, blk], out_specs=[blk, whole, whole]),
        out_shape=[sh, sh, sh], compiler_params=_cp("parallel", "arbitrary"),
    )(kstart, q, k, v, cmat, do)


def _loss_head(h, tgt, seq, *, name):
    LP, D = h.shape
    tr = _tile(LP, (384, 256, 128))

    def body(h_ref, t_ref, dh_ref, l_ref):
        i = pl.program_id(0)

        @pl.when(i == 0)
        def _():
            l_ref[...] = jnp.zeros_like(l_ref)

        row = lax.broadcasted_iota(jnp.int32, (tr, D), 0) + i * tr
        e = jnp.where((row >= N_META) & (row < N_META + seq), h_ref[...] - t_ref[...], 0.0)
        dh_ref[...] = e * (1.0 / D)
        l_ref[...] += jnp.sum(e * e, axis=0, keepdims=True) * (0.5 / D)

    row = pl.BlockSpec((tr, D), lambda i: (i, 0))
    return pl.pallas_call(
        body, name=name, grid=(LP // tr,), in_specs=[row, row], out_specs=[row, pl.BlockSpec((1, D), lambda i: (0, 0))],
        out_shape=[jax.ShapeDtypeStruct((LP, D), F32), jax.ShapeDtypeStruct((1, D), F32)],
        compiler_params=_cp("arbitrary"),
    )(h, tgt)


def _adamw(w, g, m, v, *, name):
    shape = w.shape
    C = shape[-1]
    R = math.prod(shape) // C
    tr = _tile(R, (512, 256, 128, 64, 32, 16, 8))
    c1 = 1.0 / (1.0 - ADAM_B1 ** ADAM_STEP)
    c2 = 1.0 / (1.0 - ADAM_B2 ** ADAM_STEP)

    def body(w_ref, g_ref, m_ref, v_ref, d_ref, nm_ref, nv_ref):
        gv = g_ref[...]
        nm = ADAM_B1 * m_ref[...] + (1.0 - ADAM_B1) * gv
        nv = ADAM_B2 * v_ref[...] + (1.0 - ADAM_B2) * (gv * gv)
        d_ref[...] = -ADAM_LR * ((nm * c1) / (jnp.sqrt(nv * c2) + ADAM_EPS) + ADAM_WD * w_ref[...])
        nm_ref[...] = nm
        nv_ref[...] = nv

    blk = pl.BlockSpec((tr, C), lambda i: (i, 0))
    sh = jax.ShapeDtypeStruct((R, C), F32)
    d, nm, nv = pl.pallas_call(
        body, name=name, grid=(R // tr,), in_specs=[blk] * 4, out_specs=[blk] * 3, out_shape=[sh] * 3,
        compiler_params=_cp("parallel"),
    )(w.reshape(R, C), g.reshape(R, C), m.reshape(R, C), v.reshape(R, C))
    return d.reshape(shape), nm.reshape(shape), nv.reshape(shape)


def _sum_rows(buf, *, name):
    n, R, C = buf.shape
    tr = _tile(R, (512, 256, 128, 64, 32, 16, 8))

    def body(b_ref, o_ref):
        acc = b_ref[0].astype(F32)
        for k in range(1, n):
            acc = acc + b_ref[k].astype(F32)
        o_ref[...] = acc

    return pl.pallas_call(
        body, name=name, grid=(R // tr,), in_specs=[pl.BlockSpec((n, tr, C), lambda i: (0, i, 0))],
        out_specs=pl.BlockSpec((tr, C), lambda i: (i, 0)), out_shape=jax.ShapeDtypeStruct((R, C), F32),
        compiler_params=_cp("parallel"),
    )(buf)


def _interleave(buf, *, name):
    n, R, C = buf.shape
    tr = _tile(R, (256, 128, 64, 32, 16))

    def body(b_ref, o_ref):
        for d in range(n):
            o_ref[:, d * C:(d + 1) * C] = b_ref[d]

    return pl.pallas_call(
        body, name=name, grid=(R // tr,), in_specs=[pl.BlockSpec((n, tr, C), lambda i: (0, i, 0))],
        out_specs=pl.BlockSpec((tr, n * C), lambda i: (i, 0)), out_shape=jax.ShapeDtypeStruct((R, n * C), buf.dtype),
        compiler_params=_cp("parallel"),
    )(buf)


def _deinterleave(x, *, out_dtype, name):
    R, NC = x.shape
    C = NC // N_DEV
    tr = _tile(R, (256, 128, 64, 32, 16))

    def body(x_ref, o_ref):
        for d in range(N_DEV):
            o_ref[d] = x_ref[:, d * C:(d + 1) * C].astype(out_dtype)

    return pl.pallas_call(
        body, name=name, grid=(R // tr,), in_specs=[pl.BlockSpec((tr, NC), lambda i: (i, 0))],
        out_specs=pl.BlockSpec((N_DEV, tr, C), lambda i: (0, i, 0)),
        out_shape=jax.ShapeDtypeStruct((N_DEV, R, C), out_dtype), compiler_params=_cp("parallel"),
    )(x)


def _row_gather(buf, off, n, *, name):
    nd, R, C = buf.shape
    assert off % n == 0

    def body(b_ref, o_ref):
        o_ref[...] = b_ref[...]

    return pl.pallas_call(
        body, name=name, grid=(nd,), in_specs=[pl.BlockSpec((None, n, C), lambda d: (d, off // n, 0))],
        out_specs=pl.BlockSpec((n, C), lambda d: (d, 0)), out_shape=jax.ShapeDtypeStruct((nd * n, C), buf.dtype),
        compiler_params=_cp("parallel"),
    )(buf)


def _mesh_pos():
    x, y, c = lax.axis_index("x"), lax.axis_index("y"), lax.axis_index("c")
    return x, y, c, 4 * x + 2 * y + c


def _peer(x, y, c, f):
    px, py, pc = (x + ((f >> 2) & 1)) % 2, (y + ((f >> 1) & 1)) % 2, (c + (f & 1)) % 2
    return (px, py, pc), 4 * px + 2 * py + pc


def _exchange(src, *, scatter, name):
    shape = src.shape[1:] if scatter else src.shape

    def body(src_ref, dst_ref, send_sems, recv_sems, local_sem):
        x, y, c, me = _mesh_pos()
        own = pltpu.make_async_copy(src_ref.at[me] if scatter else src_ref, dst_ref.at[me], local_sem)
        own.start()
        sends, recvs = [], []
        for f in range(1, N_DEV):
            peer, pid = _peer(x, y, c, f)
            sends.append(pltpu.make_async_remote_copy(
                src_ref=src_ref.at[pid] if scatter else src_ref, dst_ref=dst_ref.at[me],
                send_sem=send_sems.at[f - 1], recv_sem=recv_sems.at[f - 1], device_id=peer, device_id_type=MESH))
            recvs.append(pltpu.make_async_remote_copy(
                src_ref=src_ref.at[pid] if scatter else src_ref, dst_ref=dst_ref.at[pid],
                send_sem=send_sems.at[f - 1], recv_sem=recv_sems.at[f - 1], device_id=peer, device_id_type=MESH))
        for cp in sends:
            cp.start()
        for snd, rcv in zip(sends, recvs):
            snd.wait_send()
            rcv.wait_recv()
        own.wait()

    return pl.pallas_call(
        body, name=name, in_specs=[pl.BlockSpec(memory_space=pl.ANY)], out_specs=pl.BlockSpec(memory_space=pl.ANY),
        out_shape=jax.ShapeDtypeStruct((N_DEV,) + tuple(shape), src.dtype),
        scratch_shapes=[pltpu.SemaphoreType.DMA((N_DEV - 1,)), pltpu.SemaphoreType.DMA((N_DEV - 1,)),
                        pltpu.SemaphoreType.DMA],
        compiler_params=pltpu.CompilerParams(has_side_effects=True),
    )(src)


def _heads(t, H):
    LP = t.shape[0]
    return t.reshape(LP, H, HEAD_DIM).transpose(1, 0, 2)


def _unheads(t):
    H, LP, dh = t.shape
    return t.transpose(1, 0, 2).reshape(LP, H * dh)


def _ffn_fwd(h, g, w1, w3, w2, tag):
    n = _rms_fwd(h, g, name=f"ffn_norm_{tag}")
    a, b, s = _ffn_up(n, w1, w3, name=f"ffn_up_{tag}")
    h2 = _matmul([(s, w2)], res=h, alpha=FFN_RES, name=f"ffn_down_{tag}")
    return h2, (h, n, a, b, s)


def _ffn_bwd(dh, saved, g, w1, w3, w2, tag):
    h, n, a, b, s = saved
    da, db = _ffn_mid_bwd(dh, w2.T, a, b, name=f"ffn_mid_bwd_{tag}")
    dw2 = _matmul([(s, dh)], trans_a=True, alpha=FFN_RES, out_dtype=BF16, name=f"ffn_dw2_{tag}")
    dn = _matmul([(da, w1.T), (db, w3.T)], name=f"ffn_dn_{tag}")
    dw1 = _matmul([(n, da)], trans_a=True, shards=N_DEV, out_dtype=BF16, name=f"ffn_dw1_{tag}")
    dw3 = _matmul([(n, db)], trans_a=True, shards=N_DEV, out_dtype=BF16, name=f"ffn_dw3_{tag}")
    dh_in, dg = _rms_bwd(h, g, dn, res=dh, name=f"ffn_norm_bwd_{tag}")
    return dh_in, dg, dw1, dw3, dw2


def _local_step(x, tgt, W):
    seq, D = x.shape
    L = N_META + seq
    LP = -(-L // ROW_ALIGN) * ROW_ALIGN
    pad = LP - L
    H_sb = D // HEAD_DIM
    DI = W["ssm_norm_g"].shape[-1]
    H_ssm = DI // HEAD_DIM
    CONV = DI + 2 * SSM_GROUPS * D_STATE
    ZX = DI + CONV

    h0 = jnp.concatenate([W["meta_tokens"], x, jnp.zeros((pad, D), F32)], axis=0)
    tgt_p = jnp.pad(tgt, ((N_META, pad), (0, 0)))
    ng = W["norm_g"]

    h1, sv_f00 = _ffn_fwd(h0, ng[0, 0], W["ffn_w1"][0, 0], W["ffn_w3"][0, 0], W["ffn_w2"][0, 0], "00")
    u0 = _rms_fwd(h1, ng[0, 1], name="ssm_norm")
    w_in = W["ssm_in_proj"][0]
    w_zx = w_in[:, :ZX]
    w_dt = jnp.pad(w_in[:, ZX:], ((0, 0), (0, 128 - H_ssm)))
    zx = _matmul([(u0, w_zx)], name="ssm_in_zx")
    dtr = _matmul([(u0, w_dt)], name="ssm_in_dt")
    conv_w, conv_b = W["ssm_conv_w"][0], W["ssm_conv_b"][0]
    xbc = _conv_fwd(zx, conv_w, conv_b, DI, name="ssm_conv")
    dtrt = dtr[:, :H_ssm].T
    padh = lambda t: jnp.pad(t.reshape(1, H_ssm), ((0, 0), (0, 128 - H_ssm)))
    brow, bcol = padh(W["ssm_dt_bias"][0]), W["ssm_dt_bias"][0].reshape(H_ssm, 1)
    alrow, alcol = padh(W["ssm_a_log"][0]), W["ssm_a_log"][0].reshape(H_ssm, 1)
    dvec = jnp.repeat(W["ssm_d"][0], HEAD_DIM).reshape(1, DI)
    ssm_args = (xbc, dtr, dtrt, brow, bcol, alrow, alcol, dvec)
    y, states = _ssd_fwd(*ssm_args, name="ssd_fwd")
    sng = W["ssm_norm_g"].reshape(1, DI)
    yn = _gate_fwd(y, zx, sng, name="ssm_gate")
    w_out = W["ssm_out_proj"][0]
    h2 = _matmul([(yn, w_out)], res=h1, name="ssm_out")
    h3, sv_f01 = _ffn_fwd(h2, ng[0, 2], W["ffn_w1"][0, 1], W["ffn_w3"][0, 1], W["ffn_w2"][0, 1], "01")

    kv_in = _rms_fwd(h3, W["kv_norm_g"], name="kv_norm")
    kraw = _heads(_matmul([(kv_in, W["w_k"])], name="kv_k"), H_sb)
    vh = _heads(_matmul([(kv_in, W["w_v"])], out_dtype=BF16, name="kv_v"), H_sb)
    kh = _rms_fwd(kraw.reshape(H_sb * LP, HEAD_DIM), W["k_norm_g"], name="k_headnorm").reshape(H_sb, LP, HEAD_DIM)

    h4, sv_f10 = _ffn_fwd(h3, ng[1, 0], W["ffn_w1"][1, 0], W["ffn_w3"][1, 0], W["ffn_w2"][1, 0], "10")
    u1 = _rms_fwd(h4, ng[1, 1], name="sb_norm")
    qraw = _heads(_matmul([(u1, W["sb_w_q"][0])], name="sb_q"), H_sb)
    scale = HEAD_DIM ** -0.5
    qh = _rms_fwd(qraw.reshape(H_sb * LP, HEAD_DIM), W["sb_q_norm_g"][0], scale=scale,
                  name="q_headnorm").reshape(H_sb, LP, HEAD_DIM)
    zmax = 1.02 * math.sqrt(HEAD_DIM) * jnp.max(jnp.abs(W["sb_q_norm_g"])) * jnp.max(jnp.abs(W["k_norm_g"]))
    o, cmat = _sb_fwd(qh, kh, vh, jnp.full((1, LANES), zmax, F32), name="sb_fwd")
    kstart = cmat[:, ::ATT_BLOCK, LANES - 1].astype(jnp.int32)
    o_flat = _unheads(o)
    h5 = _matmul([(o_flat, W["sb_w_o"][0])], res=h4, name="sb_out")
    h6, sv_f11 = _ffn_fwd(h5, ng[1, 2], W["ffn_w1"][1, 1], W["ffn_w3"][1, 1], W["ffn_w2"][1, 1], "11")

    dh, lvec = _loss_head(h6, tgt_p, seq, name="loss_head")
    loss = jnp.sum(lvec)

    G = {}
    dng = [[None] * 3 for _ in range(2)]
    dw1 = [[None] * 2 for _ in range(2)]
    dw3 = [[None] * 2 for _ in range(2)]
    dw2 = [[None] * 2 for _ in range(2)]

    dh, dng[1][2], dw1[1][1], dw3[1][1], dw2[1][1] = _ffn_bwd(
        dh, sv_f11, ng[1, 2], W["ffn_w1"][1, 1], W["ffn_w3"][1, 1], W["ffn_w2"][1, 1], "11")
    shard_rows = lambda t: t.reshape(N_DEV, t.shape[0] // N_DEV, t.shape[1])
    g_wo = shard_rows(_matmul([(o_flat, dh)], trans_a=True, out_dtype=BF16, name="sb_dwo"))
    do = _heads(_matmul([(dh, W["sb_w_o"][0].T)], name="sb_do"), H_sb)
    dq, dk, dv = _sb_bwd(kstart, qh, kh, vh, cmat, do, name="sb_bwd")
    dqraw, dqg = _rms_bwd(qraw.reshape(H_sb * LP, HEAD_DIM), W["sb_q_norm_g"][0], dq.reshape(H_sb * LP, HEAD_DIM),
                          alpha=scale, name="q_headnorm_bwd")
    G["sb_q_norm_g"] = dqg
    dqraw = _unheads(dqraw.reshape(H_sb, LP, HEAD_DIM))
    g_wq = shard_rows(_matmul([(u1, dqraw)], trans_a=True, out_dtype=BF16, name="sb_dwq"))
    du1 = _matmul([(dqraw, W["sb_w_q"][0].T)], name="sb_du")
    dh, dng[1][1] = _rms_bwd(h4, ng[1, 1], du1, res=dh, name="sb_norm_bwd")
    dh, dng[1][0], dw1[1][0], dw3[1][0], dw2[1][0] = _ffn_bwd(
        dh, sv_f10, ng[1, 0], W["ffn_w1"][1, 0], W["ffn_w3"][1, 0], W["ffn_w2"][1, 0], "10")

    dkraw, dkg = _rms_bwd(kraw.reshape(H_sb * LP, HEAD_DIM), W["k_norm_g"], dk.reshape(H_sb * LP, HEAD_DIM),
                          name="k_headnorm_bwd")
    G["k_norm_g"] = dkg.reshape(-1)
    dkraw = _unheads(dkraw.reshape(H_sb, LP, HEAD_DIM))
    dvf = _unheads(dv)
    g_wk = shard_rows(_matmul([(kv_in, dkraw)], trans_a=True, out_dtype=BF16, name="kv_dwk"))
    g_wv = shard_rows(_matmul([(kv_in, dvf)], trans_a=True, out_dtype=BF16, name="kv_dwv"))
    dkv = _matmul([(dkraw, W["w_k"].T), (dvf, W["w_v"].T)], name="kv_din")
    dh, dkvg = _rms_bwd(h3, W["kv_norm_g"], dkv, res=dh, name="kv_norm_bwd")
    G["kv_norm_g"] = dkvg.reshape(-1)

    dh, dng[0][2], dw1[0][1], dw3[0][1], dw2[0][1] = _ffn_bwd(
        dh, sv_f01, ng[0, 2], W["ffn_w1"][0, 1], W["ffn_w3"][0, 1], W["ffn_w2"][0, 1], "01")
    g_wout = shard_rows(_matmul([(yn, dh)], trans_a=True, out_dtype=BF16, name="ssm_dwout"))
    dyn = _matmul([(dh, w_out.T)], name="ssm_dyn")
    dy, dz, dsng = _gate_bwd(y, zx, sng, dyn, name="ssm_gate_bwd")
    G["ssm_norm_g"] = dsng
    dxbc, ddtr, dbias, dalog, ddvec = _ssd_bwd(*ssm_args, dy, states, name="ssd_bwd")
    G["ssm_dt_bias"] = dbias[:, :H_ssm]
    G["ssm_a_log"] = dalog[:, :H_ssm]
    G["ssm_d"] = jnp.sum(ddvec.reshape(H_ssm, HEAD_DIM), axis=1).reshape(1, H_ssm)
    gpre, dcw, dcb = _conv_bwd_g(zx, conv_w, conv_b, dxbc, DI, name="ssm_conv_bwd_g")
    G["ssm_conv_w"] = dcw[:D_CONV][None]
    G["ssm_conv_b"] = dcb
    dxbc_pre = _conv_bwd_u(gpre, conv_w, name="ssm_conv_bwd_u")
    g_win = _deinterleave(jnp.concatenate([
        _matmul([(u0, dz)], trans_a=True, out_dtype=BF16, name="ssm_dwin_z"),
        _matmul([(u0, dxbc_pre)], trans_a=True, out_dtype=BF16, name="ssm_dwin_x"),
        _matmul([(u0, ddtr)], trans_a=True, out_dtype=BF16, name="ssm_dwin_dt")[:, :H_ssm]], axis=1),
        out_dtype=BF16, name="ssm_dwin_shards")
    du0 = _matmul([(dz, w_zx[:, :DI].T)], name="ssm_du_z")
    du0 = _matmul([(dxbc_pre, w_zx[:, DI:].T)], res=du0, name="ssm_du_x")
    du0 = _matmul([(ddtr, w_dt.T)], res=du0, name="ssm_du_dt")
    dh, dng[0][1] = _rms_bwd(h1, ng[0, 1], du0, res=dh, name="ssm_norm_bwd")
    dh, dng[0][0], dw1[0][0], dw3[0][0], dw2[0][0] = _ffn_bwd(
        dh, sv_f00, ng[0, 0], W["ffn_w1"][0, 0], W["ffn_w3"][0, 0], W["ffn_w2"][0, 0], "00")

    G["norm_g"] = jnp.stack([jnp.concatenate(r, axis=0) for r in dng])
    G["meta_tokens"] = dh[:N_META]
    flat = lambda t: [t[l][s] for l in range(2) for s in range(2)]
    groups = dict(
        cols=jnp.concatenate(flat(dw1) + flat(dw3), axis=1),
        rows=jnp.concatenate([g_wout, g_wk, g_wv, g_wq, g_wo], axis=1),
        w2=jnp.concatenate([shard_rows(t) for t in flat(dw2)], axis=1),
        win=g_win)
    return loss, dh[N_META:L], G, groups


WEIGHTS = ['meta_tokens', 'norm_g', 'ffn_w1', 'ffn_w3', 'ffn_w2', 'ssm_in_proj', 'ssm_conv_w', 'ssm_conv_b',
           'ssm_dt_bias', 'ssm_a_log', 'ssm_d', 'ssm_norm_g', 'ssm_out_proj', 'kv_norm_g', 'w_k', 'k_norm_g', 'w_v',
           'sb_w_q', 'sb_q_norm_g', 'sb_w_o']
SHARD_AXIS = {'meta_tokens': 1, 'norm_g': 2, 'ffn_w1': 3, 'ffn_w3': 3, 'ffn_w2': 2, 'ssm_in_proj': 2, 'ssm_conv_w': 2,
              'ssm_conv_b': 1, 'ssm_dt_bias': None, 'ssm_a_log': None, 'ssm_d': None, 'ssm_norm_g': 1,
              'ssm_out_proj': 1, 'kv_norm_g': None, 'w_k': 0, 'k_norm_g': None, 'w_v': 0, 'sb_w_q': 1,
              'sb_q_norm_g': None, 'sb_w_o': 1}
MATMUL_WEIGHTS = ('ffn_w1', 'ffn_w3', 'ffn_w2', 'ssm_in_proj', 'ssm_out_proj', 'w_k', 'w_v', 'sb_w_q', 'sb_w_o')
PACK_ROWS = 16


def _pack(arrs, dtype):
    flat = jnp.concatenate([a.reshape(-1).astype(dtype) for a in arrs])
    n = flat.shape[0]
    npad = -(-n // (LANES * PACK_ROWS)) * (LANES * PACK_ROWS)
    return jnp.pad(flat, (0, npad - n)).reshape(npad // LANES, LANES)


def _unpack_gathered(buf, names, shard_shapes, dtype):
    flat = buf.reshape(N_DEV, -1)
    out, off = {}, 0
    for n in names:
        shp = shard_shapes[n]
        size = math.prod(shp)
        t = flat[:, off:off + size].reshape((N_DEV,) + tuple(shp))
        off += size
        ax = SHARD_AXIS[n]
        t = jnp.moveaxis(t, 0, ax)
        full = shp[:ax] + (N_DEV * shp[ax],) + shp[ax + 1:]
        out[n] = t.reshape(full).astype(dtype)
    return out


def _weight_groups(w1, w3, w2, win, wout, wk, wv, wq, wo):
    return dict(
        cols=jnp.concatenate([w1.reshape(-1, w1.shape[-1]), w3.reshape(-1, w3.shape[-1])], axis=0),
        rows=jnp.concatenate([wout[0], wk, wv, wq[0], wo[0]], axis=0),
        w2=w2.reshape(-1, w2.shape[-1]),
        win=win[0])


def _to_shards(g, ax):
    shp = g.shape
    t = g.reshape(shp[:ax] + (N_DEV, shp[ax] // N_DEV) + shp[ax + 1:])
    return jnp.moveaxis(t, ax, 0).reshape(N_DEV, -1)


def kernel(x, meta_tokens, norm_g, ffn_w1, ffn_w3, ffn_w2, ssm_in_proj, ssm_conv_w, ssm_conv_b, ssm_dt_bias, ssm_a_log, ssm_d, ssm_norm_g, ssm_out_proj, kv_norm_g, w_k, k_norm_g, w_v, sb_w_q, sb_q_norm_g, sb_w_o, loss_target, m_meta_tokens, m_norm_g, m_ffn_w1, m_ffn_w3, m_ffn_w2, m_ssm_in_proj, m_ssm_conv_w, m_ssm_conv_b, m_ssm_dt_bias, m_ssm_a_log, m_ssm_d, m_ssm_norm_g, m_ssm_out_proj, m_kv_norm_g, m_w_k, m_k_norm_g, m_w_v, m_sb_w_q, m_sb_q_norm_g, m_sb_w_o, v_meta_tokens, v_norm_g, v_ffn_w1, v_ffn_w3, v_ffn_w2, v_ssm_in_proj, v_ssm_conv_w, v_ssm_conv_b, v_ssm_dt_bias, v_ssm_a_log, v_ssm_d, v_ssm_norm_g, v_ssm_out_proj, v_kv_norm_g, v_w_k, v_k_norm_g, v_w_v, v_sb_w_q, v_sb_q_norm_g, v_sb_w_o):
    shard = dict(meta_tokens=meta_tokens, norm_g=norm_g, ffn_w1=ffn_w1, ffn_w3=ffn_w3, ffn_w2=ffn_w2,
                 ssm_in_proj=ssm_in_proj, ssm_conv_w=ssm_conv_w, ssm_conv_b=ssm_conv_b, ssm_dt_bias=ssm_dt_bias,
                 ssm_a_log=ssm_a_log, ssm_d=ssm_d, ssm_norm_g=ssm_norm_g, ssm_out_proj=ssm_out_proj,
                 kv_norm_g=kv_norm_g, w_k=w_k, k_norm_g=k_norm_g, w_v=w_v, sb_w_q=sb_w_q, sb_q_norm_g=sb_q_norm_g,
                 sb_w_o=sb_w_o)
    mom_m = dict(zip(WEIGHTS, (m_meta_tokens, m_norm_g, m_ffn_w1, m_ffn_w3, m_ffn_w2, m_ssm_in_proj, m_ssm_conv_w,
                               m_ssm_conv_b, m_ssm_dt_bias, m_ssm_a_log, m_ssm_d, m_ssm_norm_g, m_ssm_out_proj,
                               m_kv_norm_g, m_w_k, m_k_norm_g, m_w_v, m_sb_w_q, m_sb_q_norm_g, m_sb_w_o)))
    mom_v = dict(zip(WEIGHTS, (v_meta_tokens, v_norm_g, v_ffn_w1, v_ffn_w3, v_ffn_w2, v_ssm_in_proj, v_ssm_conv_w,
                               v_ssm_conv_b, v_ssm_dt_bias, v_ssm_a_log, v_ssm_d, v_ssm_norm_g, v_ssm_out_proj,
                               v_kv_norm_g, v_w_k, v_k_norm_g, v_w_v, v_sb_w_q, v_sb_q_norm_g, v_sb_w_o)))
    sharded = [n for n in WEIGHTS if SHARD_AXIS[n] is not None]
    replicated = [n for n in WEIGHTS if SHARD_AXIS[n] is None]
    small = [n for n in sharded if n not in MATMUL_WEIGHTS]
    shapes = {n: tuple(shard[n].shape) for n in WEIGHTS}
    D = x.shape[-1]
    F8 = ffn_w1.shape[-1]
    DI8 = ssm_out_proj.shape[1]
    D8 = w_k.shape[0]

    bf = lambda t: t.astype(BF16)
    src = _weight_groups(bf(ffn_w1), bf(ffn_w3), bf(ffn_w2), bf(ssm_in_proj), bf(ssm_out_proj), bf(w_k), bf(w_v),
                         bf(sb_w_q), bf(sb_w_o))
    got = {k: _exchange(v, scatter=False, name=f"gather_{k}") for k, v in src.items()}
    gathered_small = _exchange(_pack([shard[n] for n in small], F32), scatter=False, name="gather_gains")
    W = dict(_unpack_gathered(gathered_small, small, shapes, F32))
    for n in replicated:
        W[n] = shard[n]
    w13 = _interleave(got["cols"], name="weights_cols")
    W["ffn_w1"] = w13[:4 * D].reshape(2, 2, D, N_DEV * F8)
    W["ffn_w3"] = w13[4 * D:].reshape(2, 2, D, N_DEV * F8)
    W["ffn_w2"] = jnp.stack([_row_gather(got["w2"], m * F8, F8, name=f"weights_w2_{m}")
                             for m in range(4)]).reshape(2, 2, N_DEV * F8, D)
    W["ssm_in_proj"] = _interleave(got["win"], name="weights_win")[None]
    W["ssm_out_proj"] = _row_gather(got["rows"], 0, DI8, name="weights_wout")[None]
    W["w_k"] = _row_gather(got["rows"], DI8, D8, name="weights_wk")
    W["w_v"] = _row_gather(got["rows"], DI8 + D8, D8, name="weights_wv")
    W["sb_w_q"] = _row_gather(got["rows"], DI8 + 2 * D8, D8, name="weights_wq")[None]
    W["sb_w_o"] = _row_gather(got["rows"], DI8 + 3 * D8, D8, name="weights_wo")[None]

    loss, dx, G, groups = _local_step(x[0], loss_target[0], W)
    loss = lax.psum(loss, ("x", "y", "c"))

    summed = {k: _sum_rows(_exchange(v, scatter=True, name=f"scatter_{k}"), name=f"sum_{k}") for k, v in groups.items()}
    send = jnp.concatenate([_to_shards(G[n], SHARD_AXIS[n]) for n in small], axis=1)
    n_el = send.shape[1]
    npad = -(-n_el // (LANES * PACK_ROWS)) * (LANES * PACK_ROWS)
    send = jnp.pad(send, ((0, 0), (0, npad - n_el))).reshape(N_DEV, npad // LANES, LANES)
    small_sum = _sum_rows(_exchange(send, scatter=True, name="scatter_gains"), name="sum_gains").reshape(-1)
    rep = _pack([G[n] for n in replicated], F32)
    rep_sum = _sum_rows(_exchange(rep, scatter=False, name="gather_small_grads"), name="sum_small_grads").reshape(-1)

    grads, off = {}, 0
    for n in small:
        size = math.prod(shapes[n])
        grads[n] = small_sum[off:off + size].reshape(shapes[n])
        off += size
    off = 0
    for n in replicated:
        size = math.prod(shapes[n])
        grads[n] = rep_sum[off:off + size].reshape(shapes[n])
        off += size
    grads["ffn_w1"] = summed["cols"][:4 * D].reshape(shapes["ffn_w1"])
    grads["ffn_w3"] = summed["cols"][4 * D:].reshape(shapes["ffn_w3"])
    grads["ffn_w2"] = summed["w2"].reshape(shapes["ffn_w2"])
    grads["ssm_in_proj"] = summed["win"][None]
    rows = summed["rows"]
    grads["ssm_out_proj"] = rows[:DI8][None]
    grads["w_k"] = rows[DI8:DI8 + D8]
    grads["w_v"] = rows[DI8 + D8:DI8 + 2 * D8]
    grads["sb_w_q"] = rows[DI8 + 2 * D8:DI8 + 3 * D8][None]
    grads["sb_w_o"] = rows[DI8 + 3 * D8:][None]

    delta, new_m, new_v = {}, {}, {}
    for n in WEIGHTS:
        w2 = shard[n].reshape(1, -1) if shard[n].ndim == 1 else shard[n]
        r2 = lambda t: t.reshape(w2.shape)
        d, nm, nv = _adamw(w2, r2(grads[n]), r2(mom_m[n]), r2(mom_v[n]), name=f"adamw_{n}")
        delta[n], new_m[n], new_v[n] = (t.reshape(shapes[n]) for t in (d, nm, nv))

    return (loss, dx[None], *[grads[n] for n in WEIGHTS], *[delta[n] for n in WEIGHTS],
            *[new_m[n] for n in WEIGHTS], *[new_v[n] for n in WEIGHTS])
```

```python
import functools
import math

import jax
import jax.numpy as jnp
from jax import lax
from jax.experimental import pallas as pl
from jax.experimental.pallas import tpu as pltpu

F32 = jnp.float32
BF16 = jnp.bfloat16
RMS_EPS = 1e-6
N_META = 16
HEAD_DIM = 64
SSM_GROUPS = 8
D_STATE = 128
D_CONV = 4
FFN_RES = 0.5
ADAM_LR, ADAM_B1, ADAM_B2, ADAM_EPS, ADAM_WD, ADAM_STEP = 0.001, 0.9, 0.999, 1e-08, 0.01, 10
N_DEV = 8
SSD_CHUNK = 128
ATT_BLOCK = 256
ROW_ALIGN = 768
VMEM_LIMIT_V7X = 48 * 1024 * 1024
MESH = pl.DeviceIdType.MESH
LANES = 128


def _cp(*sem):
    return pltpu.CompilerParams(dimension_semantics=sem if sem else None, vmem_limit_bytes=VMEM_LIMIT_V7X)


def _tile(n, cands):
    for c in cands:
        if n % c == 0:
            return c
    return n


def _softplus(x):
    return jnp.maximum(x, 0.0) + jnp.log(1.0 + jnp.exp(-jnp.abs(x)))


def _sigmoid(x):
    return 1.0 / (1.0 + jnp.exp(-x))


def _split3(x):
    hi = x.astype(BF16)
    r1 = x - hi.astype(F32)
    mid = r1.astype(BF16)
    lo = (r1 - mid.astype(F32)).astype(BF16)
    return hi, mid, lo


def _dot(a, b):
    return jnp.dot(a, b, preferred_element_type=F32)


def _dot_nt(a, b):
    return lax.dot_general(a, b, (((1,), (1,)), ((), ())), preferred_element_type=F32)


def _dot_tn(a, b):
    return lax.dot_general(a, b, (((0,), (0,)), ((), ())), preferred_element_type=F32)


def _dot3_left(t_bf16, x):
    hi, mid, lo = _split3(x)
    return _dot(t_bf16, hi) + _dot(t_bf16, mid) + _dot(t_bf16, lo)


def _dot3_right(x, t_bf16):
    hi, mid, lo = _split3(x)
    return _dot(hi, t_bf16) + _dot(mid, t_bf16) + _dot(lo, t_bf16)


CARRIER_MIN_FLOP = 4e10


class _Carry:
    def __init__(self, arrs, scatter, done):
        self.arrs, self.scatter, self.done = list(arrs), scatter, done


class _Queue:
    def __init__(self):
        self.items = []

    def push(self, arrs, scatter, done):
        self.items.append(_Carry(arrs, scatter, done))

    def pop(self):
        return self.items.pop(0) if self.items else None

    def flush(self, name):
        k = 0
        while self.items:
            it = self.items.pop(0)
            it.done(_exchange(it.arrs, scatter=it.scatter, name=f"{name}_{k}"))
            k += 1


class _Weights(dict):
    def __init__(self, xq):
        super().__init__()
        self.xq, self.fetched = xq, 0

    def __missing__(self, key):
        while not dict.__contains__(self, key) and self.xq.items:
            it = self.xq.items.pop(0)
            it.done(_exchange(it.arrs, scatter=it.scatter, name=f"gather_now_{self.fetched}"))
            self.fetched += 1
        return dict.__getitem__(self, key)


def _xchg_shapes(arrs, scatter):
    return [jax.ShapeDtypeStruct((N_DEV,) + tuple(a.shape[1:] if scatter else a.shape), a.dtype) for a in arrs]


def _xchg_scratch(n):
    return [pltpu.SemaphoreType.DMA((n, N_DEV - 1)), pltpu.SemaphoreType.DMA((n, N_DEV - 1)),
            pltpu.SemaphoreType.DMA((n,))]


def _xchg_copies(srcs, dsts, send_sems, recv_sems, local_sems, scatter, with_recv):
    x, y, c, me = _mesh_pos()
    own, sends, recvs = [], [], []
    for a, (s, d) in enumerate(zip(srcs, dsts)):
        own.append(pltpu.make_async_copy(s.at[me] if scatter else s, d.at[me], local_sems.at[a]))
        for f in range(1, N_DEV):
            peer, pid = _peer(x, y, c, f)
            for row, lst in ((me, sends), (pid, recvs)) if with_recv else ((me, sends),):
                lst.append(pltpu.make_async_remote_copy(
                    src_ref=s.at[pid] if scatter else s, dst_ref=d.at[row], send_sem=send_sems.at[a, f - 1],
                    recv_sem=recv_sems.at[a, f - 1], device_id=peer, device_id_type=MESH))
    return own, sends, recvs


def _xchg_start(*a):
    own, sends, _ = _xchg_copies(*a, with_recv=False)
    for cp in own + sends:
        cp.start()


def _xchg_wait(*a):
    own, sends, recvs = _xchg_copies(*a, with_recv=True)
    for snd, rcv in zip(sends, recvs):
        snd.wait_send()
        rcv.wait_recv()
    for cp in own:
        cp.wait()


def _call(body, *, name, grid, in_specs, out_specs, out_shape, scratch_shapes=(), sem, args, carry=None):
    n_in, n_out, n_scr = len(in_specs), len(out_specs), len(scratch_shapes)
    if carry is None:
        return pl.pallas_call(
            body, name=name, grid=grid, in_specs=list(in_specs), out_specs=list(out_specs), out_shape=list(out_shape),
            scratch_shapes=list(scratch_shapes), compiler_params=_cp(*sem))(*args)
    n = len(carry.arrs)
    hbm = pl.BlockSpec(memory_space=pl.ANY)

    def wrapped(*refs):
        ins, csrc = refs[:n_in], refs[n_in:n_in + n]
        outs, cdst = refs[n_in + n:n_in + n + n_out], refs[n_in + n + n_out:n_in + 2 * n + n_out]
        scr = refs[n_in + 2 * n + n_out:]
        xa = (csrc, cdst, *scr[n_scr:], carry.scatter)
        pid = [pl.program_id(a) for a in range(len(grid))]
        first = functools.reduce(jnp.logical_and, [p == 0 for p in pid])
        last = functools.reduce(jnp.logical_and, [p == g - 1 for p, g in zip(pid, grid)])

        @pl.when(first)
        def _():
            _xchg_start(*xa)

        body(*ins, *outs, *scr[:n_scr])

        @pl.when(last)
        def _():
            _xchg_wait(*xa)

    res = pl.pallas_call(
        wrapped, name=name, grid=grid, in_specs=list(in_specs) + [hbm] * n, out_specs=list(out_specs) + [hbm] * n,
        out_shape=list(out_shape) + _xchg_shapes(carry.arrs, carry.scatter),
        scratch_shapes=list(scratch_shapes) + _xchg_scratch(n), compiler_params=_cp(*["arbitrary"] * len(grid)),
    )(*args, *carry.arrs)
    carry.done(list(res[n_out:]))
    return list(res[:n_out])


def _matmul(pairs, *, name, out_dtype=F32, trans_a=False, res=None, alpha=1.0, shards=None, xq=None,
            tm=None, tn=None, tk=None):
    a0, b0 = pairs[0]
    if trans_a:
        K, M = a0.shape
    else:
        M, K = a0.shape
    N = b0.shape[1]
    tm = tm or _tile(M, (768, 512, 1408, 384, 256, 128))
    tn = tn or (N if shards else _tile(N, (512, 1408, 384, 256, 128)))
    tk = tk or _tile(K, (1024, 1408, 768, 512, 256, 128))
    nk = K // tk
    npair = len(pairs)
    has_res = res is not None
    cs = N // shards if shards else None

    def body(*refs):
        o_ref, acc = refs[-2], refs[-1]
        k = pl.program_id(2)

        @pl.when(k == 0)
        def _():
            acc[...] = jnp.zeros_like(acc)

        part = None
        for p in range(npair):
            a = refs[2 * p][...].astype(BF16)
            b = refs[2 * p + 1][...].astype(BF16)
            d = _dot_tn(a, b) if trans_a else _dot(a, b)
            part = d if part is None else part + d
        acc[...] += part

        @pl.when(k == nk - 1)
        def _():
            if shards:
                for d in range(shards):
                    v = acc[:, d * cs:(d + 1) * cs]
                    o_ref[d] = (v * alpha if alpha != 1.0 else v).astype(out_dtype)
                return
            v = acc[...]
            if alpha != 1.0:
                v = v * alpha
            if has_res:
                v = refs[2 * npair][...] + v
            o_ref[...] = v.astype(out_dtype)

    if trans_a:
        a_spec = pl.BlockSpec((tk, tm), lambda i, j, k: (k, i))
    else:
        a_spec = pl.BlockSpec((tm, tk), lambda i, j, k: (i, k))
    b_spec = pl.BlockSpec((tk, tn), lambda i, j, k: (k, j))
    if shards:
        assert not has_res and tn == N
        o_spec = pl.BlockSpec((shards, tm, cs), lambda i, j, k: (0, i, 0))
        out_shape = jax.ShapeDtypeStruct((shards, M, cs), out_dtype)
    else:
        o_spec = pl.BlockSpec((tm, tn), lambda i, j, k: (i, j))
        out_shape = jax.ShapeDtypeStruct((M, N), out_dtype)
    in_specs, args = [], []
    for a, b in pairs:
        in_specs += [a_spec, b_spec]
        args += [a, b]
    if has_res:
        in_specs.append(o_spec)
        args.append(res)
    carry = xq.pop() if (xq is not None and 2.0 * npair * M * N * K >= CARRIER_MIN_FLOP) else None
    return _call(body, name=name, grid=(M // tm, N // tn, nk), in_specs=in_specs, out_specs=[o_spec],
                 out_shape=[out_shape], scratch_shapes=[pltpu.VMEM((tm, tn), F32)],
                 sem=("parallel", "parallel", "arbitrary"), args=args, carry=carry)[0]


def _rms_fwd(h, g, *, name, scale=1.0):
    R, D = h.shape
    tr = _tile(R, (2048, 1024, 768, 512, 256, 128)) if D <= 128 else _tile(R, (384, 256, 128))

    def body(h_ref, g_ref, o_ref):
        x = h_ref[...]
        r = lax.rsqrt(jnp.mean(x * x, axis=1, keepdims=True) + RMS_EPS)
        y = x * r * g_ref[...]
        if scale != 1.0:
            y = y * scale
        o_ref[...] = y.astype(BF16)

    return pl.pallas_call(
        body, name=name, grid=(R // tr,),
        in_specs=[pl.BlockSpec((tr, D), lambda i: (i, 0)), pl.BlockSpec((1, D), lambda i: (0, 0))],
        out_specs=pl.BlockSpec((tr, D), lambda i: (i, 0)),
        out_shape=jax.ShapeDtypeStruct((R, D), BF16), compiler_params=_cp("parallel"),
    )(h, g.reshape(1, D))


def _rms_bwd(h, g, dn, res=None, *, name, alpha=1.0):
    R, D = h.shape
    tr = _tile(R, (2048, 1024, 768, 512, 256, 128)) if D <= 128 else _tile(R, (384, 256, 128))
    has_res = res is not None

    def body(*refs):
        h_ref, g_ref, dn_ref = refs[:3]
        dh_ref, dg_ref = refs[-2], refs[-1]
        i = pl.program_id(0)

        @pl.when(i == 0)
        def _():
            dg_ref[...] = jnp.zeros_like(dg_ref)

        x = h_ref[...]
        r = lax.rsqrt(jnp.mean(x * x, axis=1, keepdims=True) + RMS_EPS)
        xh = x * r
        d = dn_ref[...].astype(F32)
        if alpha != 1.0:
            d = d * alpha
        dng = d * g_ref[...]
        m = jnp.mean(dng * xh, axis=1, keepdims=True)
        dh = r * (dng - xh * m)
        if has_res:
            dh = dh + refs[3][...]
        dh_ref[...] = dh
        dg_ref[...] += jnp.sum(d * xh, axis=0, keepdims=True)

    row = pl.BlockSpec((tr, D), lambda i: (i, 0))
    vec = pl.BlockSpec((1, D), lambda i: (0, 0))
    in_specs = [row, vec, row] + ([row] if has_res else [])
    args = [h, g.reshape(1, D), dn] + ([res] if has_res else [])
    return pl.pallas_call(
        body, name=name, grid=(R // tr,), in_specs=in_specs, out_specs=[row, vec],
        out_shape=[jax.ShapeDtypeStruct((R, D), F32), jax.ShapeDtypeStruct((1, D), F32)],
        compiler_params=_cp("arbitrary"),
    )(*args)


def _ffn_up(n, w1, w3, *, name, xq=None):
    M, K = n.shape
    N = w1.shape[1]
    tm = _tile(M, (384, 256, 128))
    tn = _tile(N, (1408, 512, 256, 128))

    def body(n_ref, w1_ref, w3_ref, a_ref, b_ref, s_ref):
        x = n_ref[...]
        a = _dot(x, w1_ref[...])
        b = _dot(x, w3_ref[...])
        a_ref[...] = a.astype(BF16)
        b_ref[...] = b.astype(BF16)
        s_ref[...] = (a * _sigmoid(a) * b).astype(BF16)

    o_spec = pl.BlockSpec((tm, tn), lambda j, i: (i, j))
    w_spec = pl.BlockSpec((K, tn), lambda j, i: (0, j))
    sh = jax.ShapeDtypeStruct((M, N), BF16)
    return _call(body, name=name, grid=(N // tn, M // tm),
                 in_specs=[pl.BlockSpec((tm, K), lambda j, i: (i, 0)), w_spec, w_spec],
                 out_specs=[o_spec, o_spec, o_spec], out_shape=[sh, sh, sh], sem=("parallel", "parallel"),
                 args=(n, w1, w3), carry=xq.pop() if xq is not None else None)


def _ffn_mid_bwd(dh, w2t, a, b, *, name, xq=None):
    M, K = dh.shape
    N = w2t.shape[1]
    tm = _tile(M, (384, 256, 128))
    tn = _tile(N, (1408, 512, 256, 128))

    def body(dh_ref, w_ref, a_ref, b_ref, da_ref, db_ref):
        ds = _dot(dh_ref[...].astype(BF16), w_ref[...]) * FFN_RES
        av = a_ref[...].astype(F32)
        bv = b_ref[...].astype(F32)
        sg = _sigmoid(av)
        da_ref[...] = (ds * bv * sg * (1.0 + av * (1.0 - sg))).astype(BF16)
        db_ref[...] = (ds * av * sg).astype(BF16)

    o_spec = pl.BlockSpec((tm, tn), lambda j, i: (i, j))
    sh = jax.ShapeDtypeStruct((M, N), BF16)
    return _call(body, name=name, grid=(N // tn, M // tm),
                 in_specs=[pl.BlockSpec((tm, K), lambda j, i: (i, 0)), pl.BlockSpec((K, tn), lambda j, i: (0, j)),
                           o_spec, o_spec],
                 out_specs=[o_spec, o_spec], out_shape=[sh, sh], sem=("parallel", "parallel"),
                 args=(dh, w2t, a, b), carry=xq.pop() if xq is not None else None)


def _conv_pre(xx, w_ref, b_ref, tr):
    acc = None
    for k in range(D_CONV):
        sh = D_CONV - 1 - k
        v = (pltpu.roll(xx, sh, 0) if sh else xx)[8:8 + tr]
        t = w_ref[k:k + 1, :] * v
        acc = t if acc is None else acc + t
    return acc + b_ref[...]


def _conv_fwd(zx, w, b, col_off, *, name):
    LP = zx.shape[0]
    C = w.shape[1]
    tr = _tile(LP, (256, 128))
    tc = _tile(C, (512, 256, 128))
    co = col_off // tc

    def body(cur_ref, prev_ref, w_ref, b_ref, o_ref):
        i = pl.program_id(0)
        prev = jnp.where(i == 0, 0.0, prev_ref[...])
        pre = _conv_pre(jnp.concatenate([prev, cur_ref[...]], axis=0), w_ref, b_ref, tr)
        o_ref[...] = pre * _sigmoid(pre)

    return pl.pallas_call(
        body, name=name, grid=(LP // tr, C // tc),
        in_specs=[pl.BlockSpec((tr, tc), lambda i, j: (i, j + co)),
                  pl.BlockSpec((8, tc), lambda i, j: (jnp.maximum(i * (tr // 8) - 1, 0), j + co)),
                  pl.BlockSpec((D_CONV, tc), lambda i, j: (0, j)), pl.BlockSpec((1, tc), lambda i, j: (0, j))],
        out_specs=pl.BlockSpec((tr, tc), lambda i, j: (i, j)),
        out_shape=jax.ShapeDtypeStruct((LP, C), F32), compiler_params=_cp("parallel", "parallel"),
    )(zx, zx, w, b.reshape(1, C))


def _conv_bwd_g(zx, w, b, dact, col_off, *, name):
    LP = zx.shape[0]
    C = w.shape[1]
    tr = _tile(LP, (256, 128))
    tc = _tile(C, (512, 256, 128))
    co = col_off // tc

    def body(cur_ref, prev_ref, w_ref, b_ref, d_ref, g_ref, dw_ref, db_ref):
        i = pl.program_id(1)

        @pl.when(i == 0)
        def _():
            dw_ref[...] = jnp.zeros_like(dw_ref)
            db_ref[...] = jnp.zeros_like(db_ref)

        prev = jnp.where(i == 0, 0.0, prev_ref[...])
        xx = jnp.concatenate([prev, cur_ref[...]], axis=0)
        pre = _conv_pre(xx, w_ref, b_ref, tr)
        sg = _sigmoid(pre)
        g = d_ref[...] * sg * (1.0 + pre * (1.0 - sg))
        g_ref[...] = g
        db_ref[...] += jnp.sum(g, axis=0, keepdims=True)
        rows = []
        for k in range(D_CONV):
            sh = D_CONV - 1 - k
            v = (pltpu.roll(xx, sh, 0) if sh else xx)[8:8 + tr]
            rows.append(jnp.sum(g * v, axis=0, keepdims=True))
        rows.append(jnp.zeros((8 - D_CONV, tc), F32))
        dw_ref[...] += jnp.concatenate(rows, axis=0)

    return pl.pallas_call(
        body, name=name, grid=(C // tc, LP // tr),
        in_specs=[pl.BlockSpec((tr, tc), lambda j, i: (i, j + co)),
                  pl.BlockSpec((8, tc), lambda j, i: (jnp.maximum(i * (tr // 8) - 1, 0), j + co)),
                  pl.BlockSpec((D_CONV, tc), lambda j, i: (0, j)), pl.BlockSpec((1, tc), lambda j, i: (0, j)),
                  pl.BlockSpec((tr, tc), lambda j, i: (i, j))],
        out_specs=[pl.BlockSpec((tr, tc), lambda j, i: (i, j)), pl.BlockSpec((8, tc), lambda j, i: (0, j)),
                   pl.BlockSpec((1, tc), lambda j, i: (0, j))],
        out_shape=[jax.ShapeDtypeStruct((LP, C), F32), jax.ShapeDtypeStruct((8, C), F32), jax.ShapeDtypeStruct((1, C), F32)],
        compiler_params=_cp("parallel", "arbitrary"),
    )(zx, zx, w, b.reshape(1, C), dact)


def _conv_bwd_u(g, w, *, name):
    LP, C = g.shape
    tr = _tile(LP, (256, 128))
    tc = _tile(C, (512, 256, 128))
    nb = LP // tr

    def body(cur_ref, nxt_ref, w_ref, o_ref):
        i = pl.program_id(0)
        nxt = jnp.where(i == nb - 1, 0.0, nxt_ref[...])
        xx = jnp.concatenate([cur_ref[...], nxt], axis=0)
        acc = None
        for k in range(D_CONV):
            sh = D_CONV - 1 - k
            v = (pltpu.roll(xx, tr + 8 - sh, 0) if sh else xx)[:tr]
            t = w_ref[k:k + 1, :] * v
            acc = t if acc is None else acc + t
        o_ref[...] = acc

    return pl.pallas_call(
        body, name=name, grid=(nb, C // tc),
        in_specs=[pl.BlockSpec((tr, tc), lambda i, j: (i, j)),
                  pl.BlockSpec((8, tc), lambda i, j: (jnp.minimum((i + 1) * (tr // 8), LP // 8 - 1), j)),
                  pl.BlockSpec((D_CONV, tc), lambda i, j: (0, j))],
        out_specs=pl.BlockSpec((tr, tc), lambda i, j: (i, j)),
        out_shape=jax.ShapeDtypeStruct((LP, C), F32), compiler_params=_cp("parallel", "parallel"),
    )(g, g, w)


def _ssd_prelude(dtr_ref, dtrt_ref, brow_ref, bcol_ref, alrow_ref, alcol_ref, Q):
    ii = lax.broadcasted_iota(jnp.int32, (Q, Q), 0)
    jj = lax.broadcasted_iota(jnp.int32, (Q, Q), 1)
    tril = ii >= jj
    dt_col = _softplus(dtr_ref[...] + brow_ref[...])
    a_row_p = -jnp.exp(alrow_ref[...])
    dt_row = _softplus(dtrt_ref[...] + bcol_ref[...])
    a_col_p = -jnp.exp(alcol_ref[...])
    cum_col = _dot3_left(tril.astype(BF16), dt_col * a_row_p)
    cum_row = _dot3_right(dt_row * a_col_p, (ii <= jj).astype(BF16))
    return ii, jj, tril, dt_col, dt_row, a_row_p, cum_col, cum_row


def _col_of(mat, lane_idx, h):
    return jnp.sum(jnp.where(lane_idx == h, mat, 0.0), axis=1, keepdims=True)


def _ssd_fwd(xbc, dtr, dtrt, brow, bcol, alrow, alcol, dvec, *, name):
    LP = xbc.shape[0]
    Q = SSD_CHUNK
    nc = LP // Q
    G = SSM_GROUPS
    DI = dvec.shape[1]
    gw = DI // G
    hpg = gw // HEAD_DIM
    H = G * hpg
    boff, coff = DI, DI + G * D_STATE

    def body(xbc_ref, dtr_ref, dtrt_ref, brow_ref, bcol_ref, alrow_ref, alcol_ref, dvec_ref, y_ref, st_ref, state):
        c = pl.program_id(0)

        @pl.when(c == 0)
        def _():
            state[...] = jnp.zeros_like(state)

        st_ref[...] = state[...]
        ii, jj, tril, dt_col, dt_row, _, cum_col, cum_row = _ssd_prelude(
            dtr_ref, dtrt_ref, brow_ref, bcol_ref, alrow_ref, alcol_ref, Q)
        lane_h = lax.broadcasted_iota(jnp.int32, (Q, 128), 1)
        lane_g = lax.broadcasted_iota(jnp.int32, (Q, gw), 1) // HEAD_DIM
        for g in range(G):
            xg = xbc_ref[:, g * gw:(g + 1) * gw]
            bb = xbc_ref[:, boff + g * D_STATE: boff + (g + 1) * D_STATE].astype(BF16)
            cb = xbc_ref[:, coff + g * D_STATE: coff + (g + 1) * D_STATE].astype(BF16)
            gm = _dot_nt(cb, bb)
            sg = state[g]
            yoff = _dot(cb, sg.astype(BF16))
            ydiag = jnp.zeros((Q, gw), F32)
            esc = jnp.zeros((Q, gw), F32)
            wsc = jnp.zeros((Q, gw), F32)
            lam = jnp.zeros((1, gw), F32)
            for j in range(hpg):
                h = g * hpg + j
                ccol = _col_of(cum_col, lane_h, h)
                dcol = _col_of(dt_col, lane_h, h)
                seg = ccol - cum_row[h:h + 1, :]
                decay = jnp.exp(jnp.where(tril, seg, -jnp.inf))
                mh = gm * decay * dt_row[h:h + 1, :]
                hm = lane_g == j
                ydiag = ydiag + _dot(mh.astype(BF16), jnp.where(hm, xg, 0.0).astype(BF16))
                tot = ccol[Q - 1:Q, :]
                esc = jnp.where(hm, jnp.exp(ccol), esc)
                wsc = jnp.where(hm, jnp.exp(tot - ccol) * dcol, wsc)
                lam = jnp.where(hm[0:1], jnp.exp(tot), lam)
            y_ref[:, g * gw:(g + 1) * gw] = ydiag + yoff * esc + dvec_ref[:, g * gw:(g + 1) * gw] * xg
            state[g] = sg * lam + _dot_tn(bb, (xg * wsc).astype(BF16))

    W = xbc.shape[1]
    full = lambda shape: pl.BlockSpec(shape, lambda c: (0,) * len(shape))
    return pl.pallas_call(
        body, name=name, grid=(nc,),
        in_specs=[pl.BlockSpec((Q, W), lambda c: (c, 0)), pl.BlockSpec((Q, 128), lambda c: (c, 0)),
                  pl.BlockSpec((H, Q), lambda c: (0, c)), full((1, 128)), full((H, 1)), full((1, 128)), full((H, 1)),
                  full((1, DI))],
        out_specs=[pl.BlockSpec((Q, DI), lambda c: (c, 0)), pl.BlockSpec((None, G, D_STATE, gw), lambda c: (c, 0, 0, 0))],
        out_shape=[jax.ShapeDtypeStruct((LP, DI), F32), jax.ShapeDtypeStruct((nc, G, D_STATE, gw), F32)],
        scratch_shapes=[pltpu.VMEM((G, D_STATE, gw), F32)],
        compiler_params=_cp("arbitrary"),
    )(xbc, dtr, dtrt, brow, bcol, alrow, alcol, dvec)


def _ssd_bwd(xbc, dtr, dtrt, brow, bcol, alrow, alcol, dvec, dy, states, *, name):
    LP = xbc.shape[0]
    Q = SSD_CHUNK
    nc = LP // Q
    G = SSM_GROUPS
    DI = dvec.shape[1]
    gw = DI // G
    hpg = gw // HEAD_DIM
    H = G * hpg
    boff, coff = DI, DI + G * D_STATE
    W = xbc.shape[1]

    def body(xbc_ref, dtr_ref, dtrt_ref, brow_ref, bcol_ref, alrow_ref, alcol_ref, dvec_ref, dy_ref, st_ref,
             dxbc_ref, ddtr_ref, dbias_ref, dalog_ref, ddvec_ref, dstate):
        c = pl.program_id(0)

        @pl.when(c == 0)
        def _():
            dstate[...] = jnp.zeros_like(dstate)
            dbias_ref[...] = jnp.zeros_like(dbias_ref)
            dalog_ref[...] = jnp.zeros_like(dalog_ref)
            ddvec_ref[...] = jnp.zeros_like(ddvec_ref)

        ii, jj, tril, dt_col, dt_row, a_row_p, cum_col, cum_row = _ssd_prelude(
            dtr_ref, dtrt_ref, brow_ref, bcol_ref, alrow_ref, alcol_ref, Q)
        eye = ii == jj
        lane_h = lax.broadcasted_iota(jnp.int32, (Q, 128), 1)
        row_h = lax.broadcasted_iota(jnp.int32, (Q, 128), 0)
        lane_g = lax.broadcasted_iota(jnp.int32, (Q, gw), 1) // HEAD_DIM
        lane_s = lax.broadcasted_iota(jnp.int32, (D_STATE, gw), 1) // HEAD_DIM
        dcum_mat = jnp.zeros((Q, 128), F32)
        ddt_mat = jnp.zeros((Q, 128), F32)
        dtot_row = jnp.zeros((1, 128), F32)
        for g in range(G):
            xg = xbc_ref[:, g * gw:(g + 1) * gw]
            dyg = dy_ref[:, g * gw:(g + 1) * gw]
            bb = xbc_ref[:, boff + g * D_STATE: boff + (g + 1) * D_STATE].astype(BF16)
            cb = xbc_ref[:, coff + g * D_STATE: coff + (g + 1) * D_STATE].astype(BF16)
            sg = st_ref[g]
            dsg = dstate[g]
            sb = sg.astype(BF16)
            dsb = dsg.astype(BF16)
            xb = xg.astype(BF16)
            gm = _dot_nt(cb, bb)
            cs = _dot(cb, sb)
            bds = _dot(bb, dsb)
            dxg = dvec_ref[:, g * gw:(g + 1) * gw] * dyg
            dgm = jnp.zeros((Q, Q), F32)
            esc = jnp.zeros((Q, gw), F32)
            wsc = jnp.zeros((Q, gw), F32)
            lam = jnp.zeros((1, gw), F32)
            dycs = dyg * cs
            xbds = xg * bds
            dss = dsg * sg
            for j in range(hpg):
                h = g * hpg + j
                ccol = _col_of(cum_col, lane_h, h)
                dcol = _col_of(dt_col, lane_h, h)
                drow = dt_row[h:h + 1, :]
                seg = ccol - cum_row[h:h + 1, :]
                decay = jnp.exp(jnp.where(tril, seg, -jnp.inf))
                hm = lane_g == j
                dyh = jnp.where(hm, dyg, 0.0).astype(BF16)
                gl = gm * decay
                mh = gl * drow
                dmf = _dot_nt(dyh, xb)
                dxg = dxg + _dot_tn(mh.astype(BF16), dyh)
                dgm = dgm + dmf * decay * drow
                n_p = dmf * gl
                n_m = n_p * drow
                rowsum_n = jnp.sum(n_m, axis=1, keepdims=True)
                colsum_n = jnp.sum(jnp.where(eye, jnp.sum(n_m, axis=0, keepdims=True), 0.0), axis=1, keepdims=True)
                colsum_np = jnp.sum(jnp.where(eye, jnp.sum(n_p, axis=0, keepdims=True), 0.0), axis=1, keepdims=True)
                tot = ccol[Q - 1:Q, :]
                e = jnp.exp(ccol)
                wexp = jnp.exp(tot - ccol)
                wcol = wexp * dcol
                lamh = jnp.exp(tot)
                yoff_t = jnp.sum(jnp.where(hm, dycs, 0.0), axis=1, keepdims=True) * e
                e_s = jnp.sum(jnp.where(hm, xbds, 0.0), axis=1, keepdims=True)
                ew = e_s * wcol
                dtot = jnp.sum(ew, axis=0, keepdims=True) + lamh * jnp.sum(
                    jnp.sum(jnp.where(lane_s == j, dss, 0.0), axis=1, keepdims=True), axis=0, keepdims=True)
                dcum_h = rowsum_n + yoff_t - colsum_n - ew
                ddt_h = colsum_np + e_s * wexp
                onehot = lane_h == h
                dcum_mat = jnp.where(onehot, dcum_h, dcum_mat)
                ddt_mat = jnp.where(onehot, ddt_h, ddt_mat)
                dtot_row = jnp.where(onehot[0:1], dtot, dtot_row)
                esc = jnp.where(hm, e, esc)
                wsc = jnp.where(hm, wcol, wsc)
                lam = jnp.where(hm[0:1], lamh, lam)
            dgb = dgm.astype(BF16)
            dye = (dyg * esc).astype(BF16)
            xw = (xg * wsc).astype(BF16)
            dxbc_ref[:, g * gw:(g + 1) * gw] = dxg + bds * wsc
            dxbc_ref[:, boff + g * D_STATE: boff + (g + 1) * D_STATE] = _dot_tn(dgb, cb) + _dot_nt(xw, dsb)
            dxbc_ref[:, coff + g * D_STATE: coff + (g + 1) * D_STATE] = _dot(dgb, bb) + _dot_nt(dye, sb)
            dstate[g] = dsg * lam + _dot_tn(cb, dye)
            ddvec_ref[:, g * gw:(g + 1) * gw] += jnp.sum(dyg * xg, axis=0, keepdims=True)
        dcum_mat = dcum_mat + jnp.where(row_h == Q - 1, dtot_row, 0.0)
        da = _dot3_left((ii <= jj).astype(BF16), dcum_mat)
        ddt = ddt_mat + da * a_row_p
        dalog_ref[...] += jnp.sum(da * dt_col, axis=0, keepdims=True) * a_row_p
        ddtr = ddt * _sigmoid(dtr_ref[...] + brow_ref[...])
        ddtr_ref[...] = ddtr
        dbias_ref[...] += jnp.sum(ddtr, axis=0, keepdims=True)

    full = lambda shape: pl.BlockSpec(shape, lambda c: (0,) * len(shape))
    rc = lambda c: nc - 1 - c
    return pl.pallas_call(
        body, name=name, grid=(nc,),
        in_specs=[pl.BlockSpec((Q, W), lambda c: (rc(c), 0)), pl.BlockSpec((Q, 128), lambda c: (rc(c), 0)),
                  pl.BlockSpec((H, Q), lambda c: (0, rc(c))), full((1, 128)), full((H, 1)), full((1, 128)), full((H, 1)),
                  full((1, DI)), pl.BlockSpec((Q, DI), lambda c: (rc(c), 0)),
                  pl.BlockSpec((None, G, D_STATE, gw), lambda c: (rc(c), 0, 0, 0))],
        out_specs=[pl.BlockSpec((Q, W), lambda c: (rc(c), 0)), pl.BlockSpec((Q, 128), lambda c: (rc(c), 0)),
                   full((1, 128)), full((1, 128)), full((1, DI))],
        out_shape=[jax.ShapeDtypeStruct((LP, W), F32), jax.ShapeDtypeStruct((LP, 128), F32),
                   jax.ShapeDtypeStruct((1, 128), F32), jax.ShapeDtypeStruct((1, 128), F32),
                   jax.ShapeDtypeStruct((1, DI), F32)],
        scratch_shapes=[pltpu.VMEM((G, D_STATE, gw), F32)],
        compiler_params=_cp("arbitrary"),
    )(xbc, dtr, dtrt, brow, bcol, alrow, alcol, dvec, dy, states)


def _gate_fwd(y, zx, g, *, name):
    LP, DI = y.shape
    gw = DI // SSM_GROUPS
    tr = _tile(LP, (256, 128))

    def body(y_ref, z_ref, g_ref, o_ref):
        for k in range(SSM_GROUPS):
            sl = slice(k * gw, (k + 1) * gw)
            z = z_ref[:, sl]
            t = y_ref[:, sl] * (z * _sigmoid(z))
            r = lax.rsqrt(jnp.mean(t * t, axis=1, keepdims=True) + RMS_EPS)
            o_ref[:, sl] = (t * r * g_ref[:, sl]).astype(BF16)

    row = pl.BlockSpec((tr, DI), lambda i: (i, 0))
    return pl.pallas_call(
        body, name=name, grid=(LP // tr,), in_specs=[row, row, pl.BlockSpec((1, DI), lambda i: (0, 0))],
        out_specs=row, out_shape=jax.ShapeDtypeStruct((LP, DI), BF16), compiler_params=_cp("parallel"),
    )(y, zx, g)


def _gate_bwd(y, zx, g, dyn, *, name):
    LP, DI = y.shape
    gw = DI // SSM_GROUPS
    tr = _tile(LP, (256, 128))

    def body(y_ref, z_ref, g_ref, d_ref, dy_ref, dz_ref, dg_ref):
        i = pl.program_id(0)

        @pl.when(i == 0)
        def _():
            dg_ref[...] = jnp.zeros_like(dg_ref)

        for k in range(SSM_GROUPS):
            sl = slice(k * gw, (k + 1) * gw)
            z = z_ref[:, sl]
            yv = y_ref[:, sl]
            sg = _sigmoid(z)
            sz = z * sg
            t = yv * sz
            r = lax.rsqrt(jnp.mean(t * t, axis=1, keepdims=True) + RMS_EPS)
            th = t * r
            d = d_ref[:, sl]
            dtn = d * g_ref[:, sl]
            dt_ = r * (dtn - th * jnp.mean(dtn * th, axis=1, keepdims=True))
            dg_ref[:, sl] += jnp.sum(d * th, axis=0, keepdims=True)
            dy_ref[:, sl] = dt_ * sz
            dz_ref[:, sl] = dt_ * yv * sg * (1.0 + z * (1.0 - sg))

    row = pl.BlockSpec((tr, DI), lambda i: (i, 0))
    vec = pl.BlockSpec((1, DI), lambda i: (0, 0))
    return pl.pallas_call(
        body, name=name, grid=(LP // tr,), in_specs=[row, row, vec, row], out_specs=[row, row, vec],
        out_shape=[jax.ShapeDtypeStruct((LP, DI), F32), jax.ShapeDtypeStruct((LP, DI), F32),
                   jax.ShapeDtypeStruct((1, DI), F32)],
        compiler_params=_cp("arbitrary"),
    )(y, zx, g, dyn)


EXP_ZERO = -104.0
LOG2E = 1.4426950408889634


def _dot2_right(x, t2_bf16):
    hi = x.astype(BF16)
    lo = (x - hi.astype(F32)).astype(BF16)
    return _dot(jnp.concatenate([hi, lo], axis=1), t2_bf16)


def _tri2(T, upper):
    r = lax.broadcasted_iota(jnp.int32, (2 * T, T), 0) % T
    c = lax.broadcasted_iota(jnp.int32, (2 * T, T), 1)
    return (r <= c if upper else r >= c).astype(BF16)


def _sb_tile(q, k_blk, lower2, valid=None):
    z = _dot_nt(q, k_blk)
    sp = jnp.maximum(z, 0.0) + jnp.log(1.0 + jnp.exp2(jnp.abs(z) * (-LOG2E)))
    if valid is not None:
        sp = jnp.where(valid, sp, 0.0)
    return z, sp, z - _dot2_right(sp, lower2)


def _sb_weights(zr, c, valid=None):
    w = jnp.exp(zr + c)
    return w if valid is None else jnp.where(valid, w, 0.0)


def _sb_fwd(q, k, v, zmax, *, name):
    H, LP, dh = q.shape
    T = ATT_BLOCK
    nq = LP // T
    assert nq < LANES

    def body(q_ref, k_ref, v_ref, zb_ref, o_ref, c_ref):
        i = pl.program_id(1)
        ii = lax.broadcasted_iota(jnp.int32, (T, T), 0)
        jj = lax.broadcasted_iota(jnp.int32, (T, T), 1)
        lane = lax.broadcasted_iota(jnp.int32, (T, LANES), 1)
        lower2 = _tri2(T, upper=False)
        qv = q_ref[...]
        zb = zb_ref[0:1, 0:1]

        def kv(kb):
            ks = pl.multiple_of(kb * T, T)
            return k_ref[pl.ds(ks, T), :], v_ref[pl.ds(ks, T), :]

        kd, vd = kv(i)
        diag = jj < ii
        _, sp, zr = _sb_tile(qv, kd, lower2, diag)
        acc = _dot(_sb_weights(zr, 0.0, diag).astype(BF16), vd)
        c = -jnp.sum(sp, axis=1, keepdims=True)
        c_ref[...] = jnp.zeros_like(c_ref)

        def alive(c):
            return jnp.max(c + zb) > EXP_ZERO

        def cond(carry):
            kb, _, _, live = carry
            return (kb >= 0) & live

        def step(carry):
            kb, c, acc, _ = carry
            kt, vt = kv(kb)
            _, sp, zr = _sb_tile(qv, kt, lower2)
            acc = acc + _dot(_sb_weights(zr, c).astype(BF16), vt)
            c_ref[...] = jnp.where(lane == kb, c, c_ref[...])
            c = c - jnp.sum(sp, axis=1, keepdims=True)
            return kb - 1, c, acc, alive(c)

        kb, _, acc, _ = lax.while_loop(cond, step, (i - 1, c, acc, alive(c)))
        o_ref[...] = acc
        c_ref[...] = jnp.where(lane == LANES - 1, (kb + 1).astype(F32), c_ref[...])

    blk = pl.BlockSpec((None, T, dh), lambda h, i: (h, i, 0))
    cblk = pl.BlockSpec((None, T, LANES), lambda h, i: (h, i, 0))
    whole = pl.BlockSpec((None, LP, dh), lambda h, i: (h, 0, 0))
    return pl.pallas_call(
        body, name=name, grid=(H, nq), in_specs=[blk, whole, whole, pl.BlockSpec((1, LANES), lambda h, i: (0, 0))],
        out_specs=[blk, cblk],
        out_shape=[jax.ShapeDtypeStruct((H, LP, dh), F32), jax.ShapeDtypeStruct((H, LP, LANES), F32)],
        compiler_params=_cp("parallel", "parallel"),
    )(q, k, v, zmax)


def _sb_bwd(kstart, q, k, v, cmat, do, *, name):
    H, LP, dh = q.shape
    T = ATT_BLOCK
    nq = LP // T

    def body(ks_ref, q_ref, k_ref, v_ref, c_ref, do_ref, dq_ref, dk_ref, dv_ref):
        h = pl.program_id(0)
        i = pl.program_id(1)

        @pl.when(i == 0)
        def _():
            dk_ref[...] = jnp.zeros_like(dk_ref)
            dv_ref[...] = jnp.zeros_like(dv_ref)

        ii = lax.broadcasted_iota(jnp.int32, (T, T), 0)
        jj = lax.broadcasted_iota(jnp.int32, (T, T), 1)
        lane = lax.broadcasted_iota(jnp.int32, (T, LANES), 1)
        lower2 = _tri2(T, upper=False)
        upper2 = _tri2(T, upper=True)
        qv = q_ref[...]
        dob = do_ref[...].astype(BF16)
        cm = c_ref[...]

        def tile(kb, cg, dq, valid=None):
            ks = pl.multiple_of(kb * T, T)
            k_blk = k_ref[pl.ds(ks, T), :]
            c = jnp.sum(jnp.where(lane == kb, cm, 0.0), axis=1, keepdims=True)
            z, sp, zr = _sb_tile(qv, k_blk, lower2, valid)
            w = _sb_weights(zr, c, valid)
            gw_ = w * _dot_nt(dob, v_ref[pl.ds(ks, T), :])
            gin = _dot2_right(gw_, upper2)
            dz = gw_ - jnp.exp(z - sp) * (cg + gin)
            if valid is not None:
                dz = jnp.where(valid, dz, 0.0)
            dz = dz.astype(BF16)
            dk_ref[pl.ds(ks, T), :] += _dot_tn(dz, qv)
            dv_ref[pl.ds(ks, T), :] += _dot_tn(w.astype(BF16), dob)
            return cg + jnp.sum(gw_, axis=1, keepdims=True), dq + _dot(dz, k_blk)

        carry = lax.fori_loop(ks_ref[h, i], i, lambda kb, cr: tile(kb, *cr),
                              (jnp.zeros((T, 1), F32), jnp.zeros((T, dh), F32)))
        _, dq = tile(i, *carry, valid=jj < ii)
        dq_ref[...] = dq

    blk = pl.BlockSpec((None, T, dh), lambda h, i, ks: (h, i, 0))
    cblk = pl.BlockSpec((None, T, LANES), lambda h, i, ks: (h, i, 0))
    whole = pl.BlockSpec((None, LP, dh), lambda h, i, ks: (h, 0, 0))
    sh = jax.ShapeDtypeStruct((H, LP, dh), F32)
    return pl.pallas_call(
        body, name=name,
        grid_spec=pltpu.PrefetchScalarGridSpec(
            num_scalar_prefetch=1, grid=(H, nq), in_specs=[blk, whole, whole, cblk, blk], out_specs=[blk, whole, whole]),
        out_shape=[sh, sh, sh], compiler_params=_cp("parallel", "arbitrary"),
    )(kstart, q, k, v, cmat, do)


def _loss_head(h, tgt, seq, *, name):
    LP, D = h.shape
    tr = _tile(LP, (384, 256, 128))

    def body(h_ref, t_ref, dh_ref, l_ref):
        i = pl.program_id(0)

        @pl.when(i == 0)
        def _():
            l_ref[...] = jnp.zeros_like(l_ref)

        row = lax.broadcasted_iota(jnp.int32, (tr, D), 0) + i * tr
        e = jnp.where((row >= N_META) & (row < N_META + seq), h_ref[...] - t_ref[...], 0.0)
        dh_ref[...] = e * (1.0 / D)
        l_ref[...] += jnp.sum(e * e, axis=0, keepdims=True) * (0.5 / D)

    row = pl.BlockSpec((tr, D), lambda i: (i, 0))
    return pl.pallas_call(
        body, name=name, grid=(LP // tr,), in_specs=[row, row], out_specs=[row, pl.BlockSpec((1, D), lambda i: (0, 0))],
        out_shape=[jax.ShapeDtypeStruct((LP, D), F32), jax.ShapeDtypeStruct((1, D), F32)],
        compiler_params=_cp("arbitrary"),
    )(h, tgt)


def _adamw(w, g, m, v, *, name):
    shape = w.shape
    C = shape[-1]
    R = math.prod(shape) // C
    tr = _tile(R, (512, 256, 128, 64, 32, 16, 8))
    c1 = 1.0 / (1.0 - ADAM_B1 ** ADAM_STEP)
    c2 = 1.0 / (1.0 - ADAM_B2 ** ADAM_STEP)

    def body(w_ref, g_ref, m_ref, v_ref, d_ref, nm_ref, nv_ref):
        gv = g_ref[...]
        nm = ADAM_B1 * m_ref[...] + (1.0 - ADAM_B1) * gv
        nv = ADAM_B2 * v_ref[...] + (1.0 - ADAM_B2) * (gv * gv)
        d_ref[...] = -ADAM_LR * ((nm * c1) / (jnp.sqrt(nv * c2) + ADAM_EPS) + ADAM_WD * w_ref[...])
        nm_ref[...] = nm
        nv_ref[...] = nv

    blk = pl.BlockSpec((tr, C), lambda i: (i, 0))
    sh = jax.ShapeDtypeStruct((R, C), F32)
    d, nm, nv = pl.pallas_call(
        body, name=name, grid=(R // tr,), in_specs=[blk] * 4, out_specs=[blk] * 3, out_shape=[sh] * 3,
        compiler_params=_cp("parallel"),
    )(w.reshape(R, C), g.reshape(R, C), m.reshape(R, C), v.reshape(R, C))
    return d.reshape(shape), nm.reshape(shape), nv.reshape(shape)


def _sum_rows(buf, *, name):
    n, R, C = buf.shape
    tr = _tile(R, (512, 256, 128, 64, 32, 16, 8))

    def body(b_ref, o_ref):
        acc = b_ref[0].astype(F32)
        for k in range(1, n):
            acc = acc + b_ref[k].astype(F32)
        o_ref[...] = acc

    return pl.pallas_call(
        body, name=name, grid=(R // tr,), in_specs=[pl.BlockSpec((n, tr, C), lambda i: (0, i, 0))],
        out_specs=pl.BlockSpec((tr, C), lambda i: (i, 0)), out_shape=jax.ShapeDtypeStruct((R, C), F32),
        compiler_params=_cp("parallel"),
    )(buf)


def _interleave(buf, *, name):
    n, R, C = buf.shape
    tr = _tile(R, (256, 128, 64, 32, 16))

    def body(b_ref, o_ref):
        for d in range(n):
            o_ref[:, d * C:(d + 1) * C] = b_ref[d]

    return pl.pallas_call(
        body, name=name, grid=(R // tr,), in_specs=[pl.BlockSpec((n, tr, C), lambda i: (0, i, 0))],
        out_specs=pl.BlockSpec((tr, n * C), lambda i: (i, 0)), out_shape=jax.ShapeDtypeStruct((R, n * C), buf.dtype),
        compiler_params=_cp("parallel"),
    )(buf)


def _deinterleave(x, *, out_dtype, name):
    R, NC = x.shape
    C = NC // N_DEV
    tr = _tile(R, (256, 128, 64, 32, 16))

    def body(x_ref, o_ref):
        for d in range(N_DEV):
            o_ref[d] = x_ref[:, d * C:(d + 1) * C].astype(out_dtype)

    return pl.pallas_call(
        body, name=name, grid=(R // tr,), in_specs=[pl.BlockSpec((tr, NC), lambda i: (i, 0))],
        out_specs=pl.BlockSpec((N_DEV, tr, C), lambda i: (0, i, 0)),
        out_shape=jax.ShapeDtypeStruct((N_DEV, R, C), out_dtype), compiler_params=_cp("parallel"),
    )(x)


def _mesh_pos():
    x, y, c = lax.axis_index("x"), lax.axis_index("y"), lax.axis_index("c")
    return x, y, c, 4 * x + 2 * y + c


def _peer(x, y, c, f):
    px, py, pc = (x + ((f >> 2) & 1)) % 2, (y + ((f >> 1) & 1)) % 2, (c + (f & 1)) % 2
    return (px, py, pc), 4 * px + 2 * py + pc


def _exchange(arrs, *, scatter, name):
    n = len(arrs)
    hbm = pl.BlockSpec(memory_space=pl.ANY)

    def body(*refs):
        xa = (refs[:n], refs[n:2 * n], *refs[2 * n:], scatter)
        _xchg_start(*xa)
        _xchg_wait(*xa)

    return list(pl.pallas_call(
        body, name=name, in_specs=[hbm] * n, out_specs=[hbm] * n, out_shape=_xchg_shapes(arrs, scatter),
        scratch_shapes=_xchg_scratch(n), compiler_params=pltpu.CompilerParams(has_side_effects=True),
    )(*arrs))


def _heads(t, H):
    LP = t.shape[0]
    return t.reshape(LP, H, HEAD_DIM).transpose(1, 0, 2)


def _unheads(t):
    H, LP, dh = t.shape
    return t.transpose(1, 0, 2).reshape(LP, H * dh)


def _shard_rows(t):
    return t.reshape(N_DEV, t.shape[0] // N_DEV, t.shape[1])


def _ffn_fwd(h, g, W, l, s, xq):
    tag = f"{l}{s}"
    n = _rms_fwd(h, g, name=f"ffn_norm_{tag}")
    w1, w3 = W[("w13", l, s)]
    a, b, sw = _ffn_up(n, w1, w3, name=f"ffn_up_{tag}", xq=xq)
    h2 = _matmul([(sw, W[("w2", l, s)])], res=h, alpha=FFN_RES, name=f"ffn_down_{tag}", xq=xq)
    return h2, (h, n, a, b, sw)


def _ffn_bwd(dh, saved, g, W, l, s, xq, emit):
    tag = f"{l}{s}"
    h, n, a, b, sw = saved
    (w1, w3), w2 = W[("w13", l, s)], W[("w2", l, s)]
    da, db = _ffn_mid_bwd(dh, w2.T, a, b, name=f"ffn_mid_bwd_{tag}", xq=xq)
    dw2 = _matmul([(sw, dh)], trans_a=True, alpha=FFN_RES, out_dtype=BF16, name=f"ffn_dw2_{tag}", xq=xq)
    emit(("w2", l, s), [_shard_rows(dw2)])
    dn = _matmul([(da, w1.T), (db, w3.T)], name=f"ffn_dn_{tag}", xq=xq)
    dw1 = _matmul([(n, da)], trans_a=True, shards=N_DEV, out_dtype=BF16, name=f"ffn_dw1_{tag}", xq=xq)
    dw3 = _matmul([(n, db)], trans_a=True, shards=N_DEV, out_dtype=BF16, name=f"ffn_dw3_{tag}", xq=xq)
    emit(("w13", l, s), [dw1, dw3])
    dh_in, dg = _rms_bwd(h, g, dn, res=dh, name=f"ffn_norm_bwd_{tag}")
    return dh_in, dg


def _local_step(x, tgt, W, xq=None, recv=None):
    G = {}

    def emit(key, arrs):
        G[key] = arrs
        if xq is not None:
            xq.push(arrs, True, lambda res, key=key: recv.__setitem__(key, res))

    seq, D = x.shape
    L = N_META + seq
    LP = -(-L // ROW_ALIGN) * ROW_ALIGN
    pad = LP - L
    H_sb = D // HEAD_DIM
    DI = W["ssm_norm_g"].shape[-1]
    H_ssm = DI // HEAD_DIM
    CONV = DI + 2 * SSM_GROUPS * D_STATE
    ZX = DI + CONV

    h0 = jnp.concatenate([W["meta_tokens"], x, jnp.zeros((pad, D), F32)], axis=0)
    tgt_p = jnp.pad(tgt, ((N_META, pad), (0, 0)))
    ng = W["norm_g"]

    h1, sv_f00 = _ffn_fwd(h0, ng[0, 0], W, 0, 0, xq)
    u0 = _rms_fwd(h1, ng[0, 1], name="ssm_norm")
    w_in = W["ssm_in_proj"][0]
    w_zx = w_in[:, :ZX]
    w_dt = jnp.pad(w_in[:, ZX:], ((0, 0), (0, 128 - H_ssm)))
    zx = _matmul([(u0, w_zx)], name="ssm_in_zx", xq=xq)
    dtr = _matmul([(u0, w_dt)], name="ssm_in_dt")
    conv_w, conv_b = W["ssm_conv_w"][0], W["ssm_conv_b"][0]
    xbc = _conv_fwd(zx, conv_w, conv_b, DI, name="ssm_conv")
    dtrt = dtr[:, :H_ssm].T
    padh = lambda t: jnp.pad(t.reshape(1, H_ssm), ((0, 0), (0, 128 - H_ssm)))
    brow, bcol = padh(W["ssm_dt_bias"][0]), W["ssm_dt_bias"][0].reshape(H_ssm, 1)
    alrow, alcol = padh(W["ssm_a_log"][0]), W["ssm_a_log"][0].reshape(H_ssm, 1)
    dvec = jnp.repeat(W["ssm_d"][0], HEAD_DIM).reshape(1, DI)
    ssm_args = (xbc, dtr, dtrt, brow, bcol, alrow, alcol, dvec)
    y, states = _ssd_fwd(*ssm_args, name="ssd_fwd")
    sng = W["ssm_norm_g"].reshape(1, DI)
    yn = _gate_fwd(y, zx, sng, name="ssm_gate")
    w_out = W["ssm_out_proj"][0]
    h2 = _matmul([(yn, w_out)], res=h1, name="ssm_out", xq=xq)
    h3, sv_f01 = _ffn_fwd(h2, ng[0, 2], W, 0, 1, xq)

    kv_in = _rms_fwd(h3, W["kv_norm_g"], name="kv_norm")
    kraw = _heads(_matmul([(kv_in, W["w_k"])], name="kv_k"), H_sb)
    vh = _heads(_matmul([(kv_in, W["w_v"])], out_dtype=BF16, name="kv_v"), H_sb)
    kh = _rms_fwd(kraw.reshape(H_sb * LP, HEAD_DIM), W["k_norm_g"], name="k_headnorm").reshape(H_sb, LP, HEAD_DIM)

    h4, sv_f10 = _ffn_fwd(h3, ng[1, 0], W, 1, 0, xq)
    u1 = _rms_fwd(h4, ng[1, 1], name="sb_norm")
    qraw = _heads(_matmul([(u1, W["sb_w_q"][0])], name="sb_q"), H_sb)
    scale = HEAD_DIM ** -0.5
    qh = _rms_fwd(qraw.reshape(H_sb * LP, HEAD_DIM), W["sb_q_norm_g"][0], scale=scale,
                  name="q_headnorm").reshape(H_sb, LP, HEAD_DIM)
    zmax = 1.02 * math.sqrt(HEAD_DIM) * jnp.max(jnp.abs(W["sb_q_norm_g"])) * jnp.max(jnp.abs(W["k_norm_g"]))
    o, cmat = _sb_fwd(qh, kh, vh, jnp.full((1, LANES), zmax, F32), name="sb_fwd")
    kstart = cmat[:, ::ATT_BLOCK, LANES - 1].astype(jnp.int32)
    o_flat = _unheads(o)
    h5 = _matmul([(o_flat, W["sb_w_o"][0])], res=h4, name="sb_out")
    h6, sv_f11 = _ffn_fwd(h5, ng[1, 2], W, 1, 1, xq)

    dh, lvec = _loss_head(h6, tgt_p, seq, name="loss_head")
    loss = jnp.sum(lvec)
    dng = [[None] * 3 for _ in range(2)]
    shard_rows = _shard_rows

    dh, dng[1][2] = _ffn_bwd(dh, sv_f11, ng[1, 2], W, 1, 1, xq, emit)
    g_wo = shard_rows(_matmul([(o_flat, dh)], trans_a=True, out_dtype=BF16, name="sb_dwo"))
    do = _heads(_matmul([(dh, W["sb_w_o"][0].T)], name="sb_do"), H_sb)
    dq, dk, dv = _sb_bwd(kstart, qh, kh, vh, cmat, do, name="sb_bwd")
    dqraw, dqg = _rms_bwd(qraw.reshape(H_sb * LP, HEAD_DIM), W["sb_q_norm_g"][0], dq.reshape(H_sb * LP, HEAD_DIM),
                          alpha=scale, name="q_headnorm_bwd")
    G["sb_q_norm_g"] = dqg
    dqraw = _unheads(dqraw.reshape(H_sb, LP, HEAD_DIM))
    g_wq = shard_rows(_matmul([(u1, dqraw)], trans_a=True, out_dtype=BF16, name="sb_dwq"))
    emit("sb", [g_wq, g_wo])
    du1 = _matmul([(dqraw, W["sb_w_q"][0].T)], name="sb_du")
    dh, dng[1][1] = _rms_bwd(h4, ng[1, 1], du1, res=dh, name="sb_norm_bwd")
    dh, dng[1][0] = _ffn_bwd(dh, sv_f10, ng[1, 0], W, 1, 0, xq, emit)

    dkraw, dkg = _rms_bwd(kraw.reshape(H_sb * LP, HEAD_DIM), W["k_norm_g"], dk.reshape(H_sb * LP, HEAD_DIM),
                          name="k_headnorm_bwd")
    G["k_norm_g"] = dkg.reshape(-1)
    dkraw = _unheads(dkraw.reshape(H_sb, LP, HEAD_DIM))
    dvf = _unheads(dv)
    g_wk = shard_rows(_matmul([(kv_in, dkraw)], trans_a=True, out_dtype=BF16, name="kv_dwk"))
    g_wv = shard_rows(_matmul([(kv_in, dvf)], trans_a=True, out_dtype=BF16, name="kv_dwv"))
    emit("kv", [g_wk, g_wv])
    dkv = _matmul([(dkraw, W["w_k"].T), (dvf, W["w_v"].T)], name="kv_din", xq=xq)
    dh, dkvg = _rms_bwd(h3, W["kv_norm_g"], dkv, res=dh, name="kv_norm_bwd")
    G["kv_norm_g"] = dkvg.reshape(-1)

    dh, dng[0][2] = _ffn_bwd(dh, sv_f01, ng[0, 2], W, 0, 1, xq, emit)
    emit("wout", [shard_rows(_matmul([(yn, dh)], trans_a=True, out_dtype=BF16, name="ssm_dwout", xq=xq))])
    dyn = _matmul([(dh, w_out.T)], name="ssm_dyn", xq=xq)
    dy, dz, dsng = _gate_bwd(y, zx, sng, dyn, name="ssm_gate_bwd")
    G["ssm_norm_g"] = dsng
    dxbc, ddtr, dbias, dalog, ddvec = _ssd_bwd(*ssm_args, dy, states, name="ssd_bwd")
    G["ssm_dt_bias"] = dbias[:, :H_ssm]
    G["ssm_a_log"] = dalog[:, :H_ssm]
    G["ssm_d"] = jnp.sum(ddvec.reshape(H_ssm, HEAD_DIM), axis=1).reshape(1, H_ssm)
    gpre, dcw, dcb = _conv_bwd_g(zx, conv_w, conv_b, dxbc, DI, name="ssm_conv_bwd_g")
    G["ssm_conv_w"] = dcw[:D_CONV][None]
    G["ssm_conv_b"] = dcb
    dxbc_pre = _conv_bwd_u(gpre, conv_w, name="ssm_conv_bwd_u")
    emit("win", [_deinterleave(jnp.concatenate([
        _matmul([(u0, dz)], trans_a=True, out_dtype=BF16, name="ssm_dwin_z", xq=xq),
        _matmul([(u0, dxbc_pre)], trans_a=True, out_dtype=BF16, name="ssm_dwin_x", xq=xq),
        _matmul([(u0, ddtr)], trans_a=True, out_dtype=BF16, name="ssm_dwin_dt")[:, :H_ssm]], axis=1),
        out_dtype=BF16, name="ssm_dwin_shards")])
    du0 = _matmul([(dz, w_zx[:, :DI].T)], name="ssm_du_z", xq=xq)
    du0 = _matmul([(dxbc_pre, w_zx[:, DI:].T)], res=du0, name="ssm_du_x", xq=xq)
    du0 = _matmul([(ddtr, w_dt.T)], res=du0, name="ssm_du_dt")
    dh, dng[0][1] = _rms_bwd(h1, ng[0, 1], du0, res=dh, name="ssm_norm_bwd")
    dh, dng[0][0] = _ffn_bwd(dh, sv_f00, ng[0, 0], W, 0, 0, xq, emit)

    G["norm_g"] = jnp.stack([jnp.concatenate(r, axis=0) for r in dng])
    G["meta_tokens"] = dh[:N_META]
    return loss, dh[N_META:L], G


WEIGHTS = ['meta_tokens', 'norm_g', 'ffn_w1', 'ffn_w3', 'ffn_w2', 'ssm_in_proj', 'ssm_conv_w', 'ssm_conv_b',
           'ssm_dt_bias', 'ssm_a_log', 'ssm_d', 'ssm_norm_g', 'ssm_out_proj', 'kv_norm_g', 'w_k', 'k_norm_g', 'w_v',
           'sb_w_q', 'sb_q_norm_g', 'sb_w_o']
SHARD_AXIS = {'meta_tokens': 1, 'norm_g': 2, 'ffn_w1': 3, 'ffn_w3': 3, 'ffn_w2': 2, 'ssm_in_proj': 2, 'ssm_conv_w': 2,
              'ssm_conv_b': 1, 'ssm_dt_bias': None, 'ssm_a_log': None, 'ssm_d': None, 'ssm_norm_g': 1,
              'ssm_out_proj': 1, 'kv_norm_g': None, 'w_k': 0, 'k_norm_g': None, 'w_v': 0, 'sb_w_q': 1,
              'sb_q_norm_g': None, 'sb_w_o': 1}
MATMUL_WEIGHTS = ('ffn_w1', 'ffn_w3', 'ffn_w2', 'ssm_in_proj', 'ssm_out_proj', 'w_k', 'w_v', 'sb_w_q', 'sb_w_o')
PACK_ROWS = 16


def _pack(arrs, dtype):
    flat = jnp.concatenate([a.reshape(-1).astype(dtype) for a in arrs])
    n = flat.shape[0]
    npad = -(-n // (LANES * PACK_ROWS)) * (LANES * PACK_ROWS)
    return jnp.pad(flat, (0, npad - n)).reshape(npad // LANES, LANES)


def _unpack_gathered(buf, names, shard_shapes, dtype):
    flat = buf.reshape(N_DEV, -1)
    out, off = {}, 0
    for n in names:
        shp = shard_shapes[n]
        size = math.prod(shp)
        t = flat[:, off:off + size].reshape((N_DEV,) + tuple(shp))
        off += size
        ax = SHARD_AXIS[n]
        t = jnp.moveaxis(t, 0, ax)
        full = shp[:ax] + (N_DEV * shp[ax],) + shp[ax + 1:]
        out[n] = t.reshape(full).astype(dtype)
    return out


def _to_shards(g, ax):
    shp = g.shape
    t = g.reshape(shp[:ax] + (N_DEV, shp[ax] // N_DEV) + shp[ax + 1:])
    return jnp.moveaxis(t, ax, 0).reshape(N_DEV, -1)


def kernel(x, meta_tokens, norm_g, ffn_w1, ffn_w3, ffn_w2, ssm_in_proj, ssm_conv_w, ssm_conv_b, ssm_dt_bias, ssm_a_log, ssm_d, ssm_norm_g, ssm_out_proj, kv_norm_g, w_k, k_norm_g, w_v, sb_w_q, sb_q_norm_g, sb_w_o, loss_target, m_meta_tokens, m_norm_g, m_ffn_w1, m_ffn_w3, m_ffn_w2, m_ssm_in_proj, m_ssm_conv_w, m_ssm_conv_b, m_ssm_dt_bias, m_ssm_a_log, m_ssm_d, m_ssm_norm_g, m_ssm_out_proj, m_kv_norm_g, m_w_k, m_k_norm_g, m_w_v, m_sb_w_q, m_sb_q_norm_g, m_sb_w_o, v_meta_tokens, v_norm_g, v_ffn_w1, v_ffn_w3, v_ffn_w2, v_ssm_in_proj, v_ssm_conv_w, v_ssm_conv_b, v_ssm_dt_bias, v_ssm_a_log, v_ssm_d, v_ssm_norm_g, v_ssm_out_proj, v_kv_norm_g, v_w_k, v_k_norm_g, v_w_v, v_sb_w_q, v_sb_q_norm_g, v_sb_w_o):
    shard = dict(meta_tokens=meta_tokens, norm_g=norm_g, ffn_w1=ffn_w1, ffn_w3=ffn_w3, ffn_w2=ffn_w2,
                 ssm_in_proj=ssm_in_proj, ssm_conv_w=ssm_conv_w, ssm_conv_b=ssm_conv_b, ssm_dt_bias=ssm_dt_bias,
                 ssm_a_log=ssm_a_log, ssm_d=ssm_d, ssm_norm_g=ssm_norm_g, ssm_out_proj=ssm_out_proj,
                 kv_norm_g=kv_norm_g, w_k=w_k, k_norm_g=k_norm_g, w_v=w_v, sb_w_q=sb_w_q, sb_q_norm_g=sb_q_norm_g,
                 sb_w_o=sb_w_o)
    mom_m = dict(zip(WEIGHTS, (m_meta_tokens, m_norm_g, m_ffn_w1, m_ffn_w3, m_ffn_w2, m_ssm_in_proj, m_ssm_conv_w,
                               m_ssm_conv_b, m_ssm_dt_bias, m_ssm_a_log, m_ssm_d, m_ssm_norm_g, m_ssm_out_proj,
                               m_kv_norm_g, m_w_k, m_k_norm_g, m_w_v, m_sb_w_q, m_sb_q_norm_g, m_sb_w_o)))
    mom_v = dict(zip(WEIGHTS, (v_meta_tokens, v_norm_g, v_ffn_w1, v_ffn_w3, v_ffn_w2, v_ssm_in_proj, v_ssm_conv_w,
                               v_ssm_conv_b, v_ssm_dt_bias, v_ssm_a_log, v_ssm_d, v_ssm_norm_g, v_ssm_out_proj,
                               v_kv_norm_g, v_w_k, v_k_norm_g, v_w_v, v_sb_w_q, v_sb_q_norm_g, v_sb_w_o)))
    sharded = [n for n in WEIGHTS if SHARD_AXIS[n] is not None]
    replicated = [n for n in WEIGHTS if SHARD_AXIS[n] is None]
    small = [n for n in sharded if n not in MATMUL_WEIGHTS]
    shapes = {n: tuple(shard[n].shape) for n in WEIGHTS}
    D = x.shape[-1]
    bf = lambda t: t.astype(BF16)
    full_rows = lambda t: t.reshape(N_DEV * t.shape[1], t.shape[2])
    cols = lambda l, s: jnp.concatenate([bf(ffn_w1[l, s]), bf(ffn_w3[l, s])], axis=0)
    xq, recv = _Queue(), {}
    W = _Weights(xq)
    W.update({n: shard[n] for n in replicated})

    def have_w13(l, s):
        def done(res):
            t = _interleave(res[0], name=f"weights_w13_{l}{s}")
            W[("w13", l, s)] = (t[:D], t[D:])
        return done

    def have_w2(l, s):
        return lambda res: W.__setitem__(("w2", l, s), full_rows(res[0]))

    def have_ssm(res):
        W["ssm_in_proj"] = _interleave(res[0], name="weights_win")[None]
        W["ssm_out_proj"] = full_rows(res[1])[None]

    def have_attn(res):
        W["w_k"], W["w_v"] = full_rows(res[0]), full_rows(res[1])
        W["sb_w_q"], W["sb_w_o"] = full_rows(res[2])[None], full_rows(res[3])[None]

    first = _exchange([cols(0, 0), bf(ffn_w2[0, 0]), _pack([shard[n] for n in small], F32)], scatter=False,
                      name="gather_first")
    have_w13(0, 0)(first[:1])
    have_w2(0, 0)(first[1:2])
    W.update(_unpack_gathered(first[2], small, shapes, F32))
    xq.push([bf(ssm_in_proj[0]), bf(ssm_out_proj[0])], False, have_ssm)
    xq.push([cols(0, 1)], False, have_w13(0, 1))
    xq.push([bf(ffn_w2[0, 1])], False, have_w2(0, 1))
    xq.push([bf(w_k), bf(w_v), bf(sb_w_q[0]), bf(sb_w_o[0])], False, have_attn)
    for l, s in ((1, 0), (1, 1)):
        xq.push([cols(l, s)], False, have_w13(l, s))
        xq.push([bf(ffn_w2[l, s])], False, have_w2(l, s))

    loss, dx, G = _local_step(x[0], loss_target[0], W, xq, recv)
    loss = lax.psum(loss, ("x", "y", "c"))

    send = jnp.concatenate([_to_shards(G[n], SHARD_AXIS[n]) for n in small], axis=1)
    n_el = send.shape[1]
    npad = -(-n_el // (LANES * PACK_ROWS)) * (LANES * PACK_ROWS)
    send = jnp.pad(send, ((0, 0), (0, npad - n_el))).reshape(N_DEV, npad // LANES, LANES)
    xq.push([send], True, lambda res: recv.__setitem__("small", res))
    xq.flush("scatter_last")
    rep = _pack([G[n] for n in replicated], F32)
    rep_sum = _sum_rows(_exchange([rep], scatter=False, name="gather_small_grads")[0], name="sum_small_grads").reshape(-1)
    summed = {k: [_sum_rows(t, name=f"sum_{'_'.join(map(str, k)) if isinstance(k, tuple) else k}_{i}")
                  for i, t in enumerate(v)] for k, v in recv.items()}

    grads, off = {}, 0
    small_sum = summed["small"][0].reshape(-1)
    for n in small:
        size = math.prod(shapes[n])
        grads[n] = small_sum[off:off + size].reshape(shapes[n])
        off += size
    off = 0
    for n in replicated:
        size = math.prod(shapes[n])
        grads[n] = rep_sum[off:off + size].reshape(shapes[n])
        off += size
    ls = [(l, s) for l in range(2) for s in range(2)]
    grads["ffn_w1"] = jnp.stack([summed[("w13", l, s)][0] for l, s in ls]).reshape(shapes["ffn_w1"])
    grads["ffn_w3"] = jnp.stack([summed[("w13", l, s)][1] for l, s in ls]).reshape(shapes["ffn_w3"])
    grads["ffn_w2"] = jnp.stack([summed[("w2", l, s)][0] for l, s in ls]).reshape(shapes["ffn_w2"])
    grads["ssm_in_proj"] = summed["win"][0][None]
    grads["ssm_out_proj"] = summed["wout"][0][None]
    grads["w_k"], grads["w_v"] = summed["kv"]
    grads["sb_w_q"], grads["sb_w_o"] = summed["sb"][0][None], summed["sb"][1][None]

    delta, new_m, new_v = {}, {}, {}
    for n in WEIGHTS:
        w2 = shard[n].reshape(1, -1) if shard[n].ndim == 1 else shard[n]
        r2 = lambda t: t.reshape(w2.shape)
        d, nm, nv = _adamw(w2, r2(grads[n]), r2(mom_m[n]), r2(mom_v[n]), name=f"adamw_{n}")
        delta[n], new_m[n], new_v[n] = (t.reshape(shapes[n]) for t in (d, nm, nv))

    return (loss, dx[None], *[grads[n] for n in WEIGHTS], *[delta[n] for n in WEIGHTS],
            *[new_m[n] for n in WEIGHTS], *[new_v[n] for n in WEIGHTS])
```

```python
import functools
import math

import jax
import jax.numpy as jnp
from jax import lax
from jax.experimental import pallas as pl
from jax.experimental.pallas import tpu as pltpu

F32 = jnp.float32
BF16 = jnp.bfloat16
RMS_EPS = 1e-6
N_META = 16
HEAD_DIM = 64
SSM_GROUPS = 8
D_STATE = 128
D_CONV = 4
FFN_RES = 0.5
ADAM_LR, ADAM_B1, ADAM_B2, ADAM_EPS, ADAM_WD, ADAM_STEP = 0.001, 0.9, 0.999, 1e-08, 0.01, 10
N_DEV = 8
SSD_CHUNK = 128
ATT_BLOCK = 256
ROW_ALIGN = 768
VMEM_LIMIT_V7X = 48 * 1024 * 1024
MATMUL_VMEM_BUDGET_V7X = 30 * 1024 * 1024
MESH = pl.DeviceIdType.MESH
LANES = 128


def _cp(*sem):
    return pltpu.CompilerParams(dimension_semantics=sem if sem else None, vmem_limit_bytes=VMEM_LIMIT_V7X)


def _tile(n, cands):
    for c in cands:
        if n % c == 0:
            return c
    return n


def _softplus(x):
    return jnp.maximum(x, 0.0) + jnp.log(1.0 + jnp.exp(-jnp.abs(x)))


def _sigmoid(x):
    return 1.0 / (1.0 + jnp.exp(-x))


def _split3(x):
    hi = x.astype(BF16)
    r1 = x - hi.astype(F32)
    mid = r1.astype(BF16)
    lo = (r1 - mid.astype(F32)).astype(BF16)
    return hi, mid, lo


def _dot(a, b):
    return jnp.dot(a, b, preferred_element_type=F32)


def _dot_nt(a, b):
    return lax.dot_general(a, b, (((1,), (1,)), ((), ())), preferred_element_type=F32)


def _dot_tn(a, b):
    return lax.dot_general(a, b, (((0,), (0,)), ((), ())), preferred_element_type=F32)


def _dot3_left(t_bf16, x):
    hi, mid, lo = _split3(x)
    return _dot(t_bf16, hi) + _dot(t_bf16, mid) + _dot(t_bf16, lo)


def _dot3_right(x, t_bf16):
    hi, mid, lo = _split3(x)
    return _dot(hi, t_bf16) + _dot(mid, t_bf16) + _dot(lo, t_bf16)


CARRIER_MIN_FLOP = 4e10


EXCHANGE_US_PER_MB = 94.0
MATMUL_TFLOPS = 650.0


class _Carry:
    def __init__(self, arrs, scatter, done):
        self.arrs, self.scatter, self.done = list(arrs), scatter, done
        per_peer = sum(math.prod(a.shape[1:] if scatter else a.shape) * a.dtype.itemsize for a in self.arrs)
        self.us = EXCHANGE_US_PER_MB * per_peer / 2 ** 20


class _Queue:
    def __init__(self):
        self.items = []

    def push(self, arrs, scatter, done):
        self.items.append(_Carry(arrs, scatter, done))

    def pop(self, kernel_us):
        for k, it in enumerate(self.items):
            if not it.scatter or it.us <= 1.15 * kernel_us:
                return self.items.pop(k)
            if k == 0 and len(self.items) > 4:
                return self.items.pop(0)
        return None

    def flush(self, name):
        k = 0
        while self.items:
            it = self.items.pop(0)
            it.done(_exchange(it.arrs, scatter=it.scatter, name=f"{name}_{k}"))
            k += 1


class _Weights(dict):
    def __init__(self, xq):
        super().__init__()
        self.xq, self.fetched = xq, 0

    def __missing__(self, key):
        while not dict.__contains__(self, key) and self.xq.items:
            it = self.xq.items.pop(0)
            it.done(_exchange(it.arrs, scatter=it.scatter, name=f"gather_now_{self.fetched}"))
            self.fetched += 1
        return dict.__getitem__(self, key)


ELEMENTWISE_US_PER_MB = 1.6


def _pop_for_rows(xq, rows, cols, us_per_mb=ELEMENTWISE_US_PER_MB):
    return xq.pop(us_per_mb * rows * cols * 4 / 2 ** 20) if xq is not None else None


def _xchg_shapes(arrs, scatter):
    return [jax.ShapeDtypeStruct((N_DEV,) + tuple(a.shape[1:] if scatter else a.shape), a.dtype) for a in arrs]


def _xchg_scratch(n):
    return [pltpu.SemaphoreType.DMA((n, N_DEV - 1)), pltpu.SemaphoreType.DMA((n, N_DEV - 1)),
            pltpu.SemaphoreType.DMA((n,))]


def _xchg_copies(srcs, dsts, send_sems, recv_sems, local_sems, scatter, with_recv):
    x, y, c, me = _mesh_pos()
    own, sends, recvs = [], [], []
    for a, (s, d) in enumerate(zip(srcs, dsts)):
        own.append(pltpu.make_async_copy(s.at[me] if scatter else s, d.at[me], local_sems.at[a]))
        for f in range(1, N_DEV):
            peer, pid = _peer(x, y, c, f)
            for row, lst in ((me, sends), (pid, recvs)) if with_recv else ((me, sends),):
                lst.append(pltpu.make_async_remote_copy(
                    src_ref=s.at[pid] if scatter else s, dst_ref=d.at[row], send_sem=send_sems.at[a, f - 1],
                    recv_sem=recv_sems.at[a, f - 1], device_id=peer, device_id_type=MESH))
    return own, sends, recvs


def _xchg_start(*a):
    own, sends, _ = _xchg_copies(*a, with_recv=False)
    for cp in own + sends:
        cp.start()


def _xchg_wait(*a):
    own, sends, recvs = _xchg_copies(*a, with_recv=True)
    for snd, rcv in zip(sends, recvs):
        snd.wait_send()
        rcv.wait_recv()
    for cp in own:
        cp.wait()


def _call(body, *, name, grid, in_specs, out_specs, out_shape, scratch_shapes=(), sem, args, carry=None):
    n_in, n_out, n_scr = len(in_specs), len(out_specs), len(scratch_shapes)
    if carry is None:
        return pl.pallas_call(
            body, name=name, grid=grid, in_specs=list(in_specs), out_specs=list(out_specs), out_shape=list(out_shape),
            scratch_shapes=list(scratch_shapes), compiler_params=_cp(*sem))(*args)
    n = len(carry.arrs)
    hbm = pl.BlockSpec(memory_space=pl.ANY)

    def wrapped(*refs):
        ins, csrc = refs[:n_in], refs[n_in:n_in + n]
        outs, cdst = refs[n_in + n:n_in + n + n_out], refs[n_in + n + n_out:n_in + 2 * n + n_out]
        scr = refs[n_in + 2 * n + n_out:]
        xa = (csrc, cdst, *scr[n_scr:], carry.scatter)
        pid = [pl.program_id(a) for a in range(len(grid))]
        first = functools.reduce(jnp.logical_and, [p == 0 for p in pid])
        last = functools.reduce(jnp.logical_and, [p == g - 1 for p, g in zip(pid, grid)])

        @pl.when(first)
        def _():
            _xchg_start(*xa)

        body(*ins, *outs, *scr[:n_scr])

        @pl.when(last)
        def _():
            _xchg_wait(*xa)

    res = pl.pallas_call(
        wrapped, name=name, grid=grid, in_specs=list(in_specs) + [hbm] * n, out_specs=list(out_specs) + [hbm] * n,
        out_shape=list(out_shape) + _xchg_shapes(carry.arrs, carry.scatter),
        scratch_shapes=list(scratch_shapes) + _xchg_scratch(n), compiler_params=_cp(*["arbitrary"] * len(grid)),
    )(*args, *carry.arrs)
    carry.done(list(res[n_out:]))
    return list(res[:n_out])


def _matmul(pairs, *, name, out_dtype=F32, trans_a=False, res=None, alpha=1.0, shards=None, xq=None,
            tm=None, tn=None, tk=None):
    a0, b0 = pairs[0]
    if trans_a:
        K, M = a0.shape
    else:
        M, K = a0.shape
    N = b0.shape[1]
    npair = len(pairs)
    has_res = res is not None
    tm = tm or _tile(M, (768, 512, 1408, 384, 256, 128))
    tk = tk or _tile(K, (1024, 1408, 768, 512, 256, 128))
    if tn is None:
        sa, sb, so = a0.dtype.itemsize, b0.dtype.itemsize, jnp.dtype(out_dtype).itemsize
        for tn in ([N] if shards else [c for c in (2048, 1536, 1408, 1024, 512, 384, 256, 128) if N % c == 0] or [N]):
            if (2 * npair * tk * (tm * sa + tn * sb) + tm * tn * (4 + 2 * so + (8 if has_res else 0))
                    <= MATMUL_VMEM_BUDGET_V7X):
                break
    nk = K // tk
    cs = N // shards if shards else None

    def body(*refs):
        o_ref, acc = refs[-2], refs[-1]
        k = pl.program_id(2)

        @pl.when(k == 0)
        def _():
            acc[...] = jnp.zeros_like(acc)

        part = None
        for p in range(npair):
            a = refs[2 * p][...].astype(BF16)
            b = refs[2 * p + 1][...].astype(BF16)
            d = _dot_tn(a, b) if trans_a else _dot(a, b)
            part = d if part is None else part + d
        acc[...] += part

        @pl.when(k == nk - 1)
        def _():
            if shards:
                for d in range(shards):
                    v = acc[:, d * cs:(d + 1) * cs]
                    o_ref[d] = (v * alpha if alpha != 1.0 else v).astype(out_dtype)
                return
            v = acc[...]
            if alpha != 1.0:
                v = v * alpha
            if has_res:
                v = refs[2 * npair][...] + v
            o_ref[...] = v.astype(out_dtype)

    if trans_a:
        a_spec = pl.BlockSpec((tk, tm), lambda i, j, k: (k, i))
    else:
        a_spec = pl.BlockSpec((tm, tk), lambda i, j, k: (i, k))
    b_spec = pl.BlockSpec((tk, tn), lambda i, j, k: (k, j))
    if shards:
        assert not has_res and tn == N
        o_spec = pl.BlockSpec((shards, tm, cs), lambda i, j, k: (0, i, 0))
        out_shape = jax.ShapeDtypeStruct((shards, M, cs), out_dtype)
    else:
        o_spec = pl.BlockSpec((tm, tn), lambda i, j, k: (i, j))
        out_shape = jax.ShapeDtypeStruct((M, N), out_dtype)
    in_specs, args = [], []
    for a, b in pairs:
        in_specs += [a_spec, b_spec]
        args += [a, b]
    if has_res:
        in_specs.append(o_spec)
        args.append(res)
    flop = 2.0 * npair * M * N * K
    carry = xq.pop(flop / MATMUL_TFLOPS / 1e6) if (xq is not None and flop >= CARRIER_MIN_FLOP) else None
    return _call(body, name=name, grid=(M // tm, N // tn, nk), in_specs=in_specs, out_specs=[o_spec],
                 out_shape=[out_shape], scratch_shapes=[pltpu.VMEM((tm, tn), F32)],
                 sem=("parallel", "parallel", "arbitrary"), args=args, carry=carry)[0]


def _rms_fwd(h, g, *, name, scale=1.0):
    R, D = h.shape
    tr = _tile(R, (2048, 1024, 768, 512, 256, 128)) if D <= 128 else _tile(R, (384, 256, 128))

    def body(h_ref, g_ref, o_ref):
        x = h_ref[...]
        r = lax.rsqrt(jnp.mean(x * x, axis=1, keepdims=True) + RMS_EPS)
        y = x * r * g_ref[...]
        if scale != 1.0:
            y = y * scale
        o_ref[...] = y.astype(BF16)

    return pl.pallas_call(
        body, name=name, grid=(R // tr,),
        in_specs=[pl.BlockSpec((tr, D), lambda i: (i, 0)), pl.BlockSpec((1, D), lambda i: (0, 0))],
        out_specs=pl.BlockSpec((tr, D), lambda i: (i, 0)),
        out_shape=jax.ShapeDtypeStruct((R, D), BF16), compiler_params=_cp("parallel"),
    )(h, g.reshape(1, D))


def _rms_bwd(h, g, dn, res=None, *, name, alpha=1.0):
    R, D = h.shape
    tr = _tile(R, (2048, 1024, 768, 512, 256, 128)) if D <= 128 else _tile(R, (384, 256, 128))
    has_res = res is not None

    def body(*refs):
        h_ref, g_ref, dn_ref = refs[:3]
        dh_ref, dg_ref = refs[-2], refs[-1]
        i = pl.program_id(0)

        @pl.when(i == 0)
        def _():
            dg_ref[...] = jnp.zeros_like(dg_ref)

        x = h_ref[...]
        r = lax.rsqrt(jnp.mean(x * x, axis=1, keepdims=True) + RMS_EPS)
        xh = x * r
        d = dn_ref[...].astype(F32)
        if alpha != 1.0:
            d = d * alpha
        dng = d * g_ref[...]
        m = jnp.mean(dng * xh, axis=1, keepdims=True)
        dh = r * (dng - xh * m)
        if has_res:
            dh = dh + refs[3][...]
        dh_ref[...] = dh
        dg_ref[...] += jnp.sum(d * xh, axis=0, keepdims=True)

    row = pl.BlockSpec((tr, D), lambda i: (i, 0))
    vec = pl.BlockSpec((1, D), lambda i: (0, 0))
    in_specs = [row, vec, row] + ([row] if has_res else [])
    args = [h, g.reshape(1, D), dn] + ([res] if has_res else [])
    return pl.pallas_call(
        body, name=name, grid=(R // tr,), in_specs=in_specs, out_specs=[row, vec],
        out_shape=[jax.ShapeDtypeStruct((R, D), F32), jax.ShapeDtypeStruct((1, D), F32)],
        compiler_params=_cp("arbitrary"),
    )(*args)


def _ffn_up(n, w1, w3, *, name, xq=None):
    M, K = n.shape
    N = w1.shape[1]
    tm = _tile(M, (384, 256, 128))
    tn = _tile(N, (1408, 512, 256, 128))

    def body(n_ref, w1_ref, w3_ref, a_ref, b_ref, s_ref):
        x = n_ref[...]
        a = _dot(x, w1_ref[...])
        b = _dot(x, w3_ref[...])
        a_ref[...] = a.astype(BF16)
        b_ref[...] = b.astype(BF16)
        s_ref[...] = (a * _sigmoid(a) * b).astype(BF16)

    o_spec = pl.BlockSpec((tm, tn), lambda j, i: (i, j))
    w_spec = pl.BlockSpec((K, tn), lambda j, i: (0, j))
    sh = jax.ShapeDtypeStruct((M, N), BF16)
    return _call(body, name=name, grid=(N // tn, M // tm),
                 in_specs=[pl.BlockSpec((tm, K), lambda j, i: (i, 0)), w_spec, w_spec],
                 out_specs=[o_spec, o_spec, o_spec], out_shape=[sh, sh, sh], sem=("parallel", "parallel"),
                 args=(n, w1, w3), carry=xq.pop(4.0 * M * N * K / MATMUL_TFLOPS / 1e6) if xq is not None else None)


def _ffn_mid_bwd(dh, w2t, a, b, *, name, xq=None):
    M, K = dh.shape
    N = w2t.shape[1]
    tm = _tile(M, (384, 256, 128))
    tn = _tile(N, (1408, 512, 256, 128))

    def body(dh_ref, w_ref, a_ref, b_ref, da_ref, db_ref):
        ds = _dot(dh_ref[...].astype(BF16), w_ref[...]) * FFN_RES
        av = a_ref[...].astype(F32)
        bv = b_ref[...].astype(F32)
        sg = _sigmoid(av)
        da_ref[...] = (ds * bv * sg * (1.0 + av * (1.0 - sg))).astype(BF16)
        db_ref[...] = (ds * av * sg).astype(BF16)

    o_spec = pl.BlockSpec((tm, tn), lambda j, i: (i, j))
    sh = jax.ShapeDtypeStruct((M, N), BF16)
    return _call(body, name=name, grid=(N // tn, M // tm),
                 in_specs=[pl.BlockSpec((tm, K), lambda j, i: (i, 0)), pl.BlockSpec((K, tn), lambda j, i: (0, j)),
                           o_spec, o_spec],
                 out_specs=[o_spec, o_spec], out_shape=[sh, sh], sem=("parallel", "parallel"),
                 args=(dh, w2t, a, b), carry=xq.pop(2.0 * M * N * K / 400.0 / 1e6) if xq is not None else None)


def _conv_pre(xx, w_ref, b_ref, tr):
    acc = None
    for k in range(D_CONV):
        sh = D_CONV - 1 - k
        v = (pltpu.roll(xx, sh, 0) if sh else xx)[8:8 + tr]
        t = w_ref[k:k + 1, :] * v
        acc = t if acc is None else acc + t
    return acc + b_ref[...]


def _conv_fwd(zx, w, b, col_off, *, name, xq=None):
    LP = zx.shape[0]
    C = w.shape[1]
    tr = _tile(LP, (256, 128))
    tc = _tile(C, (512, 256, 128))
    co = col_off // tc

    def body(cur_ref, prev_ref, w_ref, b_ref, o_ref):
        i = pl.program_id(0)
        prev = jnp.where(i == 0, 0.0, prev_ref[...])
        pre = _conv_pre(jnp.concatenate([prev, cur_ref[...]], axis=0), w_ref, b_ref, tr)
        o_ref[...] = pre * _sigmoid(pre)

    return _call(
        body, name=name, grid=(LP // tr, C // tc),
        in_specs=[pl.BlockSpec((tr, tc), lambda i, j: (i, j + co)),
                  pl.BlockSpec((8, tc), lambda i, j: (jnp.maximum(i * (tr // 8) - 1, 0), j + co)),
                  pl.BlockSpec((D_CONV, tc), lambda i, j: (0, j)), pl.BlockSpec((1, tc), lambda i, j: (0, j))],
        out_specs=[pl.BlockSpec((tr, tc), lambda i, j: (i, j))],
        out_shape=[jax.ShapeDtypeStruct((LP, C), F32)], sem=("parallel", "parallel"),
        args=(zx, zx, w, b.reshape(1, C)), carry=_pop_for_rows(xq, LP, C))[0]


def _conv_bwd_g(zx, w, b, dact, col_off, *, name, xq=None):
    LP = zx.shape[0]
    C = w.shape[1]
    tr = _tile(LP, (256, 128))
    tc = _tile(C, (512, 256, 128))
    co = col_off // tc

    def body(cur_ref, prev_ref, w_ref, b_ref, d_ref, g_ref, dw_ref, db_ref):
        i = pl.program_id(1)

        @pl.when(i == 0)
        def _():
            dw_ref[...] = jnp.zeros_like(dw_ref)
            db_ref[...] = jnp.zeros_like(db_ref)

        prev = jnp.where(i == 0, 0.0, prev_ref[...])
        xx = jnp.concatenate([prev, cur_ref[...]], axis=0)
        pre = _conv_pre(xx, w_ref, b_ref, tr)
        sg = _sigmoid(pre)
        g = d_ref[...] * sg * (1.0 + pre * (1.0 - sg))
        g_ref[...] = g
        db_ref[...] += jnp.sum(g, axis=0, keepdims=True)
        rows = []
        for k in range(D_CONV):
            sh = D_CONV - 1 - k
            v = (pltpu.roll(xx, sh, 0) if sh else xx)[8:8 + tr]
            rows.append(jnp.sum(g * v, axis=0, keepdims=True))
        rows.append(jnp.zeros((8 - D_CONV, tc), F32))
        dw_ref[...] += jnp.concatenate(rows, axis=0)

    return _call(
        body, name=name, grid=(C // tc, LP // tr),
        in_specs=[pl.BlockSpec((tr, tc), lambda j, i: (i, j + co)),
                  pl.BlockSpec((8, tc), lambda j, i: (jnp.maximum(i * (tr // 8) - 1, 0), j + co)),
                  pl.BlockSpec((D_CONV, tc), lambda j, i: (0, j)), pl.BlockSpec((1, tc), lambda j, i: (0, j)),
                  pl.BlockSpec((tr, tc), lambda j, i: (i, j))],
        out_specs=[pl.BlockSpec((tr, tc), lambda j, i: (i, j)), pl.BlockSpec((8, tc), lambda j, i: (0, j)),
                   pl.BlockSpec((1, tc), lambda j, i: (0, j))],
        out_shape=[jax.ShapeDtypeStruct((LP, C), F32), jax.ShapeDtypeStruct((8, C), F32), jax.ShapeDtypeStruct((1, C), F32)],
        sem=("parallel", "arbitrary"), args=(zx, zx, w, b.reshape(1, C), dact), carry=_pop_for_rows(xq, LP, C, 2.0))


def _conv_bwd_u(g, w, *, name, xq=None):
    LP, C = g.shape
    tr = _tile(LP, (256, 128))
    tc = _tile(C, (512, 256, 128))
    nb = LP // tr

    def body(cur_ref, nxt_ref, w_ref, o_ref):
        i = pl.program_id(0)
        nxt = jnp.where(i == nb - 1, 0.0, nxt_ref[...])
        xx = jnp.concatenate([cur_ref[...], nxt], axis=0)
        acc = None
        for k in range(D_CONV):
            sh = D_CONV - 1 - k
            v = (pltpu.roll(xx, tr + 8 - sh, 0) if sh else xx)[:tr]
            t = w_ref[k:k + 1, :] * v
            acc = t if acc is None else acc + t
        o_ref[...] = acc

    return _call(
        body, name=name, grid=(nb, C // tc),
        in_specs=[pl.BlockSpec((tr, tc), lambda i, j: (i, j)),
                  pl.BlockSpec((8, tc), lambda i, j: (jnp.minimum((i + 1) * (tr // 8), LP // 8 - 1), j)),
                  pl.BlockSpec((D_CONV, tc), lambda i, j: (0, j))],
        out_specs=[pl.BlockSpec((tr, tc), lambda i, j: (i, j))],
        out_shape=[jax.ShapeDtypeStruct((LP, C), F32)], sem=("parallel", "parallel"),
        args=(g, g, w), carry=_pop_for_rows(xq, LP, C))[0]


def _ssd_prelude(dtr_ref, dtrt_ref, brow_ref, bcol_ref, alrow_ref, alcol_ref, Q):
    ii = lax.broadcasted_iota(jnp.int32, (Q, Q), 0)
    jj = lax.broadcasted_iota(jnp.int32, (Q, Q), 1)
    tril = ii >= jj
    dt_col = _softplus(dtr_ref[...] + brow_ref[...])
    a_row_p = -jnp.exp(alrow_ref[...])
    dt_row = _softplus(dtrt_ref[...] + bcol_ref[...])
    a_col_p = -jnp.exp(alcol_ref[...])
    cum_col = _dot3_left(tril.astype(BF16), dt_col * a_row_p)
    cum_row = _dot3_right(dt_row * a_col_p, (ii <= jj).astype(BF16))
    return ii, jj, tril, dt_col, dt_row, a_row_p, cum_col, cum_row


def _col_of(mat, lane_idx, h):
    return jnp.sum(jnp.where(lane_idx == h, mat, 0.0), axis=1, keepdims=True)


def _ssd_fwd(xbc, dtr, dtrt, brow, bcol, alrow, alcol, dvec, *, name, xq=None):
    LP = xbc.shape[0]
    Q = SSD_CHUNK
    nc = LP // Q
    G = SSM_GROUPS
    DI = dvec.shape[1]
    gw = DI // G
    hpg = gw // HEAD_DIM
    H = G * hpg
    boff, coff = DI, DI + G * D_STATE

    def body(xbc_ref, dtr_ref, dtrt_ref, brow_ref, bcol_ref, alrow_ref, alcol_ref, dvec_ref, y_ref, st_ref, state):
        c = pl.program_id(0)

        @pl.when(c == 0)
        def _():
            state[...] = jnp.zeros_like(state)

        st_ref[...] = state[...]
        ii, jj, tril, dt_col, dt_row, _, cum_col, cum_row = _ssd_prelude(
            dtr_ref, dtrt_ref, brow_ref, bcol_ref, alrow_ref, alcol_ref, Q)
        lane_h = lax.broadcasted_iota(jnp.int32, (Q, 128), 1)
        lane_g = lax.broadcasted_iota(jnp.int32, (Q, gw), 1) // HEAD_DIM
        for g in range(G):
            xg = xbc_ref[:, g * gw:(g + 1) * gw]
            bb = xbc_ref[:, boff + g * D_STATE: boff + (g + 1) * D_STATE].astype(BF16)
            cb = xbc_ref[:, coff + g * D_STATE: coff + (g + 1) * D_STATE].astype(BF16)
            gm = _dot_nt(cb, bb)
            sg = state[g]
            yoff = _dot(cb, sg.astype(BF16))
            ydiag = jnp.zeros((Q, gw), F32)
            esc = jnp.zeros((Q, gw), F32)
            wsc = jnp.zeros((Q, gw), F32)
            lam = jnp.zeros((1, gw), F32)
            for j in range(hpg):
                h = g * hpg + j
                ccol = _col_of(cum_col, lane_h, h)
                dcol = _col_of(dt_col, lane_h, h)
                seg = ccol - cum_row[h:h + 1, :]
                decay = jnp.exp(jnp.where(tril, seg, -jnp.inf))
                mh = gm * decay * dt_row[h:h + 1, :]
                hm = lane_g == j
                ydiag = ydiag + _dot(mh.astype(BF16), jnp.where(hm, xg, 0.0).astype(BF16))
                tot = ccol[Q - 1:Q, :]
                esc = jnp.where(hm, jnp.exp(ccol), esc)
                wsc = jnp.where(hm, jnp.exp(tot - ccol) * dcol, wsc)
                lam = jnp.where(hm[0:1], jnp.exp(tot), lam)
            y_ref[:, g * gw:(g + 1) * gw] = ydiag + yoff * esc + dvec_ref[:, g * gw:(g + 1) * gw] * xg
            state[g] = sg * lam + _dot_tn(bb, (xg * wsc).astype(BF16))

    W = xbc.shape[1]
    full = lambda shape: pl.BlockSpec(shape, lambda c: (0,) * len(shape))
    return _call(
        body, name=name, grid=(nc,),
        in_specs=[pl.BlockSpec((Q, W), lambda c: (c, 0)), pl.BlockSpec((Q, 128), lambda c: (c, 0)),
                  pl.BlockSpec((H, Q), lambda c: (0, c)), full((1, 128)), full((H, 1)), full((1, 128)), full((H, 1)),
                  full((1, DI))],
        out_specs=[pl.BlockSpec((Q, DI), lambda c: (c, 0)), pl.BlockSpec((None, G, D_STATE, gw), lambda c: (c, 0, 0, 0))],
        out_shape=[jax.ShapeDtypeStruct((LP, DI), F32), jax.ShapeDtypeStruct((nc, G, D_STATE, gw), F32)],
        scratch_shapes=[pltpu.VMEM((G, D_STATE, gw), F32)], sem=("arbitrary",),
        args=(xbc, dtr, dtrt, brow, bcol, alrow, alcol, dvec), carry=_pop_for_rows(xq, LP, W, 1.25))


def _ssd_bwd(xbc, dtr, dtrt, brow, bcol, alrow, alcol, dvec, dy, states, *, name, xq=None):
    LP = xbc.shape[0]
    Q = SSD_CHUNK
    nc = LP // Q
    G = SSM_GROUPS
    DI = dvec.shape[1]
    gw = DI // G
    hpg = gw // HEAD_DIM
    H = G * hpg
    boff, coff = DI, DI + G * D_STATE
    W = xbc.shape[1]

    def body(xbc_ref, dtr_ref, dtrt_ref, brow_ref, bcol_ref, alrow_ref, alcol_ref, dvec_ref, dy_ref, st_ref,
             dxbc_ref, ddtr_ref, dbias_ref, dalog_ref, ddvec_ref, dstate):
        c = pl.program_id(0)

        @pl.when(c == 0)
        def _():
            dstate[...] = jnp.zeros_like(dstate)
            dbias_ref[...] = jnp.zeros_like(dbias_ref)
            dalog_ref[...] = jnp.zeros_like(dalog_ref)
            ddvec_ref[...] = jnp.zeros_like(ddvec_ref)

        ii, jj, tril, dt_col, dt_row, a_row_p, cum_col, cum_row = _ssd_prelude(
            dtr_ref, dtrt_ref, brow_ref, bcol_ref, alrow_ref, alcol_ref, Q)
        eye = ii == jj
        lane_h = lax.broadcasted_iota(jnp.int32, (Q, 128), 1)
        row_h = lax.broadcasted_iota(jnp.int32, (Q, 128), 0)
        lane_g = lax.broadcasted_iota(jnp.int32, (Q, gw), 1) // HEAD_DIM
        lane_s = lax.broadcasted_iota(jnp.int32, (D_STATE, gw), 1) // HEAD_DIM
        dcum_mat = jnp.zeros((Q, 128), F32)
        ddt_mat = jnp.zeros((Q, 128), F32)
        dtot_row = jnp.zeros((1, 128), F32)
        for g in range(G):
            xg = xbc_ref[:, g * gw:(g + 1) * gw]
            dyg = dy_ref[:, g * gw:(g + 1) * gw]
            bb = xbc_ref[:, boff + g * D_STATE: boff + (g + 1) * D_STATE].astype(BF16)
            cb = xbc_ref[:, coff + g * D_STATE: coff + (g + 1) * D_STATE].astype(BF16)
            sg = st_ref[g]
            dsg = dstate[g]
            sb = sg.astype(BF16)
            dsb = dsg.astype(BF16)
            xb = xg.astype(BF16)
            gm = _dot_nt(cb, bb)
            cs = _dot(cb, sb)
            bds = _dot(bb, dsb)
            dxg = dvec_ref[:, g * gw:(g + 1) * gw] * dyg
            dgm = jnp.zeros((Q, Q), F32)
            esc = jnp.zeros((Q, gw), F32)
            wsc = jnp.zeros((Q, gw), F32)
            lam = jnp.zeros((1, gw), F32)
            dycs = dyg * cs
            xbds = xg * bds
            dss = dsg * sg
            for j in range(hpg):
                h = g * hpg + j
                ccol = _col_of(cum_col, lane_h, h)
                dcol = _col_of(dt_col, lane_h, h)
                drow = dt_row[h:h + 1, :]
                seg = ccol - cum_row[h:h + 1, :]
                decay = jnp.exp(jnp.where(tril, seg, -jnp.inf))
                hm = lane_g == j
                dyh = jnp.where(hm, dyg, 0.0).astype(BF16)
                gl = gm * decay
                mh = gl * drow
                dmf = _dot_nt(dyh, xb)
                dxg = dxg + _dot_tn(mh.astype(BF16), dyh)
                dgm = dgm + dmf * decay * drow
                n_p = dmf * gl
                n_m = n_p * drow
                rowsum_n = jnp.sum(n_m, axis=1, keepdims=True)
                colsum_n = jnp.sum(jnp.where(eye, jnp.sum(n_m, axis=0, keepdims=True), 0.0), axis=1, keepdims=True)
                colsum_np = jnp.sum(jnp.where(eye, jnp.sum(n_p, axis=0, keepdims=True), 0.0), axis=1, keepdims=True)
                tot = ccol[Q - 1:Q, :]
                e = jnp.exp(ccol)
                wexp = jnp.exp(tot - ccol)
                wcol = wexp * dcol
                lamh = jnp.exp(tot)
                yoff_t = jnp.sum(jnp.where(hm, dycs, 0.0), axis=1, keepdims=True) * e
                e_s = jnp.sum(jnp.where(hm, xbds, 0.0), axis=1, keepdims=True)
                ew = e_s * wcol
                dtot = jnp.sum(ew, axis=0, keepdims=True) + lamh * jnp.sum(
                    jnp.sum(jnp.where(lane_s == j, dss, 0.0), axis=1, keepdims=True), axis=0, keepdims=True)
                dcum_h = rowsum_n + yoff_t - colsum_n - ew
                ddt_h = colsum_np + e_s * wexp
                onehot = lane_h == h
                dcum_mat = jnp.where(onehot, dcum_h, dcum_mat)
                ddt_mat = jnp.where(onehot, ddt_h, ddt_mat)
                dtot_row = jnp.where(onehot[0:1], dtot, dtot_row)
                esc = jnp.where(hm, e, esc)
                wsc = jnp.where(hm, wcol, wsc)
                lam = jnp.where(hm[0:1], lamh, lam)
            dgb = dgm.astype(BF16)
            dye = (dyg * esc).astype(BF16)
            xw = (xg * wsc).astype(BF16)
            dxbc_ref[:, g * gw:(g + 1) * gw] = dxg + bds * wsc
            dxbc_ref[:, boff + g * D_STATE: boff + (g + 1) * D_STATE] = _dot_tn(dgb, cb) + _dot_nt(xw, dsb)
            dxbc_ref[:, coff + g * D_STATE: coff + (g + 1) * D_STATE] = _dot(dgb, bb) + _dot_nt(dye, sb)
            dstate[g] = dsg * lam + _dot_tn(cb, dye)
            ddvec_ref[:, g * gw:(g + 1) * gw] += jnp.sum(dyg * xg, axis=0, keepdims=True)
        dcum_mat = dcum_mat + jnp.where(row_h == Q - 1, dtot_row, 0.0)
        da = _dot3_left((ii <= jj).astype(BF16), dcum_mat)
        ddt = ddt_mat + da * a_row_p
        dalog_ref[...] += jnp.sum(da * dt_col, axis=0, keepdims=True) * a_row_p
        ddtr = ddt * _sigmoid(dtr_ref[...] + brow_ref[...])
        ddtr_ref[...] = ddtr
        dbias_ref[...] += jnp.sum(ddtr, axis=0, keepdims=True)

    full = lambda shape: pl.BlockSpec(shape, lambda c: (0,) * len(shape))
    rc = lambda c: nc - 1 - c
    return _call(
        body, name=name, grid=(nc,),
        in_specs=[pl.BlockSpec((Q, W), lambda c: (rc(c), 0)), pl.BlockSpec((Q, 128), lambda c: (rc(c), 0)),
                  pl.BlockSpec((H, Q), lambda c: (0, rc(c))), full((1, 128)), full((H, 1)), full((1, 128)), full((H, 1)),
                  full((1, DI)), pl.BlockSpec((Q, DI), lambda c: (rc(c), 0)),
                  pl.BlockSpec((None, G, D_STATE, gw), lambda c: (rc(c), 0, 0, 0))],
        out_specs=[pl.BlockSpec((Q, W), lambda c: (rc(c), 0)), pl.BlockSpec((Q, 128), lambda c: (rc(c), 0)),
                   full((1, 128)), full((1, 128)), full((1, DI))],
        out_shape=[jax.ShapeDtypeStruct((LP, W), F32), jax.ShapeDtypeStruct((LP, 128), F32),
                   jax.ShapeDtypeStruct((1, 128), F32), jax.ShapeDtypeStruct((1, 128), F32),
                   jax.ShapeDtypeStruct((1, DI), F32)],
        scratch_shapes=[pltpu.VMEM((G, D_STATE, gw), F32)], sem=("arbitrary",),
        args=(xbc, dtr, dtrt, brow, bcol, alrow, alcol, dvec, dy, states), carry=_pop_for_rows(xq, LP, W, 4.5))


def _gate_fwd(y, zx, g, *, name):
    LP, DI = y.shape
    gw = DI // SSM_GROUPS
    tr = _tile(LP, (256, 128))

    def body(y_ref, z_ref, g_ref, o_ref):
        for k in range(SSM_GROUPS):
            sl = slice(k * gw, (k + 1) * gw)
            z = z_ref[:, sl]
            t = y_ref[:, sl] * (z * _sigmoid(z))
            r = lax.rsqrt(jnp.mean(t * t, axis=1, keepdims=True) + RMS_EPS)
            o_ref[:, sl] = (t * r * g_ref[:, sl]).astype(BF16)

    row = pl.BlockSpec((tr, DI), lambda i: (i, 0))
    return pl.pallas_call(
        body, name=name, grid=(LP // tr,), in_specs=[row, row, pl.BlockSpec((1, DI), lambda i: (0, 0))],
        out_specs=row, out_shape=jax.ShapeDtypeStruct((LP, DI), BF16), compiler_params=_cp("parallel"),
    )(y, zx, g)


def _gate_bwd(y, zx, g, dyn, *, name):
    LP, DI = y.shape
    gw = DI // SSM_GROUPS
    tr = _tile(LP, (256, 128))

    def body(y_ref, z_ref, g_ref, d_ref, dy_ref, dz_ref, dg_ref):
        i = pl.program_id(0)

        @pl.when(i == 0)
        def _():
            dg_ref[...] = jnp.zeros_like(dg_ref)

        for k in range(SSM_GROUPS):
            sl = slice(k * gw, (k + 1) * gw)
            z = z_ref[:, sl]
            yv = y_ref[:, sl]
            sg = _sigmoid(z)
            sz = z * sg
            t = yv * sz
            r = lax.rsqrt(jnp.mean(t * t, axis=1, keepdims=True) + RMS_EPS)
            th = t * r
            d = d_ref[:, sl]
            dtn = d * g_ref[:, sl]
            dt_ = r * (dtn - th * jnp.mean(dtn * th, axis=1, keepdims=True))
            dg_ref[:, sl] += jnp.sum(d * th, axis=0, keepdims=True)
            dy_ref[:, sl] = dt_ * sz
            dz_ref[:, sl] = dt_ * yv * sg * (1.0 + z * (1.0 - sg))

    row = pl.BlockSpec((tr, DI), lambda i: (i, 0))
    vec = pl.BlockSpec((1, DI), lambda i: (0, 0))
    return pl.pallas_call(
        body, name=name, grid=(LP // tr,), in_specs=[row, row, vec, row], out_specs=[row, row, vec],
        out_shape=[jax.ShapeDtypeStruct((LP, DI), F32), jax.ShapeDtypeStruct((LP, DI), F32),
                   jax.ShapeDtypeStruct((1, DI), F32)],
        compiler_params=_cp("arbitrary"),
    )(y, zx, g, dyn)


EXP_ZERO = -104.0
LOG2E = 1.4426950408889634


def _dot2_right(x, t2_bf16):
    hi = x.astype(BF16)
    lo = (x - hi.astype(F32)).astype(BF16)
    return _dot(jnp.concatenate([hi, lo], axis=1), t2_bf16)


def _tri2(T, upper):
    r = lax.broadcasted_iota(jnp.int32, (2 * T, T), 0) % T
    c = lax.broadcasted_iota(jnp.int32, (2 * T, T), 1)
    return (r <= c if upper else r >= c).astype(BF16)


def _sb_tile(q, k_blk, lower2, valid=None):
    z = _dot_nt(q, k_blk)
    sp = jnp.maximum(z, 0.0) + jnp.log(1.0 + jnp.exp2(jnp.abs(z) * (-LOG2E)))
    if valid is not None:
        sp = jnp.where(valid, sp, 0.0)
    return z, sp, z - _dot2_right(sp, lower2)


def _sb_weights(zr, c, valid=None):
    w = jnp.exp(zr + c)
    return w if valid is None else jnp.where(valid, w, 0.0)


def _sb_fwd(q, k, v, zmax, *, name):
    H, LP, dh = q.shape
    T = ATT_BLOCK
    nq = LP // T
    assert nq < LANES

    def body(q_ref, k_ref, v_ref, zb_ref, o_ref, c_ref, kf_ref):
        i = pl.program_id(1)
        ii = lax.broadcasted_iota(jnp.int32, (T, T), 0)
        jj = lax.broadcasted_iota(jnp.int32, (T, T), 1)
        lane = lax.broadcasted_iota(jnp.int32, (T, LANES), 1)
        lower2 = _tri2(T, upper=False)
        qv = q_ref[...]
        zb = zb_ref[0:1, 0:1]

        def kv(kb):
            ks = pl.multiple_of(kb * T, T)
            return k_ref[pl.ds(ks, T), :], v_ref[pl.ds(ks, T), :]

        kd, vd = kv(i)
        diag = jj < ii
        _, sp, zr = _sb_tile(qv, kd, lower2, diag)
        acc = _dot(_sb_weights(zr, 0.0, diag).astype(BF16), vd)
        c = -jnp.sum(sp, axis=1, keepdims=True)
        c_ref[...] = jnp.zeros_like(c_ref)

        def alive(c):
            return jnp.max(c + zb) > EXP_ZERO

        def cond(carry):
            kb, _, _, live = carry
            return (kb >= 0) & live

        def step(carry):
            kb, c, acc, _ = carry
            kt, vt = kv(kb)
            _, sp, zr = _sb_tile(qv, kt, lower2)
            acc = acc + _dot(_sb_weights(zr, c).astype(BF16), vt)
            c_ref[...] = jnp.where(lane == kb, c, c_ref[...])
            c = c - jnp.sum(sp, axis=1, keepdims=True)
            return kb - 1, c, acc, alive(c)

        kb, _, acc, _ = lax.while_loop(cond, step, (i - 1, c, acc, alive(c)))
        o_ref[...] = acc
        kf_ref[...] = jnp.zeros_like(kf_ref) + (kb + 1).astype(F32)

    blk = pl.BlockSpec((None, T, dh), lambda h, i: (h, i, 0))
    cblk = pl.BlockSpec((None, T, LANES), lambda h, i: (h, i, 0))
    whole = pl.BlockSpec((None, LP, dh), lambda h, i: (h, 0, 0))
    return pl.pallas_call(
        body, name=name, grid=(H, nq), in_specs=[blk, whole, whole, pl.BlockSpec((1, LANES), lambda h, i: (0, 0))],
        out_specs=[blk, cblk, pl.BlockSpec((None, None, 8, LANES), lambda h, i: (h, i, 0, 0))],
        out_shape=[jax.ShapeDtypeStruct((H, LP, dh), F32), jax.ShapeDtypeStruct((H, LP, LANES), F32),
                   jax.ShapeDtypeStruct((H, nq, 8, LANES), F32)],
        compiler_params=_cp("parallel", "parallel"),
    )(q, k, v, zmax)


def _sb_bwd(kstart, q, k, v, cmat, do, *, name):
    H, LP, dh = q.shape
    T = ATT_BLOCK
    nq = LP // T

    def body(ks_ref, q_ref, k_ref, v_ref, c_ref, do_ref, dq_ref, dk_ref, dv_ref):
        h = pl.program_id(0)
        i = pl.program_id(1)

        @pl.when(i == 0)
        def _():
            dk_ref[...] = jnp.zeros_like(dk_ref)
            dv_ref[...] = jnp.zeros_like(dv_ref)

        ii = lax.broadcasted_iota(jnp.int32, (T, T), 0)
        jj = lax.broadcasted_iota(jnp.int32, (T, T), 1)
        lane = lax.broadcasted_iota(jnp.int32, (T, LANES), 1)
        lower2 = _tri2(T, upper=False)
        upper2 = _tri2(T, upper=True)
        qv = q_ref[...]
        dob = do_ref[...].astype(BF16)
        cm = c_ref[...]

        def tile(kb, cg, dq, valid=None):
            ks = pl.multiple_of(kb * T, T)
            k_blk = k_ref[pl.ds(ks, T), :]
            c = jnp.sum(jnp.where(lane == kb, cm, 0.0), axis=1, keepdims=True)
            z, sp, zr = _sb_tile(qv, k_blk, lower2, valid)
            w = _sb_weights(zr, c, valid)
            gw_ = w * _dot_nt(dob, v_ref[pl.ds(ks, T), :])
            gin = _dot2_right(gw_, upper2)
            dz = gw_ - jnp.exp(z - sp) * (cg + gin)
            if valid is not None:
                dz = jnp.where(valid, dz, 0.0)
            dz = dz.astype(BF16)
            dk_ref[pl.ds(ks, T), :] += _dot_tn(dz, qv)
            dv_ref[pl.ds(ks, T), :] += _dot_tn(w.astype(BF16), dob)
            return cg + jnp.sum(gw_, axis=1, keepdims=True), dq + _dot(dz, k_blk)

        carry = lax.fori_loop(ks_ref[h, i], i, lambda kb, cr: tile(kb, *cr),
                              (jnp.zeros((T, 1), F32), jnp.zeros((T, dh), F32)))
        _, dq = tile(i, *carry, valid=jj < ii)
        dq_ref[...] = dq

    blk = pl.BlockSpec((None, T, dh), lambda h, i, ks: (h, i, 0))
    cblk = pl.BlockSpec((None, T, LANES), lambda h, i, ks: (h, i, 0))
    whole = pl.BlockSpec((None, LP, dh), lambda h, i, ks: (h, 0, 0))
    sh = jax.ShapeDtypeStruct((H, LP, dh), F32)
    return pl.pallas_call(
        body, name=name,
        grid_spec=pltpu.PrefetchScalarGridSpec(
            num_scalar_prefetch=1, grid=(H, nq), in_specs=[blk, whole, whole, cblk, blk], out_specs=[blk, whole, whole]),
        out_shape=[sh, sh, sh], compiler_params=_cp("parallel", "arbitrary"),
    )(kstart, q, k, v, cmat, do)


def _loss_head(h, tgt, seq, *, name):
    LP, D = h.shape
    tr = _tile(LP, (384, 256, 128))

    def body(h_ref, t_ref, dh_ref, l_ref):
        i = pl.program_id(0)

        @pl.when(i == 0)
        def _():
            l_ref[...] = jnp.zeros_like(l_ref)

        row = lax.broadcasted_iota(jnp.int32, (tr, D), 0) + i * tr
        e = jnp.where((row >= N_META) & (row < N_META + seq), h_ref[...] - t_ref[...], 0.0)
        dh_ref[...] = e * (1.0 / D)
        l_ref[...] += jnp.sum(e * e, axis=0, keepdims=True) * (0.5 / D)

    row = pl.BlockSpec((tr, D), lambda i: (i, 0))
    return pl.pallas_call(
        body, name=name, grid=(LP // tr,), in_specs=[row, row], out_specs=[row, pl.BlockSpec((1, D), lambda i: (0, 0))],
        out_shape=[jax.ShapeDtypeStruct((LP, D), F32), jax.ShapeDtypeStruct((1, D), F32)],
        compiler_params=_cp("arbitrary"),
    )(h, tgt)


def _adamw(w, g, m, v, *, name):
    shape = w.shape
    C = shape[-1]
    R = math.prod(shape) // C
    tr = _tile(R, (512, 256, 128, 64, 32, 16, 8))
    c1 = 1.0 / (1.0 - ADAM_B1 ** ADAM_STEP)
    c2 = 1.0 / (1.0 - ADAM_B2 ** ADAM_STEP)

    def body(w_ref, g_ref, m_ref, v_ref, d_ref, nm_ref, nv_ref):
        gv = g_ref[...]
        nm = ADAM_B1 * m_ref[...] + (1.0 - ADAM_B1) * gv
        nv = ADAM_B2 * v_ref[...] + (1.0 - ADAM_B2) * (gv * gv)
        d_ref[...] = -ADAM_LR * ((nm * c1) / (jnp.sqrt(nv * c2) + ADAM_EPS) + ADAM_WD * w_ref[...])
        nm_ref[...] = nm
        nv_ref[...] = nv

    blk = pl.BlockSpec((tr, C), lambda i: (i, 0))
    sh = jax.ShapeDtypeStruct((R, C), F32)
    d, nm, nv = pl.pallas_call(
        body, name=name, grid=(R // tr,), in_specs=[blk] * 4, out_specs=[blk] * 3, out_shape=[sh] * 3,
        compiler_params=_cp("parallel"),
    )(w.reshape(R, C), g.reshape(R, C), m.reshape(R, C), v.reshape(R, C))
    return d.reshape(shape), nm.reshape(shape), nv.reshape(shape)


def _sum_rows(buf, *, name):
    n, R, C = buf.shape
    tr = _tile(R, (512, 256, 128, 64, 32, 16, 8))

    def body(b_ref, o_ref):
        acc = b_ref[0].astype(F32)
        for k in range(1, n):
            acc = acc + b_ref[k].astype(F32)
        o_ref[...] = acc

    return pl.pallas_call(
        body, name=name, grid=(R // tr,), in_specs=[pl.BlockSpec((n, tr, C), lambda i: (0, i, 0))],
        out_specs=pl.BlockSpec((tr, C), lambda i: (i, 0)), out_shape=jax.ShapeDtypeStruct((R, C), F32),
        compiler_params=_cp("parallel"),
    )(buf)


def _interleave(buf, *, name):
    n, R, C = buf.shape
    tr = _tile(R, (256, 128, 64, 32, 16))

    def body(b_ref, o_ref):
        for d in range(n):
            o_ref[:, d * C:(d + 1) * C] = b_ref[d]

    return pl.pallas_call(
        body, name=name, grid=(R // tr,), in_specs=[pl.BlockSpec((n, tr, C), lambda i: (0, i, 0))],
        out_specs=pl.BlockSpec((tr, n * C), lambda i: (i, 0)), out_shape=jax.ShapeDtypeStruct((R, n * C), buf.dtype),
        compiler_params=_cp("parallel"),
    )(buf)


def _deinterleave(x, *, out_dtype, name):
    R, NC = x.shape
    C = NC // N_DEV
    tr = _tile(R, (256, 128, 64, 32, 16))

    def body(x_ref, o_ref):
        for d in range(N_DEV):
            o_ref[d] = x_ref[:, d * C:(d + 1) * C].astype(out_dtype)

    return pl.pallas_call(
        body, name=name, grid=(R // tr,), in_specs=[pl.BlockSpec((tr, NC), lambda i: (i, 0))],
        out_specs=pl.BlockSpec((N_DEV, tr, C), lambda i: (0, i, 0)),
        out_shape=jax.ShapeDtypeStruct((N_DEV, R, C), out_dtype), compiler_params=_cp("parallel"),
    )(x)


def _mesh_pos():
    x, y, c = lax.axis_index("x"), lax.axis_index("y"), lax.axis_index("c")
    return x, y, c, 4 * x + 2 * y + c


def _peer(x, y, c, f):
    px, py, pc = (x + ((f >> 2) & 1)) % 2, (y + ((f >> 1) & 1)) % 2, (c + (f & 1)) % 2
    return (px, py, pc), 4 * px + 2 * py + pc


def _exchange(arrs, *, scatter, name):
    n = len(arrs)
    hbm = pl.BlockSpec(memory_space=pl.ANY)

    def body(*refs):
        xa = (refs[:n], refs[n:2 * n], *refs[2 * n:], scatter)
        _xchg_start(*xa)
        _xchg_wait(*xa)

    return list(pl.pallas_call(
        body, name=name, in_specs=[hbm] * n, out_specs=[hbm] * n, out_shape=_xchg_shapes(arrs, scatter),
        scratch_shapes=_xchg_scratch(n), compiler_params=pltpu.CompilerParams(has_side_effects=True),
    )(*arrs))


def _unheads(t):
    H, LP, dh = t.shape
    return t.transpose(1, 0, 2).reshape(LP, H * dh)


def _shard_rows(t):
    return t.reshape(N_DEV, t.shape[0] // N_DEV, t.shape[1])


def _ffn_fwd(h, g, W, l, s, xq):
    tag = f"{l}{s}"
    n = _rms_fwd(h, g, name=f"ffn_norm_{tag}")
    w1, w3 = W[("w13", l, s)]
    a, b, sw = _ffn_up(n, w1, w3, name=f"ffn_up_{tag}", xq=xq)
    h2 = _matmul([(sw, W[("w2", l, s)])], res=h, alpha=FFN_RES, name=f"ffn_down_{tag}", xq=xq)
    return h2, (h, n, a, b, sw)


def _ffn_bwd(dh, saved, g, W, l, s, xq, emit):
    tag = f"{l}{s}"
    h, n, a, b, sw = saved
    (w1, w3), w2 = W[("w13", l, s)], W[("w2", l, s)]
    da, db = _ffn_mid_bwd(dh, w2.T, a, b, name=f"ffn_mid_bwd_{tag}", xq=xq)
    dw2 = _matmul([(sw, dh)], trans_a=True, alpha=FFN_RES, out_dtype=BF16, name=f"ffn_dw2_{tag}", xq=xq)
    emit(("w2", l, s), [_shard_rows(dw2)])
    dn = _matmul([(da, w1.T), (db, w3.T)], name=f"ffn_dn_{tag}", xq=xq)
    dw1 = _matmul([(n, da)], trans_a=True, shards=N_DEV, out_dtype=BF16, name=f"ffn_dw1_{tag}", xq=xq)
    dw3 = _matmul([(n, db)], trans_a=True, shards=N_DEV, out_dtype=BF16, name=f"ffn_dw3_{tag}", xq=xq)
    emit(("w13", l, s), [dw1, dw3])
    dh_in, dg = _rms_bwd(h, g, dn, res=dh, name=f"ffn_norm_bwd_{tag}")
    return dh_in, dg


def _local_step(x, tgt, W, xq=None, recv=None):
    G = {}

    def emit(key, arrs):
        G[key] = arrs
        if xq is not None:
            xq.push(arrs, True, lambda res, key=key: recv.__setitem__(key, res))

    seq, D = x.shape
    L = N_META + seq
    LP = -(-L // ROW_ALIGN) * ROW_ALIGN
    pad = LP - L
    H_sb = D // HEAD_DIM
    DI = W["ssm_norm_g"].shape[-1]
    H_ssm = DI // HEAD_DIM
    CONV = DI + 2 * SSM_GROUPS * D_STATE
    ZX = DI + CONV

    h0 = jnp.concatenate([W["meta_tokens"], x, jnp.zeros((pad, D), F32)], axis=0)
    tgt_p = jnp.pad(tgt, ((N_META, pad), (0, 0)))
    ng = W["norm_g"]

    h1, sv_f00 = _ffn_fwd(h0, ng[0, 0], W, 0, 0, xq)
    u0 = _rms_fwd(h1, ng[0, 1], name="ssm_norm")
    w_in = W["ssm_in_proj"][0]
    w_zx = w_in[:, :ZX]
    w_dt = jnp.pad(w_in[:, ZX:], ((0, 0), (0, 128 - H_ssm)))
    zx = _matmul([(u0, w_zx)], name="ssm_in_zx", xq=xq)
    dtr = _matmul([(u0, w_dt)], name="ssm_in_dt")
    conv_w, conv_b = W["ssm_conv_w"][0], W["ssm_conv_b"][0]
    xbc = _conv_fwd(zx, conv_w, conv_b, DI, name="ssm_conv", xq=xq)
    dtrt = dtr[:, :H_ssm].T
    padh = lambda t: jnp.pad(t.reshape(1, H_ssm), ((0, 0), (0, 128 - H_ssm)))
    brow, bcol = padh(W["ssm_dt_bias"][0]), W["ssm_dt_bias"][0].reshape(H_ssm, 1)
    alrow, alcol = padh(W["ssm_a_log"][0]), W["ssm_a_log"][0].reshape(H_ssm, 1)
    dvec = jnp.repeat(W["ssm_d"][0], HEAD_DIM).reshape(1, DI)
    ssm_args = (xbc, dtr, dtrt, brow, bcol, alrow, alcol, dvec)
    y, states = _ssd_fwd(*ssm_args, name="ssd_fwd", xq=xq)
    sng = W["ssm_norm_g"].reshape(1, DI)
    yn = _gate_fwd(y, zx, sng, name="ssm_gate")
    w_out = W["ssm_out_proj"][0]
    h2 = _matmul([(yn, w_out)], res=h1, name="ssm_out", xq=xq)
    h3, sv_f01 = _ffn_fwd(h2, ng[0, 2], W, 0, 1, xq)

    kv_in = _rms_fwd(h3, W["kv_norm_g"], name="kv_norm")
    kraw = _matmul([(kv_in, W["w_k"])], shards=H_sb, name="kv_k")
    vh = _matmul([(kv_in, W["w_v"])], shards=H_sb, out_dtype=BF16, name="kv_v")
    kh = _rms_fwd(kraw.reshape(H_sb * LP, HEAD_DIM), W["k_norm_g"], name="k_headnorm").reshape(H_sb, LP, HEAD_DIM)

    h4, sv_f10 = _ffn_fwd(h3, ng[1, 0], W, 1, 0, xq)
    u1 = _rms_fwd(h4, ng[1, 1], name="sb_norm")
    qraw = _matmul([(u1, W["sb_w_q"][0])], shards=H_sb, name="sb_q")
    scale = HEAD_DIM ** -0.5
    qh = _rms_fwd(qraw.reshape(H_sb * LP, HEAD_DIM), W["sb_q_norm_g"][0], scale=scale,
                  name="q_headnorm").reshape(H_sb, LP, HEAD_DIM)
    zmax = 1.02 * math.sqrt(HEAD_DIM) * jnp.max(jnp.abs(W["sb_q_norm_g"])) * jnp.max(jnp.abs(W["k_norm_g"]))
    o, cmat, kfirst = _sb_fwd(qh, kh, vh, jnp.full((1, LANES), zmax, F32), name="sb_fwd")
    kstart = kfirst[:, :, 0, 0].astype(jnp.int32)
    o_flat = _unheads(o)
    h5 = _matmul([(o_flat, W["sb_w_o"][0])], res=h4, name="sb_out")
    h6, sv_f11 = _ffn_fwd(h5, ng[1, 2], W, 1, 1, xq)

    dh, lvec = _loss_head(h6, tgt_p, seq, name="loss_head")
    loss = jnp.sum(lvec)
    dng = [[None] * 3 for _ in range(2)]
    shard_rows = _shard_rows

    dh, dng[1][2] = _ffn_bwd(dh, sv_f11, ng[1, 2], W, 1, 1, xq, emit)
    g_wo = shard_rows(_matmul([(o_flat, dh)], trans_a=True, out_dtype=BF16, name="sb_dwo"))
    do = _matmul([(dh, W["sb_w_o"][0].T)], shards=H_sb, name="sb_do")
    dq, dk, dv = _sb_bwd(kstart, qh, kh, vh, cmat, do, name="sb_bwd")
    dqraw, dqg = _rms_bwd(qraw.reshape(H_sb * LP, HEAD_DIM), W["sb_q_norm_g"][0], dq.reshape(H_sb * LP, HEAD_DIM),
                          alpha=scale, name="q_headnorm_bwd")
    G["sb_q_norm_g"] = dqg
    dqraw = _unheads(dqraw.reshape(H_sb, LP, HEAD_DIM))
    g_wq = shard_rows(_matmul([(u1, dqraw)], trans_a=True, out_dtype=BF16, name="sb_dwq"))
    emit("sb", [g_wq, g_wo])
    du1 = _matmul([(dqraw, W["sb_w_q"][0].T)], name="sb_du")
    dh, dng[1][1] = _rms_bwd(h4, ng[1, 1], du1, res=dh, name="sb_norm_bwd")
    dh, dng[1][0] = _ffn_bwd(dh, sv_f10, ng[1, 0], W, 1, 0, xq, emit)

    dkraw, dkg = _rms_bwd(kraw.reshape(H_sb * LP, HEAD_DIM), W["k_norm_g"], dk.reshape(H_sb * LP, HEAD_DIM),
                          name="k_headnorm_bwd")
    G["k_norm_g"] = dkg.reshape(-1)
    dkraw = _unheads(dkraw.reshape(H_sb, LP, HEAD_DIM))
    dvf = _unheads(dv)
    g_wk = shard_rows(_matmul([(kv_in, dkraw)], trans_a=True, out_dtype=BF16, name="kv_dwk"))
    g_wv = shard_rows(_matmul([(kv_in, dvf)], trans_a=True, out_dtype=BF16, name="kv_dwv"))
    emit("kv", [g_wk, g_wv])
    dkv = _matmul([(dkraw, W["w_k"].T), (dvf, W["w_v"].T)], name="kv_din", xq=xq)
    dh, dkvg = _rms_bwd(h3, W["kv_norm_g"], dkv, res=dh, name="kv_norm_bwd")
    G["kv_norm_g"] = dkvg.reshape(-1)

    dh, dng[0][2] = _ffn_bwd(dh, sv_f01, ng[0, 2], W, 0, 1, xq, emit)
    emit("wout", [shard_rows(_matmul([(yn, dh)], trans_a=True, out_dtype=BF16, name="ssm_dwout", xq=xq))])
    dyn = _matmul([(dh, w_out.T)], name="ssm_dyn", xq=xq)
    dy, dz, dsng = _gate_bwd(y, zx, sng, dyn, name="ssm_gate_bwd")
    G["ssm_norm_g"] = dsng
    dxbc, ddtr, dbias, dalog, ddvec = _ssd_bwd(*ssm_args, dy, states, name="ssd_bwd", xq=xq)
    G["ssm_dt_bias"] = dbias[:, :H_ssm]
    G["ssm_a_log"] = dalog[:, :H_ssm]
    G["ssm_d"] = jnp.sum(ddvec.reshape(H_ssm, HEAD_DIM), axis=1).reshape(1, H_ssm)
    gpre, dcw, dcb = _conv_bwd_g(zx, conv_w, conv_b, dxbc, DI, name="ssm_conv_bwd_g", xq=xq)
    G["ssm_conv_w"] = dcw[:D_CONV][None]
    G["ssm_conv_b"] = dcb
    dxbc_pre = _conv_bwd_u(gpre, conv_w, name="ssm_conv_bwd_u", xq=xq)
    emit("win", [_deinterleave(jnp.concatenate([
        _matmul([(u0, dz)], trans_a=True, out_dtype=BF16, name="ssm_dwin_z", xq=xq),
        _matmul([(u0, dxbc_pre)], trans_a=True, out_dtype=BF16, name="ssm_dwin_x", xq=xq),
        _matmul([(u0, ddtr)], trans_a=True, out_dtype=BF16, name="ssm_dwin_dt")[:, :H_ssm]], axis=1),
        out_dtype=BF16, name="ssm_dwin_shards")])
    du0 = _matmul([(dz, w_zx[:, :DI].T)], name="ssm_du_z", xq=xq)
    du0 = _matmul([(dxbc_pre, w_zx[:, DI:].T)], res=du0, name="ssm_du_x", xq=xq)
    du0 = _matmul([(ddtr, w_dt.T)], res=du0, name="ssm_du_dt")
    dh, dng[0][1] = _rms_bwd(h1, ng[0, 1], du0, res=dh, name="ssm_norm_bwd")
    dh, dng[0][0] = _ffn_bwd(dh, sv_f00, ng[0, 0], W, 0, 0, xq, emit)

    G["norm_g"] = jnp.stack([jnp.concatenate(r, axis=0) for r in dng])
    G["meta_tokens"] = dh[:N_META]
    return loss, dh[N_META:L], G


WEIGHTS = ['meta_tokens', 'norm_g', 'ffn_w1', 'ffn_w3', 'ffn_w2', 'ssm_in_proj', 'ssm_conv_w', 'ssm_conv_b',
           'ssm_dt_bias', 'ssm_a_log', 'ssm_d', 'ssm_norm_g', 'ssm_out_proj', 'kv_norm_g', 'w_k', 'k_norm_g', 'w_v',
           'sb_w_q', 'sb_q_norm_g', 'sb_w_o']
SHARD_AXIS = {'meta_tokens': 1, 'norm_g': 2, 'ffn_w1': 3, 'ffn_w3': 3, 'ffn_w2': 2, 'ssm_in_proj': 2, 'ssm_conv_w': 2,
              'ssm_conv_b': 1, 'ssm_dt_bias': None, 'ssm_a_log': None, 'ssm_d': None, 'ssm_norm_g': 1,
              'ssm_out_proj': 1, 'kv_norm_g': None, 'w_k': 0, 'k_norm_g': None, 'w_v': 0, 'sb_w_q': 1,
              'sb_q_norm_g': None, 'sb_w_o': 1}
MATMUL_WEIGHTS = ('ffn_w1', 'ffn_w3', 'ffn_w2', 'ssm_in_proj', 'ssm_out_proj', 'w_k', 'w_v', 'sb_w_q', 'sb_w_o')
PACK_ROWS = 16


def _pack(arrs, dtype):
    flat = jnp.concatenate([a.reshape(-1).astype(dtype) for a in arrs])
    n = flat.shape[0]
    npad = -(-n // (LANES * PACK_ROWS)) * (LANES * PACK_ROWS)
    return jnp.pad(flat, (0, npad - n)).reshape(npad // LANES, LANES)


def _unpack_gathered(buf, names, shard_shapes, dtype):
    flat = buf.reshape(N_DEV, -1)
    out, off = {}, 0
    for n in names:
        shp = shard_shapes[n]
        size = math.prod(shp)
        t = flat[:, off:off + size].reshape((N_DEV,) + tuple(shp))
        off += size
        ax = SHARD_AXIS[n]
        t = jnp.moveaxis(t, 0, ax)
        full = shp[:ax] + (N_DEV * shp[ax],) + shp[ax + 1:]
        out[n] = t.reshape(full).astype(dtype)
    return out


def _to_shards(g, ax):
    shp = g.shape
    t = g.reshape(shp[:ax] + (N_DEV, shp[ax] // N_DEV) + shp[ax + 1:])
    return jnp.moveaxis(t, ax, 0).reshape(N_DEV, -1)


def kernel(x, meta_tokens, norm_g, ffn_w1, ffn_w3, ffn_w2, ssm_in_proj, ssm_conv_w, ssm_conv_b, ssm_dt_bias, ssm_a_log, ssm_d, ssm_norm_g, ssm_out_proj, kv_norm_g, w_k, k_norm_g, w_v, sb_w_q, sb_q_norm_g, sb_w_o, loss_target, m_meta_tokens, m_norm_g, m_ffn_w1, m_ffn_w3, m_ffn_w2, m_ssm_in_proj, m_ssm_conv_w, m_ssm_conv_b, m_ssm_dt_bias, m_ssm_a_log, m_ssm_d, m_ssm_norm_g, m_ssm_out_proj, m_kv_norm_g, m_w_k, m_k_norm_g, m_w_v, m_sb_w_q, m_sb_q_norm_g, m_sb_w_o, v_meta_tokens, v_norm_g, v_ffn_w1, v_ffn_w3, v_ffn_w2, v_ssm_in_proj, v_ssm_conv_w, v_ssm_conv_b, v_ssm_dt_bias, v_ssm_a_log, v_ssm_d, v_ssm_norm_g, v_ssm_out_proj, v_kv_norm_g, v_w_k, v_k_norm_g, v_w_v, v_sb_w_q, v_sb_q_norm_g, v_sb_w_o):
    shard = dict(meta_tokens=meta_tokens, norm_g=norm_g, ffn_w1=ffn_w1, ffn_w3=ffn_w3, ffn_w2=ffn_w2,
                 ssm_in_proj=ssm_in_proj, ssm_conv_w=ssm_conv_w, ssm_conv_b=ssm_conv_b, ssm_dt_bias=ssm_dt_bias,
                 ssm_a_log=ssm_a_log, ssm_d=ssm_d, ssm_norm_g=ssm_norm_g, ssm_out_proj=ssm_out_proj,
                 kv_norm_g=kv_norm_g, w_k=w_k, k_norm_g=k_norm_g, w_v=w_v, sb_w_q=sb_w_q, sb_q_norm_g=sb_q_norm_g,
                 sb_w_o=sb_w_o)
    mom_m = dict(zip(WEIGHTS, (m_meta_tokens, m_norm_g, m_ffn_w1, m_ffn_w3, m_ffn_w2, m_ssm_in_proj, m_ssm_conv_w,
                               m_ssm_conv_b, m_ssm_dt_bias, m_ssm_a_log, m_ssm_d, m_ssm_norm_g, m_ssm_out_proj,
                               m_kv_norm_g, m_w_k, m_k_norm_g, m_w_v, m_sb_w_q, m_sb_q_norm_g, m_sb_w_o)))
    mom_v = dict(zip(WEIGHTS, (v_meta_tokens, v_norm_g, v_ffn_w1, v_ffn_w3, v_ffn_w2, v_ssm_in_proj, v_ssm_conv_w,
                               v_ssm_conv_b, v_ssm_dt_bias, v_ssm_a_log, v_ssm_d, v_ssm_norm_g, v_ssm_out_proj,
                               v_kv_norm_g, v_w_k, v_k_norm_g, v_w_v, v_sb_w_q, v_sb_q_norm_g, v_sb_w_o)))
    sharded = [n for n in WEIGHTS if SHARD_AXIS[n] is not None]
    replicated = [n for n in WEIGHTS if SHARD_AXIS[n] is None]
    small = [n for n in sharded if n not in MATMUL_WEIGHTS]
    shapes = {n: tuple(shard[n].shape) for n in WEIGHTS}
    D = x.shape[-1]
    bf = lambda t: t.astype(BF16)
    full_rows = lambda t: t.reshape(N_DEV * t.shape[1], t.shape[2])
    cols = lambda l, s: jnp.concatenate([bf(ffn_w1[l, s]), bf(ffn_w3[l, s])], axis=0)
    xq, recv = _Queue(), {}
    W = _Weights(xq)
    W.update({n: shard[n] for n in replicated})

    def have_w13(l, s):
        def done(res):
            t = _interleave(res[0], name=f"weights_w13_{l}{s}")
            W[("w13", l, s)] = (t[:D], t[D:])
        return done

    def have_w2(l, s):
        return lambda res: W.__setitem__(("w2", l, s), full_rows(res[0]))

    def have_ssm(res):
        W["ssm_in_proj"] = _interleave(res[0], name="weights_win")[None]
        W["ssm_out_proj"] = full_rows(res[1])[None]

    def have_attn(res):
        W["w_k"], W["w_v"] = full_rows(res[0]), full_rows(res[1])
        W["sb_w_q"], W["sb_w_o"] = full_rows(res[2])[None], full_rows(res[3])[None]

    first = _exchange([cols(0, 0), bf(ffn_w2[0, 0]), _pack([shard[n] for n in small], F32)], scatter=False,
                      name="gather_first")
    have_w13(0, 0)(first[:1])
    have_w2(0, 0)(first[1:2])
    W.update(_unpack_gathered(first[2], small, shapes, F32))
    xq.push([bf(ssm_in_proj[0]), bf(ssm_out_proj[0])], False, have_ssm)
    xq.push([bf(ffn_w2[0, 1])], False, have_w2(0, 1))
    xq.push([cols(0, 1)], False, have_w13(0, 1))
    xq.push([bf(w_k), bf(w_v), bf(sb_w_q[0]), bf(sb_w_o[0])], False, have_attn)
    xq.push([cols(1, 0)], False, have_w13(1, 0))
    xq.push([bf(ffn_w2[1, 0])], False, have_w2(1, 0))
    xq.push([cols(1, 1)], False, have_w13(1, 1))
    xq.push([bf(ffn_w2[1, 1])], False, have_w2(1, 1))

    loss, dx, G = _local_step(x[0], loss_target[0], W, xq, recv)
    loss = lax.psum(loss, ("x", "y", "c"))

    send = jnp.concatenate([_to_shards(G[n], SHARD_AXIS[n]) for n in small], axis=1)
    n_el = send.shape[1]
    npad = -(-n_el // (LANES * PACK_ROWS)) * (LANES * PACK_ROWS)
    send = jnp.pad(send, ((0, 0), (0, npad - n_el))).reshape(N_DEV, npad // LANES, LANES)
    xq.push([send], True, lambda res: recv.__setitem__("small", res))
    xq.flush("scatter_last")
    rep = _pack([G[n] for n in replicated], F32)
    rep_sum = _sum_rows(_exchange([rep], scatter=False, name="gather_small_grads")[0], name="sum_small_grads").reshape(-1)
    summed = {k: [_sum_rows(t, name=f"sum_{'_'.join(map(str, k)) if isinstance(k, tuple) else k}_{i}")
                  for i, t in enumerate(v)] for k, v in recv.items()}

    grads, off = {}, 0
    small_sum = summed["small"][0].reshape(-1)
    for n in small:
        size = math.prod(shapes[n])
        grads[n] = small_sum[off:off + size].reshape(shapes[n])
        off += size
    off = 0
    for n in replicated:
        size = math.prod(shapes[n])
        grads[n] = rep_sum[off:off + size].reshape(shapes[n])
        off += size
    ls = [(l, s) for l in range(2) for s in range(2)]
    grads["ffn_w1"] = jnp.stack([summed[("w13", l, s)][0] for l, s in ls]).reshape(shapes["ffn_w1"])
    grads["ffn_w3"] = jnp.stack([summed[("w13", l, s)][1] for l, s in ls]).reshape(shapes["ffn_w3"])
    grads["ffn_w2"] = jnp.stack([summed[("w2", l, s)][0] for l, s in ls]).reshape(shapes["ffn_w2"])
    grads["ssm_in_proj"] = summed["win"][0][None]
    grads["ssm_out_proj"] = summed["wout"][0][None]
    grads["w_k"], grads["w_v"] = summed["kv"]
    grads["sb_w_q"], grads["sb_w_o"] = summed["sb"][0][None], summed["sb"][1][None]

    delta, new_m, new_v = {}, {}, {}
    for n in WEIGHTS:
        w2 = shard[n].reshape(1, -1) if shard[n].ndim == 1 else shard[n]
        r2 = lambda t: t.reshape(w2.shape)
        d, nm, nv = _adamw(w2, r2(grads[n]), r2(mom_m[n]), r2(mom_v[n]), name=f"adamw_{n}")
        delta[n], new_m[n], new_v[n] = (t.reshape(shapes[n]) for t in (d, nm, nv))

    return (loss, dx[None], *[grads[n] for n in WEIGHTS], *[delta[n] for n in WEIGHTS],
            *[new_m[n] for n in WEIGHTS], *[new_v[n] for n in WEIGHTS])
```

```python
import functools
import math

import jax
import jax.numpy as jnp
from jax import lax
from jax.experimental import pallas as pl
from jax.experimental.pallas import tpu as pltpu

F32 = jnp.float32
BF16 = jnp.bfloat16
RMS_EPS = 1e-6
N_META = 16
HEAD_DIM = 64
SSM_GROUPS = 8
D_STATE = 128
D_CONV = 4
FFN_RES = 0.5
ADAM_LR, ADAM_B1, ADAM_B2, ADAM_EPS, ADAM_WD, ADAM_STEP = 0.001, 0.9, 0.999, 1e-08, 0.01, 10
N_DEV = 8
SSD_CHUNK = 128
ATT_BLOCK = 256
ROW_ALIGN = 768
VMEM_LIMIT_V7X = 48 * 1024 * 1024
MATMUL_VMEM_BUDGET_V7X = 30 * 1024 * 1024
MESH = pl.DeviceIdType.MESH
LANES = 128


def _cp(*sem):
    return pltpu.CompilerParams(dimension_semantics=sem if sem else None, vmem_limit_bytes=VMEM_LIMIT_V7X)


def _tile(n, cands):
    for c in cands:
        if n % c == 0:
            return c
    return n


def _softplus(x):
    return jnp.maximum(x, 0.0) + jnp.log(1.0 + jnp.exp(-jnp.abs(x)))


def _sigmoid(x):
    return 1.0 / (1.0 + jnp.exp(-x))


def _split3(x):
    hi = x.astype(BF16)
    r1 = x - hi.astype(F32)
    mid = r1.astype(BF16)
    lo = (r1 - mid.astype(F32)).astype(BF16)
    return hi, mid, lo


def _dot(a, b):
    return jnp.dot(a, b, preferred_element_type=F32)


def _dot_nt(a, b):
    return lax.dot_general(a, b, (((1,), (1,)), ((), ())), preferred_element_type=F32)


def _dot_tn(a, b):
    return lax.dot_general(a, b, (((0,), (0,)), ((), ())), preferred_element_type=F32)


def _dot3_left(t_bf16, x):
    hi, mid, lo = _split3(x)
    return _dot(t_bf16, hi) + _dot(t_bf16, mid) + _dot(t_bf16, lo)


def _dot3_right(x, t_bf16):
    hi, mid, lo = _split3(x)
    return _dot(hi, t_bf16) + _dot(mid, t_bf16) + _dot(lo, t_bf16)


CARRIER_MIN_FLOP = 4e10


EXCHANGE_US_PER_MB = 94.0
MATMUL_TFLOPS = 650.0


class _Carry:
    def __init__(self, arrs, scatter, done):
        self.arrs, self.scatter, self.done = list(arrs), scatter, done
        per_peer = sum(math.prod(a.shape[1:] if scatter else a.shape) * a.dtype.itemsize for a in self.arrs)
        self.us = EXCHANGE_US_PER_MB * per_peer / 2 ** 20


class _Queue:
    def __init__(self):
        self.items = []

    def push(self, arrs, scatter, done):
        self.items.append(_Carry(arrs, scatter, done))

    def pop(self, kernel_us):
        for k, it in enumerate(self.items):
            if not it.scatter or it.us <= 1.15 * kernel_us:
                return self.items.pop(k)
            if k == 0 and len(self.items) > 4:
                return self.items.pop(0)
        return None

    def flush(self, name):
        k = 0
        while self.items:
            it = self.items.pop(0)
            it.done(_exchange(it.arrs, scatter=it.scatter, name=f"{name}_{k}"))
            k += 1


class _Weights(dict):
    def __init__(self, xq):
        super().__init__()
        self.xq, self.fetched = xq, 0

    def __missing__(self, key):
        while not dict.__contains__(self, key) and self.xq.items:
            it = self.xq.items.pop(0)
            it.done(_exchange(it.arrs, scatter=it.scatter, name=f"gather_now_{self.fetched}"))
            self.fetched += 1
        return dict.__getitem__(self, key)


ELEMENTWISE_US_PER_MB = 1.6


def _pop_for_rows(xq, rows, cols, us_per_mb=ELEMENTWISE_US_PER_MB):
    return xq.pop(us_per_mb * rows * cols * 4 / 2 ** 20) if xq is not None else None


def _xchg_shapes(arrs, scatter):
    return [jax.ShapeDtypeStruct((N_DEV,) + tuple(a.shape[1:] if scatter else a.shape), a.dtype) for a in arrs]


def _xchg_scratch(n):
    return [pltpu.SemaphoreType.DMA((n, N_DEV - 1)), pltpu.SemaphoreType.DMA((n, N_DEV - 1)),
            pltpu.SemaphoreType.DMA((n,))]


def _xchg_copies(srcs, dsts, send_sems, recv_sems, local_sems, scatter, with_recv):
    x, y, c, me = _mesh_pos()
    own, sends, recvs = [], [], []
    for a, (s, d) in enumerate(zip(srcs, dsts)):
        own.append(pltpu.make_async_copy(s.at[me] if scatter else s, d.at[me], local_sems.at[a]))
        for f in range(1, N_DEV):
            peer, pid = _peer(x, y, c, f)
            for row, lst in ((me, sends), (pid, recvs)) if with_recv else ((me, sends),):
                lst.append(pltpu.make_async_remote_copy(
                    src_ref=s.at[pid] if scatter else s, dst_ref=d.at[row], send_sem=send_sems.at[a, f - 1],
                    recv_sem=recv_sems.at[a, f - 1], device_id=peer, device_id_type=MESH))
    return own, sends, recvs


def _xchg_start(*a):
    own, sends, _ = _xchg_copies(*a, with_recv=False)
    for cp in own + sends:
        cp.start()


def _xchg_wait(*a):
    own, sends, recvs = _xchg_copies(*a, with_recv=True)
    for snd, rcv in zip(sends, recvs):
        snd.wait_send()
        rcv.wait_recv()
    for cp in own:
        cp.wait()


def _call(body, *, name, grid, in_specs, out_specs, out_shape, scratch_shapes=(), sem, args, carry=None):
    n_in, n_out, n_scr = len(in_specs), len(out_specs), len(scratch_shapes)
    if carry is None:
        return pl.pallas_call(
            body, name=name, grid=grid, in_specs=list(in_specs), out_specs=list(out_specs), out_shape=list(out_shape),
            scratch_shapes=list(scratch_shapes), compiler_params=_cp(*sem))(*args)
    n = len(carry.arrs)
    hbm = pl.BlockSpec(memory_space=pl.ANY)

    def wrapped(*refs):
        ins, csrc = refs[:n_in], refs[n_in:n_in + n]
        outs, cdst = refs[n_in + n:n_in + n + n_out], refs[n_in + n + n_out:n_in + 2 * n + n_out]
        scr = refs[n_in + 2 * n + n_out:]
        xa = (csrc, cdst, *scr[n_scr:], carry.scatter)
        pid = [pl.program_id(a) for a in range(len(grid))]
        first = functools.reduce(jnp.logical_and, [p == 0 for p in pid])
        last = functools.reduce(jnp.logical_and, [p == g - 1 for p, g in zip(pid, grid)])

        @pl.when(first)
        def _():
            _xchg_start(*xa)

        body(*ins, *outs, *scr[:n_scr])

        @pl.when(last)
        def _():
            _xchg_wait(*xa)

    res = pl.pallas_call(
        wrapped, name=name, grid=grid, in_specs=list(in_specs) + [hbm] * n, out_specs=list(out_specs) + [hbm] * n,
        out_shape=list(out_shape) + _xchg_shapes(carry.arrs, carry.scatter),
        scratch_shapes=list(scratch_shapes) + _xchg_scratch(n), compiler_params=_cp(*["arbitrary"] * len(grid)),
    )(*args, *carry.arrs)
    carry.done(list(res[n_out:]))
    return list(res[:n_out])


def _matmul(pairs, *, name, out_dtype=F32, trans_a=False, res=None, alpha=1.0, shards=None, xq=None,
            tm=None, tn=None, tk=None):
    a0, b0 = pairs[0]
    if trans_a:
        K, M = a0.shape
    else:
        M, K = a0.shape
    N = b0.shape[1]
    npair = len(pairs)
    has_res = res is not None
    tm = tm or _tile(M, (768, 512, 1408, 384, 256, 128))
    tk = tk or _tile(K, (1024, 1408, 768, 512, 256, 128))
    if tn is None:
        sa, sb, so = a0.dtype.itemsize, b0.dtype.itemsize, jnp.dtype(out_dtype).itemsize
        for tn in ([N] if shards else [c for c in (2048, 1536, 1408, 1024, 512, 384, 256, 128) if N % c == 0] or [N]):
            if (2 * npair * tk * (tm * sa + tn * sb) + tm * tn * (4 + 2 * so + (8 if has_res else 0))
                    <= MATMUL_VMEM_BUDGET_V7X):
                break
    nk = K // tk
    cs = N // shards if shards else None

    def body(*refs):
        o_ref, acc = refs[-2], refs[-1]
        k = pl.program_id(2)

        @pl.when(k == 0)
        def _():
            acc[...] = jnp.zeros_like(acc)

        part = None
        for p in range(npair):
            a = refs[2 * p][...].astype(BF16)
            b = refs[2 * p + 1][...].astype(BF16)
            d = _dot_tn(a, b) if trans_a else _dot(a, b)
            part = d if part is None else part + d
        acc[...] += part

        @pl.when(k == nk - 1)
        def _():
            if shards:
                for d in range(shards):
                    v = acc[:, d * cs:(d + 1) * cs]
                    o_ref[d] = (v * alpha if alpha != 1.0 else v).astype(out_dtype)
                return
            v = acc[...]
            if alpha != 1.0:
                v = v * alpha
            if has_res:
                v = refs[2 * npair][...] + v
            o_ref[...] = v.astype(out_dtype)

    if trans_a:
        a_spec = pl.BlockSpec((tk, tm), lambda i, j, k: (k, i))
    else:
        a_spec = pl.BlockSpec((tm, tk), lambda i, j, k: (i, k))
    b_spec = pl.BlockSpec((tk, tn), lambda i, j, k: (k, j))
    if shards:
        assert not has_res and tn == N
        o_spec = pl.BlockSpec((shards, tm, cs), lambda i, j, k: (0, i, 0))
        out_shape = jax.ShapeDtypeStruct((shards, M, cs), out_dtype)
    else:
        o_spec = pl.BlockSpec((tm, tn), lambda i, j, k: (i, j))
        out_shape = jax.ShapeDtypeStruct((M, N), out_dtype)
    in_specs, args = [], []
    for a, b in pairs:
        in_specs += [a_spec, b_spec]
        args += [a, b]
    if has_res:
        in_specs.append(o_spec)
        args.append(res)
    flop = 2.0 * npair * M * N * K
    carry = xq.pop(flop / MATMUL_TFLOPS / 1e6) if (xq is not None and flop >= CARRIER_MIN_FLOP) else None
    return _call(body, name=name, grid=(M // tm, N // tn, nk), in_specs=in_specs, out_specs=[o_spec],
                 out_shape=[out_shape], scratch_shapes=[pltpu.VMEM((tm, tn), F32)],
                 sem=("parallel", "parallel", "arbitrary"), args=args, carry=carry)[0]


def _rms_fwd(h, g, *, name, scale=1.0):
    R, D = h.shape
    tr = _tile(R, (2048, 1024, 768, 512, 256, 128)) if D <= 128 else _tile(R, (384, 256, 128))

    def body(h_ref, g_ref, o_ref):
        x = h_ref[...]
        r = lax.rsqrt(jnp.mean(x * x, axis=1, keepdims=True) + RMS_EPS)
        y = x * r * g_ref[...]
        if scale != 1.0:
            y = y * scale
        o_ref[...] = y.astype(BF16)

    return pl.pallas_call(
        body, name=name, grid=(R // tr,),
        in_specs=[pl.BlockSpec((tr, D), lambda i: (i, 0)), pl.BlockSpec((1, D), lambda i: (0, 0))],
        out_specs=pl.BlockSpec((tr, D), lambda i: (i, 0)),
        out_shape=jax.ShapeDtypeStruct((R, D), BF16), compiler_params=_cp("parallel"),
    )(h, g.reshape(1, D))


def _rms_bwd(h, g, dn, res=None, *, name, alpha=1.0):
    R, D = h.shape
    tr = _tile(R, (2048, 1024, 768, 512, 256, 128)) if D <= 128 else _tile(R, (384, 256, 128))
    has_res = res is not None

    def body(*refs):
        h_ref, g_ref, dn_ref = refs[:3]
        dh_ref, dg_ref = refs[-2], refs[-1]
        i = pl.program_id(0)

        @pl.when(i == 0)
        def _():
            dg_ref[...] = jnp.zeros_like(dg_ref)

        x = h_ref[...]
        r = lax.rsqrt(jnp.mean(x * x, axis=1, keepdims=True) + RMS_EPS)
        xh = x * r
        d = dn_ref[...].astype(F32)
        if alpha != 1.0:
            d = d * alpha
        dng = d * g_ref[...]
        m = jnp.mean(dng * xh, axis=1, keepdims=True)
        dh = r * (dng - xh * m)
        if has_res:
            dh = dh + refs[3][...]
        dh_ref[...] = dh
        dg_ref[...] += jnp.sum(d * xh, axis=0, keepdims=True)

    row = pl.BlockSpec((tr, D), lambda i: (i, 0))
    vec = pl.BlockSpec((1, D), lambda i: (0, 0))
    in_specs = [row, vec, row] + ([row] if has_res else [])
    args = [h, g.reshape(1, D), dn] + ([res] if has_res else [])
    return pl.pallas_call(
        body, name=name, grid=(R // tr,), in_specs=in_specs, out_specs=[row, vec],
        out_shape=[jax.ShapeDtypeStruct((R, D), F32), jax.ShapeDtypeStruct((1, D), F32)],
        compiler_params=_cp("arbitrary"),
    )(*args)


def _ffn_up(n, w1, w3, *, name, xq=None):
    M, K = n.shape
    N = w1.shape[1]
    tm = _tile(M, (384, 256, 128))
    tn = _tile(N, (1408, 512, 256, 128))

    def body(n_ref, w1_ref, w3_ref, da_ref, db_ref, s_ref):
        x = n_ref[...]
        a = _dot(x, w1_ref[...])
        b = _dot(x, w3_ref[...])
        sg = _sigmoid(a)
        silu = a * sg
        da_ref[...] = (b * sg * (1.0 + a * (1.0 - sg))).astype(BF16)
        db_ref[...] = silu.astype(BF16)
        s_ref[...] = (silu * b).astype(BF16)

    o_spec = pl.BlockSpec((tm, tn), lambda j, i: (i, j))
    w_spec = pl.BlockSpec((K, tn), lambda j, i: (0, j))
    sh = jax.ShapeDtypeStruct((M, N), BF16)
    return _call(body, name=name, grid=(N // tn, M // tm),
                 in_specs=[pl.BlockSpec((tm, K), lambda j, i: (i, 0)), w_spec, w_spec],
                 out_specs=[o_spec, o_spec, o_spec], out_shape=[sh, sh, sh], sem=("parallel", "parallel"),
                 args=(n, w1, w3), carry=xq.pop(4.0 * M * N * K / MATMUL_TFLOPS / 1e6) if xq is not None else None)


def _ffn_mid_bwd(dh, w2t, s_a, s_b, *, name, xq=None):
    M, K = dh.shape
    N = w2t.shape[1]
    tm = _tile(M, (384, 256, 128))
    tn = _tile(N, (1408, 512, 256, 128))

    def body(dh_ref, w_ref, a_ref, b_ref, da_ref, db_ref):
        ds = _dot(dh_ref[...].astype(BF16), w_ref[...]) * FFN_RES
        da_ref[...] = (ds * a_ref[...].astype(F32)).astype(BF16)
        db_ref[...] = (ds * b_ref[...].astype(F32)).astype(BF16)

    o_spec = pl.BlockSpec((tm, tn), lambda j, i: (i, j))
    sh = jax.ShapeDtypeStruct((M, N), BF16)
    return _call(body, name=name, grid=(N // tn, M // tm),
                 in_specs=[pl.BlockSpec((tm, K), lambda j, i: (i, 0)), pl.BlockSpec((K, tn), lambda j, i: (0, j)),
                           o_spec, o_spec],
                 out_specs=[o_spec, o_spec], out_shape=[sh, sh], sem=("parallel", "parallel"),
                 args=(dh, w2t, s_a, s_b), carry=xq.pop(2.0 * M * N * K / MATMUL_TFLOPS / 1e6) if xq is not None else None)


def _conv_pre(xx, w_ref, b_ref, tr):
    acc = None
    for k in range(D_CONV):
        sh = D_CONV - 1 - k
        v = (pltpu.roll(xx, sh, 0) if sh else xx)[8:8 + tr]
        t = w_ref[k:k + 1, :] * v
        acc = t if acc is None else acc + t
    return acc + b_ref[...]


def _conv_fwd(zx, w, b, col_off, *, name, xq=None):
    LP = zx.shape[0]
    C = w.shape[1]
    tr = _tile(LP, (256, 128))
    tc = _tile(C, (512, 256, 128))
    co = col_off // tc

    def body(cur_ref, prev_ref, w_ref, b_ref, o_ref):
        i = pl.program_id(0)
        prev = jnp.where(i == 0, 0.0, prev_ref[...])
        pre = _conv_pre(jnp.concatenate([prev, cur_ref[...]], axis=0), w_ref, b_ref, tr)
        o_ref[...] = pre * _sigmoid(pre)

    return _call(
        body, name=name, grid=(LP // tr, C // tc),
        in_specs=[pl.BlockSpec((tr, tc), lambda i, j: (i, j + co)),
                  pl.BlockSpec((8, tc), lambda i, j: (jnp.maximum(i * (tr // 8) - 1, 0), j + co)),
                  pl.BlockSpec((D_CONV, tc), lambda i, j: (0, j)), pl.BlockSpec((1, tc), lambda i, j: (0, j))],
        out_specs=[pl.BlockSpec((tr, tc), lambda i, j: (i, j))],
        out_shape=[jax.ShapeDtypeStruct((LP, C), F32)], sem=("parallel", "parallel"),
        args=(zx, zx, w, b.reshape(1, C)), carry=_pop_for_rows(xq, LP, C))[0]


def _conv_bwd_g(zx, w, b, dact, col_off, *, name, xq=None):
    LP = zx.shape[0]
    C = w.shape[1]
    tr = _tile(LP, (256, 128))
    tc = _tile(C, (512, 256, 128))
    co = col_off // tc

    def body(cur_ref, prev_ref, w_ref, b_ref, d_ref, g_ref, dw_ref, db_ref):
        i = pl.program_id(1)

        @pl.when(i == 0)
        def _():
            dw_ref[...] = jnp.zeros_like(dw_ref)
            db_ref[...] = jnp.zeros_like(db_ref)

        prev = jnp.where(i == 0, 0.0, prev_ref[...])
        xx = jnp.concatenate([prev, cur_ref[...]], axis=0)
        pre = _conv_pre(xx, w_ref, b_ref, tr)
        sg = _sigmoid(pre)
        g = d_ref[...] * sg * (1.0 + pre * (1.0 - sg))
        g_ref[...] = g
        db_ref[...] += jnp.sum(g, axis=0, keepdims=True)
        rows = []
        for k in range(D_CONV):
            sh = D_CONV - 1 - k
            v = (pltpu.roll(xx, sh, 0) if sh else xx)[8:8 + tr]
            rows.append(jnp.sum(g * v, axis=0, keepdims=True))
        rows.append(jnp.zeros((8 - D_CONV, tc), F32))
        dw_ref[...] += jnp.concatenate(rows, axis=0)

    return _call(
        body, name=name, grid=(C // tc, LP // tr),
        in_specs=[pl.BlockSpec((tr, tc), lambda j, i: (i, j + co)),
                  pl.BlockSpec((8, tc), lambda j, i: (jnp.maximum(i * (tr // 8) - 1, 0), j + co)),
                  pl.BlockSpec((D_CONV, tc), lambda j, i: (0, j)), pl.BlockSpec((1, tc), lambda j, i: (0, j)),
                  pl.BlockSpec((tr, tc), lambda j, i: (i, j))],
        out_specs=[pl.BlockSpec((tr, tc), lambda j, i: (i, j)), pl.BlockSpec((8, tc), lambda j, i: (0, j)),
                   pl.BlockSpec((1, tc), lambda j, i: (0, j))],
        out_shape=[jax.ShapeDtypeStruct((LP, C), F32), jax.ShapeDtypeStruct((8, C), F32), jax.ShapeDtypeStruct((1, C), F32)],
        sem=("parallel", "arbitrary"), args=(zx, zx, w, b.reshape(1, C), dact), carry=_pop_for_rows(xq, LP, C, 2.0))


def _conv_bwd_u(g, w, *, name, xq=None):
    LP, C = g.shape
    tr = _tile(LP, (256, 128))
    tc = _tile(C, (512, 256, 128))
    nb = LP // tr

    def body(cur_ref, nxt_ref, w_ref, o_ref):
        i = pl.program_id(0)
        nxt = jnp.where(i == nb - 1, 0.0, nxt_ref[...])
        xx = jnp.concatenate([cur_ref[...], nxt], axis=0)
        acc = None
        for k in range(D_CONV):
            sh = D_CONV - 1 - k
            v = (pltpu.roll(xx, tr + 8 - sh, 0) if sh else xx)[:tr]
            t = w_ref[k:k + 1, :] * v
            acc = t if acc is None else acc + t
        o_ref[...] = acc

    return _call(
        body, name=name, grid=(nb, C // tc),
        in_specs=[pl.BlockSpec((tr, tc), lambda i, j: (i, j)),
                  pl.BlockSpec((8, tc), lambda i, j: (jnp.minimum((i + 1) * (tr // 8), LP // 8 - 1), j)),
                  pl.BlockSpec((D_CONV, tc), lambda i, j: (0, j))],
        out_specs=[pl.BlockSpec((tr, tc), lambda i, j: (i, j))],
        out_shape=[jax.ShapeDtypeStruct((LP, C), F32)], sem=("parallel", "parallel"),
        args=(g, g, w), carry=_pop_for_rows(xq, LP, C))[0]


def _ssd_prelude(dtr_ref, dtrt_ref, brow_ref, bcol_ref, alrow_ref, alcol_ref, Q):
    ii = lax.broadcasted_iota(jnp.int32, (Q, Q), 0)
    jj = lax.broadcasted_iota(jnp.int32, (Q, Q), 1)
    tril = ii >= jj
    dt_col = _softplus(dtr_ref[...] + brow_ref[...])
    a_row_p = -jnp.exp(alrow_ref[...])
    dt_row = _softplus(dtrt_ref[...] + bcol_ref[...])
    a_col_p = -jnp.exp(alcol_ref[...])
    cum_col = _dot3_left(tril.astype(BF16), dt_col * a_row_p)
    cum_row = _dot3_right(dt_row * a_col_p, (ii <= jj).astype(BF16))
    return ii, jj, tril, dt_col, dt_row, a_row_p, cum_col, cum_row


def _col_of(mat, lane_idx, h):
    return jnp.sum(jnp.where(lane_idx == h, mat, 0.0), axis=1, keepdims=True)


def _ssd_fwd(xbc, dtr, dtrt, brow, bcol, alrow, alcol, dvec, *, name, xq=None):
    LP = xbc.shape[0]
    Q = SSD_CHUNK
    nc = LP // Q
    G = SSM_GROUPS
    DI = dvec.shape[1]
    gw = DI // G
    hpg = gw // HEAD_DIM
    H = G * hpg
    boff, coff = DI, DI + G * D_STATE

    def body(xbc_ref, dtr_ref, dtrt_ref, brow_ref, bcol_ref, alrow_ref, alcol_ref, dvec_ref, y_ref, st_ref, state):
        c = pl.program_id(0)

        @pl.when(c == 0)
        def _():
            state[...] = jnp.zeros_like(state)

        st_ref[...] = state[...]
        ii, jj, tril, dt_col, dt_row, _, cum_col, cum_row = _ssd_prelude(
            dtr_ref, dtrt_ref, brow_ref, bcol_ref, alrow_ref, alcol_ref, Q)
        lane_h = lax.broadcasted_iota(jnp.int32, (Q, 128), 1)
        lane_g = lax.broadcasted_iota(jnp.int32, (Q, gw), 1) // HEAD_DIM
        for g in range(G):
            xg = xbc_ref[:, g * gw:(g + 1) * gw]
            bb = xbc_ref[:, boff + g * D_STATE: boff + (g + 1) * D_STATE].astype(BF16)
            cb = xbc_ref[:, coff + g * D_STATE: coff + (g + 1) * D_STATE].astype(BF16)
            gm = _dot_nt(cb, bb)
            sg = state[g]
            yoff = _dot(cb, sg.astype(BF16))
            ydiag = jnp.zeros((Q, gw), F32)
            esc = jnp.zeros((Q, gw), F32)
            wsc = jnp.zeros((Q, gw), F32)
            lam = jnp.zeros((1, gw), F32)
            for j in range(hpg):
                h = g * hpg + j
                ccol = _col_of(cum_col, lane_h, h)
                dcol = _col_of(dt_col, lane_h, h)
                seg = ccol - cum_row[h:h + 1, :]
                decay = jnp.exp(jnp.where(tril, seg, -jnp.inf))
                mh = gm * decay * dt_row[h:h + 1, :]
                hm = lane_g == j
                ydiag = ydiag + _dot(mh.astype(BF16), jnp.where(hm, xg, 0.0).astype(BF16))
                tot = ccol[Q - 1:Q, :]
                esc = jnp.where(hm, jnp.exp(ccol), esc)
                wsc = jnp.where(hm, jnp.exp(tot - ccol) * dcol, wsc)
                lam = jnp.where(hm[0:1], jnp.exp(tot), lam)
            y_ref[:, g * gw:(g + 1) * gw] = ydiag + yoff * esc + dvec_ref[:, g * gw:(g + 1) * gw] * xg
            state[g] = sg * lam + _dot_tn(bb, (xg * wsc).astype(BF16))

    W = xbc.shape[1]
    full = lambda shape: pl.BlockSpec(shape, lambda c: (0,) * len(shape))
    return _call(
        body, name=name, grid=(nc,),
        in_specs=[pl.BlockSpec((Q, W), lambda c: (c, 0)), pl.BlockSpec((Q, 128), lambda c: (c, 0)),
                  pl.BlockSpec((H, Q), lambda c: (0, c)), full((1, 128)), full((H, 1)), full((1, 128)), full((H, 1)),
                  full((1, DI))],
        out_specs=[pl.BlockSpec((Q, DI), lambda c: (c, 0)), pl.BlockSpec((None, G, D_STATE, gw), lambda c: (c, 0, 0, 0))],
        out_shape=[jax.ShapeDtypeStruct((LP, DI), F32), jax.ShapeDtypeStruct((nc, G, D_STATE, gw), F32)],
        scratch_shapes=[pltpu.VMEM((G, D_STATE, gw), F32)], sem=("arbitrary",),
        args=(xbc, dtr, dtrt, brow, bcol, alrow, alcol, dvec), carry=_pop_for_rows(xq, LP, W, 1.25))


def _ssd_bwd(xbc, dtr, dtrt, brow, bcol, alrow, alcol, dvec, dy, states, *, name, xq=None):
    LP = xbc.shape[0]
    Q = SSD_CHUNK
    nc = LP // Q
    G = SSM_GROUPS
    DI = dvec.shape[1]
    gw = DI // G
    hpg = gw // HEAD_DIM
    H = G * hpg
    boff, coff = DI, DI + G * D_STATE
    W = xbc.shape[1]

    def body(xbc_ref, dtr_ref, dtrt_ref, brow_ref, bcol_ref, alrow_ref, alcol_ref, dvec_ref, dy_ref, st_ref,
             dxbc_ref, ddtr_ref, dbias_ref, dalog_ref, ddvec_ref, dstate):
        c = pl.program_id(0)

        @pl.when(c == 0)
        def _():
            dstate[...] = jnp.zeros_like(dstate)
            dbias_ref[...] = jnp.zeros_like(dbias_ref)
            dalog_ref[...] = jnp.zeros_like(dalog_ref)
            ddvec_ref[...] = jnp.zeros_like(ddvec_ref)

        ii, jj, tril, dt_col, dt_row, a_row_p, cum_col, cum_row = _ssd_prelude(
            dtr_ref, dtrt_ref, brow_ref, bcol_ref, alrow_ref, alcol_ref, Q)
        eye = ii == jj
        lane_h = lax.broadcasted_iota(jnp.int32, (Q, 128), 1)
        row_h = lax.broadcasted_iota(jnp.int32, (Q, 128), 0)
        lane_g = lax.broadcasted_iota(jnp.int32, (Q, gw), 1) // HEAD_DIM
        lane_s = lax.broadcasted_iota(jnp.int32, (D_STATE, gw), 1) // HEAD_DIM
        dcum_mat = jnp.zeros((Q, 128), F32)
        ddt_mat = jnp.zeros((Q, 128), F32)
        dtot_row = jnp.zeros((1, 128), F32)
        for g in range(G):
            xg = xbc_ref[:, g * gw:(g + 1) * gw]
            dyg = dy_ref[:, g * gw:(g + 1) * gw]
            bb = xbc_ref[:, boff + g * D_STATE: boff + (g + 1) * D_STATE].astype(BF16)
            cb = xbc_ref[:, coff + g * D_STATE: coff + (g + 1) * D_STATE].astype(BF16)
            sg = st_ref[g]
            dsg = dstate[g]
            sb = sg.astype(BF16)
            dsb = dsg.astype(BF16)
            xb = xg.astype(BF16)
            gm = _dot_nt(cb, bb)
            cs = _dot(cb, sb)
            bds = _dot(bb, dsb)
            dxg = dvec_ref[:, g * gw:(g + 1) * gw] * dyg
            dgm = jnp.zeros((Q, Q), F32)
            esc = jnp.zeros((Q, gw), F32)
            wsc = jnp.zeros((Q, gw), F32)
            lam = jnp.zeros((1, gw), F32)
            dycs = dyg * cs
            xbds = xg * bds
            dss = dsg * sg
            for j in range(hpg):
                h = g * hpg + j
                ccol = _col_of(cum_col, lane_h, h)
                dcol = _col_of(dt_col, lane_h, h)
                drow = dt_row[h:h + 1, :]
                seg = ccol - cum_row[h:h + 1, :]
                decay = jnp.exp(jnp.where(tril, seg, -jnp.inf))
                hm = lane_g == j
                dyh = jnp.where(hm, dyg, 0.0).astype(BF16)
                gl = gm * decay
                mh = gl * drow
                dmf = _dot_nt(dyh, xb)
                dxg = dxg + _dot_tn(mh.astype(BF16), dyh)
                dgm = dgm + dmf * decay * drow
                n_p = dmf * gl
                n_m = n_p * drow
                rowsum_n = jnp.sum(n_m, axis=1, keepdims=True)
                colsum_n = jnp.sum(jnp.where(eye, jnp.sum(n_m, axis=0, keepdims=True), 0.0), axis=1, keepdims=True)
                colsum_np = jnp.sum(jnp.where(eye, jnp.sum(n_p, axis=0, keepdims=True), 0.0), axis=1, keepdims=True)
                tot = ccol[Q - 1:Q, :]
                e = jnp.exp(ccol)
                wexp = jnp.exp(tot - ccol)
                wcol = wexp * dcol
                lamh = jnp.exp(tot)
                yoff_t = jnp.sum(jnp.where(hm, dycs, 0.0), axis=1, keepdims=True) * e
                e_s = jnp.sum(jnp.where(hm, xbds, 0.0), axis=1, keepdims=True)
                ew = e_s * wcol
                dtot = jnp.sum(ew, axis=0, keepdims=True) + lamh * jnp.sum(
                    jnp.sum(jnp.where(lane_s == j, dss, 0.0), axis=1, keepdims=True), axis=0, keepdims=True)
                dcum_h = rowsum_n + yoff_t - colsum_n - ew
                ddt_h = colsum_np + e_s * wexp
                onehot = lane_h == h
                dcum_mat = jnp.where(onehot, dcum_h, dcum_mat)
                ddt_mat = jnp.where(onehot, ddt_h, ddt_mat)
                dtot_row = jnp.where(onehot[0:1], dtot, dtot_row)
                esc = jnp.where(hm, e, esc)
                wsc = jnp.where(hm, wcol, wsc)
                lam = jnp.where(hm[0:1], lamh, lam)
            dgb = dgm.astype(BF16)
            dye = (dyg * esc).astype(BF16)
            xw = (xg * wsc).astype(BF16)
            dxbc_ref[:, g * gw:(g + 1) * gw] = dxg + bds * wsc
            dxbc_ref[:, boff + g * D_STATE: boff + (g + 1) * D_STATE] = _dot_tn(dgb, cb) + _dot_nt(xw, dsb)
            dxbc_ref[:, coff + g * D_STATE: coff + (g + 1) * D_STATE] = _dot(dgb, bb) + _dot_nt(dye, sb)
            dstate[g] = dsg * lam + _dot_tn(cb, dye)
            ddvec_ref[:, g * gw:(g + 1) * gw] += jnp.sum(dyg * xg, axis=0, keepdims=True)
        dcum_mat = dcum_mat + jnp.where(row_h == Q - 1, dtot_row, 0.0)
        da = _dot3_left((ii <= jj).astype(BF16), dcum_mat)
        ddt = ddt_mat + da * a_row_p
        dalog_ref[...] += jnp.sum(da * dt_col, axis=0, keepdims=True) * a_row_p
        ddtr = ddt * _sigmoid(dtr_ref[...] + brow_ref[...])
        ddtr_ref[...] = ddtr
        dbias_ref[...] += jnp.sum(ddtr, axis=0, keepdims=True)

    full = lambda shape: pl.BlockSpec(shape, lambda c: (0,) * len(shape))
    rc = lambda c: nc - 1 - c
    return _call(
        body, name=name, grid=(nc,),
        in_specs=[pl.BlockSpec((Q, W), lambda c: (rc(c), 0)), pl.BlockSpec((Q, 128), lambda c: (rc(c), 0)),
                  pl.BlockSpec((H, Q), lambda c: (0, rc(c))), full((1, 128)), full((H, 1)), full((1, 128)), full((H, 1)),
                  full((1, DI)), pl.BlockSpec((Q, DI), lambda c: (rc(c), 0)),
                  pl.BlockSpec((None, G, D_STATE, gw), lambda c: (rc(c), 0, 0, 0))],
        out_specs=[pl.BlockSpec((Q, W), lambda c: (rc(c), 0)), pl.BlockSpec((Q, 128), lambda c: (rc(c), 0)),
                   full((1, 128)), full((1, 128)), full((1, DI))],
        out_shape=[jax.ShapeDtypeStruct((LP, W), F32), jax.ShapeDtypeStruct((LP, 128), F32),
                   jax.ShapeDtypeStruct((1, 128), F32), jax.ShapeDtypeStruct((1, 128), F32),
                   jax.ShapeDtypeStruct((1, DI), F32)],
        scratch_shapes=[pltpu.VMEM((G, D_STATE, gw), F32)], sem=("arbitrary",),
        args=(xbc, dtr, dtrt, brow, bcol, alrow, alcol, dvec, dy, states), carry=_pop_for_rows(xq, LP, W, 4.5))


def _gate_fwd(y, zx, g, *, name):
    LP, DI = y.shape
    gw = DI // SSM_GROUPS
    tr = _tile(LP, (256, 128))

    def body(y_ref, z_ref, g_ref, o_ref):
        for k in range(SSM_GROUPS):
            sl = slice(k * gw, (k + 1) * gw)
            z = z_ref[:, sl]
            t = y_ref[:, sl] * (z * _sigmoid(z))
            r = lax.rsqrt(jnp.mean(t * t, axis=1, keepdims=True) + RMS_EPS)
            o_ref[:, sl] = (t * r * g_ref[:, sl]).astype(BF16)

    row = pl.BlockSpec((tr, DI), lambda i: (i, 0))
    return pl.pallas_call(
        body, name=name, grid=(LP // tr,), in_specs=[row, row, pl.BlockSpec((1, DI), lambda i: (0, 0))],
        out_specs=row, out_shape=jax.ShapeDtypeStruct((LP, DI), BF16), compiler_params=_cp("parallel"),
    )(y, zx, g)


def _gate_bwd(y, zx, g, dyn, *, name):
    LP, DI = y.shape
    gw = DI // SSM_GROUPS
    tr = _tile(LP, (256, 128))

    def body(y_ref, z_ref, g_ref, d_ref, dy_ref, dz_ref, dg_ref):
        i = pl.program_id(0)

        @pl.when(i == 0)
        def _():
            dg_ref[...] = jnp.zeros_like(dg_ref)

        for k in range(SSM_GROUPS):
            sl = slice(k * gw, (k + 1) * gw)
            z = z_ref[:, sl]
            yv = y_ref[:, sl]
            sg = _sigmoid(z)
            sz = z * sg
            t = yv * sz
            r = lax.rsqrt(jnp.mean(t * t, axis=1, keepdims=True) + RMS_EPS)
            th = t * r
            d = d_ref[:, sl]
            dtn = d * g_ref[:, sl]
            dt_ = r * (dtn - th * jnp.mean(dtn * th, axis=1, keepdims=True))
            dg_ref[:, sl] += jnp.sum(d * th, axis=0, keepdims=True)
            dy_ref[:, sl] = dt_ * sz
            dz_ref[:, sl] = dt_ * yv * sg * (1.0 + z * (1.0 - sg))

    row = pl.BlockSpec((tr, DI), lambda i: (i, 0))
    vec = pl.BlockSpec((1, DI), lambda i: (0, 0))
    return pl.pallas_call(
        body, name=name, grid=(LP // tr,), in_specs=[row, row, vec, row], out_specs=[row, row, vec],
        out_shape=[jax.ShapeDtypeStruct((LP, DI), F32), jax.ShapeDtypeStruct((LP, DI), F32),
                   jax.ShapeDtypeStruct((1, DI), F32)],
        compiler_params=_cp("arbitrary"),
    )(y, zx, g, dyn)


EXP_ZERO = -104.0
LOG2E = 1.4426950408889634


def _dot2_right(x, t2_bf16):
    hi = x.astype(BF16)
    lo = (x - hi.astype(F32)).astype(BF16)
    return _dot(jnp.concatenate([hi, lo], axis=1), t2_bf16)


def _tri2(T, upper):
    r = lax.broadcasted_iota(jnp.int32, (2 * T, T), 0) % T
    c = lax.broadcasted_iota(jnp.int32, (2 * T, T), 1)
    return (r <= c if upper else r >= c).astype(BF16)


def _sb_tile(q, k_blk, lower2, valid=None):
    z = _dot_nt(q, k_blk)
    sp = jnp.maximum(z, 0.0) + jnp.log(1.0 + jnp.exp2(jnp.abs(z) * (-LOG2E)))
    if valid is not None:
        sp = jnp.where(valid, sp, 0.0)
    return z, sp, z - _dot2_right(sp, lower2)


def _sb_weights(zr, c, valid=None):
    w = jnp.exp(zr + c)
    return w if valid is None else jnp.where(valid, w, 0.0)


def _sb_fwd(q, k, v, zmax, *, name):
    H, LP, dh = q.shape
    T = ATT_BLOCK
    nq = LP // T
    assert nq < LANES

    def body(q_ref, k_ref, v_ref, zb_ref, o_ref, c_ref, kf_ref):
        i = pl.program_id(1)
        ii = lax.broadcasted_iota(jnp.int32, (T, T), 0)
        jj = lax.broadcasted_iota(jnp.int32, (T, T), 1)
        lane = lax.broadcasted_iota(jnp.int32, (T, LANES), 1)
        lower2 = _tri2(T, upper=False)
        qv = q_ref[...]
        zb = zb_ref[0:1, 0:1]

        def kv(kb):
            ks = pl.multiple_of(kb * T, T)
            return k_ref[pl.ds(ks, T), :], v_ref[pl.ds(ks, T), :]

        kd, vd = kv(i)
        k1, v1 = kv(jnp.maximum(i - 1, 0))
        diag = jj < ii
        prev = jnp.full((T, T), i > 0)
        _, sp0, zr0 = _sb_tile(qv, kd, lower2, diag)
        _, sp1, zr1 = _sb_tile(qv, k1, lower2, prev)
        c0 = -jnp.sum(sp0, axis=1, keepdims=True)
        acc = (_dot(_sb_weights(zr0, 0.0, diag).astype(BF16), vd)
               + _dot(_sb_weights(zr1, c0, prev).astype(BF16), v1))
        c = c0 - jnp.sum(sp1, axis=1, keepdims=True)
        c_ref[...] = jnp.where(lane == i - 1, c0, 0.0)

        def alive(c):
            return jnp.max(c + zb) > EXP_ZERO

        def cond(carry):
            kb, _, _, live = carry
            return (kb >= 0) & live

        def step(carry):
            kb, c, acc, _ = carry
            kt, vt = kv(kb)
            _, sp, zr = _sb_tile(qv, kt, lower2)
            acc = acc + _dot(_sb_weights(zr, c).astype(BF16), vt)
            c_ref[...] = jnp.where(lane == kb, c, c_ref[...])
            c = c - jnp.sum(sp, axis=1, keepdims=True)
            return kb - 1, c, acc, alive(c)

        kb, _, acc, _ = lax.while_loop(cond, step, (i - 2, c, acc, alive(c)))
        o_ref[...] = acc
        kf_ref[...] = jnp.zeros_like(kf_ref) + (kb + 1).astype(F32)

    blk = pl.BlockSpec((None, T, dh), lambda h, i: (h, i, 0))
    cblk = pl.BlockSpec((None, T, LANES), lambda h, i: (h, i, 0))
    whole = pl.BlockSpec((None, LP, dh), lambda h, i: (h, 0, 0))
    return pl.pallas_call(
        body, name=name, grid=(H, nq), in_specs=[blk, whole, whole, pl.BlockSpec((1, LANES), lambda h, i: (0, 0))],
        out_specs=[blk, cblk, pl.BlockSpec((None, None, 8, LANES), lambda h, i: (h, i, 0, 0))],
        out_shape=[jax.ShapeDtypeStruct((H, LP, dh), F32), jax.ShapeDtypeStruct((H, LP, LANES), F32),
                   jax.ShapeDtypeStruct((H, nq, 8, LANES), F32)],
        compiler_params=_cp("parallel", "parallel"),
    )(q, k, v, zmax)


def _sb_bwd(kstart, q, k, v, cmat, do, *, name):
    H, LP, dh = q.shape
    T = ATT_BLOCK
    nq = LP // T

    def body(ks_ref, q_ref, k_ref, v_ref, c_ref, do_ref, dq_ref, dk_ref, dv_ref):
        h = pl.program_id(0)
        i = pl.program_id(1)

        @pl.when(i == 0)
        def _():
            dk_ref[...] = jnp.zeros_like(dk_ref)
            dv_ref[...] = jnp.zeros_like(dv_ref)

        ii = lax.broadcasted_iota(jnp.int32, (T, T), 0)
        jj = lax.broadcasted_iota(jnp.int32, (T, T), 1)
        lane = lax.broadcasted_iota(jnp.int32, (T, LANES), 1)
        lower2 = _tri2(T, upper=False)
        upper2 = _tri2(T, upper=True)
        qv = q_ref[...]
        dob = do_ref[...].astype(BF16)
        cm = c_ref[...]

        def front(kb, valid=None):
            ks = pl.multiple_of(kb * T, T)
            k_blk = k_ref[pl.ds(ks, T), :]
            c = jnp.sum(jnp.where(lane == kb, cm, 0.0), axis=1, keepdims=True)
            z, sp, zr = _sb_tile(qv, k_blk, lower2, valid)
            w = _sb_weights(zr, c, valid)
            gw_ = w * _dot_nt(dob, v_ref[pl.ds(ks, T), :])
            gin = _dot2_right(gw_, upper2)
            return ks, k_blk, w, gw_, gin, jnp.exp(z - sp)

        def back(t, cg, dq, valid=None):
            ks, k_blk, w, gw_, gin, sig = t
            dz = gw_ - sig * (cg + gin)
            if valid is not None:
                dz = jnp.where(valid, dz, 0.0)
            dz = dz.astype(BF16)
            dk_ref[pl.ds(ks, T), :] += _dot_tn(dz, qv)
            dv_ref[pl.ds(ks, T), :] += _dot_tn(w.astype(BF16), dob)
            return cg + jnp.sum(gw_, axis=1, keepdims=True), dq + _dot(dz, k_blk)

        cg, dq = lax.fori_loop(ks_ref[h, i], i - 1, lambda kb, cr: back(front(kb), *cr),
                               (jnp.zeros((T, 1), F32), jnp.zeros((T, dh), F32)))
        diag = jj < ii
        prev = jnp.full((T, T), i > 0)
        t1 = front(jnp.maximum(i - 1, 0), prev)
        t0 = front(i, diag)
        cg, dq = back(t1, cg, dq, prev)
        _, dq = back(t0, cg, dq, diag)
        dq_ref[...] = dq

    blk = pl.BlockSpec((None, T, dh), lambda h, i, ks: (h, i, 0))
    cblk = pl.BlockSpec((None, T, LANES), lambda h, i, ks: (h, i, 0))
    whole = pl.BlockSpec((None, LP, dh), lambda h, i, ks: (h, 0, 0))
    sh = jax.ShapeDtypeStruct((H, LP, dh), F32)
    return pl.pallas_call(
        body, name=name,
        grid_spec=pltpu.PrefetchScalarGridSpec(
            num_scalar_prefetch=1, grid=(H, nq), in_specs=[blk, whole, whole, cblk, blk], out_specs=[blk, whole, whole]),
        out_shape=[sh, sh, sh], compiler_params=_cp("parallel", "arbitrary"),
    )(kstart, q, k, v, cmat, do)


def _loss_head(h, tgt, seq, *, name):
    LP, D = h.shape
    tr = _tile(LP, (384, 256, 128))

    def body(h_ref, t_ref, dh_ref, l_ref):
        i = pl.program_id(0)

        @pl.when(i == 0)
        def _():
            l_ref[...] = jnp.zeros_like(l_ref)

        row = lax.broadcasted_iota(jnp.int32, (tr, D), 0) + i * tr
        e = jnp.where((row >= N_META) & (row < N_META + seq), h_ref[...] - t_ref[...], 0.0)
        dh_ref[...] = e * (1.0 / D)
        l_ref[...] += jnp.sum(e * e, axis=0, keepdims=True) * (0.5 / D)

    row = pl.BlockSpec((tr, D), lambda i: (i, 0))
    return pl.pallas_call(
        body, name=name, grid=(LP // tr,), in_specs=[row, row], out_specs=[row, pl.BlockSpec((1, D), lambda i: (0, 0))],
        out_shape=[jax.ShapeDtypeStruct((LP, D), F32), jax.ShapeDtypeStruct((1, D), F32)],
        compiler_params=_cp("arbitrary"),
    )(h, tgt)


def _adamw(w, g, m, v, *, name):
    shape = w.shape
    C = shape[-1]
    R = math.prod(shape) // C
    tr = _tile(R, (512, 256, 128, 64, 32, 16, 8))
    c1 = 1.0 / (1.0 - ADAM_B1 ** ADAM_STEP)
    c2 = 1.0 / (1.0 - ADAM_B2 ** ADAM_STEP)

    def body(w_ref, g_ref, m_ref, v_ref, d_ref, nm_ref, nv_ref):
        gv = g_ref[...]
        nm = ADAM_B1 * m_ref[...] + (1.0 - ADAM_B1) * gv
        nv = ADAM_B2 * v_ref[...] + (1.0 - ADAM_B2) * (gv * gv)
        d_ref[...] = -ADAM_LR * ((nm * c1) / (jnp.sqrt(nv * c2) + ADAM_EPS) + ADAM_WD * w_ref[...])
        nm_ref[...] = nm
        nv_ref[...] = nv

    blk = pl.BlockSpec((tr, C), lambda i: (i, 0))
    sh = jax.ShapeDtypeStruct((R, C), F32)
    d, nm, nv = pl.pallas_call(
        body, name=name, grid=(R // tr,), in_specs=[blk] * 4, out_specs=[blk] * 3, out_shape=[sh] * 3,
        compiler_params=_cp("parallel"),
    )(w.reshape(R, C), g.reshape(R, C), m.reshape(R, C), v.reshape(R, C))
    return d.reshape(shape), nm.reshape(shape), nv.reshape(shape)


def _sum_rows(buf, *, name):
    n, R, C = buf.shape
    tr = _tile(R, (512, 256, 128, 64, 32, 16, 8))

    def body(b_ref, o_ref):
        acc = b_ref[0].astype(F32)
        for k in range(1, n):
            acc = acc + b_ref[k].astype(F32)
        o_ref[...] = acc

    return pl.pallas_call(
        body, name=name, grid=(R // tr,), in_specs=[pl.BlockSpec((n, tr, C), lambda i: (0, i, 0))],
        out_specs=pl.BlockSpec((tr, C), lambda i: (i, 0)), out_shape=jax.ShapeDtypeStruct((R, C), F32),
        compiler_params=_cp("parallel"),
    )(buf)


def _interleave(buf, *, name):
    n, R, C = buf.shape
    tr = _tile(R, (256, 128, 64, 32, 16))

    def body(b_ref, o_ref):
        for d in range(n):
            o_ref[:, d * C:(d + 1) * C] = b_ref[d]

    return pl.pallas_call(
        body, name=name, grid=(R // tr,), in_specs=[pl.BlockSpec((n, tr, C), lambda i: (0, i, 0))],
        out_specs=pl.BlockSpec((tr, n * C), lambda i: (i, 0)), out_shape=jax.ShapeDtypeStruct((R, n * C), buf.dtype),
        compiler_params=_cp("parallel"),
    )(buf)


def _deinterleave(x, *, out_dtype, name):
    R, NC = x.shape
    C = NC // N_DEV
    tr = _tile(R, (256, 128, 64, 32, 16))

    def body(x_ref, o_ref):
        for d in range(N_DEV):
            o_ref[d] = x_ref[:, d * C:(d + 1) * C].astype(out_dtype)

    return pl.pallas_call(
        body, name=name, grid=(R // tr,), in_specs=[pl.BlockSpec((tr, NC), lambda i: (i, 0))],
        out_specs=pl.BlockSpec((N_DEV, tr, C), lambda i: (0, i, 0)),
        out_shape=jax.ShapeDtypeStruct((N_DEV, R, C), out_dtype), compiler_params=_cp("parallel"),
    )(x)


def _mesh_pos():
    x, y, c = lax.axis_index("x"), lax.axis_index("y"), lax.axis_index("c")
    return x, y, c, 4 * x + 2 * y + c


def _peer(x, y, c, f):
    px, py, pc = (x + ((f >> 2) & 1)) % 2, (y + ((f >> 1) & 1)) % 2, (c + (f & 1)) % 2
    return (px, py, pc), 4 * px + 2 * py + pc


def _exchange(arrs, *, scatter, name):
    n = len(arrs)
    hbm = pl.BlockSpec(memory_space=pl.ANY)

    def body(*refs):
        xa = (refs[:n], refs[n:2 * n], *refs[2 * n:], scatter)
        _xchg_start(*xa)
        _xchg_wait(*xa)

    return list(pl.pallas_call(
        body, name=name, in_specs=[hbm] * n, out_specs=[hbm] * n, out_shape=_xchg_shapes(arrs, scatter),
        scratch_shapes=_xchg_scratch(n), compiler_params=pltpu.CompilerParams(has_side_effects=True),
    )(*arrs))


def _unheads(t):
    H, LP, dh = t.shape
    return t.transpose(1, 0, 2).reshape(LP, H * dh)


def _shard_rows(t):
    return t.reshape(N_DEV, t.shape[0] // N_DEV, t.shape[1])


def _ffn_fwd(h, g, W, l, s, xq):
    tag = f"{l}{s}"
    n = _rms_fwd(h, g, name=f"ffn_norm_{tag}")
    w1, w3 = W[("w13", l, s)]
    a, b, sw = _ffn_up(n, w1, w3, name=f"ffn_up_{tag}", xq=xq)
    h2 = _matmul([(sw, W[("w2", l, s)])], res=h, alpha=FFN_RES, name=f"ffn_down_{tag}", xq=xq)
    return h2, (h, n, a, b, sw)


def _ffn_bwd(dh, saved, g, W, l, s, xq, emit):
    tag = f"{l}{s}"
    h, n, a, b, sw = saved
    (w1, w3), w2 = W[("w13", l, s)], W[("w2", l, s)]
    da, db = _ffn_mid_bwd(dh, w2.T, a, b, name=f"ffn_mid_bwd_{tag}", xq=xq)
    dw2 = _matmul([(sw, dh)], trans_a=True, alpha=FFN_RES, out_dtype=BF16, name=f"ffn_dw2_{tag}", xq=xq)
    emit(("w2", l, s), [_shard_rows(dw2)])
    dn = _matmul([(da, w1.T), (db, w3.T)], name=f"ffn_dn_{tag}", xq=xq)
    dw1 = _matmul([(n, da)], trans_a=True, shards=N_DEV, out_dtype=BF16, name=f"ffn_dw1_{tag}", xq=xq)
    dw3 = _matmul([(n, db)], trans_a=True, shards=N_DEV, out_dtype=BF16, name=f"ffn_dw3_{tag}", xq=xq)
    emit(("w13", l, s), [dw1, dw3])
    dh_in, dg = _rms_bwd(h, g, dn, res=dh, name=f"ffn_norm_bwd_{tag}")
    return dh_in, dg


def _local_step(x, tgt, W, xq=None, recv=None):
    G = {}

    def emit(key, arrs):
        G[key] = arrs
        if xq is not None:
            xq.push(arrs, True, lambda res, key=key: recv.__setitem__(key, res))

    seq, D = x.shape
    L = N_META + seq
    LP = -(-L // ROW_ALIGN) * ROW_ALIGN
    pad = LP - L
    H_sb = D // HEAD_DIM
    DI = W["ssm_norm_g"].shape[-1]
    H_ssm = DI // HEAD_DIM
    CONV = DI + 2 * SSM_GROUPS * D_STATE
    ZX = DI + CONV

    h0 = jnp.concatenate([W["meta_tokens"], x, jnp.zeros((pad, D), F32)], axis=0)
    tgt_p = jnp.pad(tgt, ((N_META, pad), (0, 0)))
    ng = W["norm_g"]

    h1, sv_f00 = _ffn_fwd(h0, ng[0, 0], W, 0, 0, xq)
    u0 = _rms_fwd(h1, ng[0, 1], name="ssm_norm")
    w_in = W["ssm_in_proj"][0]
    w_zx = w_in[:, :ZX]
    w_dt = jnp.pad(w_in[:, ZX:], ((0, 0), (0, 128 - H_ssm)))
    zx = _matmul([(u0, w_zx)], name="ssm_in_zx", xq=xq)
    dtr = _matmul([(u0, w_dt)], name="ssm_in_dt")
    conv_w, conv_b = W["ssm_conv_w"][0], W["ssm_conv_b"][0]
    xbc = _conv_fwd(zx, conv_w, conv_b, DI, name="ssm_conv", xq=xq)
    dtrt = dtr[:, :H_ssm].T
    padh = lambda t: jnp.pad(t.reshape(1, H_ssm), ((0, 0), (0, 128 - H_ssm)))
    brow, bcol = padh(W["ssm_dt_bias"][0]), W["ssm_dt_bias"][0].reshape(H_ssm, 1)
    alrow, alcol = padh(W["ssm_a_log"][0]), W["ssm_a_log"][0].reshape(H_ssm, 1)
    dvec = jnp.repeat(W["ssm_d"][0], HEAD_DIM).reshape(1, DI)
    ssm_args = (xbc, dtr, dtrt, brow, bcol, alrow, alcol, dvec)
    y, states = _ssd_fwd(*ssm_args, name="ssd_fwd", xq=xq)
    sng = W["ssm_norm_g"].reshape(1, DI)
    yn = _gate_fwd(y, zx, sng, name="ssm_gate")
    w_out = W["ssm_out_proj"][0]
    h2 = _matmul([(yn, w_out)], res=h1, name="ssm_out", xq=xq)
    h3, sv_f01 = _ffn_fwd(h2, ng[0, 2], W, 0, 1, xq)

    kv_in = _rms_fwd(h3, W["kv_norm_g"], name="kv_norm")
    kraw = _matmul([(kv_in, W["w_k"])], shards=H_sb, name="kv_k")
    vh = _matmul([(kv_in, W["w_v"])], shards=H_sb, out_dtype=BF16, name="kv_v")
    kh = _rms_fwd(kraw.reshape(H_sb * LP, HEAD_DIM), W["k_norm_g"], name="k_headnorm").reshape(H_sb, LP, HEAD_DIM)

    h4, sv_f10 = _ffn_fwd(h3, ng[1, 0], W, 1, 0, xq)
    u1 = _rms_fwd(h4, ng[1, 1], name="sb_norm")
    qraw = _matmul([(u1, W["sb_w_q"][0])], shards=H_sb, name="sb_q")
    scale = HEAD_DIM ** -0.5
    qh = _rms_fwd(qraw.reshape(H_sb * LP, HEAD_DIM), W["sb_q_norm_g"][0], scale=scale,
                  name="q_headnorm").reshape(H_sb, LP, HEAD_DIM)
    zmax = 1.02 * math.sqrt(HEAD_DIM) * jnp.max(jnp.abs(W["sb_q_norm_g"])) * jnp.max(jnp.abs(W["k_norm_g"]))
    o, cmat, kfirst = _sb_fwd(qh, kh, vh, jnp.full((1, LANES), zmax, F32), name="sb_fwd")
    kstart = kfirst[:, :, 0, 0].astype(jnp.int32)
    o_flat = _unheads(o)
    h5 = _matmul([(o_flat, W["sb_w_o"][0])], res=h4, name="sb_out")
    h6, sv_f11 = _ffn_fwd(h5, ng[1, 2], W, 1, 1, xq)

    dh, lvec = _loss_head(h6, tgt_p, seq, name="loss_head")
    loss = jnp.sum(lvec)
    dng = [[None] * 3 for _ in range(2)]
    shard_rows = _shard_rows

    dh, dng[1][2] = _ffn_bwd(dh, sv_f11, ng[1, 2], W, 1, 1, xq, emit)
    g_wo = shard_rows(_matmul([(o_flat, dh)], trans_a=True, out_dtype=BF16, name="sb_dwo"))
    do = _matmul([(dh, W["sb_w_o"][0].T)], shards=H_sb, name="sb_do")
    dq, dk, dv = _sb_bwd(kstart, qh, kh, vh, cmat, do, name="sb_bwd")
    dqraw, dqg = _rms_bwd(qraw.reshape(H_sb * LP, HEAD_DIM), W["sb_q_norm_g"][0], dq.reshape(H_sb * LP, HEAD_DIM),
                          alpha=scale, name="q_headnorm_bwd")
    G["sb_q_norm_g"] = dqg
    dqraw = _unheads(dqraw.reshape(H_sb, LP, HEAD_DIM))
    g_wq = shard_rows(_matmul([(u1, dqraw)], trans_a=True, out_dtype=BF16, name="sb_dwq"))
    emit("sb", [g_wq, g_wo])
    du1 = _matmul([(dqraw, W["sb_w_q"][0].T)], name="sb_du")
    dh, dng[1][1] = _rms_bwd(h4, ng[1, 1], du1, res=dh, name="sb_norm_bwd")
    dh, dng[1][0] = _ffn_bwd(dh, sv_f10, ng[1, 0], W, 1, 0, xq, emit)

    dkraw, dkg = _rms_bwd(kraw.reshape(H_sb * LP, HEAD_DIM), W["k_norm_g"], dk.reshape(H_sb * LP, HEAD_DIM),
                          name="k_headnorm_bwd")
    G["k_norm_g"] = dkg.reshape(-1)
    dkraw = _unheads(dkraw.reshape(H_sb, LP, HEAD_DIM))
    dvf = _unheads(dv)
    g_wk = shard_rows(_matmul([(kv_in, dkraw)], trans_a=True, out_dtype=BF16, name="kv_dwk"))
    g_wv = shard_rows(_matmul([(kv_in, dvf)], trans_a=True, out_dtype=BF16, name="kv_dwv"))
    emit("kv", [g_wk, g_wv])
    dkv = _matmul([(dkraw, W["w_k"].T), (dvf, W["w_v"].T)], name="kv_din", xq=xq)
    dh, dkvg = _rms_bwd(h3, W["kv_norm_g"], dkv, res=dh, name="kv_norm_bwd")
    G["kv_norm_g"] = dkvg.reshape(-1)

    dh, dng[0][2] = _ffn_bwd(dh, sv_f01, ng[0, 2], W, 0, 1, xq, emit)
    emit("wout", [shard_rows(_matmul([(yn, dh)], trans_a=True, out_dtype=BF16, name="ssm_dwout", xq=xq))])
    dyn = _matmul([(dh, w_out.T)], name="ssm_dyn", xq=xq)
    dy, dz, dsng = _gate_bwd(y, zx, sng, dyn, name="ssm_gate_bwd")
    G["ssm_norm_g"] = dsng
    dxbc, ddtr, dbias, dalog, ddvec = _ssd_bwd(*ssm_args, dy, states, name="ssd_bwd", xq=xq)
    G["ssm_dt_bias"] = dbias[:, :H_ssm]
    G["ssm_a_log"] = dalog[:, :H_ssm]
    G["ssm_d"] = jnp.sum(ddvec.reshape(H_ssm, HEAD_DIM), axis=1).reshape(1, H_ssm)
    gpre, dcw, dcb = _conv_bwd_g(zx, conv_w, conv_b, dxbc, DI, name="ssm_conv_bwd_g", xq=xq)
    G["ssm_conv_w"] = dcw[:D_CONV][None]
    G["ssm_conv_b"] = dcb
    dxbc_pre = _conv_bwd_u(gpre, conv_w, name="ssm_conv_bwd_u", xq=xq)
    emit("win", [_deinterleave(jnp.concatenate([
        _matmul([(u0, dz)], trans_a=True, out_dtype=BF16, name="ssm_dwin_z", xq=xq),
        _matmul([(u0, dxbc_pre)], trans_a=True, out_dtype=BF16, name="ssm_dwin_x", xq=xq),
        _matmul([(u0, ddtr)], trans_a=True, out_dtype=BF16, name="ssm_dwin_dt")[:, :H_ssm]], axis=1),
        out_dtype=BF16, name="ssm_dwin_shards")])
    du0 = _matmul([(dz, w_zx[:, :DI].T)], name="ssm_du_z", xq=xq)
    du0 = _matmul([(dxbc_pre, w_zx[:, DI:].T)], res=du0, name="ssm_du_x", xq=xq)
    du0 = _matmul([(ddtr, w_dt.T)], res=du0, name="ssm_du_dt")
    dh, dng[0][1] = _rms_bwd(h1, ng[0, 1], du0, res=dh, name="ssm_norm_bwd")
    dh, dng[0][0] = _ffn_bwd(dh, sv_f00, ng[0, 0], W, 0, 0, xq, emit)

    G["norm_g"] = jnp.stack([jnp.concatenate(r, axis=0) for r in dng])
    G["meta_tokens"] = dh[:N_META]
    return loss, dh[N_META:L], G


WEIGHTS = ['meta_tokens', 'norm_g', 'ffn_w1', 'ffn_w3', 'ffn_w2', 'ssm_in_proj', 'ssm_conv_w', 'ssm_conv_b',
           'ssm_dt_bias', 'ssm_a_log', 'ssm_d', 'ssm_norm_g', 'ssm_out_proj', 'kv_norm_g', 'w_k', 'k_norm_g', 'w_v',
           'sb_w_q', 'sb_q_norm_g', 'sb_w_o']
SHARD_AXIS = {'meta_tokens': 1, 'norm_g': 2, 'ffn_w1': 3, 'ffn_w3': 3, 'ffn_w2': 2, 'ssm_in_proj': 2, 'ssm_conv_w': 2,
              'ssm_conv_b': 1, 'ssm_dt_bias': None, 'ssm_a_log': None, 'ssm_d': None, 'ssm_norm_g': 1,
              'ssm_out_proj': 1, 'kv_norm_g': None, 'w_k': 0, 'k_norm_g': None, 'w_v': 0, 'sb_w_q': 1,
              'sb_q_norm_g': None, 'sb_w_o': 1}
MATMUL_WEIGHTS = ('ffn_w1', 'ffn_w3', 'ffn_w2', 'ssm_in_proj', 'ssm_out_proj', 'w_k', 'w_v', 'sb_w_q', 'sb_w_o')
PACK_ROWS = 16


def _pack(arrs, dtype):
    flat = jnp.concatenate([a.reshape(-1).astype(dtype) for a in arrs])
    n = flat.shape[0]
    npad = -(-n // (LANES * PACK_ROWS)) * (LANES * PACK_ROWS)
    return jnp.pad(flat, (0, npad - n)).reshape(npad // LANES, LANES)


def _unpack_gathered(buf, names, shard_shapes, dtype):
    flat = buf.reshape(N_DEV, -1)
    out, off = {}, 0
    for n in names:
        shp = shard_shapes[n]
        size = math.prod(shp)
        t = flat[:, off:off + size].reshape((N_DEV,) + tuple(shp))
        off += size
        ax = SHARD_AXIS[n]
        t = jnp.moveaxis(t, 0, ax)
        full = shp[:ax] + (N_DEV * shp[ax],) + shp[ax + 1:]
        out[n] = t.reshape(full).astype(dtype)
    return out


def _to_shards(g, ax):
    shp = g.shape
    t = g.reshape(shp[:ax] + (N_DEV, shp[ax] // N_DEV) + shp[ax + 1:])
    return jnp.moveaxis(t, ax, 0).reshape(N_DEV, -1)


def kernel(x, meta_tokens, norm_g, ffn_w1, ffn_w3, ffn_w2, ssm_in_proj, ssm_conv_w, ssm_conv_b, ssm_dt_bias, ssm_a_log, ssm_d, ssm_norm_g, ssm_out_proj, kv_norm_g, w_k, k_norm_g, w_v, sb_w_q, sb_q_norm_g, sb_w_o, loss_target, m_meta_tokens, m_norm_g, m_ffn_w1, m_ffn_w3, m_ffn_w2, m_ssm_in_proj, m_ssm_conv_w, m_ssm_conv_b, m_ssm_dt_bias, m_ssm_a_log, m_ssm_d, m_ssm_norm_g, m_ssm_out_proj, m_kv_norm_g, m_w_k, m_k_norm_g, m_w_v, m_sb_w_q, m_sb_q_norm_g, m_sb_w_o, v_meta_tokens, v_norm_g, v_ffn_w1, v_ffn_w3, v_ffn_w2, v_ssm_in_proj, v_ssm_conv_w, v_ssm_conv_b, v_ssm_dt_bias, v_ssm_a_log, v_ssm_d, v_ssm_norm_g, v_ssm_out_proj, v_kv_norm_g, v_w_k, v_k_norm_g, v_w_v, v_sb_w_q, v_sb_q_norm_g, v_sb_w_o):
    shard = dict(meta_tokens=meta_tokens, norm_g=norm_g, ffn_w1=ffn_w1, ffn_w3=ffn_w3, ffn_w2=ffn_w2,
                 ssm_in_proj=ssm_in_proj, ssm_conv_w=ssm_conv_w, ssm_conv_b=ssm_conv_b, ssm_dt_bias=ssm_dt_bias,
                 ssm_a_log=ssm_a_log, ssm_d=ssm_d, ssm_norm_g=ssm_norm_g, ssm_out_proj=ssm_out_proj,
                 kv_norm_g=kv_norm_g, w_k=w_k, k_norm_g=k_norm_g, w_v=w_v, sb_w_q=sb_w_q, sb_q_norm_g=sb_q_norm_g,
                 sb_w_o=sb_w_o)
    mom_m = dict(zip(WEIGHTS, (m_meta_tokens, m_norm_g, m_ffn_w1, m_ffn_w3, m_ffn_w2, m_ssm_in_proj, m_ssm_conv_w,
                               m_ssm_conv_b, m_ssm_dt_bias, m_ssm_a_log, m_ssm_d, m_ssm_norm_g, m_ssm_out_proj,
                               m_kv_norm_g, m_w_k, m_k_norm_g, m_w_v, m_sb_w_q, m_sb_q_norm_g, m_sb_w_o)))
    mom_v = dict(zip(WEIGHTS, (v_meta_tokens, v_norm_g, v_ffn_w1, v_ffn_w3, v_ffn_w2, v_ssm_in_proj, v_ssm_conv_w,
                               v_ssm_conv_b, v_ssm_dt_bias, v_ssm_a_log, v_ssm_d, v_ssm_norm_g, v_ssm_out_proj,
                               v_kv_norm_g, v_w_k, v_k_norm_g, v_w_v, v_sb_w_q, v_sb_q_norm_g, v_sb_w_o)))
    sharded = [n for n in WEIGHTS if SHARD_AXIS[n] is not None]
    replicated = [n for n in WEIGHTS if SHARD_AXIS[n] is None]
    small = [n for n in sharded if n not in MATMUL_WEIGHTS]
    shapes = {n: tuple(shard[n].shape) for n in WEIGHTS}
    D = x.shape[-1]
    bf = lambda t: t.astype(BF16)
    full_rows = lambda t: t.reshape(N_DEV * t.shape[1], t.shape[2])
    cols = lambda l, s: jnp.concatenate([bf(ffn_w1[l, s]), bf(ffn_w3[l, s])], axis=0)
    xq, recv = _Queue(), {}
    W = _Weights(xq)
    W.update({n: shard[n] for n in replicated})

    def have_w13(l, s):
        def done(res):
            t = _interleave(res[0], name=f"weights_w13_{l}{s}")
            W[("w13", l, s)] = (t[:D], t[D:])
        return done

    def have_w2(l, s):
        return lambda res: W.__setitem__(("w2", l, s), full_rows(res[0]))

    def have_ssm(res):
        W["ssm_in_proj"] = _interleave(res[0], name="weights_win")[None]
        W["ssm_out_proj"] = full_rows(res[1])[None]

    def have_attn(res):
        W["w_k"], W["w_v"] = full_rows(res[0]), full_rows(res[1])
        W["sb_w_q"], W["sb_w_o"] = full_rows(res[2])[None], full_rows(res[3])[None]

    first = _exchange([cols(0, 0), bf(ffn_w2[0, 0]), _pack([shard[n] for n in small], F32)], scatter=False,
                      name="gather_first")
    have_w13(0, 0)(first[:1])
    have_w2(0, 0)(first[1:2])
    W.update(_unpack_gathered(first[2], small, shapes, F32))
    xq.push([bf(ssm_in_proj[0]), bf(ssm_out_proj[0])], False, have_ssm)
    xq.push([bf(ffn_w2[0, 1])], False, have_w2(0, 1))
    xq.push([cols(0, 1)], False, have_w13(0, 1))
    xq.push([bf(w_k), bf(w_v), bf(sb_w_q[0]), bf(sb_w_o[0])], False, have_attn)
    xq.push([cols(1, 0)], False, have_w13(1, 0))
    xq.push([bf(ffn_w2[1, 0])], False, have_w2(1, 0))
    xq.push([cols(1, 1)], False, have_w13(1, 1))
    xq.push([bf(ffn_w2[1, 1])], False, have_w2(1, 1))

    loss, dx, G = _local_step(x[0], loss_target[0], W, xq, recv)
    loss = lax.psum(loss, ("x", "y", "c"))

    send = jnp.concatenate([_to_shards(G[n], SHARD_AXIS[n]) for n in small], axis=1)
    n_el = send.shape[1]
    npad = -(-n_el // (LANES * PACK_ROWS)) * (LANES * PACK_ROWS)
    send = jnp.pad(send, ((0, 0), (0, npad - n_el))).reshape(N_DEV, npad // LANES, LANES)
    xq.push([send], True, lambda res: recv.__setitem__("small", res))
    xq.flush("scatter_last")
    rep = _pack([G[n] for n in replicated], F32)
    rep_sum = _sum_rows(_exchange([rep], scatter=False, name="gather_small_grads")[0], name="sum_small_grads").reshape(-1)
    summed = {k: [_sum_rows(t, name=f"sum_{'_'.join(map(str, k)) if isinstance(k, tuple) else k}_{i}")
                  for i, t in enumerate(v)] for k, v in recv.items()}

    grads, off = {}, 0
    small_sum = summed["small"][0].reshape(-1)
    for n in small:
        size = math.prod(shapes[n])
        grads[n] = small_sum[off:off + size].reshape(shapes[n])
        off += size
    off = 0
    for n in replicated:
        size = math.prod(shapes[n])
        grads[n] = rep_sum[off:off + size].reshape(shapes[n])
        off += size
    ls = [(l, s) for l in range(2) for s in range(2)]
    grads["ffn_w1"] = jnp.stack([summed[("w13", l, s)][0] for l, s in ls]).reshape(shapes["ffn_w1"])
    grads["ffn_w3"] = jnp.stack([summed[("w13", l, s)][1] for l, s in ls]).reshape(shapes["ffn_w3"])
    grads["ffn_w2"] = jnp.stack([summed[("w2", l, s)][0] for l, s in ls]).reshape(shapes["ffn_w2"])
    grads["ssm_in_proj"] = summed["win"][0][None]
    grads["ssm_out_proj"] = summed["wout"][0][None]
    grads["w_k"], grads["w_v"] = summed["kv"]
    grads["sb_w_q"], grads["sb_w_o"] = summed["sb"][0][None], summed["sb"][1][None]

    delta, new_m, new_v = {}, {}, {}
    for n in WEIGHTS:
        w2 = shard[n].reshape(1, -1) if shard[n].ndim == 1 else shard[n]
        r2 = lambda t: t.reshape(w2.shape)
        d, nm, nv = _adamw(w2, r2(grads[n]), r2(mom_m[n]), r2(mom_v[n]), name=f"adamw_{n}")
        delta[n], new_m[n], new_v[n] = (t.reshape(shapes[n]) for t in (d, nm, nv))

    return (loss, dx[None], *[grads[n] for n in WEIGHTS], *[delta[n] for n in WEIGHTS],
            *[new_m[n] for n in WEIGHTS], *[new_v[n] for n in WEIGHTS])
```

```python
import functools
import math

import jax
import jax.numpy as jnp
from jax import lax
from jax.experimental import pallas as pl
from jax.experimental.pallas import tpu as pltpu

F32 = jnp.float32
BF16 = jnp.bfloat16
RMS_EPS = 1e-6
N_META = 16
HEAD_DIM = 64
SSM_GROUPS = 8
D_STATE = 128
D_CONV = 4
FFN_RES = 0.5
ADAM_LR, ADAM_B1, ADAM_B2, ADAM_EPS, ADAM_WD, ADAM_STEP = 0.001, 0.9, 0.999, 1e-08, 0.01, 10
N_DEV = 8
SSD_CHUNK = 128
ATT_BLOCK = 256
ROW_ALIGN = 768
VMEM_LIMIT_V7X = 48 * 1024 * 1024
MATMUL_VMEM_BUDGET_V7X = 30 * 1024 * 1024
MESH = pl.DeviceIdType.MESH
LANES = 128


def _cp(*sem):
    return pltpu.CompilerParams(dimension_semantics=sem if sem else None, vmem_limit_bytes=VMEM_LIMIT_V7X)


def _tile(n, cands):
    for c in cands:
        if n % c == 0:
            return c
    return n


def _softplus(x):
    return jnp.maximum(x, 0.0) + jnp.log(1.0 + jnp.exp(-jnp.abs(x)))


def _sigmoid(x):
    return 1.0 / (1.0 + jnp.exp(-x))


def _split3(x):
    hi = x.astype(BF16)
    r1 = x - hi.astype(F32)
    mid = r1.astype(BF16)
    lo = (r1 - mid.astype(F32)).astype(BF16)
    return hi, mid, lo


def _dot(a, b):
    return jnp.dot(a, b, preferred_element_type=F32)


def _dot_nt(a, b):
    return lax.dot_general(a, b, (((1,), (1,)), ((), ())), preferred_element_type=F32)


def _dot_tn(a, b):
    return lax.dot_general(a, b, (((0,), (0,)), ((), ())), preferred_element_type=F32)


def _dot3_left(t_bf16, x):
    hi, mid, lo = _split3(x)
    return _dot(t_bf16, hi) + _dot(t_bf16, mid) + _dot(t_bf16, lo)


def _dot3_right(x, t_bf16):
    hi, mid, lo = _split3(x)
    return _dot(hi, t_bf16) + _dot(mid, t_bf16) + _dot(lo, t_bf16)


CARRIER_MIN_FLOP = 4e10


EXCHANGE_US_PER_MB = 94.0
MATMUL_TFLOPS = 650.0


class _Carry:
    def __init__(self, arrs, scatter, done):
        self.arrs, self.scatter, self.done = list(arrs), scatter, done
        per_peer = sum(math.prod(a.shape[1:] if scatter else a.shape) * a.dtype.itemsize for a in self.arrs)
        self.us = EXCHANGE_US_PER_MB * per_peer / 2 ** 20


class _Queue:
    def __init__(self):
        self.items = []

    def push(self, arrs, scatter, done):
        self.items.append(_Carry(arrs, scatter, done))

    def pop(self, kernel_us):
        for k, it in enumerate(self.items):
            if not it.scatter or it.us <= 1.15 * kernel_us:
                return self.items.pop(k)
            if k == 0 and len(self.items) > 4:
                return self.items.pop(0)
        return None

    def flush(self, name):
        k = 0
        while self.items:
            it = self.items.pop(0)
            it.done(_exchange(it.arrs, scatter=it.scatter, name=f"{name}_{k}"))
            k += 1


class _Weights(dict):
    def __init__(self, xq):
        super().__init__()
        self.xq, self.fetched = xq, 0

    def __missing__(self, key):
        while not dict.__contains__(self, key) and self.xq.items:
            it = self.xq.items.pop(0)
            it.done(_exchange(it.arrs, scatter=it.scatter, name=f"gather_now_{self.fetched}"))
            self.fetched += 1
        return dict.__getitem__(self, key)


ELEMENTWISE_US_PER_MB = 1.6


def _pop_for_rows(xq, rows, cols, us_per_mb=ELEMENTWISE_US_PER_MB):
    return xq.pop(us_per_mb * rows * cols * 4 / 2 ** 20) if xq is not None else None


def _xchg_shapes(arrs, scatter):
    return [jax.ShapeDtypeStruct((N_DEV,) + tuple(a.shape[1:] if scatter else a.shape), a.dtype) for a in arrs]


def _xchg_scratch(n):
    return [pltpu.SemaphoreType.DMA((n, N_DEV - 1)), pltpu.SemaphoreType.DMA((n, N_DEV - 1)),
            pltpu.SemaphoreType.DMA((n,))]


def _xchg_copies(srcs, dsts, send_sems, recv_sems, local_sems, scatter, with_recv):
    x, y, c, me = _mesh_pos()
    own, sends, recvs = [], [], []
    for a, (s, d) in enumerate(zip(srcs, dsts)):
        own.append(pltpu.make_async_copy(s.at[me] if scatter else s, d.at[me], local_sems.at[a]))
        for f in range(1, N_DEV):
            peer, pid = _peer(x, y, c, f)
            for row, lst in ((me, sends), (pid, recvs)) if with_recv else ((me, sends),):
                lst.append(pltpu.make_async_remote_copy(
                    src_ref=s.at[pid] if scatter else s, dst_ref=d.at[row], send_sem=send_sems.at[a, f - 1],
                    recv_sem=recv_sems.at[a, f - 1], device_id=peer, device_id_type=MESH))
    return own, sends, recvs


def _xchg_start(*a):
    own, sends, _ = _xchg_copies(*a, with_recv=False)
    for cp in own + sends:
        cp.start()


def _xchg_wait(*a):
    own, sends, recvs = _xchg_copies(*a, with_recv=True)
    for snd, rcv in zip(sends, recvs):
        snd.wait_send()
        rcv.wait_recv()
    for cp in own:
        cp.wait()


def _call(body, *, name, grid, in_specs, out_specs, out_shape, scratch_shapes=(), sem, args, carry=None):
    n_in, n_out, n_scr = len(in_specs), len(out_specs), len(scratch_shapes)
    if carry is None:
        return pl.pallas_call(
            body, name=name, grid=grid, in_specs=list(in_specs), out_specs=list(out_specs), out_shape=list(out_shape),
            scratch_shapes=list(scratch_shapes), compiler_params=_cp(*sem))(*args)
    n = len(carry.arrs)
    hbm = pl.BlockSpec(memory_space=pl.ANY)

    def wrapped(*refs):
        ins, csrc = refs[:n_in], refs[n_in:n_in + n]
        outs, cdst = refs[n_in + n:n_in + n + n_out], refs[n_in + n + n_out:n_in + 2 * n + n_out]
        scr = refs[n_in + 2 * n + n_out:]
        xa = (csrc, cdst, *scr[n_scr:], carry.scatter)
        pid = [pl.program_id(a) for a in range(len(grid))]
        first = functools.reduce(jnp.logical_and, [p == 0 for p in pid])
        last = functools.reduce(jnp.logical_and, [p == g - 1 for p, g in zip(pid, grid)])

        @pl.when(first)
        def _():
            _xchg_start(*xa)

        body(*ins, *outs, *scr[:n_scr])

        @pl.when(last)
        def _():
            _xchg_wait(*xa)

    res = pl.pallas_call(
        wrapped, name=name, grid=grid, in_specs=list(in_specs) + [hbm] * n, out_specs=list(out_specs) + [hbm] * n,
        out_shape=list(out_shape) + _xchg_shapes(carry.arrs, carry.scatter),
        scratch_shapes=list(scratch_shapes) + _xchg_scratch(n), compiler_params=_cp(*["arbitrary"] * len(grid)),
    )(*args, *carry.arrs)
    carry.done(list(res[n_out:]))
    return list(res[:n_out])


def _matmul(pairs, *, name, out_dtype=F32, trans_a=False, trans_b=False, b_off=0, res=None, alpha=1.0, shards=None,
            xq=None, tm=None, tn=None, tk=None):
    a0, b0 = pairs[0]
    if trans_a:
        K, M = a0.shape
    else:
        M, K = a0.shape
    N = b0.shape[0] if trans_b else b0.shape[1]
    npair = len(pairs)
    has_res = res is not None
    tm = tm or _tile(M, (768, 512, 1408, 384, 256, 128))
    tk = tk or _tile(K, (1024, 1408, 768, 512, 256, 128))
    if tn is None:
        sa, sb, so = a0.dtype.itemsize, b0.dtype.itemsize, jnp.dtype(out_dtype).itemsize
        for tn in ([N] if shards else [c for c in (2048, 1536, 1408, 1024, 512, 384, 256, 128) if N % c == 0] or [N]):
            if (2 * npair * tk * (tm * sa + tn * sb) + tm * tn * (4 + 2 * so + (8 if has_res else 0))
                    <= MATMUL_VMEM_BUDGET_V7X):
                break
    nk = K // tk
    cs = N // shards if shards else None

    def body(*refs):
        o_ref, acc = refs[-2], refs[-1]
        k = pl.program_id(2)

        @pl.when(k == 0)
        def _():
            acc[...] = jnp.zeros_like(acc)

        part = None
        for p in range(npair):
            a = refs[2 * p][...].astype(BF16)
            b = refs[2 * p + 1][...].astype(BF16)
            d = _dot_tn(a, b) if trans_a else _dot_nt(a, b) if trans_b else _dot(a, b)
            part = d if part is None else part + d
        acc[...] += part

        @pl.when(k == nk - 1)
        def _():
            if shards:
                for d in range(shards):
                    v = acc[:, d * cs:(d + 1) * cs]
                    o_ref[d] = (v * alpha if alpha != 1.0 else v).astype(out_dtype)
                return
            v = acc[...]
            if alpha != 1.0:
                v = v * alpha
            if has_res:
                v = refs[2 * npair][...] + v
            o_ref[...] = v.astype(out_dtype)

    if trans_a:
        a_spec = pl.BlockSpec((tk, tm), lambda i, j, k: (k, i))
    else:
        a_spec = pl.BlockSpec((tm, tk), lambda i, j, k: (i, k))
    if trans_b:
        assert not trans_a and b_off % tk == 0
        b_spec = pl.BlockSpec((tn, tk), lambda i, j, k: (j, k + b_off // tk))
    else:
        b_spec = pl.BlockSpec((tk, tn), lambda i, j, k: (k, j))
    if shards:
        assert not has_res and tn == N
        o_spec = pl.BlockSpec((shards, tm, cs), lambda i, j, k: (0, i, 0))
        out_shape = jax.ShapeDtypeStruct((shards, M, cs), out_dtype)
    else:
        o_spec = pl.BlockSpec((tm, tn), lambda i, j, k: (i, j))
        out_shape = jax.ShapeDtypeStruct((M, N), out_dtype)
    in_specs, args = [], []
    for a, b in pairs:
        in_specs += [a_spec, b_spec]
        args += [a, b]
    if has_res:
        in_specs.append(o_spec)
        args.append(res)
    flop = 2.0 * npair * M * N * K
    carry = xq.pop(flop / MATMUL_TFLOPS / 1e6) if (xq is not None and flop >= CARRIER_MIN_FLOP) else None
    return _call(body, name=name, grid=(M // tm, N // tn, nk), in_specs=in_specs, out_specs=[o_spec],
                 out_shape=[out_shape], scratch_shapes=[pltpu.VMEM((tm, tn), F32)],
                 sem=("parallel", "parallel", "arbitrary"), args=args, carry=carry)[0]


def _rms_fwd(h, g, *, name, scale=1.0):
    R, D = h.shape
    tr = _tile(R, (2048, 1024, 768, 512, 256, 128)) if D <= 128 else _tile(R, (384, 256, 128))

    def body(h_ref, g_ref, o_ref):
        x = h_ref[...]
        r = lax.rsqrt(jnp.mean(x * x, axis=1, keepdims=True) + RMS_EPS)
        y = x * r * g_ref[...]
        if scale != 1.0:
            y = y * scale
        o_ref[...] = y.astype(BF16)

    return pl.pallas_call(
        body, name=name, grid=(R // tr,),
        in_specs=[pl.BlockSpec((tr, D), lambda i: (i, 0)), pl.BlockSpec((1, D), lambda i: (0, 0))],
        out_specs=pl.BlockSpec((tr, D), lambda i: (i, 0)),
        out_shape=jax.ShapeDtypeStruct((R, D), BF16), compiler_params=_cp("parallel"),
    )(h, g.reshape(1, D))


def _rms_bwd(h, g, dn, res=None, *, name, alpha=1.0):
    R, D = h.shape
    tr = _tile(R, (2048, 1024, 768, 512, 256, 128)) if D <= 128 else _tile(R, (384, 256, 128))
    has_res = res is not None

    def body(*refs):
        h_ref, g_ref, dn_ref = refs[:3]
        dh_ref, dg_ref = refs[-2], refs[-1]
        i = pl.program_id(0)

        @pl.when(i == 0)
        def _():
            dg_ref[...] = jnp.zeros_like(dg_ref)

        x = h_ref[...]
        r = lax.rsqrt(jnp.mean(x * x, axis=1, keepdims=True) + RMS_EPS)
        xh = x * r
        d = dn_ref[...].astype(F32)
        if alpha != 1.0:
            d = d * alpha
        dng = d * g_ref[...]
        m = jnp.mean(dng * xh, axis=1, keepdims=True)
        dh = r * (dng - xh * m)
        if has_res:
            dh = dh + refs[3][...]
        dh_ref[...] = dh
        dg_ref[...] += jnp.sum(d * xh, axis=0, keepdims=True)

    row = pl.BlockSpec((tr, D), lambda i: (i, 0))
    vec = pl.BlockSpec((1, D), lambda i: (0, 0))
    in_specs = [row, vec, row] + ([row] if has_res else [])
    args = [h, g.reshape(1, D), dn] + ([res] if has_res else [])
    return pl.pallas_call(
        body, name=name, grid=(R // tr,), in_specs=in_specs, out_specs=[row, vec],
        out_shape=[jax.ShapeDtypeStruct((R, D), F32), jax.ShapeDtypeStruct((1, D), F32)],
        compiler_params=_cp("arbitrary"),
    )(*args)


def _ffn_up(n, w1, w3, *, name, xq=None):
    M, K = n.shape
    N = w1.shape[1]
    tm = _tile(M, (384, 256, 128))
    tn = _tile(N, (1408, 512, 256, 128))

    def body(n_ref, w1_ref, w3_ref, da_ref, db_ref, s_ref):
        x = n_ref[...]
        a = _dot(x, w1_ref[...])
        b = _dot(x, w3_ref[...])
        sg = _sigmoid(a)
        silu = a * sg
        da_ref[...] = (b * sg * (1.0 + a * (1.0 - sg))).astype(BF16)
        db_ref[...] = silu.astype(BF16)
        s_ref[...] = (silu * b).astype(BF16)

    o_spec = pl.BlockSpec((tm, tn), lambda j, i: (i, j))
    w_spec = pl.BlockSpec((K, tn), lambda j, i: (0, j))
    sh = jax.ShapeDtypeStruct((M, N), BF16)
    return _call(body, name=name, grid=(N // tn, M // tm),
                 in_specs=[pl.BlockSpec((tm, K), lambda j, i: (i, 0)), w_spec, w_spec],
                 out_specs=[o_spec, o_spec, o_spec], out_shape=[sh, sh, sh], sem=("parallel", "parallel"),
                 args=(n, w1, w3), carry=xq.pop(4.0 * M * N * K / MATMUL_TFLOPS / 1e6) if xq is not None else None)


def _ffn_mid_bwd(dh, w2, s_a, s_b, *, name, xq=None):
    M, K = dh.shape
    N = w2.shape[0]
    tm = _tile(M, (384, 256, 128))
    tn = _tile(N, (1408, 512, 256, 128))

    def body(dh_ref, w_ref, a_ref, b_ref, da_ref, db_ref):
        ds = _dot_nt(dh_ref[...].astype(BF16), w_ref[...]) * FFN_RES
        da_ref[...] = (ds * a_ref[...].astype(F32)).astype(BF16)
        db_ref[...] = (ds * b_ref[...].astype(F32)).astype(BF16)

    o_spec = pl.BlockSpec((tm, tn), lambda j, i: (i, j))
    sh = jax.ShapeDtypeStruct((M, N), BF16)
    return _call(body, name=name, grid=(N // tn, M // tm),
                 in_specs=[pl.BlockSpec((tm, K), lambda j, i: (i, 0)), pl.BlockSpec((tn, K), lambda j, i: (j, 0)),
                           o_spec, o_spec],
                 out_specs=[o_spec, o_spec], out_shape=[sh, sh], sem=("parallel", "parallel"),
                 args=(dh, w2, s_a, s_b), carry=xq.pop(2.0 * M * N * K / MATMUL_TFLOPS / 1e6) if xq is not None else None)


def _conv_pre(xx, w_ref, b_ref, tr):
    acc = None
    for k in range(D_CONV):
        sh = D_CONV - 1 - k
        v = (pltpu.roll(xx, sh, 0) if sh else xx)[8:8 + tr]
        t = w_ref[k:k + 1, :] * v
        acc = t if acc is None else acc + t
    return acc + b_ref[...]


def _conv_fwd(zx, w, b, col_off, *, name, xq=None):
    LP = zx.shape[0]
    C = w.shape[1]
    tr = _tile(LP, (256, 128))
    tc = _tile(C, (512, 256, 128))
    co = col_off // tc

    def body(cur_ref, prev_ref, w_ref, b_ref, o_ref):
        i = pl.program_id(0)
        prev = jnp.where(i == 0, 0.0, prev_ref[...])
        pre = _conv_pre(jnp.concatenate([prev, cur_ref[...]], axis=0), w_ref, b_ref, tr)
        o_ref[...] = pre * _sigmoid(pre)

    return _call(
        body, name=name, grid=(LP // tr, C // tc),
        in_specs=[pl.BlockSpec((tr, tc), lambda i, j: (i, j + co)),
                  pl.BlockSpec((8, tc), lambda i, j: (jnp.maximum(i * (tr // 8) - 1, 0), j + co)),
                  pl.BlockSpec((D_CONV, tc), lambda i, j: (0, j)), pl.BlockSpec((1, tc), lambda i, j: (0, j))],
        out_specs=[pl.BlockSpec((tr, tc), lambda i, j: (i, j))],
        out_shape=[jax.ShapeDtypeStruct((LP, C), F32)], sem=("parallel", "parallel"),
        args=(zx, zx, w, b.reshape(1, C)), carry=_pop_for_rows(xq, LP, C))[0]


def _conv_bwd_g(zx, w, b, dact, col_off, *, name, xq=None):
    LP = zx.shape[0]
    C = w.shape[1]
    tr = _tile(LP, (256, 128))
    tc = _tile(C, (512, 256, 128))
    co = col_off // tc

    def body(cur_ref, prev_ref, w_ref, b_ref, d_ref, g_ref, dw_ref, db_ref):
        i = pl.program_id(1)

        @pl.when(i == 0)
        def _():
            dw_ref[...] = jnp.zeros_like(dw_ref)
            db_ref[...] = jnp.zeros_like(db_ref)

        prev = jnp.where(i == 0, 0.0, prev_ref[...])
        xx = jnp.concatenate([prev, cur_ref[...]], axis=0)
        pre = _conv_pre(xx, w_ref, b_ref, tr)
        sg = _sigmoid(pre)
        g = d_ref[...] * sg * (1.0 + pre * (1.0 - sg))
        g_ref[...] = g
        db_ref[...] += jnp.sum(g, axis=0, keepdims=True)
        rows = []
        for k in range(D_CONV):
            sh = D_CONV - 1 - k
            v = (pltpu.roll(xx, sh, 0) if sh else xx)[8:8 + tr]
            rows.append(jnp.sum(g * v, axis=0, keepdims=True))
        rows.append(jnp.zeros((8 - D_CONV, tc), F32))
        dw_ref[...] += jnp.concatenate(rows, axis=0)

    return _call(
        body, name=name, grid=(C // tc, LP // tr),
        in_specs=[pl.BlockSpec((tr, tc), lambda j, i: (i, j + co)),
                  pl.BlockSpec((8, tc), lambda j, i: (jnp.maximum(i * (tr // 8) - 1, 0), j + co)),
                  pl.BlockSpec((D_CONV, tc), lambda j, i: (0, j)), pl.BlockSpec((1, tc), lambda j, i: (0, j)),
                  pl.BlockSpec((tr, tc), lambda j, i: (i, j))],
        out_specs=[pl.BlockSpec((tr, tc), lambda j, i: (i, j)), pl.BlockSpec((8, tc), lambda j, i: (0, j)),
                   pl.BlockSpec((1, tc), lambda j, i: (0, j))],
        out_shape=[jax.ShapeDtypeStruct((LP, C), F32), jax.ShapeDtypeStruct((8, C), F32), jax.ShapeDtypeStruct((1, C), F32)],
        sem=("parallel", "arbitrary"), args=(zx, zx, w, b.reshape(1, C), dact), carry=_pop_for_rows(xq, LP, C, 2.0))


def _conv_bwd_u(g, w, *, name, xq=None):
    LP, C = g.shape
    tr = _tile(LP, (256, 128))
    tc = _tile(C, (512, 256, 128))
    nb = LP // tr

    def body(cur_ref, nxt_ref, w_ref, o_ref):
        i = pl.program_id(0)
        nxt = jnp.where(i == nb - 1, 0.0, nxt_ref[...])
        xx = jnp.concatenate([cur_ref[...], nxt], axis=0)
        acc = None
        for k in range(D_CONV):
            sh = D_CONV - 1 - k
            v = (pltpu.roll(xx, tr + 8 - sh, 0) if sh else xx)[:tr]
            t = w_ref[k:k + 1, :] * v
            acc = t if acc is None else acc + t
        o_ref[...] = acc

    return _call(
        body, name=name, grid=(nb, C // tc),
        in_specs=[pl.BlockSpec((tr, tc), lambda i, j: (i, j)),
                  pl.BlockSpec((8, tc), lambda i, j: (jnp.minimum((i + 1) * (tr // 8), LP // 8 - 1), j)),
                  pl.BlockSpec((D_CONV, tc), lambda i, j: (0, j))],
        out_specs=[pl.BlockSpec((tr, tc), lambda i, j: (i, j))],
        out_shape=[jax.ShapeDtypeStruct((LP, C), F32)], sem=("parallel", "parallel"),
        args=(g, g, w), carry=_pop_for_rows(xq, LP, C))[0]


def _ssd_prelude(dtr_ref, dtrt_ref, brow_ref, bcol_ref, alrow_ref, alcol_ref, Q):
    ii = lax.broadcasted_iota(jnp.int32, (Q, Q), 0)
    jj = lax.broadcasted_iota(jnp.int32, (Q, Q), 1)
    tril = ii >= jj
    dt_col = _softplus(dtr_ref[...] + brow_ref[...])
    a_row_p = -jnp.exp(alrow_ref[...])
    dt_row = _softplus(dtrt_ref[...] + bcol_ref[...])
    a_col_p = -jnp.exp(alcol_ref[...])
    cum_col = _dot3_left(tril.astype(BF16), dt_col * a_row_p)
    cum_row = _dot3_right(dt_row * a_col_p, (ii <= jj).astype(BF16))
    return ii, jj, tril, dt_col, dt_row, a_row_p, cum_col, cum_row


def _col_of(mat, lane_idx, h):
    return jnp.sum(jnp.where(lane_idx == h, mat, 0.0), axis=1, keepdims=True)


def _ssd_fwd(xbc, dtr, dtrt, brow, bcol, alrow, alcol, dvec, *, name, xq=None):
    LP = xbc.shape[0]
    Q = SSD_CHUNK
    nc = LP // Q
    G = SSM_GROUPS
    DI = dvec.shape[1]
    gw = DI // G
    hpg = gw // HEAD_DIM
    H = G * hpg
    boff, coff = DI, DI + G * D_STATE

    def body(xbc_ref, dtr_ref, dtrt_ref, brow_ref, bcol_ref, alrow_ref, alcol_ref, dvec_ref, y_ref, st_ref, state):
        c = pl.program_id(0)

        @pl.when(c == 0)
        def _():
            state[...] = jnp.zeros_like(state)

        st_ref[...] = state[...]
        ii, jj, tril, dt_col, dt_row, _, cum_col, cum_row = _ssd_prelude(
            dtr_ref, dtrt_ref, brow_ref, bcol_ref, alrow_ref, alcol_ref, Q)
        lane_h = lax.broadcasted_iota(jnp.int32, (Q, 128), 1)
        lane_g = lax.broadcasted_iota(jnp.int32, (Q, gw), 1) // HEAD_DIM
        for g in range(G):
            xg = xbc_ref[:, g * gw:(g + 1) * gw]
            bb = xbc_ref[:, boff + g * D_STATE: boff + (g + 1) * D_STATE].astype(BF16)
            cb = xbc_ref[:, coff + g * D_STATE: coff + (g + 1) * D_STATE].astype(BF16)
            gm = _dot_nt(cb, bb)
            sg = state[g]
            yoff = _dot(cb, sg.astype(BF16))
            ydiag = jnp.zeros((Q, gw), F32)
            esc = jnp.zeros((Q, gw), F32)
            wsc = jnp.zeros((Q, gw), F32)
            lam = jnp.zeros((1, gw), F32)
            for j in range(hpg):
                h = g * hpg + j
                ccol = _col_of(cum_col, lane_h, h)
                dcol = _col_of(dt_col, lane_h, h)
                seg = ccol - cum_row[h:h + 1, :]
                decay = jnp.exp(jnp.where(tril, seg, -jnp.inf))
                mh = gm * decay * dt_row[h:h + 1, :]
                hm = lane_g == j
                ydiag = ydiag + _dot(mh.astype(BF16), jnp.where(hm, xg, 0.0).astype(BF16))
                tot = ccol[Q - 1:Q, :]
                esc = jnp.where(hm, jnp.exp(ccol), esc)
                wsc = jnp.where(hm, jnp.exp(tot - ccol) * dcol, wsc)
                lam = jnp.where(hm[0:1], jnp.exp(tot), lam)
            y_ref[:, g * gw:(g + 1) * gw] = ydiag + yoff * esc + dvec_ref[:, g * gw:(g + 1) * gw] * xg
            state[g] = sg * lam + _dot_tn(bb, (xg * wsc).astype(BF16))

    W = xbc.shape[1]
    full = lambda shape: pl.BlockSpec(shape, lambda c: (0,) * len(shape))
    return _call(
        body, name=name, grid=(nc,),
        in_specs=[pl.BlockSpec((Q, W), lambda c: (c, 0)), pl.BlockSpec((Q, 128), lambda c: (c, 0)),
                  pl.BlockSpec((H, Q), lambda c: (0, c)), full((1, 128)), full((H, 1)), full((1, 128)), full((H, 1)),
                  full((1, DI))],
        out_specs=[pl.BlockSpec((Q, DI), lambda c: (c, 0)), pl.BlockSpec((None, G, D_STATE, gw), lambda c: (c, 0, 0, 0))],
        out_shape=[jax.ShapeDtypeStruct((LP, DI), F32), jax.ShapeDtypeStruct((nc, G, D_STATE, gw), F32)],
        scratch_shapes=[pltpu.VMEM((G, D_STATE, gw), F32)], sem=("arbitrary",),
        args=(xbc, dtr, dtrt, brow, bcol, alrow, alcol, dvec), carry=_pop_for_rows(xq, LP, W, 1.25))


def _ssd_bwd(xbc, dtr, dtrt, brow, bcol, alrow, alcol, dvec, dy, states, *, name, xq=None):
    LP = xbc.shape[0]
    Q = SSD_CHUNK
    nc = LP // Q
    G = SSM_GROUPS
    DI = dvec.shape[1]
    gw = DI // G
    hpg = gw // HEAD_DIM
    H = G * hpg
    boff, coff = DI, DI + G * D_STATE
    W = xbc.shape[1]

    def body(xbc_ref, dtr_ref, dtrt_ref, brow_ref, bcol_ref, alrow_ref, alcol_ref, dvec_ref, dy_ref, st_ref,
             dxbc_ref, ddtr_ref, dbias_ref, dalog_ref, ddvec_ref, dstate):
        c = pl.program_id(0)

        @pl.when(c == 0)
        def _():
            dstate[...] = jnp.zeros_like(dstate)
            dbias_ref[...] = jnp.zeros_like(dbias_ref)
            dalog_ref[...] = jnp.zeros_like(dalog_ref)
            ddvec_ref[...] = jnp.zeros_like(ddvec_ref)

        ii, jj, tril, dt_col, dt_row, a_row_p, cum_col, cum_row = _ssd_prelude(
            dtr_ref, dtrt_ref, brow_ref, bcol_ref, alrow_ref, alcol_ref, Q)
        eye = ii == jj
        lane_h = lax.broadcasted_iota(jnp.int32, (Q, 128), 1)
        row_h = lax.broadcasted_iota(jnp.int32, (Q, 128), 0)
        lane_g = lax.broadcasted_iota(jnp.int32, (Q, gw), 1) // HEAD_DIM
        lane_s = lax.broadcasted_iota(jnp.int32, (D_STATE, gw), 1) // HEAD_DIM
        dcum_mat = jnp.zeros((Q, 128), F32)
        ddt_mat = jnp.zeros((Q, 128), F32)
        dtot_row = jnp.zeros((1, 128), F32)
        for g in range(G):
            xg = xbc_ref[:, g * gw:(g + 1) * gw]
            dyg = dy_ref[:, g * gw:(g + 1) * gw]
            bb = xbc_ref[:, boff + g * D_STATE: boff + (g + 1) * D_STATE].astype(BF16)
            cb = xbc_ref[:, coff + g * D_STATE: coff + (g + 1) * D_STATE].astype(BF16)
            sg = st_ref[g]
            dsg = dstate[g]
            sb = sg.astype(BF16)
            dsb = dsg.astype(BF16)
            xb = xg.astype(BF16)
            gm = _dot_nt(cb, bb)
            cs = _dot(cb, sb)
            bds = _dot(bb, dsb)
            dxg = dvec_ref[:, g * gw:(g + 1) * gw] * dyg
            dgm = jnp.zeros((Q, Q), F32)
            esc = jnp.zeros((Q, gw), F32)
            wsc = jnp.zeros((Q, gw), F32)
            lam = jnp.zeros((1, gw), F32)
            dycs = dyg * cs
            xbds = xg * bds
            dss = dsg * sg
            for j in range(hpg):
                h = g * hpg + j
                ccol = _col_of(cum_col, lane_h, h)
                dcol = _col_of(dt_col, lane_h, h)
                drow = dt_row[h:h + 1, :]
                seg = ccol - cum_row[h:h + 1, :]
                decay = jnp.exp(jnp.where(tril, seg, -jnp.inf))
                hm = lane_g == j
                dyh = jnp.where(hm, dyg, 0.0).astype(BF16)
                gl = gm * decay
                mh = gl * drow
                dmf = _dot_nt(dyh, xb)
                dxg = dxg + _dot_tn(mh.astype(BF16), dyh)
                dgm = dgm + dmf * decay * drow
                n_p = dmf * gl
                n_m = n_p * drow
                rowsum_n = jnp.sum(n_m, axis=1, keepdims=True)
                colsum_n = jnp.sum(jnp.where(eye, jnp.sum(n_m, axis=0, keepdims=True), 0.0), axis=1, keepdims=True)
                colsum_np = jnp.sum(jnp.where(eye, jnp.sum(n_p, axis=0, keepdims=True), 0.0), axis=1, keepdims=True)
                tot = ccol[Q - 1:Q, :]
                e = jnp.exp(ccol)
                wexp = jnp.exp(tot - ccol)
                wcol = wexp * dcol
                lamh = jnp.exp(tot)
                yoff_t = jnp.sum(jnp.where(hm, dycs, 0.0), axis=1, keepdims=True) * e
                e_s = jnp.sum(jnp.where(hm, xbds, 0.0), axis=1, keepdims=True)
                ew = e_s * wcol
                dtot = jnp.sum(ew, axis=0, keepdims=True) + lamh * jnp.sum(
                    jnp.sum(jnp.where(lane_s == j, dss, 0.0), axis=1, keepdims=True), axis=0, keepdims=True)
                dcum_h = rowsum_n + yoff_t - colsum_n - ew
                ddt_h = colsum_np + e_s * wexp
                onehot = lane_h == h
                dcum_mat = jnp.where(onehot, dcum_h, dcum_mat)
                ddt_mat = jnp.where(onehot, ddt_h, ddt_mat)
                dtot_row = jnp.where(onehot[0:1], dtot, dtot_row)
                esc = jnp.where(hm, e, esc)
                wsc = jnp.where(hm, wcol, wsc)
                lam = jnp.where(hm[0:1], lamh, lam)
            dgb = dgm.astype(BF16)
            dye = (dyg * esc).astype(BF16)
            xw = (xg * wsc).astype(BF16)
            dxbc_ref[:, g * gw:(g + 1) * gw] = dxg + bds * wsc
            dxbc_ref[:, boff + g * D_STATE: boff + (g + 1) * D_STATE] = _dot_tn(dgb, cb) + _dot_nt(xw, dsb)
            dxbc_ref[:, coff + g * D_STATE: coff + (g + 1) * D_STATE] = _dot(dgb, bb) + _dot_nt(dye, sb)
            dstate[g] = dsg * lam + _dot_tn(cb, dye)
            ddvec_ref[:, g * gw:(g + 1) * gw] += jnp.sum(dyg * xg, axis=0, keepdims=True)
        dcum_mat = dcum_mat + jnp.where(row_h == Q - 1, dtot_row, 0.0)
        da = _dot3_left((ii <= jj).astype(BF16), dcum_mat)
        ddt = ddt_mat + da * a_row_p
        dalog_ref[...] += jnp.sum(da * dt_col, axis=0, keepdims=True) * a_row_p
        ddtr = ddt * _sigmoid(dtr_ref[...] + brow_ref[...])
        ddtr_ref[...] = ddtr
        dbias_ref[...] += jnp.sum(ddtr, axis=0, keepdims=True)

    full = lambda shape: pl.BlockSpec(shape, lambda c: (0,) * len(shape))
    rc = lambda c: nc - 1 - c
    return _call(
        body, name=name, grid=(nc,),
        in_specs=[pl.BlockSpec((Q, W), lambda c: (rc(c), 0)), pl.BlockSpec((Q, 128), lambda c: (rc(c), 0)),
                  pl.BlockSpec((H, Q), lambda c: (0, rc(c))), full((1, 128)), full((H, 1)), full((1, 128)), full((H, 1)),
                  full((1, DI)), pl.BlockSpec((Q, DI), lambda c: (rc(c), 0)),
                  pl.BlockSpec((None, G, D_STATE, gw), lambda c: (rc(c), 0, 0, 0))],
        out_specs=[pl.BlockSpec((Q, W), lambda c: (rc(c), 0)), pl.BlockSpec((Q, 128), lambda c: (rc(c), 0)),
                   full((1, 128)), full((1, 128)), full((1, DI))],
        out_shape=[jax.ShapeDtypeStruct((LP, W), F32), jax.ShapeDtypeStruct((LP, 128), F32),
                   jax.ShapeDtypeStruct((1, 128), F32), jax.ShapeDtypeStruct((1, 128), F32),
                   jax.ShapeDtypeStruct((1, DI), F32)],
        scratch_shapes=[pltpu.VMEM((G, D_STATE, gw), F32)], sem=("arbitrary",),
        args=(xbc, dtr, dtrt, brow, bcol, alrow, alcol, dvec, dy, states), carry=_pop_for_rows(xq, LP, W, 4.5))


def _gate_fwd(y, zx, g, *, name):
    LP, DI = y.shape
    gw = DI // SSM_GROUPS
    tr = _tile(LP, (256, 128))

    def body(y_ref, z_ref, g_ref, o_ref):
        for k in range(SSM_GROUPS):
            sl = slice(k * gw, (k + 1) * gw)
            z = z_ref[:, sl]
            t = y_ref[:, sl] * (z * _sigmoid(z))
            r = lax.rsqrt(jnp.mean(t * t, axis=1, keepdims=True) + RMS_EPS)
            o_ref[:, sl] = (t * r * g_ref[:, sl]).astype(BF16)

    row = pl.BlockSpec((tr, DI), lambda i: (i, 0))
    return pl.pallas_call(
        body, name=name, grid=(LP // tr,), in_specs=[row, row, pl.BlockSpec((1, DI), lambda i: (0, 0))],
        out_specs=row, out_shape=jax.ShapeDtypeStruct((LP, DI), BF16), compiler_params=_cp("parallel"),
    )(y, zx, g)


def _gate_bwd(y, zx, g, dyn, *, name):
    LP, DI = y.shape
    gw = DI // SSM_GROUPS
    tr = _tile(LP, (256, 128))

    def body(y_ref, z_ref, g_ref, d_ref, dy_ref, dz_ref, dg_ref):
        i = pl.program_id(0)

        @pl.when(i == 0)
        def _():
            dg_ref[...] = jnp.zeros_like(dg_ref)

        for k in range(SSM_GROUPS):
            sl = slice(k * gw, (k + 1) * gw)
            z = z_ref[:, sl]
            yv = y_ref[:, sl]
            sg = _sigmoid(z)
            sz = z * sg
            t = yv * sz
            r = lax.rsqrt(jnp.mean(t * t, axis=1, keepdims=True) + RMS_EPS)
            th = t * r
            d = d_ref[:, sl]
            dtn = d * g_ref[:, sl]
            dt_ = r * (dtn - th * jnp.mean(dtn * th, axis=1, keepdims=True))
            dg_ref[:, sl] += jnp.sum(d * th, axis=0, keepdims=True)
            dy_ref[:, sl] = dt_ * sz
            dz_ref[:, sl] = dt_ * yv * sg * (1.0 + z * (1.0 - sg))

    row = pl.BlockSpec((tr, DI), lambda i: (i, 0))
    vec = pl.BlockSpec((1, DI), lambda i: (0, 0))
    return pl.pallas_call(
        body, name=name, grid=(LP // tr,), in_specs=[row, row, vec, row], out_specs=[row, row, vec],
        out_shape=[jax.ShapeDtypeStruct((LP, DI), F32), jax.ShapeDtypeStruct((LP, DI), F32),
                   jax.ShapeDtypeStruct((1, DI), F32)],
        compiler_params=_cp("arbitrary"),
    )(y, zx, g, dyn)


EXP_ZERO = -104.0
LOG2E = 1.4426950408889634


def _dot2_right(x, t2_bf16):
    hi = x.astype(BF16)
    lo = (x - hi.astype(F32)).astype(BF16)
    return _dot(jnp.concatenate([hi, lo], axis=1), t2_bf16)


def _tri2(T, upper):
    r = lax.broadcasted_iota(jnp.int32, (2 * T, T), 0) % T
    c = lax.broadcasted_iota(jnp.int32, (2 * T, T), 1)
    return (r <= c if upper else r >= c).astype(BF16)


def _sb_tile(q, k_blk, lower2, valid=None):
    z = _dot_nt(q, k_blk)
    sp = jnp.maximum(z, 0.0) + jnp.log(1.0 + jnp.exp2(jnp.abs(z) * (-LOG2E)))
    if valid is not None:
        sp = jnp.where(valid, sp, 0.0)
    return z, sp, z - _dot2_right(sp, lower2)


def _sb_weights(zr, c, valid=None):
    w = jnp.exp(zr + c)
    return w if valid is None else jnp.where(valid, w, 0.0)


def _sb_fwd(q, k, v, zmax, *, name):
    H, LP, dh = q.shape
    T = ATT_BLOCK
    nq = LP // T
    assert nq < LANES

    def body(q_ref, k_ref, v_ref, zb_ref, o_ref, c_ref, kf_ref):
        i = pl.program_id(1)
        ii = lax.broadcasted_iota(jnp.int32, (T, T), 0)
        jj = lax.broadcasted_iota(jnp.int32, (T, T), 1)
        lane = lax.broadcasted_iota(jnp.int32, (T, LANES), 1)
        lower2 = _tri2(T, upper=False)
        qv = q_ref[...]
        zb = zb_ref[0:1, 0:1]

        def kv(kb):
            ks = pl.multiple_of(kb * T, T)
            return k_ref[pl.ds(ks, T), :], v_ref[pl.ds(ks, T), :]

        kd, vd = kv(i)
        k1, v1 = kv(jnp.maximum(i - 1, 0))
        diag = jj < ii
        prev = jnp.full((T, T), i > 0)
        _, sp0, zr0 = _sb_tile(qv, kd, lower2, diag)
        _, sp1, zr1 = _sb_tile(qv, k1, lower2, prev)
        c0 = -jnp.sum(sp0, axis=1, keepdims=True)
        acc = (_dot(_sb_weights(zr0, 0.0, diag).astype(BF16), vd)
               + _dot(_sb_weights(zr1, c0, prev).astype(BF16), v1))
        c = c0 - jnp.sum(sp1, axis=1, keepdims=True)
        c_ref[...] = jnp.where(lane == i - 1, c0, 0.0)

        def alive(c):
            return jnp.max(c + zb) > EXP_ZERO

        def cond(carry):
            kb, _, _, live = carry
            return (kb >= 0) & live

        def step(carry):
            kb, c, acc, _ = carry
            kt, vt = kv(kb)
            _, sp, zr = _sb_tile(qv, kt, lower2)
            acc = acc + _dot(_sb_weights(zr, c).astype(BF16), vt)
            c_ref[...] = jnp.where(lane == kb, c, c_ref[...])
            c = c - jnp.sum(sp, axis=1, keepdims=True)
            return kb - 1, c, acc, alive(c)

        kb, _, acc, _ = lax.while_loop(cond, step, (i - 2, c, acc, alive(c)))
        o_ref[...] = acc
        kf_ref[...] = jnp.zeros_like(kf_ref) + (kb + 1).astype(F32)

    blk = pl.BlockSpec((None, T, dh), lambda h, i: (h, i, 0))
    cblk = pl.BlockSpec((None, T, LANES), lambda h, i: (h, i, 0))
    whole = pl.BlockSpec((None, LP, dh), lambda h, i: (h, 0, 0))
    return pl.pallas_call(
        body, name=name, grid=(H, nq), in_specs=[blk, whole, whole, pl.BlockSpec((1, LANES), lambda h, i: (0, 0))],
        out_specs=[blk, cblk, pl.BlockSpec((None, None, 8, LANES), lambda h, i: (h, i, 0, 0))],
        out_shape=[jax.ShapeDtypeStruct((H, LP, dh), F32), jax.ShapeDtypeStruct((H, LP, LANES), F32),
                   jax.ShapeDtypeStruct((H, nq, 8, LANES), F32)],
        compiler_params=_cp("parallel", "parallel"),
    )(q, k, v, zmax)


def _sb_bwd(kstart, q, k, v, cmat, do, *, name):
    H, LP, dh = q.shape
    T = ATT_BLOCK
    nq = LP // T

    def body(ks_ref, q_ref, k_ref, v_ref, c_ref, do_ref, dq_ref, dk_ref, dv_ref):
        h = pl.program_id(0)
        i = pl.program_id(1)

        @pl.when(i == 0)
        def _():
            dk_ref[...] = jnp.zeros_like(dk_ref)
            dv_ref[...] = jnp.zeros_like(dv_ref)

        ii = lax.broadcasted_iota(jnp.int32, (T, T), 0)
        jj = lax.broadcasted_iota(jnp.int32, (T, T), 1)
        lane = lax.broadcasted_iota(jnp.int32, (T, LANES), 1)
        lower2 = _tri2(T, upper=False)
        upper2 = _tri2(T, upper=True)
        qv = q_ref[...]
        dob = do_ref[...].astype(BF16)
        cm = c_ref[...]

        def front(kb, valid=None):
            ks = pl.multiple_of(kb * T, T)
            k_blk = k_ref[pl.ds(ks, T), :]
            c = jnp.sum(jnp.where(lane == kb, cm, 0.0), axis=1, keepdims=True)
            z, sp, zr = _sb_tile(qv, k_blk, lower2, valid)
            w = _sb_weights(zr, c, valid)
            gw_ = w * _dot_nt(dob, v_ref[pl.ds(ks, T), :])
            gin = _dot2_right(gw_, upper2)
            return ks, k_blk, w, gw_, gin, jnp.exp(z - sp)

        def back(t, cg, dq, valid=None):
            ks, k_blk, w, gw_, gin, sig = t
            dz = gw_ - sig * (cg + gin)
            if valid is not None:
                dz = jnp.where(valid, dz, 0.0)
            dz = dz.astype(BF16)
            dk_ref[pl.ds(ks, T), :] += _dot_tn(dz, qv)
            dv_ref[pl.ds(ks, T), :] += _dot_tn(w.astype(BF16), dob)
            return cg + jnp.sum(gw_, axis=1, keepdims=True), dq + _dot(dz, k_blk)

        cg, dq = lax.fori_loop(ks_ref[h, i], i - 1, lambda kb, cr: back(front(kb), *cr),
                               (jnp.zeros((T, 1), F32), jnp.zeros((T, dh), F32)))
        diag = jj < ii
        prev = jnp.full((T, T), i > 0)
        t1 = front(jnp.maximum(i - 1, 0), prev)
        t0 = front(i, diag)
        cg, dq = back(t1, cg, dq, prev)
        _, dq = back(t0, cg, dq, diag)
        dq_ref[...] = dq

    blk = pl.BlockSpec((None, T, dh), lambda h, i, ks: (h, i, 0))
    cblk = pl.BlockSpec((None, T, LANES), lambda h, i, ks: (h, i, 0))
    whole = pl.BlockSpec((None, LP, dh), lambda h, i, ks: (h, 0, 0))
    sh = jax.ShapeDtypeStruct((H, LP, dh), F32)
    return pl.pallas_call(
        body, name=name,
        grid_spec=pltpu.PrefetchScalarGridSpec(
            num_scalar_prefetch=1, grid=(H, nq), in_specs=[blk, whole, whole, cblk, blk], out_specs=[blk, whole, whole]),
        out_shape=[sh, sh, sh], compiler_params=_cp("parallel", "arbitrary"),
    )(kstart, q, k, v, cmat, do)


def _loss_head(h, tgt, seq, *, name):
    LP, D = h.shape
    tr = _tile(LP, (384, 256, 128))

    def body(h_ref, t_ref, dh_ref, l_ref):
        i = pl.program_id(0)

        @pl.when(i == 0)
        def _():
            l_ref[...] = jnp.zeros_like(l_ref)

        row = lax.broadcasted_iota(jnp.int32, (tr, D), 0) + i * tr
        e = jnp.where((row >= N_META) & (row < N_META + seq), h_ref[...] - t_ref[...], 0.0)
        dh_ref[...] = e * (1.0 / D)
        l_ref[...] += jnp.sum(e * e, axis=0, keepdims=True) * (0.5 / D)

    row = pl.BlockSpec((tr, D), lambda i: (i, 0))
    return pl.pallas_call(
        body, name=name, grid=(LP // tr,), in_specs=[row, row], out_specs=[row, pl.BlockSpec((1, D), lambda i: (0, 0))],
        out_shape=[jax.ShapeDtypeStruct((LP, D), F32), jax.ShapeDtypeStruct((1, D), F32)],
        compiler_params=_cp("arbitrary"),
    )(h, tgt)


def _adamw(w, g, m, v, *, name):
    shape = w.shape
    C = shape[-1]
    R = math.prod(shape) // C
    tr = _tile(R, (512, 256, 128, 64, 32, 16, 8))
    c1 = 1.0 / (1.0 - ADAM_B1 ** ADAM_STEP)
    c2 = 1.0 / (1.0 - ADAM_B2 ** ADAM_STEP)

    def body(w_ref, g_ref, m_ref, v_ref, d_ref, nm_ref, nv_ref):
        gv = g_ref[...]
        nm = ADAM_B1 * m_ref[...] + (1.0 - ADAM_B1) * gv
        nv = ADAM_B2 * v_ref[...] + (1.0 - ADAM_B2) * (gv * gv)
        d_ref[...] = -ADAM_LR * ((nm * c1) / (jnp.sqrt(nv * c2) + ADAM_EPS) + ADAM_WD * w_ref[...])
        nm_ref[...] = nm
        nv_ref[...] = nv

    blk = pl.BlockSpec((tr, C), lambda i: (i, 0))
    sh = jax.ShapeDtypeStruct((R, C), F32)
    d, nm, nv = pl.pallas_call(
        body, name=name, grid=(R // tr,), in_specs=[blk] * 4, out_specs=[blk] * 3, out_shape=[sh] * 3,
        compiler_params=_cp("parallel"),
    )(w.reshape(R, C), g.reshape(R, C), m.reshape(R, C), v.reshape(R, C))
    return d.reshape(shape), nm.reshape(shape), nv.reshape(shape)


def _sum_rows(buf, *, name):
    n, R, C = buf.shape
    tr = _tile(R, (512, 256, 128, 64, 32, 16, 8))

    def body(b_ref, o_ref):
        acc = b_ref[0].astype(F32)
        for k in range(1, n):
            acc = acc + b_ref[k].astype(F32)
        o_ref[...] = acc

    return pl.pallas_call(
        body, name=name, grid=(R // tr,), in_specs=[pl.BlockSpec((n, tr, C), lambda i: (0, i, 0))],
        out_specs=pl.BlockSpec((tr, C), lambda i: (i, 0)), out_shape=jax.ShapeDtypeStruct((R, C), F32),
        compiler_params=_cp("parallel"),
    )(buf)


def _interleave(buf, *, name):
    n, R, C = buf.shape
    tr = _tile(R, (256, 128, 64, 32, 16))

    def body(b_ref, o_ref):
        for d in range(n):
            o_ref[:, d * C:(d + 1) * C] = b_ref[d]

    return pl.pallas_call(
        body, name=name, grid=(R // tr,), in_specs=[pl.BlockSpec((n, tr, C), lambda i: (0, i, 0))],
        out_specs=pl.BlockSpec((tr, n * C), lambda i: (i, 0)), out_shape=jax.ShapeDtypeStruct((R, n * C), buf.dtype),
        compiler_params=_cp("parallel"),
    )(buf)


def _deinterleave(x, *, out_dtype, name):
    R, NC = x.shape
    C = NC // N_DEV
    tr = _tile(R, (256, 128, 64, 32, 16))

    def body(x_ref, o_ref):
        for d in range(N_DEV):
            o_ref[d] = x_ref[:, d * C:(d + 1) * C].astype(out_dtype)

    return pl.pallas_call(
        body, name=name, grid=(R // tr,), in_specs=[pl.BlockSpec((tr, NC), lambda i: (i, 0))],
        out_specs=pl.BlockSpec((N_DEV, tr, C), lambda i: (0, i, 0)),
        out_shape=jax.ShapeDtypeStruct((N_DEV, R, C), out_dtype), compiler_params=_cp("parallel"),
    )(x)


def _mesh_pos():
    x, y, c = lax.axis_index("x"), lax.axis_index("y"), lax.axis_index("c")
    return x, y, c, 4 * x + 2 * y + c


def _peer(x, y, c, f):
    px, py, pc = (x + ((f >> 2) & 1)) % 2, (y + ((f >> 1) & 1)) % 2, (c + (f & 1)) % 2
    return (px, py, pc), 4 * px + 2 * py + pc


def _exchange(arrs, *, scatter, name):
    n = len(arrs)
    hbm = pl.BlockSpec(memory_space=pl.ANY)

    def body(*refs):
        xa = (refs[:n], refs[n:2 * n], *refs[2 * n:], scatter)
        _xchg_start(*xa)
        _xchg_wait(*xa)

    return list(pl.pallas_call(
        body, name=name, in_specs=[hbm] * n, out_specs=[hbm] * n, out_shape=_xchg_shapes(arrs, scatter),
        scratch_shapes=_xchg_scratch(n), compiler_params=pltpu.CompilerParams(has_side_effects=True),
    )(*arrs))


def _unheads(t):
    H, LP, dh = t.shape
    return t.transpose(1, 0, 2).reshape(LP, H * dh)


def _shard_rows(t):
    return t.reshape(N_DEV, t.shape[0] // N_DEV, t.shape[1])


def _ffn_fwd(h, g, W, l, s, xq):
    tag = f"{l}{s}"
    n = _rms_fwd(h, g, name=f"ffn_norm_{tag}")
    w1, w3 = W[("w13", l, s)]
    a, b, sw = _ffn_up(n, w1, w3, name=f"ffn_up_{tag}", xq=xq)
    h2 = _matmul([(sw, W[("w2", l, s)])], res=h, alpha=FFN_RES, name=f"ffn_down_{tag}", xq=xq)
    return h2, (h, n, a, b, sw)


def _ffn_bwd(dh, saved, g, W, l, s, xq, emit):
    tag = f"{l}{s}"
    h, n, a, b, sw = saved
    (w1, w3), w2 = W[("w13", l, s)], W[("w2", l, s)]
    da, db = _ffn_mid_bwd(dh, w2, a, b, name=f"ffn_mid_bwd_{tag}", xq=xq)
    dw2 = _matmul([(sw, dh)], trans_a=True, alpha=FFN_RES, out_dtype=BF16, name=f"ffn_dw2_{tag}", xq=xq)
    emit(("w2", l, s), [_shard_rows(dw2)])
    dn = _matmul([(da, w1), (db, w3)], trans_b=True, name=f"ffn_dn_{tag}", xq=xq)
    dw1 = _matmul([(n, da)], trans_a=True, shards=N_DEV, out_dtype=BF16, name=f"ffn_dw1_{tag}", xq=xq)
    dw3 = _matmul([(n, db)], trans_a=True, shards=N_DEV, out_dtype=BF16, name=f"ffn_dw3_{tag}", xq=xq)
    emit(("w13", l, s), [dw1, dw3])
    dh_in, dg = _rms_bwd(h, g, dn, res=dh, name=f"ffn_norm_bwd_{tag}")
    return dh_in, dg


def _local_step(x, tgt, W, xq=None, recv=None):
    G = {}

    def emit(key, arrs):
        G[key] = arrs
        if xq is not None:
            xq.push(arrs, True, lambda res, key=key: recv.__setitem__(key, res))

    seq, D = x.shape
    L = N_META + seq
    LP = -(-L // ROW_ALIGN) * ROW_ALIGN
    pad = LP - L
    H_sb = D // HEAD_DIM
    DI = W["ssm_norm_g"].shape[-1]
    H_ssm = DI // HEAD_DIM
    CONV = DI + 2 * SSM_GROUPS * D_STATE
    ZX = DI + CONV

    h0 = jnp.concatenate([W["meta_tokens"], x, jnp.zeros((pad, D), F32)], axis=0)
    tgt_p = jnp.pad(tgt, ((N_META, pad), (0, 0)))
    ng = W["norm_g"]

    h1, sv_f00 = _ffn_fwd(h0, ng[0, 0], W, 0, 0, xq)
    u0 = _rms_fwd(h1, ng[0, 1], name="ssm_norm")
    w_in = W["ssm_in_proj"][0]
    w_zx = w_in[:, :ZX]
    w_dt = jnp.pad(w_in[:, ZX:], ((0, 0), (0, 128 - H_ssm)))
    zx = _matmul([(u0, w_zx)], name="ssm_in_zx", xq=xq)
    dtr = _matmul([(u0, w_dt)], name="ssm_in_dt")
    conv_w, conv_b = W["ssm_conv_w"][0], W["ssm_conv_b"][0]
    xbc = _conv_fwd(zx, conv_w, conv_b, DI, name="ssm_conv", xq=xq)
    dtrt = dtr[:, :H_ssm].T
    padh = lambda t: jnp.pad(t.reshape(1, H_ssm), ((0, 0), (0, 128 - H_ssm)))
    brow, bcol = padh(W["ssm_dt_bias"][0]), W["ssm_dt_bias"][0].reshape(H_ssm, 1)
    alrow, alcol = padh(W["ssm_a_log"][0]), W["ssm_a_log"][0].reshape(H_ssm, 1)
    dvec = jnp.repeat(W["ssm_d"][0], HEAD_DIM).reshape(1, DI)
    ssm_args = (xbc, dtr, dtrt, brow, bcol, alrow, alcol, dvec)
    y, states = _ssd_fwd(*ssm_args, name="ssd_fwd", xq=xq)
    sng = W["ssm_norm_g"].reshape(1, DI)
    yn = _gate_fwd(y, zx, sng, name="ssm_gate")
    w_out = W["ssm_out_proj"][0]
    h2 = _matmul([(yn, w_out)], res=h1, name="ssm_out", xq=xq)
    h3, sv_f01 = _ffn_fwd(h2, ng[0, 2], W, 0, 1, xq)

    kv_in = _rms_fwd(h3, W["kv_norm_g"], name="kv_norm")
    kraw = _matmul([(kv_in, W["w_k"])], shards=H_sb, name="kv_k")
    vh = _matmul([(kv_in, W["w_v"])], shards=H_sb, out_dtype=BF16, name="kv_v")
    kh = _rms_fwd(kraw.reshape(H_sb * LP, HEAD_DIM), W["k_norm_g"], name="k_headnorm").reshape(H_sb, LP, HEAD_DIM)

    h4, sv_f10 = _ffn_fwd(h3, ng[1, 0], W, 1, 0, xq)
    u1 = _rms_fwd(h4, ng[1, 1], name="sb_norm")
    qraw = _matmul([(u1, W["sb_w_q"][0])], shards=H_sb, name="sb_q")
    scale = HEAD_DIM ** -0.5
    qh = _rms_fwd(qraw.reshape(H_sb * LP, HEAD_DIM), W["sb_q_norm_g"][0], scale=scale,
                  name="q_headnorm").reshape(H_sb, LP, HEAD_DIM)
    zmax = 1.02 * math.sqrt(HEAD_DIM) * jnp.max(jnp.abs(W["sb_q_norm_g"])) * jnp.max(jnp.abs(W["k_norm_g"]))
    o, cmat, kfirst = _sb_fwd(qh, kh, vh, jnp.full((1, LANES), zmax, F32), name="sb_fwd")
    kstart = kfirst[:, :, 0, 0].astype(jnp.int32)
    o_flat = _unheads(o)
    h5 = _matmul([(o_flat, W["sb_w_o"][0])], res=h4, name="sb_out")
    h6, sv_f11 = _ffn_fwd(h5, ng[1, 2], W, 1, 1, xq)

    dh, lvec = _loss_head(h6, tgt_p, seq, name="loss_head")
    loss = jnp.sum(lvec)
    dng = [[None] * 3 for _ in range(2)]
    shard_rows = _shard_rows

    dh, dng[1][2] = _ffn_bwd(dh, sv_f11, ng[1, 2], W, 1, 1, xq, emit)
    g_wo = shard_rows(_matmul([(o_flat, dh)], trans_a=True, out_dtype=BF16, name="sb_dwo"))
    do = _matmul([(dh, W["sb_w_o"][0])], trans_b=True, shards=H_sb, name="sb_do")
    dq, dk, dv = _sb_bwd(kstart, qh, kh, vh, cmat, do, name="sb_bwd")
    dqraw, dqg = _rms_bwd(qraw.reshape(H_sb * LP, HEAD_DIM), W["sb_q_norm_g"][0], dq.reshape(H_sb * LP, HEAD_DIM),
                          alpha=scale, name="q_headnorm_bwd")
    G["sb_q_norm_g"] = dqg
    dqraw = _unheads(dqraw.reshape(H_sb, LP, HEAD_DIM))
    g_wq = shard_rows(_matmul([(u1, dqraw)], trans_a=True, out_dtype=BF16, name="sb_dwq"))
    emit("sb", [g_wq, g_wo])
    du1 = _matmul([(dqraw, W["sb_w_q"][0])], trans_b=True, name="sb_du")
    dh, dng[1][1] = _rms_bwd(h4, ng[1, 1], du1, res=dh, name="sb_norm_bwd")
    dh, dng[1][0] = _ffn_bwd(dh, sv_f10, ng[1, 0], W, 1, 0, xq, emit)

    dkraw, dkg = _rms_bwd(kraw.reshape(H_sb * LP, HEAD_DIM), W["k_norm_g"], dk.reshape(H_sb * LP, HEAD_DIM),
                          name="k_headnorm_bwd")
    G["k_norm_g"] = dkg.reshape(-1)
    dkraw = _unheads(dkraw.reshape(H_sb, LP, HEAD_DIM))
    dvf = _unheads(dv)
    g_wk = shard_rows(_matmul([(kv_in, dkraw)], trans_a=True, out_dtype=BF16, name="kv_dwk"))
    g_wv = shard_rows(_matmul([(kv_in, dvf)], trans_a=True, out_dtype=BF16, name="kv_dwv"))
    emit("kv", [g_wk, g_wv])
    dkv = _matmul([(dkraw, W["w_k"]), (dvf, W["w_v"])], trans_b=True, name="kv_din", xq=xq)
    dh, dkvg = _rms_bwd(h3, W["kv_norm_g"], dkv, res=dh, name="kv_norm_bwd")
    G["kv_norm_g"] = dkvg.reshape(-1)

    dh, dng[0][2] = _ffn_bwd(dh, sv_f01, ng[0, 2], W, 0, 1, xq, emit)
    emit("wout", [shard_rows(_matmul([(yn, dh)], trans_a=True, out_dtype=BF16, name="ssm_dwout", xq=xq))])
    dyn = _matmul([(dh, w_out)], trans_b=True, name="ssm_dyn", xq=xq)
    dy, dz, dsng = _gate_bwd(y, zx, sng, dyn, name="ssm_gate_bwd")
    G["ssm_norm_g"] = dsng
    dxbc, ddtr, dbias, dalog, ddvec = _ssd_bwd(*ssm_args, dy, states, name="ssd_bwd", xq=xq)
    G["ssm_dt_bias"] = dbias[:, :H_ssm]
    G["ssm_a_log"] = dalog[:, :H_ssm]
    G["ssm_d"] = jnp.sum(ddvec.reshape(H_ssm, HEAD_DIM), axis=1).reshape(1, H_ssm)
    gpre, dcw, dcb = _conv_bwd_g(zx, conv_w, conv_b, dxbc, DI, name="ssm_conv_bwd_g", xq=xq)
    G["ssm_conv_w"] = dcw[:D_CONV][None]
    G["ssm_conv_b"] = dcb
    dxbc_pre = _conv_bwd_u(gpre, conv_w, name="ssm_conv_bwd_u", xq=xq)
    emit("win", [_deinterleave(jnp.concatenate([
        _matmul([(u0, dz)], trans_a=True, out_dtype=BF16, name="ssm_dwin_z", xq=xq),
        _matmul([(u0, dxbc_pre)], trans_a=True, out_dtype=BF16, name="ssm_dwin_x", xq=xq),
        _matmul([(u0, ddtr)], trans_a=True, out_dtype=BF16, name="ssm_dwin_dt")[:, :H_ssm]], axis=1),
        out_dtype=BF16, name="ssm_dwin_shards")])
    du0 = _matmul([(dz, w_zx)], trans_b=True, name="ssm_du_z", xq=xq)
    du0 = _matmul([(dxbc_pre, w_zx)], trans_b=True, b_off=DI, res=du0, name="ssm_du_x", xq=xq)
    du0 = _matmul([(ddtr, w_dt)], trans_b=True, res=du0, name="ssm_du_dt")
    dh, dng[0][1] = _rms_bwd(h1, ng[0, 1], du0, res=dh, name="ssm_norm_bwd")
    dh, dng[0][0] = _ffn_bwd(dh, sv_f00, ng[0, 0], W, 0, 0, xq, emit)

    G["norm_g"] = jnp.stack([jnp.concatenate(r, axis=0) for r in dng])
    G["meta_tokens"] = dh[:N_META]
    return loss, dh[N_META:L], G


WEIGHTS = ['meta_tokens', 'norm_g', 'ffn_w1', 'ffn_w3', 'ffn_w2', 'ssm_in_proj', 'ssm_conv_w', 'ssm_conv_b',
           'ssm_dt_bias', 'ssm_a_log', 'ssm_d', 'ssm_norm_g', 'ssm_out_proj', 'kv_norm_g', 'w_k', 'k_norm_g', 'w_v',
           'sb_w_q', 'sb_q_norm_g', 'sb_w_o']
SHARD_AXIS = {'meta_tokens': 1, 'norm_g': 2, 'ffn_w1': 3, 'ffn_w3': 3, 'ffn_w2': 2, 'ssm_in_proj': 2, 'ssm_conv_w': 2,
              'ssm_conv_b': 1, 'ssm_dt_bias': None, 'ssm_a_log': None, 'ssm_d': None, 'ssm_norm_g': 1,
              'ssm_out_proj': 1, 'kv_norm_g': None, 'w_k': 0, 'k_norm_g': None, 'w_v': 0, 'sb_w_q': 1,
              'sb_q_norm_g': None, 'sb_w_o': 1}
MATMUL_WEIGHTS = ('ffn_w1', 'ffn_w3', 'ffn_w2', 'ssm_in_proj', 'ssm_out_proj', 'w_k', 'w_v', 'sb_w_q', 'sb_w_o')
PACK_ROWS = 16


def _pack(arrs, dtype):
    flat = jnp.concatenate([a.reshape(-1).astype(dtype) for a in arrs])
    n = flat.shape[0]
    npad = -(-n // (LANES * PACK_ROWS)) * (LANES * PACK_ROWS)
    return jnp.pad(flat, (0, npad - n)).reshape(npad // LANES, LANES)


def _unpack_gathered(buf, names, shard_shapes, dtype):
    flat = buf.reshape(N_DEV, -1)
    out, off = {}, 0
    for n in names:
        shp = shard_shapes[n]
        size = math.prod(shp)
        t = flat[:, off:off + size].reshape((N_DEV,) + tuple(shp))
        off += size
        ax = SHARD_AXIS[n]
        t = jnp.moveaxis(t, 0, ax)
        full = shp[:ax] + (N_DEV * shp[ax],) + shp[ax + 1:]
        out[n] = t.reshape(full).astype(dtype)
    return out


def _to_shards(g, ax):
    shp = g.shape
    t = g.reshape(shp[:ax] + (N_DEV, shp[ax] // N_DEV) + shp[ax + 1:])
    return jnp.moveaxis(t, ax, 0).reshape(N_DEV, -1)


def kernel(x, meta_tokens, norm_g, ffn_w1, ffn_w3, ffn_w2, ssm_in_proj, ssm_conv_w, ssm_conv_b, ssm_dt_bias, ssm_a_log, ssm_d, ssm_norm_g, ssm_out_proj, kv_norm_g, w_k, k_norm_g, w_v, sb_w_q, sb_q_norm_g, sb_w_o, loss_target, m_meta_tokens, m_norm_g, m_ffn_w1, m_ffn_w3, m_ffn_w2, m_ssm_in_proj, m_ssm_conv_w, m_ssm_conv_b, m_ssm_dt_bias, m_ssm_a_log, m_ssm_d, m_ssm_norm_g, m_ssm_out_proj, m_kv_norm_g, m_w_k, m_k_norm_g, m_w_v, m_sb_w_q, m_sb_q_norm_g, m_sb_w_o, v_meta_tokens, v_norm_g, v_ffn_w1, v_ffn_w3, v_ffn_w2, v_ssm_in_proj, v_ssm_conv_w, v_ssm_conv_b, v_ssm_dt_bias, v_ssm_a_log, v_ssm_d, v_ssm_norm_g, v_ssm_out_proj, v_kv_norm_g, v_w_k, v_k_norm_g, v_w_v, v_sb_w_q, v_sb_q_norm_g, v_sb_w_o):
    shard = dict(meta_tokens=meta_tokens, norm_g=norm_g, ffn_w1=ffn_w1, ffn_w3=ffn_w3, ffn_w2=ffn_w2,
                 ssm_in_proj=ssm_in_proj, ssm_conv_w=ssm_conv_w, ssm_conv_b=ssm_conv_b, ssm_dt_bias=ssm_dt_bias,
                 ssm_a_log=ssm_a_log, ssm_d=ssm_d, ssm_norm_g=ssm_norm_g, ssm_out_proj=ssm_out_proj,
                 kv_norm_g=kv_norm_g, w_k=w_k, k_norm_g=k_norm_g, w_v=w_v, sb_w_q=sb_w_q, sb_q_norm_g=sb_q_norm_g,
                 sb_w_o=sb_w_o)
    mom_m = dict(zip(WEIGHTS, (m_meta_tokens, m_norm_g, m_ffn_w1, m_ffn_w3, m_ffn_w2, m_ssm_in_proj, m_ssm_conv_w,
                               m_ssm_conv_b, m_ssm_dt_bias, m_ssm_a_log, m_ssm_d, m_ssm_norm_g, m_ssm_out_proj,
                               m_kv_norm_g, m_w_k, m_k_norm_g, m_w_v, m_sb_w_q, m_sb_q_norm_g, m_sb_w_o)))
    mom_v = dict(zip(WEIGHTS, (v_meta_tokens, v_norm_g, v_ffn_w1, v_ffn_w3, v_ffn_w2, v_ssm_in_proj, v_ssm_conv_w,
                               v_ssm_conv_b, v_ssm_dt_bias, v_ssm_a_log, v_ssm_d, v_ssm_norm_g, v_ssm_out_proj,
                               v_kv_norm_g, v_w_k, v_k_norm_g, v_w_v, v_sb_w_q, v_sb_q_norm_g, v_sb_w_o)))
    sharded = [n for n in WEIGHTS if SHARD_AXIS[n] is not None]
    replicated = [n for n in WEIGHTS if SHARD_AXIS[n] is None]
    small = [n for n in sharded if n not in MATMUL_WEIGHTS]
    shapes = {n: tuple(shard[n].shape) for n in WEIGHTS}
    D = x.shape[-1]
    bf = lambda t: t.astype(BF16)
    full_rows = lambda t: t.reshape(N_DEV * t.shape[1], t.shape[2])
    cols = lambda l, s: jnp.concatenate([bf(ffn_w1[l, s]), bf(ffn_w3[l, s])], axis=0)
    xq, recv = _Queue(), {}
    W = _Weights(xq)
    W.update({n: shard[n] for n in replicated})

    def have_w13(l, s):
        def done(res):
            t = _interleave(res[0], name=f"weights_w13_{l}{s}")
            W[("w13", l, s)] = (t[:D], t[D:])
        return done

    def have_w2(l, s):
        return lambda res: W.__setitem__(("w2", l, s), full_rows(res[0]))

    def have_win(res):
        W["ssm_in_proj"] = _interleave(res[0], name="weights_win")[None]

    def have_rows(res):
        W["ssm_out_proj"] = full_rows(res[0])[None]
        W["w_k"], W["w_v"] = full_rows(res[1]), full_rows(res[2])
        W["sb_w_q"], W["sb_w_o"] = full_rows(res[3])[None], full_rows(res[4])[None]

    first = _exchange([cols(0, 0), bf(ffn_w2[0, 0]), _pack([shard[n] for n in small], F32)], scatter=False,
                      name="gather_first")
    have_w13(0, 0)(first[:1])
    have_w2(0, 0)(first[1:2])
    W.update(_unpack_gathered(first[2], small, shapes, F32))
    xq.push([bf(ssm_in_proj[0])], False, have_win)
    xq.push([bf(ffn_w2[0, 1])], False, have_w2(0, 1))
    xq.push([cols(0, 1)], False, have_w13(0, 1))
    xq.push([bf(ssm_out_proj[0]), bf(w_k), bf(w_v), bf(sb_w_q[0]), bf(sb_w_o[0])], False, have_rows)
    xq.push([cols(1, 0)], False, have_w13(1, 0))
    xq.push([bf(ffn_w2[1, 0])], False, have_w2(1, 0))
    xq.push([cols(1, 1)], False, have_w13(1, 1))
    xq.push([bf(ffn_w2[1, 1])], False, have_w2(1, 1))

    loss, dx, G = _local_step(x[0], loss_target[0], W, xq, recv)
    loss = lax.psum(loss, ("x", "y", "c"))

    send = jnp.concatenate([_to_shards(G[n], SHARD_AXIS[n]) for n in small], axis=1)
    n_el = send.shape[1]
    npad = -(-n_el // (LANES * PACK_ROWS)) * (LANES * PACK_ROWS)
    send = jnp.pad(send, ((0, 0), (0, npad - n_el))).reshape(N_DEV, npad // LANES, LANES)
    xq.push([send], True, lambda res: recv.__setitem__("small", res))
    xq.flush("scatter_last")
    rep = _pack([G[n] for n in replicated], F32)
    rep_sum = _sum_rows(_exchange([rep], scatter=False, name="gather_small_grads")[0], name="sum_small_grads").reshape(-1)
    summed = {k: [_sum_rows(t, name=f"sum_{'_'.join(map(str, k)) if isinstance(k, tuple) else k}_{i}")
                  for i, t in enumerate(v)] for k, v in recv.items()}

    grads, off = {}, 0
    small_sum = summed["small"][0].reshape(-1)
    for n in small:
        size = math.prod(shapes[n])
        grads[n] = small_sum[off:off + size].reshape(shapes[n])
        off += size
    off = 0
    for n in replicated:
        size = math.prod(shapes[n])
        grads[n] = rep_sum[off:off + size].reshape(shapes[n])
        off += size
    ls = [(l, s) for l in range(2) for s in range(2)]
    grads["ffn_w1"] = jnp.stack([summed[("w13", l, s)][0] for l, s in ls]).reshape(shapes["ffn_w1"])
    grads["ffn_w3"] = jnp.stack([summed[("w13", l, s)][1] for l, s in ls]).reshape(shapes["ffn_w3"])
    grads["ffn_w2"] = jnp.stack([summed[("w2", l, s)][0] for l, s in ls]).reshape(shapes["ffn_w2"])
    grads["ssm_in_proj"] = summed["win"][0][None]
    grads["ssm_out_proj"] = summed["wout"][0][None]
    grads["w_k"], grads["w_v"] = summed["kv"]
    grads["sb_w_q"], grads["sb_w_o"] = summed["sb"][0][None], summed["sb"][1][None]

    delta, new_m, new_v = {}, {}, {}
    for n in WEIGHTS:
        w2 = shard[n].reshape(1, -1) if shard[n].ndim == 1 else shard[n]
        r2 = lambda t: t.reshape(w2.shape)
        d, nm, nv = _adamw(w2, r2(grads[n]), r2(mom_m[n]), r2(mom_v[n]), name=f"adamw_{n}")
        delta[n], new_m[n], new_v[n] = (t.reshape(shapes[n]) for t in (d, nm, nv))

    return (loss, dx[None], *[grads[n] for n in WEIGHTS], *[delta[n] for n in WEIGHTS],
            *[new_m[n] for n in WEIGHTS], *[new_v[n] for n in WEIGHTS])
```

```python
import functools
import math

import jax
import jax.numpy as jnp
from jax import lax
from jax.experimental import pallas as pl
from jax.experimental.pallas import tpu as pltpu

F32 = jnp.float32
BF16 = jnp.bfloat16
RMS_EPS = 1e-6
N_META = 16
HEAD_DIM = 64
SSM_GROUPS = 8
D_STATE = 128
D_CONV = 4
FFN_RES = 0.5
ADAM_LR, ADAM_B1, ADAM_B2, ADAM_EPS, ADAM_WD, ADAM_STEP = 0.001, 0.9, 0.999, 1e-08, 0.01, 10
N_DEV = 8
SSD_CHUNK = 128
ATT_BLOCK = 256
ROW_ALIGN = 768
VMEM_LIMIT_V7X = 48 * 1024 * 1024
MATMUL_VMEM_BUDGET_V7X = 30 * 1024 * 1024
MESH = pl.DeviceIdType.MESH
LANES = 128


def _cp(*sem):
    return pltpu.CompilerParams(dimension_semantics=sem if sem else None, vmem_limit_bytes=VMEM_LIMIT_V7X)


def _tile(n, cands):
    for c in cands:
        if n % c == 0:
            return c
    return n


def _softplus(x):
    return jnp.maximum(x, 0.0) + jnp.log(1.0 + jnp.exp(-jnp.abs(x)))


def _sigmoid(x):
    return 1.0 / (1.0 + jnp.exp(-x))


def _split3(x):
    hi = x.astype(BF16)
    r1 = x - hi.astype(F32)
    mid = r1.astype(BF16)
    lo = (r1 - mid.astype(F32)).astype(BF16)
    return hi, mid, lo


def _dot(a, b):
    return jnp.dot(a, b, preferred_element_type=F32)


def _dot_nt(a, b):
    return lax.dot_general(a, b, (((1,), (1,)), ((), ())), preferred_element_type=F32)


def _dot_tn(a, b):
    return lax.dot_general(a, b, (((0,), (0,)), ((), ())), preferred_element_type=F32)


def _dot3_left(t_bf16, x):
    hi, mid, lo = _split3(x)
    return _dot(t_bf16, hi) + _dot(t_bf16, mid) + _dot(t_bf16, lo)


def _dot3_right(x, t_bf16):
    hi, mid, lo = _split3(x)
    return _dot(hi, t_bf16) + _dot(mid, t_bf16) + _dot(lo, t_bf16)


CARRIER_MIN_FLOP = 4e10


EXCHANGE_US_PER_MB = 94.0
MATMUL_TFLOPS = 650.0


class _Carry:
    def __init__(self, arrs, scatter, done):
        self.arrs, self.scatter, self.done = list(arrs), scatter, done
        per_peer = sum(math.prod(a.shape[1:] if scatter else a.shape) * a.dtype.itemsize for a in self.arrs)
        self.us = EXCHANGE_US_PER_MB * per_peer / 2 ** 20


class _Queue:
    def __init__(self):
        self.items = []

    def push(self, arrs, scatter, done):
        self.items.append(_Carry(arrs, scatter, done))

    def skip(self):
        self.items.append(None)

    def pop(self, kernel_us):
        if self.items and self.items[0] is None:
            self.items.pop(0)
            return None
        for k, it in enumerate(self.items):
            if it is None:
                break
            if not it.scatter or it.us <= 1.15 * kernel_us:
                return self.items.pop(k)
            if k == 0 and len(self.items) > 4:
                return self.items.pop(0)
        return None

    def flush(self, name):
        for k, it in enumerate([it for it in self.items if it is not None]):
            it.done(_exchange(it.arrs, scatter=it.scatter, name=f"{name}_{k}"))
        self.items = []


class _Weights(dict):
    def __init__(self, xq):
        super().__init__()
        self.xq, self.fetched = xq, 0

    def __missing__(self, key):
        while not dict.__contains__(self, key) and self.xq.items:
            it = self.xq.items.pop(0)
            if it is not None:
                it.done(_exchange(it.arrs, scatter=it.scatter, name=f"gather_now_{self.fetched}"))
                self.fetched += 1
        return dict.__getitem__(self, key)


ELEMENTWISE_US_PER_MB = 1.6


def _pop_for_rows(xq, rows, cols, us_per_mb=ELEMENTWISE_US_PER_MB):
    return xq.pop(us_per_mb * rows * cols * 4 / 2 ** 20) if xq is not None else None


def _xchg_shapes(arrs, scatter):
    return [jax.ShapeDtypeStruct((N_DEV,) + tuple(a.shape[1:] if scatter else a.shape), a.dtype) for a in arrs]


def _xchg_scratch(n):
    return [pltpu.SemaphoreType.DMA((n, N_DEV - 1)), pltpu.SemaphoreType.DMA((n, N_DEV - 1)),
            pltpu.SemaphoreType.DMA((n,))]


def _xchg_copies(srcs, dsts, send_sems, recv_sems, local_sems, scatter, with_recv):
    x, y, c, me = _mesh_pos()
    own, sends, recvs = [], [], []
    for a, (s, d) in enumerate(zip(srcs, dsts)):
        own.append(pltpu.make_async_copy(s.at[me] if scatter else s, d.at[me], local_sems.at[a]))
        for f in range(1, N_DEV):
            peer, pid = _peer(x, y, c, f)
            for row, lst in ((me, sends), (pid, recvs)) if with_recv else ((me, sends),):
                lst.append(pltpu.make_async_remote_copy(
                    src_ref=s.at[pid] if scatter else s, dst_ref=d.at[row], send_sem=send_sems.at[a, f - 1],
                    recv_sem=recv_sems.at[a, f - 1], device_id=peer, device_id_type=MESH))
    return own, sends, recvs


def _xchg_start(*a):
    own, sends, _ = _xchg_copies(*a, with_recv=False)
    for cp in own + sends:
        cp.start()


def _xchg_wait(*a):
    own, sends, recvs = _xchg_copies(*a, with_recv=True)
    for snd, rcv in zip(sends, recvs):
        snd.wait_send()
        rcv.wait_recv()
    for cp in own:
        cp.wait()


def _call(body, *, name, grid, in_specs, out_specs, out_shape, scratch_shapes=(), sem, args, carry=None):
    n_in, n_out, n_scr = len(in_specs), len(out_specs), len(scratch_shapes)
    if carry is None:
        return pl.pallas_call(
            body, name=name, grid=grid, in_specs=list(in_specs), out_specs=list(out_specs), out_shape=list(out_shape),
            scratch_shapes=list(scratch_shapes), compiler_params=_cp(*sem))(*args)
    n = len(carry.arrs)
    hbm = pl.BlockSpec(memory_space=pl.ANY)

    def wrapped(*refs):
        ins, csrc = refs[:n_in], refs[n_in:n_in + n]
        outs, cdst = refs[n_in + n:n_in + n + n_out], refs[n_in + n + n_out:n_in + 2 * n + n_out]
        scr = refs[n_in + 2 * n + n_out:]
        xa = (csrc, cdst, *scr[n_scr:], carry.scatter)
        pid = [pl.program_id(a) for a in range(len(grid))]
        first = functools.reduce(jnp.logical_and, [p == 0 for p in pid])
        last = functools.reduce(jnp.logical_and, [p == g - 1 for p, g in zip(pid, grid)])

        @pl.when(first)
        def _():
            _xchg_start(*xa)

        body(*ins, *outs, *scr[:n_scr])

        @pl.when(last)
        def _():
            _xchg_wait(*xa)

    res = pl.pallas_call(
        wrapped, name=name, grid=grid, in_specs=list(in_specs) + [hbm] * n, out_specs=list(out_specs) + [hbm] * n,
        out_shape=list(out_shape) + _xchg_shapes(carry.arrs, carry.scatter),
        scratch_shapes=list(scratch_shapes) + _xchg_scratch(n), compiler_params=_cp(*["arbitrary"] * len(grid)),
    )(*args, *carry.arrs)
    carry.done(list(res[n_out:]))
    return list(res[:n_out])


def _matmul(pairs, *, name, out_dtype=F32, trans_a=False, trans_b=False, b_off=0, res=None, alpha=1.0, shards=None,
            xq=None, tm=None, tn=None, tk=None):
    a0, b0 = pairs[0]
    ha, hb = a0.ndim == 3, b0.ndim == 3
    shape2 = lambda t: (t.shape[1], t.shape[0] * t.shape[2]) if t.ndim == 3 else t.shape
    if trans_a:
        K, M = shape2(a0)
    else:
        M, K = shape2(a0)
    N = b0.shape[0] if trans_b else shape2(b0)[1]
    npair = len(pairs)
    has_res = res is not None
    tm = tm or (M if (ha and trans_a) else _tile(M, (768, 512, 1408, 384, 256, 128)))
    tk = tk or (K if (ha and not trans_a) else _tile(K, (1024, 1408, 768, 512, 256, 128)))
    if tn is None:
        sa, sb, so = a0.dtype.itemsize, b0.dtype.itemsize, jnp.dtype(out_dtype).itemsize
        for tn in ([N] if (shards or hb) else
                   [c for c in (2048, 1536, 1408, 1024, 512, 384, 256, 128) if N % c == 0] or [N]):
            if (2 * npair * tk * (tm * sa + tn * sb) + tm * tn * (4 + 2 * so + (8 if has_res else 0))
                    <= MATMUL_VMEM_BUDGET_V7X):
                break
    nk = K // tk
    cs = N // shards if shards else None
    n_in = 2 * npair + (1 if has_res else 0)
    a_blk = (tk, tm) if trans_a else (tm, tk)

    def body(*refs):
        o_ref, acc, flat = refs[n_in], refs[n_in + 1], list(refs[n_in + 2:])
        k = pl.program_id(2)

        @pl.when(k == 0)
        def _():
            acc[...] = jnp.zeros_like(acc)

        def operand(ref, heads):
            if not heads:
                return ref[...].astype(BF16)
            buf = flat.pop(0)
            for h in range(ref.shape[0]):
                buf[:, h * HEAD_DIM:(h + 1) * HEAD_DIM] = ref[h].astype(BF16)
            return buf[...]

        part = None
        for p in range(npair):
            a = operand(refs[2 * p], ha)
            b = operand(refs[2 * p + 1], hb)
            d = _dot_tn(a, b) if trans_a else _dot_nt(a, b) if trans_b else _dot(a, b)
            part = d if part is None else part + d
        acc[...] += part

        @pl.when(k == nk - 1)
        def _():
            if shards:
                for d in range(shards):
                    v = acc[:, d * cs:(d + 1) * cs]
                    o_ref[d] = (v * alpha if alpha != 1.0 else v).astype(out_dtype)
                return
            v = acc[...]
            if alpha != 1.0:
                v = v * alpha
            if has_res:
                v = refs[2 * npair][...] + v
            o_ref[...] = v.astype(out_dtype)

    if ha:
        assert (tm == M) if trans_a else (tk == K)
        a_spec = pl.BlockSpec((a0.shape[0], a_blk[0], HEAD_DIM),
                              (lambda i, j, k: (0, k, 0)) if trans_a else (lambda i, j, k: (0, i, 0)))
    elif trans_a:
        a_spec = pl.BlockSpec((tk, tm), lambda i, j, k: (k, i))
    else:
        a_spec = pl.BlockSpec((tm, tk), lambda i, j, k: (i, k))
    if hb:
        assert not trans_b and tn == N
        b_spec = pl.BlockSpec((b0.shape[0], tk, HEAD_DIM), lambda i, j, k: (0, k, 0))
    elif trans_b:
        assert not trans_a and b_off % tk == 0
        b_spec = pl.BlockSpec((tn, tk), lambda i, j, k: (j, k + b_off // tk))
    else:
        b_spec = pl.BlockSpec((tk, tn), lambda i, j, k: (k, j))
    flat_scratch = [pltpu.VMEM(a_blk, BF16)] * (npair if ha else 0) + [pltpu.VMEM((tk, tn), BF16)] * (npair if hb else 0)
    if shards:
        assert not has_res and tn == N
        o_spec = pl.BlockSpec((shards, tm, cs), lambda i, j, k: (0, i, 0))
        out_shape = jax.ShapeDtypeStruct((shards, M, cs), out_dtype)
    else:
        o_spec = pl.BlockSpec((tm, tn), lambda i, j, k: (i, j))
        out_shape = jax.ShapeDtypeStruct((M, N), out_dtype)
    in_specs, args = [], []
    for a, b in pairs:
        in_specs += [a_spec, b_spec]
        args += [a, b]
    if has_res:
        in_specs.append(o_spec)
        args.append(res)
    flop = 2.0 * npair * M * N * K
    carry = xq.pop(flop / MATMUL_TFLOPS / 1e6) if (xq is not None and flop >= CARRIER_MIN_FLOP) else None
    return _call(body, name=name, grid=(M // tm, N // tn, nk), in_specs=in_specs, out_specs=[o_spec],
                 out_shape=[out_shape], scratch_shapes=[pltpu.VMEM((tm, tn), F32)] + flat_scratch,
                 sem=("parallel", "parallel", "arbitrary"), args=args, carry=carry)[0]


def _rms_fwd(h, g, *, name, scale=1.0):
    R, D = h.shape
    tr = _tile(R, (2048, 1024, 768, 512, 256, 128)) if D <= 128 else _tile(R, (384, 256, 128))

    def body(h_ref, g_ref, o_ref):
        x = h_ref[...]
        r = lax.rsqrt(jnp.mean(x * x, axis=1, keepdims=True) + RMS_EPS)
        y = x * r * g_ref[...]
        if scale != 1.0:
            y = y * scale
        o_ref[...] = y.astype(BF16)

    return pl.pallas_call(
        body, name=name, grid=(R // tr,),
        in_specs=[pl.BlockSpec((tr, D), lambda i: (i, 0)), pl.BlockSpec((1, D), lambda i: (0, 0))],
        out_specs=pl.BlockSpec((tr, D), lambda i: (i, 0)),
        out_shape=jax.ShapeDtypeStruct((R, D), BF16), compiler_params=_cp("parallel"),
    )(h, g.reshape(1, D))


def _rms_bwd(h, g, dn, res=None, *, name, alpha=1.0):
    R, D = h.shape
    tr = _tile(R, (2048, 1024, 768, 512, 256, 128)) if D <= 128 else _tile(R, (384, 256, 128))
    has_res = res is not None

    def body(*refs):
        h_ref, g_ref, dn_ref = refs[:3]
        dh_ref, dg_ref = refs[-2], refs[-1]
        i = pl.program_id(0)

        @pl.when(i == 0)
        def _():
            dg_ref[...] = jnp.zeros_like(dg_ref)

        x = h_ref[...]
        r = lax.rsqrt(jnp.mean(x * x, axis=1, keepdims=True) + RMS_EPS)
        xh = x * r
        d = dn_ref[...].astype(F32)
        if alpha != 1.0:
            d = d * alpha
        dng = d * g_ref[...]
        m = jnp.mean(dng * xh, axis=1, keepdims=True)
        dh = r * (dng - xh * m)
        if has_res:
            dh = dh + refs[3][...]
        dh_ref[...] = dh
        dg_ref[...] += jnp.sum(d * xh, axis=0, keepdims=True)

    row = pl.BlockSpec((tr, D), lambda i: (i, 0))
    vec = pl.BlockSpec((1, D), lambda i: (0, 0))
    in_specs = [row, vec, row] + ([row] if has_res else [])
    args = [h, g.reshape(1, D), dn] + ([res] if has_res else [])
    return pl.pallas_call(
        body, name=name, grid=(R // tr,), in_specs=in_specs, out_specs=[row, vec],
        out_shape=[jax.ShapeDtypeStruct((R, D), F32), jax.ShapeDtypeStruct((1, D), F32)],
        compiler_params=_cp("arbitrary"),
    )(*args)


def _ffn_up(n, w1, w3, *, name, xq=None):
    M, K = n.shape
    N = w1.shape[1]
    tm = _tile(M, (384, 256, 128))
    tn = _tile(N, (1408, 512, 256, 128))

    def body(n_ref, w1_ref, w3_ref, da_ref, db_ref, s_ref):
        x = n_ref[...]
        a = _dot(x, w1_ref[...])
        b = _dot(x, w3_ref[...])
        sg = _sigmoid(a)
        silu = a * sg
        da_ref[...] = (b * sg * (1.0 + a * (1.0 - sg))).astype(BF16)
        db_ref[...] = silu.astype(BF16)
        s_ref[...] = (silu * b).astype(BF16)

    o_spec = pl.BlockSpec((tm, tn), lambda j, i: (i, j))
    w_spec = pl.BlockSpec((K, tn), lambda j, i: (0, j))
    sh = jax.ShapeDtypeStruct((M, N), BF16)
    return _call(body, name=name, grid=(N // tn, M // tm),
                 in_specs=[pl.BlockSpec((tm, K), lambda j, i: (i, 0)), w_spec, w_spec],
                 out_specs=[o_spec, o_spec, o_spec], out_shape=[sh, sh, sh], sem=("parallel", "parallel"),
                 args=(n, w1, w3), carry=xq.pop(4.0 * M * N * K / MATMUL_TFLOPS / 1e6) if xq is not None else None)


def _ffn_mid_bwd(dh, w2, s_a, s_b, *, name, xq=None):
    M, K = dh.shape
    N = w2.shape[0]
    tm = _tile(M, (384, 256, 128))
    tn = _tile(N, (1408, 512, 256, 128))

    def body(dh_ref, w_ref, a_ref, b_ref, da_ref, db_ref):
        ds = _dot_nt(dh_ref[...].astype(BF16), w_ref[...]) * FFN_RES
        da_ref[...] = (ds * a_ref[...].astype(F32)).astype(BF16)
        db_ref[...] = (ds * b_ref[...].astype(F32)).astype(BF16)

    o_spec = pl.BlockSpec((tm, tn), lambda j, i: (i, j))
    sh = jax.ShapeDtypeStruct((M, N), BF16)
    return _call(body, name=name, grid=(N // tn, M // tm),
                 in_specs=[pl.BlockSpec((tm, K), lambda j, i: (i, 0)), pl.BlockSpec((tn, K), lambda j, i: (j, 0)),
                           o_spec, o_spec],
                 out_specs=[o_spec, o_spec], out_shape=[sh, sh], sem=("parallel", "parallel"),
                 args=(dh, w2, s_a, s_b), carry=xq.pop(2.0 * M * N * K / MATMUL_TFLOPS / 1e6) if xq is not None else None)


def _conv_pre(xx, w_ref, b_ref, tr):
    acc = None
    for k in range(D_CONV):
        sh = D_CONV - 1 - k
        v = (pltpu.roll(xx, sh, 0) if sh else xx)[8:8 + tr]
        t = w_ref[k:k + 1, :] * v
        acc = t if acc is None else acc + t
    return acc + b_ref[...]


def _conv_fwd(zx, w, b, col_off, *, name, xq=None):
    LP = zx.shape[0]
    C = w.shape[1]
    tr = _tile(LP, (256, 128))
    tc = _tile(C, (512, 256, 128))
    co = col_off // tc

    def body(cur_ref, prev_ref, w_ref, b_ref, o_ref):
        i = pl.program_id(0)
        prev = jnp.where(i == 0, 0.0, prev_ref[...])
        pre = _conv_pre(jnp.concatenate([prev, cur_ref[...]], axis=0), w_ref, b_ref, tr)
        o_ref[...] = pre * _sigmoid(pre)

    return _call(
        body, name=name, grid=(LP // tr, C // tc),
        in_specs=[pl.BlockSpec((tr, tc), lambda i, j: (i, j + co)),
                  pl.BlockSpec((8, tc), lambda i, j: (jnp.maximum(i * (tr // 8) - 1, 0), j + co)),
                  pl.BlockSpec((D_CONV, tc), lambda i, j: (0, j)), pl.BlockSpec((1, tc), lambda i, j: (0, j))],
        out_specs=[pl.BlockSpec((tr, tc), lambda i, j: (i, j))],
        out_shape=[jax.ShapeDtypeStruct((LP, C), F32)], sem=("parallel", "parallel"),
        args=(zx, zx, w, b.reshape(1, C)), carry=_pop_for_rows(xq, LP, C))[0]


def _conv_bwd_g(zx, w, b, dact, col_off, *, name, xq=None):
    LP = zx.shape[0]
    C = w.shape[1]
    tr = _tile(LP, (256, 128))
    tc = _tile(C, (512, 256, 128))
    co = col_off // tc

    def body(cur_ref, prev_ref, w_ref, b_ref, d_ref, g_ref, dw_ref, db_ref):
        i = pl.program_id(1)

        @pl.when(i == 0)
        def _():
            dw_ref[...] = jnp.zeros_like(dw_ref)
            db_ref[...] = jnp.zeros_like(db_ref)

        prev = jnp.where(i == 0, 0.0, prev_ref[...])
        xx = jnp.concatenate([prev, cur_ref[...]], axis=0)
        pre = _conv_pre(xx, w_ref, b_ref, tr)
        sg = _sigmoid(pre)
        g = d_ref[...] * sg * (1.0 + pre * (1.0 - sg))
        g_ref[...] = g
        db_ref[...] += jnp.sum(g, axis=0, keepdims=True)
        rows = []
        for k in range(D_CONV):
            sh = D_CONV - 1 - k
            v = (pltpu.roll(xx, sh, 0) if sh else xx)[8:8 + tr]
            rows.append(jnp.sum(g * v, axis=0, keepdims=True))
        rows.append(jnp.zeros((8 - D_CONV, tc), F32))
        dw_ref[...] += jnp.concatenate(rows, axis=0)

    return _call(
        body, name=name, grid=(C // tc, LP // tr),
        in_specs=[pl.BlockSpec((tr, tc), lambda j, i: (i, j + co)),
                  pl.BlockSpec((8, tc), lambda j, i: (jnp.maximum(i * (tr // 8) - 1, 0), j + co)),
                  pl.BlockSpec((D_CONV, tc), lambda j, i: (0, j)), pl.BlockSpec((1, tc), lambda j, i: (0, j)),
                  pl.BlockSpec((tr, tc), lambda j, i: (i, j))],
        out_specs=[pl.BlockSpec((tr, tc), lambda j, i: (i, j)), pl.BlockSpec((8, tc), lambda j, i: (0, j)),
                   pl.BlockSpec((1, tc), lambda j, i: (0, j))],
        out_shape=[jax.ShapeDtypeStruct((LP, C), F32), jax.ShapeDtypeStruct((8, C), F32), jax.ShapeDtypeStruct((1, C), F32)],
        sem=("parallel", "arbitrary"), args=(zx, zx, w, b.reshape(1, C), dact), carry=_pop_for_rows(xq, LP, C, 2.0))


def _conv_bwd_u(g, w, *, name, xq=None):
    LP, C = g.shape
    tr = _tile(LP, (256, 128))
    tc = _tile(C, (512, 256, 128))
    nb = LP // tr

    def body(cur_ref, nxt_ref, w_ref, o_ref):
        i = pl.program_id(0)
        nxt = jnp.where(i == nb - 1, 0.0, nxt_ref[...])
        xx = jnp.concatenate([cur_ref[...], nxt], axis=0)
        acc = None
        for k in range(D_CONV):
            sh = D_CONV - 1 - k
            v = (pltpu.roll(xx, tr + 8 - sh, 0) if sh else xx)[:tr]
            t = w_ref[k:k + 1, :] * v
            acc = t if acc is None else acc + t
        o_ref[...] = acc

    return _call(
        body, name=name, grid=(nb, C // tc),
        in_specs=[pl.BlockSpec((tr, tc), lambda i, j: (i, j)),
                  pl.BlockSpec((8, tc), lambda i, j: (jnp.minimum((i + 1) * (tr // 8), LP // 8 - 1), j)),
                  pl.BlockSpec((D_CONV, tc), lambda i, j: (0, j))],
        out_specs=[pl.BlockSpec((tr, tc), lambda i, j: (i, j))],
        out_shape=[jax.ShapeDtypeStruct((LP, C), F32)], sem=("parallel", "parallel"),
        args=(g, g, w), carry=_pop_for_rows(xq, LP, C))[0]


def _ssd_prelude(dtr_ref, dtrt_ref, brow_ref, bcol_ref, alrow_ref, alcol_ref, Q):
    ii = lax.broadcasted_iota(jnp.int32, (Q, Q), 0)
    jj = lax.broadcasted_iota(jnp.int32, (Q, Q), 1)
    tril = ii >= jj
    dt_col = _softplus(dtr_ref[...] + brow_ref[...])
    a_row_p = -jnp.exp(alrow_ref[...])
    dt_row = _softplus(dtrt_ref[...] + bcol_ref[...])
    a_col_p = -jnp.exp(alcol_ref[...])
    cum_col = _dot3_left(tril.astype(BF16), dt_col * a_row_p)
    cum_row = _dot3_right(dt_row * a_col_p, (ii <= jj).astype(BF16))
    return ii, jj, tril, dt_col, dt_row, a_row_p, cum_col, cum_row


def _col_of(mat, lane_idx, h):
    return jnp.sum(jnp.where(lane_idx == h, mat, 0.0), axis=1, keepdims=True)


def _ssd_fwd(xbc, dtr, dtrt, brow, bcol, alrow, alcol, dvec, *, name, xq=None):
    LP = xbc.shape[0]
    Q = SSD_CHUNK
    nc = LP // Q
    G = SSM_GROUPS
    DI = dvec.shape[1]
    gw = DI // G
    hpg = gw // HEAD_DIM
    H = G * hpg
    boff, coff = DI, DI + G * D_STATE

    def body(xbc_ref, dtr_ref, dtrt_ref, brow_ref, bcol_ref, alrow_ref, alcol_ref, dvec_ref, y_ref, st_ref, state):
        c = pl.program_id(0)

        @pl.when(c == 0)
        def _():
            state[...] = jnp.zeros_like(state)

        st_ref[...] = state[...]
        ii, jj, tril, dt_col, dt_row, _, cum_col, cum_row = _ssd_prelude(
            dtr_ref, dtrt_ref, brow_ref, bcol_ref, alrow_ref, alcol_ref, Q)
        lane_h = lax.broadcasted_iota(jnp.int32, (Q, 128), 1)
        lane_g = lax.broadcasted_iota(jnp.int32, (Q, gw), 1) // HEAD_DIM
        for g in range(G):
            xg = xbc_ref[:, g * gw:(g + 1) * gw]
            bb = xbc_ref[:, boff + g * D_STATE: boff + (g + 1) * D_STATE].astype(BF16)
            cb = xbc_ref[:, coff + g * D_STATE: coff + (g + 1) * D_STATE].astype(BF16)
            gm = _dot_nt(cb, bb)
            sg = state[g]
            yoff = _dot(cb, sg.astype(BF16))
            ydiag = jnp.zeros((Q, gw), F32)
            esc = jnp.zeros((Q, gw), F32)
            wsc = jnp.zeros((Q, gw), F32)
            lam = jnp.zeros((1, gw), F32)
            for j in range(hpg):
                h = g * hpg + j
                ccol = _col_of(cum_col, lane_h, h)
                dcol = _col_of(dt_col, lane_h, h)
                seg = ccol - cum_row[h:h + 1, :]
                decay = jnp.exp(jnp.where(tril, seg, -jnp.inf))
                mh = gm * decay * dt_row[h:h + 1, :]
                hm = lane_g == j
                ydiag = ydiag + _dot(mh.astype(BF16), jnp.where(hm, xg, 0.0).astype(BF16))
                tot = ccol[Q - 1:Q, :]
                esc = jnp.where(hm, jnp.exp(ccol), esc)
                wsc = jnp.where(hm, jnp.exp(tot - ccol) * dcol, wsc)
                lam = jnp.where(hm[0:1], jnp.exp(tot), lam)
            y_ref[:, g * gw:(g + 1) * gw] = ydiag + yoff * esc + dvec_ref[:, g * gw:(g + 1) * gw] * xg
            state[g] = sg * lam + _dot_tn(bb, (xg * wsc).astype(BF16))

    W = xbc.shape[1]
    full = lambda shape: pl.BlockSpec(shape, lambda c: (0,) * len(shape))
    return _call(
        body, name=name, grid=(nc,),
        in_specs=[pl.BlockSpec((Q, W), lambda c: (c, 0)), pl.BlockSpec((Q, 128), lambda c: (c, 0)),
                  pl.BlockSpec((H, Q), lambda c: (0, c)), full((1, 128)), full((H, 1)), full((1, 128)), full((H, 1)),
                  full((1, DI))],
        out_specs=[pl.BlockSpec((Q, DI), lambda c: (c, 0)), pl.BlockSpec((None, G, D_STATE, gw), lambda c: (c, 0, 0, 0))],
        out_shape=[jax.ShapeDtypeStruct((LP, DI), F32), jax.ShapeDtypeStruct((nc, G, D_STATE, gw), F32)],
        scratch_shapes=[pltpu.VMEM((G, D_STATE, gw), F32)], sem=("arbitrary",),
        args=(xbc, dtr, dtrt, brow, bcol, alrow, alcol, dvec), carry=_pop_for_rows(xq, LP, W, 1.25))


def _ssd_bwd(xbc, dtr, dtrt, brow, bcol, alrow, alcol, dvec, dy, states, *, name, xq=None):
    LP = xbc.shape[0]
    Q = SSD_CHUNK
    nc = LP // Q
    G = SSM_GROUPS
    DI = dvec.shape[1]
    gw = DI // G
    hpg = gw // HEAD_DIM
    H = G * hpg
    boff, coff = DI, DI + G * D_STATE
    W = xbc.shape[1]

    def body(xbc_ref, dtr_ref, dtrt_ref, brow_ref, bcol_ref, alrow_ref, alcol_ref, dvec_ref, dy_ref, st_ref,
             dxbc_ref, ddtr_ref, dbias_ref, dalog_ref, ddvec_ref, dstate):
        c = pl.program_id(0)

        @pl.when(c == 0)
        def _():
            dstate[...] = jnp.zeros_like(dstate)
            dbias_ref[...] = jnp.zeros_like(dbias_ref)
            dalog_ref[...] = jnp.zeros_like(dalog_ref)
            ddvec_ref[...] = jnp.zeros_like(ddvec_ref)

        ii, jj, tril, dt_col, dt_row, a_row_p, cum_col, cum_row = _ssd_prelude(
            dtr_ref, dtrt_ref, brow_ref, bcol_ref, alrow_ref, alcol_ref, Q)
        eye = ii == jj
        lane_h = lax.broadcasted_iota(jnp.int32, (Q, 128), 1)
        row_h = lax.broadcasted_iota(jnp.int32, (Q, 128), 0)
        lane_g = lax.broadcasted_iota(jnp.int32, (Q, gw), 1) // HEAD_DIM
        lane_s = lax.broadcasted_iota(jnp.int32, (D_STATE, gw), 1) // HEAD_DIM
        dcum_mat = jnp.zeros((Q, 128), F32)
        ddt_mat = jnp.zeros((Q, 128), F32)
        dtot_row = jnp.zeros((1, 128), F32)
        for g in range(G):
            xg = xbc_ref[:, g * gw:(g + 1) * gw]
            dyg = dy_ref[:, g * gw:(g + 1) * gw]
            bb = xbc_ref[:, boff + g * D_STATE: boff + (g + 1) * D_STATE].astype(BF16)
            cb = xbc_ref[:, coff + g * D_STATE: coff + (g + 1) * D_STATE].astype(BF16)
            sg = st_ref[g]
            dsg = dstate[g]
            sb = sg.astype(BF16)
            dsb = dsg.astype(BF16)
            xb = xg.astype(BF16)
            gm = _dot_nt(cb, bb)
            cs = _dot(cb, sb)
            bds = _dot(bb, dsb)
            dxg = dvec_ref[:, g * gw:(g + 1) * gw] * dyg
            dgm = jnp.zeros((Q, Q), F32)
            esc = jnp.zeros((Q, gw), F32)
            wsc = jnp.zeros((Q, gw), F32)
            lam = jnp.zeros((1, gw), F32)
            dycs = dyg * cs
            xbds = xg * bds
            dss = dsg * sg
            for j in range(hpg):
                h = g * hpg + j
                ccol = _col_of(cum_col, lane_h, h)
                dcol = _col_of(dt_col, lane_h, h)
                drow = dt_row[h:h + 1, :]
                seg = ccol - cum_row[h:h + 1, :]
                decay = jnp.exp(jnp.where(tril, seg, -jnp.inf))
                hm = lane_g == j
                dyh = jnp.where(hm, dyg, 0.0).astype(BF16)
                gl = gm * decay
                mh = gl * drow
                dmf = _dot_nt(dyh, xb)
                dxg = dxg + _dot_tn(mh.astype(BF16), dyh)
                dgm = dgm + dmf * decay * drow
                n_p = dmf * gl
                n_m = n_p * drow
                rowsum_n = jnp.sum(n_m, axis=1, keepdims=True)
                colsum_n = jnp.sum(jnp.where(eye, jnp.sum(n_m, axis=0, keepdims=True), 0.0), axis=1, keepdims=True)
                colsum_np = jnp.sum(jnp.where(eye, jnp.sum(n_p, axis=0, keepdims=True), 0.0), axis=1, keepdims=True)
                tot = ccol[Q - 1:Q, :]
                e = jnp.exp(ccol)
                wexp = jnp.exp(tot - ccol)
                wcol = wexp * dcol
                lamh = jnp.exp(tot)
                yoff_t = jnp.sum(jnp.where(hm, dycs, 0.0), axis=1, keepdims=True) * e
                e_s = jnp.sum(jnp.where(hm, xbds, 0.0), axis=1, keepdims=True)
                ew = e_s * wcol
                dtot = jnp.sum(ew, axis=0, keepdims=True) + lamh * jnp.sum(
                    jnp.sum(jnp.where(lane_s == j, dss, 0.0), axis=1, keepdims=True), axis=0, keepdims=True)
                dcum_h = rowsum_n + yoff_t - colsum_n - ew
                ddt_h = colsum_np + e_s * wexp
                onehot = lane_h == h
                dcum_mat = jnp.where(onehot, dcum_h, dcum_mat)
                ddt_mat = jnp.where(onehot, ddt_h, ddt_mat)
                dtot_row = jnp.where(onehot[0:1], dtot, dtot_row)
                esc = jnp.where(hm, e, esc)
                wsc = jnp.where(hm, wcol, wsc)
                lam = jnp.where(hm[0:1], lamh, lam)
            dgb = dgm.astype(BF16)
            dye = (dyg * esc).astype(BF16)
            xw = (xg * wsc).astype(BF16)
            dxbc_ref[:, g * gw:(g + 1) * gw] = dxg + bds * wsc
            dxbc_ref[:, boff + g * D_STATE: boff + (g + 1) * D_STATE] = _dot_tn(dgb, cb) + _dot_nt(xw, dsb)
            dxbc_ref[:, coff + g * D_STATE: coff + (g + 1) * D_STATE] = _dot(dgb, bb) + _dot_nt(dye, sb)
            dstate[g] = dsg * lam + _dot_tn(cb, dye)
            ddvec_ref[:, g * gw:(g + 1) * gw] += jnp.sum(dyg * xg, axis=0, keepdims=True)
        dcum_mat = dcum_mat + jnp.where(row_h == Q - 1, dtot_row, 0.0)
        da = _dot3_left((ii <= jj).astype(BF16), dcum_mat)
        ddt = ddt_mat + da * a_row_p
        dalog_ref[...] += jnp.sum(da * dt_col, axis=0, keepdims=True) * a_row_p
        ddtr = ddt * _sigmoid(dtr_ref[...] + brow_ref[...])
        ddtr_ref[...] = ddtr
        dbias_ref[...] += jnp.sum(ddtr, axis=0, keepdims=True)

    full = lambda shape: pl.BlockSpec(shape, lambda c: (0,) * len(shape))
    rc = lambda c: nc - 1 - c
    return _call(
        body, name=name, grid=(nc,),
        in_specs=[pl.BlockSpec((Q, W), lambda c: (rc(c), 0)), pl.BlockSpec((Q, 128), lambda c: (rc(c), 0)),
                  pl.BlockSpec((H, Q), lambda c: (0, rc(c))), full((1, 128)), full((H, 1)), full((1, 128)), full((H, 1)),
                  full((1, DI)), pl.BlockSpec((Q, DI), lambda c: (rc(c), 0)),
                  pl.BlockSpec((None, G, D_STATE, gw), lambda c: (rc(c), 0, 0, 0))],
        out_specs=[pl.BlockSpec((Q, W), lambda c: (rc(c), 0)), pl.BlockSpec((Q, 128), lambda c: (rc(c), 0)),
                   full((1, 128)), full((1, 128)), full((1, DI))],
        out_shape=[jax.ShapeDtypeStruct((LP, W), F32), jax.ShapeDtypeStruct((LP, 128), F32),
                   jax.ShapeDtypeStruct((1, 128), F32), jax.ShapeDtypeStruct((1, 128), F32),
                   jax.ShapeDtypeStruct((1, DI), F32)],
        scratch_shapes=[pltpu.VMEM((G, D_STATE, gw), F32)], sem=("arbitrary",),
        args=(xbc, dtr, dtrt, brow, bcol, alrow, alcol, dvec, dy, states), carry=_pop_for_rows(xq, LP, W, 4.5))


def _gate_fwd(y, zx, g, *, name):
    LP, DI = y.shape
    gw = DI // SSM_GROUPS
    tr = _tile(LP, (256, 128))

    def body(y_ref, z_ref, g_ref, o_ref):
        for k in range(SSM_GROUPS):
            sl = slice(k * gw, (k + 1) * gw)
            z = z_ref[:, sl]
            t = y_ref[:, sl] * (z * _sigmoid(z))
            r = lax.rsqrt(jnp.mean(t * t, axis=1, keepdims=True) + RMS_EPS)
            o_ref[:, sl] = (t * r * g_ref[:, sl]).astype(BF16)

    row = pl.BlockSpec((tr, DI), lambda i: (i, 0))
    return pl.pallas_call(
        body, name=name, grid=(LP // tr,), in_specs=[row, row, pl.BlockSpec((1, DI), lambda i: (0, 0))],
        out_specs=row, out_shape=jax.ShapeDtypeStruct((LP, DI), BF16), compiler_params=_cp("parallel"),
    )(y, zx, g)


def _gate_bwd(y, zx, g, dyn, *, name):
    LP, DI = y.shape
    gw = DI // SSM_GROUPS
    tr = _tile(LP, (256, 128))

    def body(y_ref, z_ref, g_ref, d_ref, dy_ref, dz_ref, dg_ref):
        i = pl.program_id(0)

        @pl.when(i == 0)
        def _():
            dg_ref[...] = jnp.zeros_like(dg_ref)

        for k in range(SSM_GROUPS):
            sl = slice(k * gw, (k + 1) * gw)
            z = z_ref[:, sl]
            yv = y_ref[:, sl]
            sg = _sigmoid(z)
            sz = z * sg
            t = yv * sz
            r = lax.rsqrt(jnp.mean(t * t, axis=1, keepdims=True) + RMS_EPS)
            th = t * r
            d = d_ref[:, sl]
            dtn = d * g_ref[:, sl]
            dt_ = r * (dtn - th * jnp.mean(dtn * th, axis=1, keepdims=True))
            dg_ref[:, sl] += jnp.sum(d * th, axis=0, keepdims=True)
            dy_ref[:, sl] = dt_ * sz
            dz_ref[:, sl] = dt_ * yv * sg * (1.0 + z * (1.0 - sg))

    row = pl.BlockSpec((tr, DI), lambda i: (i, 0))
    vec = pl.BlockSpec((1, DI), lambda i: (0, 0))
    return pl.pallas_call(
        body, name=name, grid=(LP // tr,), in_specs=[row, row, vec, row], out_specs=[row, row, vec],
        out_shape=[jax.ShapeDtypeStruct((LP, DI), F32), jax.ShapeDtypeStruct((LP, DI), F32),
                   jax.ShapeDtypeStruct((1, DI), F32)],
        compiler_params=_cp("arbitrary"),
    )(y, zx, g, dyn)


EXP_ZERO = -104.0
LOG2E = 1.4426950408889634


def _dot2_right(x, t2_bf16):
    hi = x.astype(BF16)
    lo = (x - hi.astype(F32)).astype(BF16)
    return _dot(jnp.concatenate([hi, lo], axis=1), t2_bf16)


def _tri2(T, upper):
    r = lax.broadcasted_iota(jnp.int32, (2 * T, T), 0) % T
    c = lax.broadcasted_iota(jnp.int32, (2 * T, T), 1)
    return (r <= c if upper else r >= c).astype(BF16)


def _sb_tile(q, k_blk, lower2, valid=None):
    z = _dot_nt(q, k_blk)
    sp = jnp.maximum(z, 0.0) + jnp.log(1.0 + jnp.exp2(jnp.abs(z) * (-LOG2E)))
    if valid is not None:
        sp = jnp.where(valid, sp, 0.0)
    return z, sp, z - _dot2_right(sp, lower2)


def _sb_weights(zr, c, valid=None):
    w = jnp.exp(zr + c)
    return w if valid is None else jnp.where(valid, w, 0.0)


def _sb_fwd(q, k, v, zmax, *, name):
    H, LP, dh = q.shape
    T = ATT_BLOCK
    nq = LP // T
    assert nq < LANES

    def body(q_ref, k_ref, v_ref, zb_ref, o_ref, c_ref, kf_ref):
        i = pl.program_id(1)
        ii = lax.broadcasted_iota(jnp.int32, (T, T), 0)
        jj = lax.broadcasted_iota(jnp.int32, (T, T), 1)
        lane = lax.broadcasted_iota(jnp.int32, (T, LANES), 1)
        lower2 = _tri2(T, upper=False)
        qv = q_ref[...]
        zb = zb_ref[0:1, 0:1]

        def kv(kb):
            ks = pl.multiple_of(kb * T, T)
            return k_ref[pl.ds(ks, T), :], v_ref[pl.ds(ks, T), :]

        kd, vd = kv(i)
        k1, v1 = kv(jnp.maximum(i - 1, 0))
        diag = jj < ii
        prev = jnp.full((T, T), i > 0)
        _, sp0, zr0 = _sb_tile(qv, kd, lower2, diag)
        _, sp1, zr1 = _sb_tile(qv, k1, lower2, prev)
        c0 = -jnp.sum(sp0, axis=1, keepdims=True)
        acc = (_dot(_sb_weights(zr0, 0.0, diag).astype(BF16), vd)
               + _dot(_sb_weights(zr1, c0, prev).astype(BF16), v1))
        c = c0 - jnp.sum(sp1, axis=1, keepdims=True)
        c_ref[...] = jnp.where(lane == i - 1, c0, 0.0)

        def alive(c):
            return jnp.max(c + zb) > EXP_ZERO

        def cond(carry):
            kb, _, _, live = carry
            return (kb >= 0) & live

        def step(carry):
            kb, c, acc, _ = carry
            kt, vt = kv(kb)
            _, sp, zr = _sb_tile(qv, kt, lower2)
            acc = acc + _dot(_sb_weights(zr, c).astype(BF16), vt)
            c_ref[...] = jnp.where(lane == kb, c, c_ref[...])
            c = c - jnp.sum(sp, axis=1, keepdims=True)
            return kb - 1, c, acc, alive(c)

        kb, _, acc, _ = lax.while_loop(cond, step, (i - 2, c, acc, alive(c)))
        o_ref[...] = acc
        kf_ref[...] = jnp.zeros_like(kf_ref) + (kb + 1).astype(F32)

    blk = pl.BlockSpec((None, T, dh), lambda h, i: (h, i, 0))
    cblk = pl.BlockSpec((None, T, LANES), lambda h, i: (h, i, 0))
    whole = pl.BlockSpec((None, LP, dh), lambda h, i: (h, 0, 0))
    return pl.pallas_call(
        body, name=name, grid=(H, nq), in_specs=[blk, whole, whole, pl.BlockSpec((1, LANES), lambda h, i: (0, 0))],
        out_specs=[blk, cblk, pl.BlockSpec((None, None, 8, LANES), lambda h, i: (h, i, 0, 0))],
        out_shape=[jax.ShapeDtypeStruct((H, LP, dh), F32), jax.ShapeDtypeStruct((H, LP, LANES), F32),
                   jax.ShapeDtypeStruct((H, nq, 8, LANES), F32)],
        compiler_params=_cp("parallel", "parallel"),
    )(q, k, v, zmax)


def _sb_bwd(kstart, q, k, v, cmat, do, *, name):
    H, LP, dh = q.shape
    T = ATT_BLOCK
    nq = LP // T

    def body(ks_ref, q_ref, k_ref, v_ref, c_ref, do_ref, dq_ref, dk_ref, dv_ref):
        h = pl.program_id(0)
        i = pl.program_id(1)

        @pl.when(i == 0)
        def _():
            dk_ref[...] = jnp.zeros_like(dk_ref)
            dv_ref[...] = jnp.zeros_like(dv_ref)

        ii = lax.broadcasted_iota(jnp.int32, (T, T), 0)
        jj = lax.broadcasted_iota(jnp.int32, (T, T), 1)
        lane = lax.broadcasted_iota(jnp.int32, (T, LANES), 1)
        lower2 = _tri2(T, upper=False)
        upper2 = _tri2(T, upper=True)
        qv = q_ref[...]
        dob = do_ref[...].astype(BF16)
        cm = c_ref[...]

        def front(kb, valid=None):
            ks = pl.multiple_of(kb * T, T)
            k_blk = k_ref[pl.ds(ks, T), :]
            c = jnp.sum(jnp.where(lane == kb, cm, 0.0), axis=1, keepdims=True)
            z, sp, zr = _sb_tile(qv, k_blk, lower2, valid)
            w = _sb_weights(zr, c, valid)
            gw_ = w * _dot_nt(dob, v_ref[pl.ds(ks, T), :])
            gin = _dot2_right(gw_, upper2)
            return ks, k_blk, w, gw_, gin, jnp.exp(z - sp)

        def back(t, cg, dq, valid=None):
            ks, k_blk, w, gw_, gin, sig = t
            dz = gw_ - sig * (cg + gin)
            if valid is not None:
                dz = jnp.where(valid, dz, 0.0)
            dz = dz.astype(BF16)
            dk_ref[pl.ds(ks, T), :] += _dot_tn(dz, qv)
            dv_ref[pl.ds(ks, T), :] += _dot_tn(w.astype(BF16), dob)
            return cg + jnp.sum(gw_, axis=1, keepdims=True), dq + _dot(dz, k_blk)

        cg, dq = lax.fori_loop(ks_ref[h, i], i - 1, lambda kb, cr: back(front(kb), *cr),
                               (jnp.zeros((T, 1), F32), jnp.zeros((T, dh), F32)))
        diag = jj < ii
        prev = jnp.full((T, T), i > 0)
        t1 = front(jnp.maximum(i - 1, 0), prev)
        t0 = front(i, diag)
        cg, dq = back(t1, cg, dq, prev)
        _, dq = back(t0, cg, dq, diag)
        dq_ref[...] = dq

    blk = pl.BlockSpec((None, T, dh), lambda h, i, ks: (h, i, 0))
    cblk = pl.BlockSpec((None, T, LANES), lambda h, i, ks: (h, i, 0))
    whole = pl.BlockSpec((None, LP, dh), lambda h, i, ks: (h, 0, 0))
    sh = jax.ShapeDtypeStruct((H, LP, dh), F32)
    return pl.pallas_call(
        body, name=name,
        grid_spec=pltpu.PrefetchScalarGridSpec(
            num_scalar_prefetch=1, grid=(H, nq), in_specs=[blk, whole, whole, cblk, blk], out_specs=[blk, whole, whole]),
        out_shape=[sh, sh, sh], compiler_params=_cp("parallel", "arbitrary"),
    )(kstart, q, k, v, cmat, do)


def _loss_head(h, tgt, seq, *, name):
    LP, D = h.shape
    tr = _tile(LP, (384, 256, 128))

    def body(h_ref, t_ref, dh_ref, l_ref):
        i = pl.program_id(0)

        @pl.when(i == 0)
        def _():
            l_ref[...] = jnp.zeros_like(l_ref)

        row = lax.broadcasted_iota(jnp.int32, (tr, D), 0) + i * tr
        e = jnp.where((row >= N_META) & (row < N_META + seq), h_ref[...] - t_ref[...], 0.0)
        dh_ref[...] = e * (1.0 / D)
        l_ref[...] += jnp.sum(e * e, axis=0, keepdims=True) * (0.5 / D)

    row = pl.BlockSpec((tr, D), lambda i: (i, 0))
    return pl.pallas_call(
        body, name=name, grid=(LP // tr,), in_specs=[row, row], out_specs=[row, pl.BlockSpec((1, D), lambda i: (0, 0))],
        out_shape=[jax.ShapeDtypeStruct((LP, D), F32), jax.ShapeDtypeStruct((1, D), F32)],
        compiler_params=_cp("arbitrary"),
    )(h, tgt)


def _adamw(w, g, m, v, *, name):
    shape = w.shape
    C = shape[-1]
    R = math.prod(shape) // C
    tr = _tile(R, (512, 256, 128, 64, 32, 16, 8))
    c1 = 1.0 / (1.0 - ADAM_B1 ** ADAM_STEP)
    c2 = 1.0 / (1.0 - ADAM_B2 ** ADAM_STEP)

    def body(w_ref, g_ref, m_ref, v_ref, d_ref, nm_ref, nv_ref):
        gv = g_ref[...]
        nm = ADAM_B1 * m_ref[...] + (1.0 - ADAM_B1) * gv
        nv = ADAM_B2 * v_ref[...] + (1.0 - ADAM_B2) * (gv * gv)
        d_ref[...] = -ADAM_LR * ((nm * c1) / (jnp.sqrt(nv * c2) + ADAM_EPS) + ADAM_WD * w_ref[...])
        nm_ref[...] = nm
        nv_ref[...] = nv

    blk = pl.BlockSpec((tr, C), lambda i: (i, 0))
    sh = jax.ShapeDtypeStruct((R, C), F32)
    d, nm, nv = pl.pallas_call(
        body, name=name, grid=(R // tr,), in_specs=[blk] * 4, out_specs=[blk] * 3, out_shape=[sh] * 3,
        compiler_params=_cp("parallel"),
    )(w.reshape(R, C), g.reshape(R, C), m.reshape(R, C), v.reshape(R, C))
    return d.reshape(shape), nm.reshape(shape), nv.reshape(shape)


def _sum_rows(buf, *, name):
    n, R, C = buf.shape
    tr = _tile(R, (512, 256, 128, 64, 32, 16, 8))

    def body(b_ref, o_ref):
        acc = b_ref[0].astype(F32)
        for k in range(1, n):
            acc = acc + b_ref[k].astype(F32)
        o_ref[...] = acc

    return pl.pallas_call(
        body, name=name, grid=(R // tr,), in_specs=[pl.BlockSpec((n, tr, C), lambda i: (0, i, 0))],
        out_specs=pl.BlockSpec((tr, C), lambda i: (i, 0)), out_shape=jax.ShapeDtypeStruct((R, C), F32),
        compiler_params=_cp("parallel"),
    )(buf)


def _interleave(buf, *, name):
    n, R, C = buf.shape
    tr = _tile(R, (256, 128, 64, 32, 16))

    def body(b_ref, o_ref):
        for d in range(n):
            o_ref[:, d * C:(d + 1) * C] = b_ref[d]

    return pl.pallas_call(
        body, name=name, grid=(R // tr,), in_specs=[pl.BlockSpec((n, tr, C), lambda i: (0, i, 0))],
        out_specs=pl.BlockSpec((tr, n * C), lambda i: (i, 0)), out_shape=jax.ShapeDtypeStruct((R, n * C), buf.dtype),
        compiler_params=_cp("parallel"),
    )(buf)


def _deinterleave(x, *, out_dtype, name):
    R, NC = x.shape
    C = NC // N_DEV
    tr = _tile(R, (256, 128, 64, 32, 16))

    def body(x_ref, o_ref):
        for d in range(N_DEV):
            o_ref[d] = x_ref[:, d * C:(d + 1) * C].astype(out_dtype)

    return pl.pallas_call(
        body, name=name, grid=(R // tr,), in_specs=[pl.BlockSpec((tr, NC), lambda i: (i, 0))],
        out_specs=pl.BlockSpec((N_DEV, tr, C), lambda i: (0, i, 0)),
        out_shape=jax.ShapeDtypeStruct((N_DEV, R, C), out_dtype), compiler_params=_cp("parallel"),
    )(x)


def _mesh_pos():
    x, y, c = lax.axis_index("x"), lax.axis_index("y"), lax.axis_index("c")
    return x, y, c, 4 * x + 2 * y + c


def _peer(x, y, c, f):
    px, py, pc = (x + ((f >> 2) & 1)) % 2, (y + ((f >> 1) & 1)) % 2, (c + (f & 1)) % 2
    return (px, py, pc), 4 * px + 2 * py + pc


def _exchange(arrs, *, scatter, name):
    n = len(arrs)
    hbm = pl.BlockSpec(memory_space=pl.ANY)

    def body(*refs):
        xa = (refs[:n], refs[n:2 * n], *refs[2 * n:], scatter)
        _xchg_start(*xa)
        _xchg_wait(*xa)

    return list(pl.pallas_call(
        body, name=name, in_specs=[hbm] * n, out_specs=[hbm] * n, out_shape=_xchg_shapes(arrs, scatter),
        scratch_shapes=_xchg_scratch(n), compiler_params=pltpu.CompilerParams(has_side_effects=True),
    )(*arrs))


def _shard_rows(t):
    return t.reshape(N_DEV, t.shape[0] // N_DEV, t.shape[1])


def _ffn_fwd(h, g, W, l, s, xq):
    tag = f"{l}{s}"
    n = _rms_fwd(h, g, name=f"ffn_norm_{tag}")
    w1, w3 = W[("w13", l, s)]
    a, b, sw = _ffn_up(n, w1, w3, name=f"ffn_up_{tag}", xq=xq)
    h2 = _matmul([(sw, W[("w2", l, s)])], res=h, alpha=FFN_RES, name=f"ffn_down_{tag}", xq=xq)
    return h2, (h, n, a, b, sw)


def _ffn_bwd(dh, saved, g, W, l, s, xq, emit):
    tag = f"{l}{s}"
    h, n, a, b, sw = saved
    (w1, w3), w2 = W[("w13", l, s)], W[("w2", l, s)]
    da, db = _ffn_mid_bwd(dh, w2, a, b, name=f"ffn_mid_bwd_{tag}", xq=xq)
    dw2 = _matmul([(sw, dh)], trans_a=True, alpha=FFN_RES, out_dtype=BF16, name=f"ffn_dw2_{tag}", xq=xq)
    emit(("w2", l, s), [_shard_rows(dw2)])
    dn = _matmul([(da, w1), (db, w3)], trans_b=True, name=f"ffn_dn_{tag}", xq=xq)
    dw1 = _matmul([(n, da)], trans_a=True, shards=N_DEV, out_dtype=BF16, name=f"ffn_dw1_{tag}", xq=xq)
    dw3 = _matmul([(n, db)], trans_a=True, shards=N_DEV, out_dtype=BF16, name=f"ffn_dw3_{tag}", xq=xq)
    emit(("w13", l, s), [dw1, dw3])
    dh_in, dg = _rms_bwd(h, g, dn, res=dh, name=f"ffn_norm_bwd_{tag}")
    return dh_in, dg


def _local_step(x, tgt, W, xq=None, recv=None):
    G = {}

    def emit(key, arrs):
        G[key] = arrs
        if xq is not None:
            xq.push(arrs, True, lambda res, key=key: recv.__setitem__(key, res))

    seq, D = x.shape
    L = N_META + seq
    LP = -(-L // ROW_ALIGN) * ROW_ALIGN
    pad = LP - L
    H_sb = D // HEAD_DIM
    DI = W["ssm_norm_g"].shape[-1]
    H_ssm = DI // HEAD_DIM
    CONV = DI + 2 * SSM_GROUPS * D_STATE
    ZX = DI + CONV

    h0 = jnp.concatenate([W["meta_tokens"], x, jnp.zeros((pad, D), F32)], axis=0)
    tgt_p = jnp.pad(tgt, ((N_META, pad), (0, 0)))
    ng = W["norm_g"]

    h1, sv_f00 = _ffn_fwd(h0, ng[0, 0], W, 0, 0, xq)
    u0 = _rms_fwd(h1, ng[0, 1], name="ssm_norm")
    w_in = W["ssm_in_proj"][0]
    w_zx = w_in[:, :ZX]
    w_dt = jnp.pad(w_in[:, ZX:], ((0, 0), (0, 128 - H_ssm)))
    zx = _matmul([(u0, w_zx)], name="ssm_in_zx", xq=xq)
    dtr = _matmul([(u0, w_dt)], name="ssm_in_dt")
    conv_w, conv_b = W["ssm_conv_w"][0], W["ssm_conv_b"][0]
    xbc = _conv_fwd(zx, conv_w, conv_b, DI, name="ssm_conv", xq=xq)
    dtrt = dtr[:, :H_ssm].T
    padh = lambda t: jnp.pad(t.reshape(1, H_ssm), ((0, 0), (0, 128 - H_ssm)))
    brow, bcol = padh(W["ssm_dt_bias"][0]), W["ssm_dt_bias"][0].reshape(H_ssm, 1)
    alrow, alcol = padh(W["ssm_a_log"][0]), W["ssm_a_log"][0].reshape(H_ssm, 1)
    dvec = jnp.repeat(W["ssm_d"][0], HEAD_DIM).reshape(1, DI)
    ssm_args = (xbc, dtr, dtrt, brow, bcol, alrow, alcol, dvec)
    y, states = _ssd_fwd(*ssm_args, name="ssd_fwd", xq=xq)
    sng = W["ssm_norm_g"].reshape(1, DI)
    yn = _gate_fwd(y, zx, sng, name="ssm_gate")
    w_out = W["ssm_out_proj"][0]
    h2 = _matmul([(yn, w_out)], res=h1, name="ssm_out", xq=xq)
    h3, sv_f01 = _ffn_fwd(h2, ng[0, 2], W, 0, 1, xq)

    kv_in = _rms_fwd(h3, W["kv_norm_g"], name="kv_norm")
    kraw = _matmul([(kv_in, W["w_k"])], shards=H_sb, name="kv_k")
    vh = _matmul([(kv_in, W["w_v"])], shards=H_sb, out_dtype=BF16, name="kv_v")
    kh = _rms_fwd(kraw.reshape(H_sb * LP, HEAD_DIM), W["k_norm_g"], name="k_headnorm").reshape(H_sb, LP, HEAD_DIM)

    h4, sv_f10 = _ffn_fwd(h3, ng[1, 0], W, 1, 0, xq)
    u1 = _rms_fwd(h4, ng[1, 1], name="sb_norm")
    qraw = _matmul([(u1, W["sb_w_q"][0])], shards=H_sb, name="sb_q")
    scale = HEAD_DIM ** -0.5
    qh = _rms_fwd(qraw.reshape(H_sb * LP, HEAD_DIM), W["sb_q_norm_g"][0], scale=scale,
                  name="q_headnorm").reshape(H_sb, LP, HEAD_DIM)
    zmax = 1.02 * math.sqrt(HEAD_DIM) * jnp.max(jnp.abs(W["sb_q_norm_g"])) * jnp.max(jnp.abs(W["k_norm_g"]))
    o, cmat, kfirst = _sb_fwd(qh, kh, vh, jnp.full((1, LANES), zmax, F32), name="sb_fwd")
    kstart = kfirst[:, :, 0, 0].astype(jnp.int32)
    h5 = _matmul([(o, W["sb_w_o"][0])], res=h4, name="sb_out")
    h6, sv_f11 = _ffn_fwd(h5, ng[1, 2], W, 1, 1, xq)

    dh, lvec = _loss_head(h6, tgt_p, seq, name="loss_head")
    loss = jnp.sum(lvec)
    dng = [[None] * 3 for _ in range(2)]
    shard_rows = _shard_rows

    dh, dng[1][2] = _ffn_bwd(dh, sv_f11, ng[1, 2], W, 1, 1, xq, emit)
    g_wo = shard_rows(_matmul([(o, dh)], trans_a=True, out_dtype=BF16, name="sb_dwo"))
    do = _matmul([(dh, W["sb_w_o"][0])], trans_b=True, shards=H_sb, name="sb_do")
    dq, dk, dv = _sb_bwd(kstart, qh, kh, vh, cmat, do, name="sb_bwd")
    dqraw, dqg = _rms_bwd(qraw.reshape(H_sb * LP, HEAD_DIM), W["sb_q_norm_g"][0], dq.reshape(H_sb * LP, HEAD_DIM),
                          alpha=scale, name="q_headnorm_bwd")
    G["sb_q_norm_g"] = dqg
    dqraw = dqraw.reshape(H_sb, LP, HEAD_DIM)
    g_wq = shard_rows(_matmul([(u1, dqraw)], trans_a=True, out_dtype=BF16, name="sb_dwq"))
    emit("sb", [g_wq, g_wo])
    du1 = _matmul([(dqraw, W["sb_w_q"][0])], trans_b=True, name="sb_du")
    dh, dng[1][1] = _rms_bwd(h4, ng[1, 1], du1, res=dh, name="sb_norm_bwd")
    dh, dng[1][0] = _ffn_bwd(dh, sv_f10, ng[1, 0], W, 1, 0, xq, emit)

    dkraw, dkg = _rms_bwd(kraw.reshape(H_sb * LP, HEAD_DIM), W["k_norm_g"], dk.reshape(H_sb * LP, HEAD_DIM),
                          name="k_headnorm_bwd")
    G["k_norm_g"] = dkg.reshape(-1)
    dkraw = dkraw.reshape(H_sb, LP, HEAD_DIM)
    dvf = dv
    g_wk = shard_rows(_matmul([(kv_in, dkraw)], trans_a=True, out_dtype=BF16, name="kv_dwk"))
    g_wv = shard_rows(_matmul([(kv_in, dvf)], trans_a=True, out_dtype=BF16, name="kv_dwv"))
    emit("kv", [g_wk, g_wv])
    dkv = _matmul([(dkraw, W["w_k"]), (dvf, W["w_v"])], trans_b=True, name="kv_din", xq=xq)
    dh, dkvg = _rms_bwd(h3, W["kv_norm_g"], dkv, res=dh, name="kv_norm_bwd")
    G["kv_norm_g"] = dkvg.reshape(-1)

    dh, dng[0][2] = _ffn_bwd(dh, sv_f01, ng[0, 2], W, 0, 1, xq, emit)
    emit("wout", [shard_rows(_matmul([(yn, dh)], trans_a=True, out_dtype=BF16, name="ssm_dwout", xq=xq))])
    dyn = _matmul([(dh, w_out)], trans_b=True, name="ssm_dyn", xq=xq)
    dy, dz, dsng = _gate_bwd(y, zx, sng, dyn, name="ssm_gate_bwd")
    G["ssm_norm_g"] = dsng
    dxbc, ddtr, dbias, dalog, ddvec = _ssd_bwd(*ssm_args, dy, states, name="ssd_bwd", xq=xq)
    G["ssm_dt_bias"] = dbias[:, :H_ssm]
    G["ssm_a_log"] = dalog[:, :H_ssm]
    G["ssm_d"] = jnp.sum(ddvec.reshape(H_ssm, HEAD_DIM), axis=1).reshape(1, H_ssm)
    gpre, dcw, dcb = _conv_bwd_g(zx, conv_w, conv_b, dxbc, DI, name="ssm_conv_bwd_g", xq=xq)
    G["ssm_conv_w"] = dcw[:D_CONV][None]
    G["ssm_conv_b"] = dcb
    dxbc_pre = _conv_bwd_u(gpre, conv_w, name="ssm_conv_bwd_u", xq=xq)
    emit("win", [_deinterleave(jnp.concatenate([
        _matmul([(u0, dz)], trans_a=True, out_dtype=BF16, name="ssm_dwin_z", xq=xq),
        _matmul([(u0, dxbc_pre)], trans_a=True, out_dtype=BF16, name="ssm_dwin_x", xq=xq),
        _matmul([(u0, ddtr)], trans_a=True, out_dtype=BF16, name="ssm_dwin_dt")[:, :H_ssm]], axis=1),
        out_dtype=BF16, name="ssm_dwin_shards")])
    du0 = _matmul([(dz, w_zx)], trans_b=True, name="ssm_du_z", xq=xq)
    du0 = _matmul([(dxbc_pre, w_zx)], trans_b=True, b_off=DI, res=du0, name="ssm_du_x", xq=xq)
    du0 = _matmul([(ddtr, w_dt)], trans_b=True, res=du0, name="ssm_du_dt")
    dh, dng[0][1] = _rms_bwd(h1, ng[0, 1], du0, res=dh, name="ssm_norm_bwd")
    dh, dng[0][0] = _ffn_bwd(dh, sv_f00, ng[0, 0], W, 0, 0, xq, emit)

    G["norm_g"] = jnp.stack([jnp.concatenate(r, axis=0) for r in dng])
    G["meta_tokens"] = dh[:N_META]
    return loss, dh[N_META:L], G


WEIGHTS = ['meta_tokens', 'norm_g', 'ffn_w1', 'ffn_w3', 'ffn_w2', 'ssm_in_proj', 'ssm_conv_w', 'ssm_conv_b',
           'ssm_dt_bias', 'ssm_a_log', 'ssm_d', 'ssm_norm_g', 'ssm_out_proj', 'kv_norm_g', 'w_k', 'k_norm_g', 'w_v',
           'sb_w_q', 'sb_q_norm_g', 'sb_w_o']
SHARD_AXIS = {'meta_tokens': 1, 'norm_g': 2, 'ffn_w1': 3, 'ffn_w3': 3, 'ffn_w2': 2, 'ssm_in_proj': 2, 'ssm_conv_w': 2,
              'ssm_conv_b': 1, 'ssm_dt_bias': None, 'ssm_a_log': None, 'ssm_d': None, 'ssm_norm_g': 1,
              'ssm_out_proj': 1, 'kv_norm_g': None, 'w_k': 0, 'k_norm_g': None, 'w_v': 0, 'sb_w_q': 1,
              'sb_q_norm_g': None, 'sb_w_o': 1}
MATMUL_WEIGHTS = ('ffn_w1', 'ffn_w3', 'ffn_w2', 'ssm_in_proj', 'ssm_out_proj', 'w_k', 'w_v', 'sb_w_q', 'sb_w_o')
PACK_ROWS = 16


def _pack(arrs, dtype):
    flat = jnp.concatenate([a.reshape(-1).astype(dtype) for a in arrs])
    n = flat.shape[0]
    npad = -(-n // (LANES * PACK_ROWS)) * (LANES * PACK_ROWS)
    return jnp.pad(flat, (0, npad - n)).reshape(npad // LANES, LANES)


def _unpack_gathered(buf, names, shard_shapes, dtype):
    flat = buf.reshape(N_DEV, -1)
    out, off = {}, 0
    for n in names:
        shp = shard_shapes[n]
        size = math.prod(shp)
        t = flat[:, off:off + size].reshape((N_DEV,) + tuple(shp))
        off += size
        ax = SHARD_AXIS[n]
        t = jnp.moveaxis(t, 0, ax)
        full = shp[:ax] + (N_DEV * shp[ax],) + shp[ax + 1:]
        out[n] = t.reshape(full).astype(dtype)
    return out


def _to_shards(g, ax):
    shp = g.shape
    t = g.reshape(shp[:ax] + (N_DEV, shp[ax] // N_DEV) + shp[ax + 1:])
    return jnp.moveaxis(t, ax, 0).reshape(N_DEV, -1)


def kernel(x, meta_tokens, norm_g, ffn_w1, ffn_w3, ffn_w2, ssm_in_proj, ssm_conv_w, ssm_conv_b, ssm_dt_bias, ssm_a_log, ssm_d, ssm_norm_g, ssm_out_proj, kv_norm_g, w_k, k_norm_g, w_v, sb_w_q, sb_q_norm_g, sb_w_o, loss_target, m_meta_tokens, m_norm_g, m_ffn_w1, m_ffn_w3, m_ffn_w2, m_ssm_in_proj, m_ssm_conv_w, m_ssm_conv_b, m_ssm_dt_bias, m_ssm_a_log, m_ssm_d, m_ssm_norm_g, m_ssm_out_proj, m_kv_norm_g, m_w_k, m_k_norm_g, m_w_v, m_sb_w_q, m_sb_q_norm_g, m_sb_w_o, v_meta_tokens, v_norm_g, v_ffn_w1, v_ffn_w3, v_ffn_w2, v_ssm_in_proj, v_ssm_conv_w, v_ssm_conv_b, v_ssm_dt_bias, v_ssm_a_log, v_ssm_d, v_ssm_norm_g, v_ssm_out_proj, v_kv_norm_g, v_w_k, v_k_norm_g, v_w_v, v_sb_w_q, v_sb_q_norm_g, v_sb_w_o):
    shard = dict(meta_tokens=meta_tokens, norm_g=norm_g, ffn_w1=ffn_w1, ffn_w3=ffn_w3, ffn_w2=ffn_w2,
                 ssm_in_proj=ssm_in_proj, ssm_conv_w=ssm_conv_w, ssm_conv_b=ssm_conv_b, ssm_dt_bias=ssm_dt_bias,
                 ssm_a_log=ssm_a_log, ssm_d=ssm_d, ssm_norm_g=ssm_norm_g, ssm_out_proj=ssm_out_proj,
                 kv_norm_g=kv_norm_g, w_k=w_k, k_norm_g=k_norm_g, w_v=w_v, sb_w_q=sb_w_q, sb_q_norm_g=sb_q_norm_g,
                 sb_w_o=sb_w_o)
    mom_m = dict(zip(WEIGHTS, (m_meta_tokens, m_norm_g, m_ffn_w1, m_ffn_w3, m_ffn_w2, m_ssm_in_proj, m_ssm_conv_w,
                               m_ssm_conv_b, m_ssm_dt_bias, m_ssm_a_log, m_ssm_d, m_ssm_norm_g, m_ssm_out_proj,
                               m_kv_norm_g, m_w_k, m_k_norm_g, m_w_v, m_sb_w_q, m_sb_q_norm_g, m_sb_w_o)))
    mom_v = dict(zip(WEIGHTS, (v_meta_tokens, v_norm_g, v_ffn_w1, v_ffn_w3, v_ffn_w2, v_ssm_in_proj, v_ssm_conv_w,
                               v_ssm_conv_b, v_ssm_dt_bias, v_ssm_a_log, v_ssm_d, v_ssm_norm_g, v_ssm_out_proj,
                               v_kv_norm_g, v_w_k, v_k_norm_g, v_w_v, v_sb_w_q, v_sb_q_norm_g, v_sb_w_o)))
    sharded = [n for n in WEIGHTS if SHARD_AXIS[n] is not None]
    replicated = [n for n in WEIGHTS if SHARD_AXIS[n] is None]
    small = [n for n in sharded if n not in MATMUL_WEIGHTS]
    shapes = {n: tuple(shard[n].shape) for n in WEIGHTS}
    D = x.shape[-1]
    bf = lambda t: t.astype(BF16)
    full_rows = lambda t: t.reshape(N_DEV * t.shape[1], t.shape[2])
    cols = lambda l, s: jnp.concatenate([bf(ffn_w1[l, s]), bf(ffn_w3[l, s])], axis=0)
    xq, recv = _Queue(), {}
    W = _Weights(xq)
    W.update({n: shard[n] for n in replicated})

    def have_w13(l, s):
        def done(res):
            t = _interleave(res[0], name=f"weights_w13_{l}{s}")
            W[("w13", l, s)] = (t[:D], t[D:])
        return done

    def have_w2(l, s):
        return lambda res: W.__setitem__(("w2", l, s), full_rows(res[0]))

    def have_win(res):
        W["ssm_in_proj"] = _interleave(res[0], name="weights_win")[None]

    def have_rows(res):
        W["ssm_out_proj"] = full_rows(res[0])[None]
        W["w_k"], W["w_v"] = full_rows(res[1]), full_rows(res[2])
        W["sb_w_q"], W["sb_w_o"] = full_rows(res[3])[None], full_rows(res[4])[None]

    first = _exchange([cols(0, 0), bf(ffn_w2[0, 0]), _pack([shard[n] for n in small], F32)], scatter=False,
                      name="gather_first")
    have_w13(0, 0)(first[:1])
    have_w2(0, 0)(first[1:2])
    W.update(_unpack_gathered(first[2], small, shapes, F32))
    xq.push([bf(ssm_in_proj[0])], False, have_win)
    xq.push([bf(ffn_w2[0, 1])], False, have_w2(0, 1))
    xq.push([cols(0, 1)], False, have_w13(0, 1))
    xq.push([bf(ssm_out_proj[0]), bf(w_k), bf(w_v), bf(sb_w_q[0]), bf(sb_w_o[0])], False, have_rows)
    xq.push([cols(1, 0)], False, have_w13(1, 0))
    xq.push([bf(ffn_w2[1, 0])], False, have_w2(1, 0))
    xq.skip()
    xq.skip()
    xq.push([cols(1, 1)], False, have_w13(1, 1))
    xq.push([bf(ffn_w2[1, 1])], False, have_w2(1, 1))

    loss, dx, G = _local_step(x[0], loss_target[0], W, xq, recv)
    loss = lax.psum(loss, ("x", "y", "c"))

    send = jnp.concatenate([_to_shards(G[n], SHARD_AXIS[n]) for n in small], axis=1)
    n_el = send.shape[1]
    npad = -(-n_el // (LANES * PACK_ROWS)) * (LANES * PACK_ROWS)
    send = jnp.pad(send, ((0, 0), (0, npad - n_el))).reshape(N_DEV, npad // LANES, LANES)
    xq.push([send], True, lambda res: recv.__setitem__("small", res))
    xq.flush("scatter_last")
    rep = _pack([G[n] for n in replicated], F32)
    rep_sum = _sum_rows(_exchange([rep], scatter=False, name="gather_small_grads")[0], name="sum_small_grads").reshape(-1)
    summed = {k: [_sum_rows(t, name=f"sum_{'_'.join(map(str, k)) if isinstance(k, tuple) else k}_{i}")
                  for i, t in enumerate(v)] for k, v in recv.items()}

    grads, off = {}, 0
    small_sum = summed["small"][0].reshape(-1)
    for n in small:
        size = math.prod(shapes[n])
        grads[n] = small_sum[off:off + size].reshape(shapes[n])
        off += size
    off = 0
    for n in replicated:
        size = math.prod(shapes[n])
        grads[n] = rep_sum[off:off + size].reshape(shapes[n])
        off += size
    ls = [(l, s) for l in range(2) for s in range(2)]
    grads["ffn_w1"] = jnp.stack([summed[("w13", l, s)][0] for l, s in ls]).reshape(shapes["ffn_w1"])
    grads["ffn_w3"] = jnp.stack([summed[("w13", l, s)][1] for l, s in ls]).reshape(shapes["ffn_w3"])
    grads["ffn_w2"] = jnp.stack([summed[("w2", l, s)][0] for l, s in ls]).reshape(shapes["ffn_w2"])
    grads["ssm_in_proj"] = summed["win"][0][None]
    grads["ssm_out_proj"] = summed["wout"][0][None]
    grads["w_k"], grads["w_v"] = summed["kv"]
    grads["sb_w_q"], grads["sb_w_o"] = summed["sb"][0][None], summed["sb"][1][None]

    delta, new_m, new_v = {}, {}, {}
    for n in WEIGHTS:
        w2 = shard[n].reshape(1, -1) if shard[n].ndim == 1 else shard[n]
        r2 = lambda t: t.reshape(w2.shape)
        d, nm, nv = _adamw(w2, r2(grads[n]), r2(mom_m[n]), r2(mom_v[n]), name=f"adamw_{n}")
        delta[n], new_m[n], new_v[n] = (t.reshape(shapes[n]) for t in (d, nm, nv))

    return (loss, dx[None], *[grads[n] for n in WEIGHTS], *[delta[n] for n in WEIGHTS],
            *[new_m[n] for n in WEIGHTS], *[new_v[n] for n in WEIGHTS])
```

```python
import functools
import math

import jax
import jax.numpy as jnp
from jax import lax
from jax.experimental import pallas as pl
from jax.experimental.pallas import tpu as pltpu

F32 = jnp.float32
BF16 = jnp.bfloat16
RMS_EPS = 1e-6
N_META = 16
HEAD_DIM = 64
SSM_GROUPS = 8
D_STATE = 128
D_CONV = 4
FFN_RES = 0.5
ADAM_LR, ADAM_B1, ADAM_B2, ADAM_EPS, ADAM_WD, ADAM_STEP = 0.001, 0.9, 0.999, 1e-08, 0.01, 10
N_DEV = 8
SSD_CHUNK = 128
ATT_BLOCK = 256
ROW_ALIGN = 768
VMEM_LIMIT_V7X = 48 * 1024 * 1024
MATMUL_VMEM_BUDGET_V7X = 30 * 1024 * 1024
MESH = pl.DeviceIdType.MESH
LANES = 128


def _cp(*sem):
    return pltpu.CompilerParams(dimension_semantics=sem if sem else None, vmem_limit_bytes=VMEM_LIMIT_V7X)


def _tile(n, cands):
    for c in cands:
        if n % c == 0:
            return c
    return n


def _softplus(x):
    return jnp.maximum(x, 0.0) + jnp.log(1.0 + jnp.exp(-jnp.abs(x)))


def _sigmoid(x):
    return 1.0 / (1.0 + jnp.exp(-x))


def _split3(x):
    hi = x.astype(BF16)
    r1 = x - hi.astype(F32)
    mid = r1.astype(BF16)
    lo = (r1 - mid.astype(F32)).astype(BF16)
    return hi, mid, lo


def _dot(a, b):
    return jnp.dot(a, b, preferred_element_type=F32)


def _dot_nt(a, b):
    return lax.dot_general(a, b, (((1,), (1,)), ((), ())), preferred_element_type=F32)


def _dot_tn(a, b):
    return lax.dot_general(a, b, (((0,), (0,)), ((), ())), preferred_element_type=F32)


def _dot3_left(t_bf16, x):
    hi, mid, lo = _split3(x)
    return _dot(t_bf16, hi) + _dot(t_bf16, mid) + _dot(t_bf16, lo)


def _dot3_right(x, t_bf16):
    hi, mid, lo = _split3(x)
    return _dot(hi, t_bf16) + _dot(mid, t_bf16) + _dot(lo, t_bf16)


CARRIER_MIN_FLOP = 4e10


EXCHANGE_US_PER_MB = 94.0
MATMUL_TFLOPS = 650.0


class _Carry:
    def __init__(self, arrs, scatter, done):
        self.arrs, self.scatter, self.done = list(arrs), scatter, done
        per_peer = sum(math.prod(a.shape[1:] if scatter else a.shape) * a.dtype.itemsize for a in self.arrs)
        self.us = EXCHANGE_US_PER_MB * per_peer / 2 ** 20


class _Queue:
    def __init__(self):
        self.items = []

    def push(self, arrs, scatter, done):
        self.items.append(_Carry(arrs, scatter, done))

    def skip(self):
        self.items.append(None)

    def pop(self, kernel_us):
        if self.items and self.items[0] is None:
            self.items.pop(0)
            return None
        for k, it in enumerate(self.items):
            if it is None:
                break
            if not it.scatter or it.us <= 1.15 * kernel_us:
                return self.items.pop(k)
            if k == 0 and len(self.items) > 4:
                return self.items.pop(0)
        return None

    def flush(self, name):
        for k, it in enumerate([it for it in self.items if it is not None]):
            it.done(_exchange(it.arrs, scatter=it.scatter, name=f"{name}_{k}"))
        self.items = []


class _Weights(dict):
    def __init__(self, xq):
        super().__init__()
        self.xq, self.fetched = xq, 0

    def __missing__(self, key):
        while not dict.__contains__(self, key) and self.xq.items:
            it = self.xq.items.pop(0)
            if it is not None:
                it.done(_exchange(it.arrs, scatter=it.scatter, name=f"gather_now_{self.fetched}"))
                self.fetched += 1
        return dict.__getitem__(self, key)


ELEMENTWISE_US_PER_MB = 1.6


def _pop_for_rows(xq, rows, cols, us_per_mb=ELEMENTWISE_US_PER_MB):
    return xq.pop(us_per_mb * rows * cols * 4 / 2 ** 20) if xq is not None else None


def _xchg_shapes(arrs, scatter):
    return [jax.ShapeDtypeStruct((N_DEV,) + tuple(a.shape[1:] if scatter else a.shape), a.dtype) for a in arrs]


def _xchg_scratch(n):
    return [pltpu.SemaphoreType.DMA((n, N_DEV - 1)), pltpu.SemaphoreType.DMA((n, N_DEV - 1)),
            pltpu.SemaphoreType.DMA((n,))]


def _xchg_copies(srcs, dsts, send_sems, recv_sems, local_sems, scatter, with_recv):
    x, y, c, me = _mesh_pos()
    own, sends, recvs = [], [], []
    for a, (s, d) in enumerate(zip(srcs, dsts)):
        own.append(pltpu.make_async_copy(s.at[me] if scatter else s, d.at[me], local_sems.at[a]))
        for f in range(1, N_DEV):
            peer, pid = _peer(x, y, c, f)
            for row, lst in ((me, sends), (pid, recvs)) if with_recv else ((me, sends),):
                lst.append(pltpu.make_async_remote_copy(
                    src_ref=s.at[pid] if scatter else s, dst_ref=d.at[row], send_sem=send_sems.at[a, f - 1],
                    recv_sem=recv_sems.at[a, f - 1], device_id=peer, device_id_type=MESH))
    return own, sends, recvs


def _xchg_start(*a):
    own, sends, _ = _xchg_copies(*a, with_recv=False)
    for cp in own + sends:
        cp.start()


def _xchg_wait(*a):
    own, sends, recvs = _xchg_copies(*a, with_recv=True)
    for snd, rcv in zip(sends, recvs):
        snd.wait_send()
        rcv.wait_recv()
    for cp in own:
        cp.wait()


def _call(body, *, name, grid, in_specs, out_specs, out_shape, scratch_shapes=(), sem, args, carry=None):
    n_in, n_out, n_scr = len(in_specs), len(out_specs), len(scratch_shapes)
    if carry is None:
        return pl.pallas_call(
            body, name=name, grid=grid, in_specs=list(in_specs), out_specs=list(out_specs), out_shape=list(out_shape),
            scratch_shapes=list(scratch_shapes), compiler_params=_cp(*sem))(*args)
    n = len(carry.arrs)
    hbm = pl.BlockSpec(memory_space=pl.ANY)

    def wrapped(*refs):
        ins, csrc = refs[:n_in], refs[n_in:n_in + n]
        outs, cdst = refs[n_in + n:n_in + n + n_out], refs[n_in + n + n_out:n_in + 2 * n + n_out]
        scr = refs[n_in + 2 * n + n_out:]
        xa = (csrc, cdst, *scr[n_scr:], carry.scatter)
        pid = [pl.program_id(a) for a in range(len(grid))]
        first = functools.reduce(jnp.logical_and, [p == 0 for p in pid])
        last = functools.reduce(jnp.logical_and, [p == g - 1 for p, g in zip(pid, grid)])

        @pl.when(first)
        def _():
            _xchg_start(*xa)

        body(*ins, *outs, *scr[:n_scr])

        @pl.when(last)
        def _():
            _xchg_wait(*xa)

    res = pl.pallas_call(
        wrapped, name=name, grid=grid, in_specs=list(in_specs) + [hbm] * n, out_specs=list(out_specs) + [hbm] * n,
        out_shape=list(out_shape) + _xchg_shapes(carry.arrs, carry.scatter),
        scratch_shapes=list(scratch_shapes) + _xchg_scratch(n), compiler_params=_cp(*["arbitrary"] * len(grid)),
    )(*args, *carry.arrs)
    carry.done(list(res[n_out:]))
    return list(res[:n_out])


def _matmul(pairs, *, name, out_dtype=F32, trans_a=False, trans_b=False, b_off=0, res=None, alpha=1.0, shards=None,
            xq=None, tm=None, tn=None, tk=None):
    a0, b0 = pairs[0]
    ha, hb = a0.ndim == 3, b0.ndim == 3
    shape2 = lambda t: (t.shape[1], t.shape[0] * t.shape[2]) if t.ndim == 3 else t.shape
    if trans_a:
        K, M = shape2(a0)
    else:
        M, K = shape2(a0)
    N = b0.shape[0] if trans_b else shape2(b0)[1]
    npair = len(pairs)
    has_res = res is not None
    tm = tm or (M if (ha and trans_a) else _tile(M, (768, 512, 1408, 384, 256, 128)))
    tk = tk or (K if (ha and not trans_a) else _tile(K, (1024, 1408, 768, 512, 256, 128)))
    if tn is None:
        sa, sb, so = a0.dtype.itemsize, b0.dtype.itemsize, jnp.dtype(out_dtype).itemsize
        for tn in ([N] if (shards or hb) else
                   [c for c in (2048, 1536, 1408, 1024, 512, 384, 256, 128) if N % c == 0] or [N]):
            if (2 * npair * tk * (tm * sa + tn * sb) + tm * tn * (4 + 2 * so + (8 if has_res else 0))
                    <= MATMUL_VMEM_BUDGET_V7X):
                break
    nk = K // tk
    cs = N // shards if shards else None
    n_in = 2 * npair + (1 if has_res else 0)
    a_blk = (tk, tm) if trans_a else (tm, tk)

    def body(*refs):
        o_ref, acc, flat = refs[n_in], refs[n_in + 1], list(refs[n_in + 2:])
        k = pl.program_id(2)

        @pl.when(k == 0)
        def _():
            acc[...] = jnp.zeros_like(acc)

        def operand(ref, heads):
            if not heads:
                return ref[...].astype(BF16)
            buf = flat.pop(0)
            for h in range(ref.shape[0]):
                buf[:, h * HEAD_DIM:(h + 1) * HEAD_DIM] = ref[h].astype(BF16)
            return buf[...]

        part = None
        for p in range(npair):
            a = operand(refs[2 * p], ha)
            b = operand(refs[2 * p + 1], hb)
            d = _dot_tn(a, b) if trans_a else _dot_nt(a, b) if trans_b else _dot(a, b)
            part = d if part is None else part + d
        acc[...] += part

        @pl.when(k == nk - 1)
        def _():
            if shards:
                for d in range(shards):
                    v = acc[:, d * cs:(d + 1) * cs]
                    o_ref[d] = (v * alpha if alpha != 1.0 else v).astype(out_dtype)
                return
            v = acc[...]
            if alpha != 1.0:
                v = v * alpha
            if has_res:
                v = refs[2 * npair][...] + v
            o_ref[...] = v.astype(out_dtype)

    if ha:
        assert (tm == M) if trans_a else (tk == K)
        a_spec = pl.BlockSpec((a0.shape[0], a_blk[0], HEAD_DIM),
                              (lambda i, j, k: (0, k, 0)) if trans_a else (lambda i, j, k: (0, i, 0)))
    elif trans_a:
        a_spec = pl.BlockSpec((tk, tm), lambda i, j, k: (k, i))
    else:
        a_spec = pl.BlockSpec((tm, tk), lambda i, j, k: (i, k))
    if hb:
        assert not trans_b and tn == N
        b_spec = pl.BlockSpec((b0.shape[0], tk, HEAD_DIM), lambda i, j, k: (0, k, 0))
    elif trans_b:
        assert not trans_a and b_off % tk == 0
        b_spec = pl.BlockSpec((tn, tk), lambda i, j, k: (j, k + b_off // tk))
    else:
        b_spec = pl.BlockSpec((tk, tn), lambda i, j, k: (k, j))
    flat_scratch = [pltpu.VMEM(a_blk, BF16)] * (npair if ha else 0) + [pltpu.VMEM((tk, tn), BF16)] * (npair if hb else 0)
    if shards:
        assert not has_res and tn == N
        o_spec = pl.BlockSpec((shards, tm, cs), lambda i, j, k: (0, i, 0))
        out_shape = jax.ShapeDtypeStruct((shards, M, cs), out_dtype)
    else:
        o_spec = pl.BlockSpec((tm, tn), lambda i, j, k: (i, j))
        out_shape = jax.ShapeDtypeStruct((M, N), out_dtype)
    in_specs, args = [], []
    for a, b in pairs:
        in_specs += [a_spec, b_spec]
        args += [a, b]
    if has_res:
        in_specs.append(o_spec)
        args.append(res)
    flop = 2.0 * npair * M * N * K
    carry = xq.pop(flop / MATMUL_TFLOPS / 1e6) if (xq is not None and flop >= CARRIER_MIN_FLOP) else None
    return _call(body, name=name, grid=(M // tm, N // tn, nk), in_specs=in_specs, out_specs=[o_spec],
                 out_shape=[out_shape], scratch_shapes=[pltpu.VMEM((tm, tn), F32)] + flat_scratch,
                 sem=("parallel", "parallel", "arbitrary"), args=args, carry=carry)[0]


def _rms_fwd(h, g, *, name, scale=1.0):
    R, D = h.shape
    tr = _tile(R, (2048, 1024, 768, 512, 256, 128)) if D <= 128 else _tile(R, (384, 256, 128))

    def body(h_ref, g_ref, o_ref):
        x = h_ref[...]
        r = lax.rsqrt(jnp.mean(x * x, axis=1, keepdims=True) + RMS_EPS)
        y = x * r * g_ref[...]
        if scale != 1.0:
            y = y * scale
        o_ref[...] = y.astype(BF16)

    return pl.pallas_call(
        body, name=name, grid=(R // tr,),
        in_specs=[pl.BlockSpec((tr, D), lambda i: (i, 0)), pl.BlockSpec((1, D), lambda i: (0, 0))],
        out_specs=pl.BlockSpec((tr, D), lambda i: (i, 0)),
        out_shape=jax.ShapeDtypeStruct((R, D), BF16), compiler_params=_cp("parallel"),
    )(h, g.reshape(1, D))


def _rms_bwd(h, g, dn, res=None, *, name, alpha=1.0):
    R, D = h.shape
    tr = _tile(R, (2048, 1024, 768, 512, 256, 128)) if D <= 128 else _tile(R, (384, 256, 128))
    has_res = res is not None

    def body(*refs):
        h_ref, g_ref, dn_ref = refs[:3]
        dh_ref, dg_ref = refs[-2], refs[-1]
        i = pl.program_id(0)

        @pl.when(i == 0)
        def _():
            dg_ref[...] = jnp.zeros_like(dg_ref)

        x = h_ref[...]
        r = lax.rsqrt(jnp.mean(x * x, axis=1, keepdims=True) + RMS_EPS)
        xh = x * r
        d = dn_ref[...].astype(F32)
        if alpha != 1.0:
            d = d * alpha
        dng = d * g_ref[...]
        m = jnp.mean(dng * xh, axis=1, keepdims=True)
        dh = r * (dng - xh * m)
        if has_res:
            dh = dh + refs[3][...]
        dh_ref[...] = dh
        dg_ref[...] += jnp.sum(d * xh, axis=0, keepdims=True)

    row = pl.BlockSpec((tr, D), lambda i: (i, 0))
    vec = pl.BlockSpec((1, D), lambda i: (0, 0))
    in_specs = [row, vec, row] + ([row] if has_res else [])
    args = [h, g.reshape(1, D), dn] + ([res] if has_res else [])
    return pl.pallas_call(
        body, name=name, grid=(R // tr,), in_specs=in_specs, out_specs=[row, vec],
        out_shape=[jax.ShapeDtypeStruct((R, D), F32), jax.ShapeDtypeStruct((1, D), F32)],
        compiler_params=_cp("arbitrary"),
    )(*args)


def _ffn_up(n, w1, w3, *, name, xq=None):
    M, K = n.shape
    N = w1.shape[1]
    tm = _tile(M, (384, 256, 128))
    tn = _tile(N, (1408, 512, 256, 128))

    def body(n_ref, w1_ref, w3_ref, da_ref, db_ref, s_ref):
        x = n_ref[...]
        a = _dot(x, w1_ref[...])
        b = _dot(x, w3_ref[...])
        sg = _sigmoid(a)
        silu = a * sg
        da_ref[...] = (b * sg * (1.0 + a * (1.0 - sg))).astype(BF16)
        db_ref[...] = silu.astype(BF16)
        s_ref[...] = (silu * b).astype(BF16)

    o_spec = pl.BlockSpec((tm, tn), lambda j, i: (i, j))
    w_spec = pl.BlockSpec((K, tn), lambda j, i: (0, j))
    sh = jax.ShapeDtypeStruct((M, N), BF16)
    return _call(body, name=name, grid=(N // tn, M // tm),
                 in_specs=[pl.BlockSpec((tm, K), lambda j, i: (i, 0)), w_spec, w_spec],
                 out_specs=[o_spec, o_spec, o_spec], out_shape=[sh, sh, sh], sem=("parallel", "parallel"),
                 args=(n, w1, w3), carry=xq.pop(4.0 * M * N * K / MATMUL_TFLOPS / 1e6) if xq is not None else None)


def _ffn_mid_bwd(dh, w2, s_a, s_b, *, name, xq=None):
    M, K = dh.shape
    N = w2.shape[0]
    tm = _tile(M, (384, 256, 128))
    tn = _tile(N, (1408, 512, 256, 128))

    def body(dh_ref, w_ref, a_ref, b_ref, da_ref, db_ref):
        ds = _dot_nt(dh_ref[...].astype(BF16), w_ref[...]) * FFN_RES
        da_ref[...] = (ds * a_ref[...].astype(F32)).astype(BF16)
        db_ref[...] = (ds * b_ref[...].astype(F32)).astype(BF16)

    o_spec = pl.BlockSpec((tm, tn), lambda j, i: (i, j))
    sh = jax.ShapeDtypeStruct((M, N), BF16)
    return _call(body, name=name, grid=(N // tn, M // tm),
                 in_specs=[pl.BlockSpec((tm, K), lambda j, i: (i, 0)), pl.BlockSpec((tn, K), lambda j, i: (j, 0)),
                           o_spec, o_spec],
                 out_specs=[o_spec, o_spec], out_shape=[sh, sh], sem=("parallel", "parallel"),
                 args=(dh, w2, s_a, s_b), carry=xq.pop(2.0 * M * N * K / MATMUL_TFLOPS / 1e6) if xq is not None else None)


def _conv_pre(xx, w_ref, b_ref, tr):
    acc = None
    for k in range(D_CONV):
        sh = D_CONV - 1 - k
        v = (pltpu.roll(xx, sh, 0) if sh else xx)[8:8 + tr]
        t = w_ref[k:k + 1, :] * v
        acc = t if acc is None else acc + t
    return acc + b_ref[...]


def _conv_fwd(zx, w, b, col_off, *, name, xq=None):
    LP = zx.shape[0]
    C = w.shape[1]
    tr = _tile(LP, (256, 128))
    tc = _tile(C, (512, 256, 128))
    co = col_off // tc

    def body(cur_ref, prev_ref, w_ref, b_ref, o_ref):
        i = pl.program_id(0)
        prev = jnp.where(i == 0, 0.0, prev_ref[...])
        pre = _conv_pre(jnp.concatenate([prev, cur_ref[...]], axis=0), w_ref, b_ref, tr)
        o_ref[...] = pre * _sigmoid(pre)

    return _call(
        body, name=name, grid=(LP // tr, C // tc),
        in_specs=[pl.BlockSpec((tr, tc), lambda i, j: (i, j + co)),
                  pl.BlockSpec((8, tc), lambda i, j: (jnp.maximum(i * (tr // 8) - 1, 0), j + co)),
                  pl.BlockSpec((D_CONV, tc), lambda i, j: (0, j)), pl.BlockSpec((1, tc), lambda i, j: (0, j))],
        out_specs=[pl.BlockSpec((tr, tc), lambda i, j: (i, j))],
        out_shape=[jax.ShapeDtypeStruct((LP, C), F32)], sem=("parallel", "parallel"),
        args=(zx, zx, w, b.reshape(1, C)), carry=_pop_for_rows(xq, LP, C))[0]


def _conv_bwd_g(zx, w, b, dact, col_off, *, name, xq=None):
    LP = zx.shape[0]
    C = w.shape[1]
    tr = _tile(LP, (256, 128))
    tc = _tile(C, (512, 256, 128))
    co = col_off // tc

    def body(cur_ref, prev_ref, w_ref, b_ref, d_ref, g_ref, dw_ref, db_ref):
        i = pl.program_id(1)

        @pl.when(i == 0)
        def _():
            dw_ref[...] = jnp.zeros_like(dw_ref)
            db_ref[...] = jnp.zeros_like(db_ref)

        prev = jnp.where(i == 0, 0.0, prev_ref[...])
        xx = jnp.concatenate([prev, cur_ref[...]], axis=0)
        pre = _conv_pre(xx, w_ref, b_ref, tr)
        sg = _sigmoid(pre)
        g = d_ref[...] * sg * (1.0 + pre * (1.0 - sg))
        g_ref[...] = g
        db_ref[...] += jnp.sum(g, axis=0, keepdims=True)
        rows = []
        for k in range(D_CONV):
            sh = D_CONV - 1 - k
            v = (pltpu.roll(xx, sh, 0) if sh else xx)[8:8 + tr]
            rows.append(jnp.sum(g * v, axis=0, keepdims=True))
        rows.append(jnp.zeros((8 - D_CONV, tc), F32))
        dw_ref[...] += jnp.concatenate(rows, axis=0)

    return _call(
        body, name=name, grid=(C // tc, LP // tr),
        in_specs=[pl.BlockSpec((tr, tc), lambda j, i: (i, j + co)),
                  pl.BlockSpec((8, tc), lambda j, i: (jnp.maximum(i * (tr // 8) - 1, 0), j + co)),
                  pl.BlockSpec((D_CONV, tc), lambda j, i: (0, j)), pl.BlockSpec((1, tc), lambda j, i: (0, j)),
                  pl.BlockSpec((tr, tc), lambda j, i: (i, j))],
        out_specs=[pl.BlockSpec((tr, tc), lambda j, i: (i, j)), pl.BlockSpec((8, tc), lambda j, i: (0, j)),
                   pl.BlockSpec((1, tc), lambda j, i: (0, j))],
        out_shape=[jax.ShapeDtypeStruct((LP, C), F32), jax.ShapeDtypeStruct((8, C), F32), jax.ShapeDtypeStruct((1, C), F32)],
        sem=("parallel", "arbitrary"), args=(zx, zx, w, b.reshape(1, C), dact), carry=_pop_for_rows(xq, LP, C, 2.0))


def _conv_bwd_u(g, w, *, name, xq=None):
    LP, C = g.shape
    tr = _tile(LP, (256, 128))
    tc = _tile(C, (512, 256, 128))
    nb = LP // tr

    def body(cur_ref, nxt_ref, w_ref, o_ref):
        i = pl.program_id(0)
        nxt = jnp.where(i == nb - 1, 0.0, nxt_ref[...])
        xx = jnp.concatenate([cur_ref[...], nxt], axis=0)
        acc = None
        for k in range(D_CONV):
            sh = D_CONV - 1 - k
            v = (pltpu.roll(xx, tr + 8 - sh, 0) if sh else xx)[:tr]
            t = w_ref[k:k + 1, :] * v
            acc = t if acc is None else acc + t
        o_ref[...] = acc

    return _call(
        body, name=name, grid=(nb, C // tc),
        in_specs=[pl.BlockSpec((tr, tc), lambda i, j: (i, j)),
                  pl.BlockSpec((8, tc), lambda i, j: (jnp.minimum((i + 1) * (tr // 8), LP // 8 - 1), j)),
                  pl.BlockSpec((D_CONV, tc), lambda i, j: (0, j))],
        out_specs=[pl.BlockSpec((tr, tc), lambda i, j: (i, j))],
        out_shape=[jax.ShapeDtypeStruct((LP, C), F32)], sem=("parallel", "parallel"),
        args=(g, g, w), carry=_pop_for_rows(xq, LP, C))[0]


def _ssd_prelude(dtr_ref, dtrt_ref, brow_ref, bcol_ref, alrow_ref, alcol_ref, Q):
    ii = lax.broadcasted_iota(jnp.int32, (Q, Q), 0)
    jj = lax.broadcasted_iota(jnp.int32, (Q, Q), 1)
    tril = ii >= jj
    dt_col = _softplus(dtr_ref[...] + brow_ref[...])
    a_row_p = -jnp.exp(alrow_ref[...])
    dt_row = _softplus(dtrt_ref[...] + bcol_ref[...])
    a_col_p = -jnp.exp(alcol_ref[...])
    cum_col = _dot3_left(tril.astype(BF16), dt_col * a_row_p)
    cum_row = _dot3_right(dt_row * a_col_p, (ii <= jj).astype(BF16))
    return ii, jj, tril, dt_col, dt_row, a_row_p, cum_col, cum_row


def _col_of(mat, lane_idx, h):
    return jnp.sum(jnp.where(lane_idx == h, mat, 0.0), axis=1, keepdims=True)


def _ssd_fwd(xbc, dtr, dtrt, brow, bcol, alrow, alcol, dvec, *, name, xq=None):
    LP = xbc.shape[0]
    Q = SSD_CHUNK
    nc = LP // Q
    G = SSM_GROUPS
    DI = dvec.shape[1]
    gw = DI // G
    hpg = gw // HEAD_DIM
    H = G * hpg
    boff, coff = DI, DI + G * D_STATE

    def body(xbc_ref, dtr_ref, dtrt_ref, brow_ref, bcol_ref, alrow_ref, alcol_ref, dvec_ref, y_ref, st_ref, state):
        c = pl.program_id(0)

        @pl.when(c == 0)
        def _():
            state[...] = jnp.zeros_like(state)

        st_ref[...] = state[...]
        ii, jj, tril, dt_col, dt_row, _, cum_col, cum_row = _ssd_prelude(
            dtr_ref, dtrt_ref, brow_ref, bcol_ref, alrow_ref, alcol_ref, Q)
        lane_h = lax.broadcasted_iota(jnp.int32, (Q, 128), 1)
        lane_g = lax.broadcasted_iota(jnp.int32, (Q, gw), 1) // HEAD_DIM
        for g in range(G):
            xg = xbc_ref[:, g * gw:(g + 1) * gw]
            bb = xbc_ref[:, boff + g * D_STATE: boff + (g + 1) * D_STATE].astype(BF16)
            cb = xbc_ref[:, coff + g * D_STATE: coff + (g + 1) * D_STATE].astype(BF16)
            gm = _dot_nt(cb, bb)
            sg = state[g]
            yoff = _dot(cb, sg.astype(BF16))
            ydiag = jnp.zeros((Q, gw), F32)
            esc = jnp.zeros((Q, gw), F32)
            wsc = jnp.zeros((Q, gw), F32)
            lam = jnp.zeros((1, gw), F32)
            for j in range(hpg):
                h = g * hpg + j
                ccol = _col_of(cum_col, lane_h, h)
                dcol = _col_of(dt_col, lane_h, h)
                seg = ccol - cum_row[h:h + 1, :]
                decay = jnp.exp(jnp.where(tril, seg, -jnp.inf))
                mh = gm * decay * dt_row[h:h + 1, :]
                hm = lane_g == j
                ydiag = ydiag + _dot(mh.astype(BF16), jnp.where(hm, xg, 0.0).astype(BF16))
                tot = ccol[Q - 1:Q, :]
                esc = jnp.where(hm, jnp.exp(ccol), esc)
                wsc = jnp.where(hm, jnp.exp(tot - ccol) * dcol, wsc)
                lam = jnp.where(hm[0:1], jnp.exp(tot), lam)
            y_ref[:, g * gw:(g + 1) * gw] = ydiag + yoff * esc + dvec_ref[:, g * gw:(g + 1) * gw] * xg
            state[g] = sg * lam + _dot_tn(bb, (xg * wsc).astype(BF16))

    W = xbc.shape[1]
    full = lambda shape: pl.BlockSpec(shape, lambda c: (0,) * len(shape))
    return _call(
        body, name=name, grid=(nc,),
        in_specs=[pl.BlockSpec((Q, W), lambda c: (c, 0)), pl.BlockSpec((Q, 128), lambda c: (c, 0)),
                  pl.BlockSpec((H, Q), lambda c: (0, c)), full((1, 128)), full((H, 1)), full((1, 128)), full((H, 1)),
                  full((1, DI))],
        out_specs=[pl.BlockSpec((Q, DI), lambda c: (c, 0)), pl.BlockSpec((None, G, D_STATE, gw), lambda c: (c, 0, 0, 0))],
        out_shape=[jax.ShapeDtypeStruct((LP, DI), F32), jax.ShapeDtypeStruct((nc, G, D_STATE, gw), F32)],
        scratch_shapes=[pltpu.VMEM((G, D_STATE, gw), F32)], sem=("arbitrary",),
        args=(xbc, dtr, dtrt, brow, bcol, alrow, alcol, dvec), carry=_pop_for_rows(xq, LP, W, 1.25))


def _ssd_bwd(xbc, dtr, dtrt, brow, bcol, alrow, alcol, dvec, dy, states, *, name, xq=None):
    LP = xbc.shape[0]
    Q = SSD_CHUNK
    nc = LP // Q
    G = SSM_GROUPS
    DI = dvec.shape[1]
    gw = DI // G
    hpg = gw // HEAD_DIM
    H = G * hpg
    boff, coff = DI, DI + G * D_STATE
    W = xbc.shape[1]

    def body(xbc_ref, dtr_ref, dtrt_ref, brow_ref, bcol_ref, alrow_ref, alcol_ref, dvec_ref, dy_ref, st_ref,
             dxbc_ref, ddtr_ref, dbias_ref, dalog_ref, ddvec_ref, dstate):
        c = pl.program_id(0)

        @pl.when(c == 0)
        def _():
            dstate[...] = jnp.zeros_like(dstate)
            dbias_ref[...] = jnp.zeros_like(dbias_ref)
            dalog_ref[...] = jnp.zeros_like(dalog_ref)
            ddvec_ref[...] = jnp.zeros_like(ddvec_ref)

        ii, jj, tril, dt_col, dt_row, a_row_p, cum_col, cum_row = _ssd_prelude(
            dtr_ref, dtrt_ref, brow_ref, bcol_ref, alrow_ref, alcol_ref, Q)
        eye = ii == jj
        lane_h = lax.broadcasted_iota(jnp.int32, (Q, 128), 1)
        row_h = lax.broadcasted_iota(jnp.int32, (Q, 128), 0)
        lane_g = lax.broadcasted_iota(jnp.int32, (Q, gw), 1) // HEAD_DIM
        lane_s = lax.broadcasted_iota(jnp.int32, (D_STATE, gw), 1) // HEAD_DIM
        dcum_mat = jnp.zeros((Q, 128), F32)
        ddt_mat = jnp.zeros((Q, 128), F32)
        dtot_row = jnp.zeros((1, 128), F32)
        for g in range(G):
            xg = xbc_ref[:, g * gw:(g + 1) * gw]
            dyg = dy_ref[:, g * gw:(g + 1) * gw]
            bb = xbc_ref[:, boff + g * D_STATE: boff + (g + 1) * D_STATE].astype(BF16)
            cb = xbc_ref[:, coff + g * D_STATE: coff + (g + 1) * D_STATE].astype(BF16)
            sg = st_ref[g]
            dsg = dstate[g]
            sb = sg.astype(BF16)
            dsb = dsg.astype(BF16)
            xb = xg.astype(BF16)
            gm = _dot_nt(cb, bb)
            cs = _dot(cb, sb)
            bds = _dot(bb, dsb)
            dxg = dvec_ref[:, g * gw:(g + 1) * gw] * dyg
            dgm = jnp.zeros((Q, Q), F32)
            esc = jnp.zeros((Q, gw), F32)
            wsc = jnp.zeros((Q, gw), F32)
            lam = jnp.zeros((1, gw), F32)
            dycs = dyg * cs
            xbds = xg * bds
            dss = dsg * sg
            for j in range(hpg):
                h = g * hpg + j
                ccol = _col_of(cum_col, lane_h, h)
                dcol = _col_of(dt_col, lane_h, h)
                drow = dt_row[h:h + 1, :]
                seg = ccol - cum_row[h:h + 1, :]
                decay = jnp.exp(jnp.where(tril, seg, -jnp.inf))
                hm = lane_g == j
                dyh = jnp.where(hm, dyg, 0.0).astype(BF16)
                gl = gm * decay
                mh = gl * drow
                dmf = _dot_nt(dyh, xb)
                dxg = dxg + _dot_tn(mh.astype(BF16), dyh)
                dgm = dgm + dmf * decay * drow
                n_p = dmf * gl
                n_m = n_p * drow
                rowsum_n = jnp.sum(n_m, axis=1, keepdims=True)
                colsum_n = jnp.sum(jnp.where(eye, jnp.sum(n_m, axis=0, keepdims=True), 0.0), axis=1, keepdims=True)
                colsum_np = jnp.sum(jnp.where(eye, jnp.sum(n_p, axis=0, keepdims=True), 0.0), axis=1, keepdims=True)
                tot = ccol[Q - 1:Q, :]
                e = jnp.exp(ccol)
                wexp = jnp.exp(tot - ccol)
                wcol = wexp * dcol
                lamh = jnp.exp(tot)
                yoff_t = jnp.sum(jnp.where(hm, dycs, 0.0), axis=1, keepdims=True) * e
                e_s = jnp.sum(jnp.where(hm, xbds, 0.0), axis=1, keepdims=True)
                ew = e_s * wcol
                dtot = jnp.sum(ew, axis=0, keepdims=True) + lamh * jnp.sum(
                    jnp.sum(jnp.where(lane_s == j, dss, 0.0), axis=1, keepdims=True), axis=0, keepdims=True)
                dcum_h = rowsum_n + yoff_t - colsum_n - ew
                ddt_h = colsum_np + e_s * wexp
                onehot = lane_h == h
                dcum_mat = jnp.where(onehot, dcum_h, dcum_mat)
                ddt_mat = jnp.where(onehot, ddt_h, ddt_mat)
                dtot_row = jnp.where(onehot[0:1], dtot, dtot_row)
                esc = jnp.where(hm, e, esc)
                wsc = jnp.where(hm, wcol, wsc)
                lam = jnp.where(hm[0:1], lamh, lam)
            dgb = dgm.astype(BF16)
            dye = (dyg * esc).astype(BF16)
            xw = (xg * wsc).astype(BF16)
            dxbc_ref[:, g * gw:(g + 1) * gw] = dxg + bds * wsc
            dxbc_ref[:, boff + g * D_STATE: boff + (g + 1) * D_STATE] = _dot_tn(dgb, cb) + _dot_nt(xw, dsb)
            dxbc_ref[:, coff + g * D_STATE: coff + (g + 1) * D_STATE] = _dot(dgb, bb) + _dot_nt(dye, sb)
            dstate[g] = dsg * lam + _dot_tn(cb, dye)
            ddvec_ref[:, g * gw:(g + 1) * gw] += jnp.sum(dyg * xg, axis=0, keepdims=True)
        dcum_mat = dcum_mat + jnp.where(row_h == Q - 1, dtot_row, 0.0)
        da = _dot3_left((ii <= jj).astype(BF16), dcum_mat)
        ddt = ddt_mat + da * a_row_p
        dalog_ref[...] += jnp.sum(da * dt_col, axis=0, keepdims=True) * a_row_p
        ddtr = ddt * _sigmoid(dtr_ref[...] + brow_ref[...])
        ddtr_ref[...] = ddtr
        dbias_ref[...] += jnp.sum(ddtr, axis=0, keepdims=True)

    full = lambda shape: pl.BlockSpec(shape, lambda c: (0,) * len(shape))
    rc = lambda c: nc - 1 - c
    return _call(
        body, name=name, grid=(nc,),
        in_specs=[pl.BlockSpec((Q, W), lambda c: (rc(c), 0)), pl.BlockSpec((Q, 128), lambda c: (rc(c), 0)),
                  pl.BlockSpec((H, Q), lambda c: (0, rc(c))), full((1, 128)), full((H, 1)), full((1, 128)), full((H, 1)),
                  full((1, DI)), pl.BlockSpec((Q, DI), lambda c: (rc(c), 0)),
                  pl.BlockSpec((None, G, D_STATE, gw), lambda c: (rc(c), 0, 0, 0))],
        out_specs=[pl.BlockSpec((Q, W), lambda c: (rc(c), 0)), pl.BlockSpec((Q, 128), lambda c: (rc(c), 0)),
                   full((1, 128)), full((1, 128)), full((1, DI))],
        out_shape=[jax.ShapeDtypeStruct((LP, W), F32), jax.ShapeDtypeStruct((LP, 128), F32),
                   jax.ShapeDtypeStruct((1, 128), F32), jax.ShapeDtypeStruct((1, 128), F32),
                   jax.ShapeDtypeStruct((1, DI), F32)],
        scratch_shapes=[pltpu.VMEM((G, D_STATE, gw), F32)], sem=("arbitrary",),
        args=(xbc, dtr, dtrt, brow, bcol, alrow, alcol, dvec, dy, states), carry=_pop_for_rows(xq, LP, W, 4.5))


def _gate_fwd(y, zx, g, *, name):
    LP, DI = y.shape
    gw = DI // SSM_GROUPS
    tr = _tile(LP, (256, 128))

    def body(y_ref, z_ref, g_ref, o_ref):
        for k in range(SSM_GROUPS):
            sl = slice(k * gw, (k + 1) * gw)
            z = z_ref[:, sl]
            t = y_ref[:, sl] * (z * _sigmoid(z))
            r = lax.rsqrt(jnp.mean(t * t, axis=1, keepdims=True) + RMS_EPS)
            o_ref[:, sl] = (t * r * g_ref[:, sl]).astype(BF16)

    row = pl.BlockSpec((tr, DI), lambda i: (i, 0))
    return pl.pallas_call(
        body, name=name, grid=(LP // tr,), in_specs=[row, row, pl.BlockSpec((1, DI), lambda i: (0, 0))],
        out_specs=row, out_shape=jax.ShapeDtypeStruct((LP, DI), BF16), compiler_params=_cp("parallel"),
    )(y, zx, g)


def _gate_bwd(y, zx, g, dyn, *, name):
    LP, DI = y.shape
    gw = DI // SSM_GROUPS
    tr = _tile(LP, (256, 128))

    def body(y_ref, z_ref, g_ref, d_ref, dy_ref, dz_ref, dg_ref):
        i = pl.program_id(0)

        @pl.when(i == 0)
        def _():
            dg_ref[...] = jnp.zeros_like(dg_ref)

        for k in range(SSM_GROUPS):
            sl = slice(k * gw, (k + 1) * gw)
            z = z_ref[:, sl]
            yv = y_ref[:, sl]
            sg = _sigmoid(z)
            sz = z * sg
            t = yv * sz
            r = lax.rsqrt(jnp.mean(t * t, axis=1, keepdims=True) + RMS_EPS)
            th = t * r
            d = d_ref[:, sl]
            dtn = d * g_ref[:, sl]
            dt_ = r * (dtn - th * jnp.mean(dtn * th, axis=1, keepdims=True))
            dg_ref[:, sl] += jnp.sum(d * th, axis=0, keepdims=True)
            dy_ref[:, sl] = dt_ * sz
            dz_ref[:, sl] = dt_ * yv * sg * (1.0 + z * (1.0 - sg))

    row = pl.BlockSpec((tr, DI), lambda i: (i, 0))
    vec = pl.BlockSpec((1, DI), lambda i: (0, 0))
    return pl.pallas_call(
        body, name=name, grid=(LP // tr,), in_specs=[row, row, vec, row], out_specs=[row, row, vec],
        out_shape=[jax.ShapeDtypeStruct((LP, DI), F32), jax.ShapeDtypeStruct((LP, DI), F32),
                   jax.ShapeDtypeStruct((1, DI), F32)],
        compiler_params=_cp("arbitrary"),
    )(y, zx, g, dyn)


EXP_ZERO = -104.0
LOG2E = 1.4426950408889634


def _dot2_right(x, t2_bf16):
    hi = x.astype(BF16)
    lo = (x - hi.astype(F32)).astype(BF16)
    return _dot(jnp.concatenate([hi, lo], axis=1), t2_bf16)


def _tri2(T, upper):
    r = lax.broadcasted_iota(jnp.int32, (2 * T, T), 0) % T
    c = lax.broadcasted_iota(jnp.int32, (2 * T, T), 1)
    return (r <= c if upper else r >= c).astype(BF16)


def _sb_tile(q, k_blk, lower2, valid=None):
    z = _dot_nt(q, k_blk)
    sp = jnp.maximum(z, 0.0) + jnp.log(1.0 + jnp.exp2(jnp.abs(z) * (-LOG2E)))
    if valid is not None:
        sp = jnp.where(valid, sp, 0.0)
    return z, sp, z - _dot2_right(sp, lower2)


def _sb_weights(zr, c, valid=None):
    w = jnp.exp(zr + c)
    return w if valid is None else jnp.where(valid, w, 0.0)


def _sb_fwd(q, k, v, zmax, *, name):
    H, LP, dh = q.shape
    T = ATT_BLOCK
    nq = LP // T
    assert nq < LANES

    def body(q_ref, k_ref, v_ref, zb_ref, o_ref, c_ref, kf_ref):
        i = pl.program_id(1)
        ii = lax.broadcasted_iota(jnp.int32, (T, T), 0)
        jj = lax.broadcasted_iota(jnp.int32, (T, T), 1)
        lane = lax.broadcasted_iota(jnp.int32, (T, LANES), 1)
        lower2 = _tri2(T, upper=False)
        qv = q_ref[...]
        zb = zb_ref[0:1, 0:1]

        def kv(kb):
            ks = pl.multiple_of(kb * T, T)
            return k_ref[pl.ds(ks, T), :], v_ref[pl.ds(ks, T), :]

        kd, vd = kv(i)
        k1, v1 = kv(jnp.maximum(i - 1, 0))
        diag = jj < ii
        prev = jnp.full((T, T), i > 0)
        _, sp0, zr0 = _sb_tile(qv, kd, lower2, diag)
        _, sp1, zr1 = _sb_tile(qv, k1, lower2, prev)
        c0 = -jnp.sum(sp0, axis=1, keepdims=True)
        acc = (_dot(_sb_weights(zr0, 0.0, diag).astype(BF16), vd)
               + _dot(_sb_weights(zr1, c0, prev).astype(BF16), v1))
        c = c0 - jnp.sum(sp1, axis=1, keepdims=True)
        c_ref[...] = jnp.where(lane == i - 1, c0, 0.0)

        def alive(c):
            return jnp.max(c + zb) > EXP_ZERO

        def cond(carry):
            kb, _, _, live = carry
            return (kb >= 0) & live

        def step(carry):
            kb, c, acc, _ = carry
            kt, vt = kv(kb)
            _, sp, zr = _sb_tile(qv, kt, lower2)
            acc = acc + _dot(_sb_weights(zr, c).astype(BF16), vt)
            c_ref[...] = jnp.where(lane == kb, c, c_ref[...])
            c = c - jnp.sum(sp, axis=1, keepdims=True)
            return kb - 1, c, acc, alive(c)

        kb, _, acc, _ = lax.while_loop(cond, step, (i - 2, c, acc, alive(c)))
        o_ref[...] = acc
        kf_ref[...] = jnp.zeros_like(kf_ref) + (kb + 1).astype(F32)

    blk = pl.BlockSpec((None, T, dh), lambda h, i: (h, i, 0))
    cblk = pl.BlockSpec((None, T, LANES), lambda h, i: (h, i, 0))
    whole = pl.BlockSpec((None, LP, dh), lambda h, i: (h, 0, 0))
    return pl.pallas_call(
        body, name=name, grid=(H, nq), in_specs=[blk, whole, whole, pl.BlockSpec((1, LANES), lambda h, i: (0, 0))],
        out_specs=[blk, cblk, pl.BlockSpec((None, None, 8, LANES), lambda h, i: (h, i, 0, 0))],
        out_shape=[jax.ShapeDtypeStruct((H, LP, dh), F32), jax.ShapeDtypeStruct((H, LP, LANES), F32),
                   jax.ShapeDtypeStruct((H, nq, 8, LANES), F32)],
        compiler_params=_cp("parallel", "parallel"),
    )(q, k, v, zmax)


def _sb_bwd(kstart, q, k, v, cmat, do, *, name):
    H, LP, dh = q.shape
    T = ATT_BLOCK
    nq = LP // T

    def body(ks_ref, q_ref, k_ref, v_ref, c_ref, do_ref, dq_ref, dk_ref, dv_ref):
        h = pl.program_id(0)
        i = pl.program_id(1)

        @pl.when(i == 0)
        def _():
            dk_ref[...] = jnp.zeros_like(dk_ref)
            dv_ref[...] = jnp.zeros_like(dv_ref)

        ii = lax.broadcasted_iota(jnp.int32, (T, T), 0)
        jj = lax.broadcasted_iota(jnp.int32, (T, T), 1)
        lane = lax.broadcasted_iota(jnp.int32, (T, LANES), 1)
        lower2 = _tri2(T, upper=False)
        upper2 = _tri2(T, upper=True)
        qv = q_ref[...]
        dob = do_ref[...].astype(BF16)
        cm = c_ref[...]

        def front(kb, valid=None):
            ks = pl.multiple_of(kb * T, T)
            k_blk = k_ref[pl.ds(ks, T), :]
            c = jnp.sum(jnp.where(lane == kb, cm, 0.0), axis=1, keepdims=True)
            z, sp, zr = _sb_tile(qv, k_blk, lower2, valid)
            w = _sb_weights(zr, c, valid)
            gw_ = w * _dot_nt(dob, v_ref[pl.ds(ks, T), :])
            gin = _dot2_right(gw_, upper2)
            return ks, k_blk, w, gw_, gin, jnp.exp(z - sp)

        def back(t, cg, dq, valid=None):
            ks, k_blk, w, gw_, gin, sig = t
            dz = gw_ - sig * (cg + gin)
            if valid is not None:
                dz = jnp.where(valid, dz, 0.0)
            dz = dz.astype(BF16)
            dk_ref[pl.ds(ks, T), :] += _dot_tn(dz, qv)
            dv_ref[pl.ds(ks, T), :] += _dot_tn(w.astype(BF16), dob)
            return cg + jnp.sum(gw_, axis=1, keepdims=True), dq + _dot(dz, k_blk)

        cg, dq = lax.fori_loop(ks_ref[h, i], i - 1, lambda kb, cr: back(front(kb), *cr),
                               (jnp.zeros((T, 1), F32), jnp.zeros((T, dh), F32)))
        diag = jj < ii
        prev = jnp.full((T, T), i > 0)
        t1 = front(jnp.maximum(i - 1, 0), prev)
        t0 = front(i, diag)
        cg, dq = back(t1, cg, dq, prev)
        _, dq = back(t0, cg, dq, diag)
        dq_ref[...] = dq

    blk = pl.BlockSpec((None, T, dh), lambda h, i, ks: (h, i, 0))
    cblk = pl.BlockSpec((None, T, LANES), lambda h, i, ks: (h, i, 0))
    whole = pl.BlockSpec((None, LP, dh), lambda h, i, ks: (h, 0, 0))
    sh = jax.ShapeDtypeStruct((H, LP, dh), F32)
    return pl.pallas_call(
        body, name=name,
        grid_spec=pltpu.PrefetchScalarGridSpec(
            num_scalar_prefetch=1, grid=(H, nq), in_specs=[blk, whole, whole, cblk, blk], out_specs=[blk, whole, whole]),
        out_shape=[sh, sh, sh], compiler_params=_cp("parallel", "arbitrary"),
    )(kstart, q, k, v, cmat, do)


def _loss_head(h, tgt, seq, *, name):
    LP, D = h.shape
    tr = _tile(LP, (384, 256, 128))

    def body(h_ref, t_ref, dh_ref, l_ref):
        i = pl.program_id(0)

        @pl.when(i == 0)
        def _():
            l_ref[...] = jnp.zeros_like(l_ref)

        row = lax.broadcasted_iota(jnp.int32, (tr, D), 0) + i * tr
        e = jnp.where((row >= N_META) & (row < N_META + seq), h_ref[...] - t_ref[...], 0.0)
        dh_ref[...] = e * (1.0 / D)
        l_ref[...] += jnp.sum(e * e, axis=0, keepdims=True) * (0.5 / D)

    row = pl.BlockSpec((tr, D), lambda i: (i, 0))
    return pl.pallas_call(
        body, name=name, grid=(LP // tr,), in_specs=[row, row], out_specs=[row, pl.BlockSpec((1, D), lambda i: (0, 0))],
        out_shape=[jax.ShapeDtypeStruct((LP, D), F32), jax.ShapeDtypeStruct((1, D), F32)],
        compiler_params=_cp("arbitrary"),
    )(h, tgt)


def _adamw(w, g, m, v, *, name):
    shape = w.shape
    C = shape[-1]
    R = math.prod(shape) // C
    tr = _tile(R, (512, 256, 128, 64, 32, 16, 8))
    c1 = 1.0 / (1.0 - ADAM_B1 ** ADAM_STEP)
    c2 = 1.0 / (1.0 - ADAM_B2 ** ADAM_STEP)

    def body(w_ref, g_ref, m_ref, v_ref, d_ref, nm_ref, nv_ref):
        gv = g_ref[...]
        nm = ADAM_B1 * m_ref[...] + (1.0 - ADAM_B1) * gv
        nv = ADAM_B2 * v_ref[...] + (1.0 - ADAM_B2) * (gv * gv)
        d_ref[...] = -ADAM_LR * ((nm * c1) / (jnp.sqrt(nv * c2) + ADAM_EPS) + ADAM_WD * w_ref[...])
        nm_ref[...] = nm
        nv_ref[...] = nv

    blk = pl.BlockSpec((tr, C), lambda i: (i, 0))
    sh = jax.ShapeDtypeStruct((R, C), F32)
    d, nm, nv = pl.pallas_call(
        body, name=name, grid=(R // tr,), in_specs=[blk] * 4, out_specs=[blk] * 3, out_shape=[sh] * 3,
        compiler_params=_cp("parallel"),
    )(w.reshape(R, C), g.reshape(R, C), m.reshape(R, C), v.reshape(R, C))
    return d.reshape(shape), nm.reshape(shape), nv.reshape(shape)


def _sum_rows(buf, *, name):
    n, R, C = buf.shape
    tr = _tile(R, (512, 256, 128, 64, 32, 16, 8))

    def body(b_ref, o_ref):
        acc = b_ref[0].astype(F32)
        for k in range(1, n):
            acc = acc + b_ref[k].astype(F32)
        o_ref[...] = acc

    return pl.pallas_call(
        body, name=name, grid=(R // tr,), in_specs=[pl.BlockSpec((n, tr, C), lambda i: (0, i, 0))],
        out_specs=pl.BlockSpec((tr, C), lambda i: (i, 0)), out_shape=jax.ShapeDtypeStruct((R, C), F32),
        compiler_params=_cp("parallel"),
    )(buf)


def _interleave(buf, *, name):
    n, R, C = buf.shape
    tr = _tile(R, (256, 128, 64, 32, 16))

    def body(b_ref, o_ref):
        for d in range(n):
            o_ref[:, d * C:(d + 1) * C] = b_ref[d]

    return pl.pallas_call(
        body, name=name, grid=(R // tr,), in_specs=[pl.BlockSpec((n, tr, C), lambda i: (0, i, 0))],
        out_specs=pl.BlockSpec((tr, n * C), lambda i: (i, 0)), out_shape=jax.ShapeDtypeStruct((R, n * C), buf.dtype),
        compiler_params=_cp("parallel"),
    )(buf)


def _deinterleave(x, *, out_dtype, name):
    R, NC = x.shape
    C = NC // N_DEV
    tr = _tile(R, (256, 128, 64, 32, 16))

    def body(x_ref, o_ref):
        for d in range(N_DEV):
            o_ref[d] = x_ref[:, d * C:(d + 1) * C].astype(out_dtype)

    return pl.pallas_call(
        body, name=name, grid=(R // tr,), in_specs=[pl.BlockSpec((tr, NC), lambda i: (i, 0))],
        out_specs=pl.BlockSpec((N_DEV, tr, C), lambda i: (0, i, 0)),
        out_shape=jax.ShapeDtypeStruct((N_DEV, R, C), out_dtype), compiler_params=_cp("parallel"),
    )(x)


def _mesh_pos():
    x, y, c = lax.axis_index("x"), lax.axis_index("y"), lax.axis_index("c")
    return x, y, c, 4 * x + 2 * y + c


def _peer(x, y, c, f):
    px, py, pc = (x + ((f >> 2) & 1)) % 2, (y + ((f >> 1) & 1)) % 2, (c + (f & 1)) % 2
    return (px, py, pc), 4 * px + 2 * py + pc


def _exchange(arrs, *, scatter, name):
    n = len(arrs)
    hbm = pl.BlockSpec(memory_space=pl.ANY)

    def body(*refs):
        xa = (refs[:n], refs[n:2 * n], *refs[2 * n:], scatter)
        _xchg_start(*xa)
        _xchg_wait(*xa)

    return list(pl.pallas_call(
        body, name=name, in_specs=[hbm] * n, out_specs=[hbm] * n, out_shape=_xchg_shapes(arrs, scatter),
        scratch_shapes=_xchg_scratch(n), compiler_params=pltpu.CompilerParams(has_side_effects=True),
    )(*arrs))


def _shard_rows(t):
    return t.reshape(N_DEV, t.shape[0] // N_DEV, t.shape[1])


def _ffn_fwd(h, g, W, l, s, xq):
    tag = f"{l}{s}"
    n = _rms_fwd(h, g, name=f"ffn_norm_{tag}")
    w1, w3 = W[("w13", l, s)]
    a, b, sw = _ffn_up(n, w1, w3, name=f"ffn_up_{tag}", xq=xq)
    h2 = _matmul([(sw, W[("w2", l, s)])], res=h, alpha=FFN_RES, name=f"ffn_down_{tag}", xq=xq)
    return h2, (h, n, a, b, sw)


def _ffn_bwd(dh, saved, g, W, l, s, xq, emit):
    tag = f"{l}{s}"
    h, n, a, b, sw = saved
    (w1, w3), w2 = W[("w13", l, s)], W[("w2", l, s)]
    da, db = _ffn_mid_bwd(dh, w2, a, b, name=f"ffn_mid_bwd_{tag}", xq=xq)
    dw2 = _matmul([(sw, dh)], trans_a=True, alpha=FFN_RES, out_dtype=BF16, name=f"ffn_dw2_{tag}", xq=xq)
    emit(("w2", l, s), [_shard_rows(dw2)])
    dn = _matmul([(da, w1), (db, w3)], trans_b=True, name=f"ffn_dn_{tag}", xq=xq)
    dw1 = _matmul([(n, da)], trans_a=True, shards=N_DEV, out_dtype=BF16, name=f"ffn_dw1_{tag}", xq=xq)
    dw3 = _matmul([(n, db)], trans_a=True, shards=N_DEV, out_dtype=BF16, name=f"ffn_dw3_{tag}", xq=xq)
    emit(("w13", l, s), [dw1, dw3])
    dh_in, dg = _rms_bwd(h, g, dn, res=dh, name=f"ffn_norm_bwd_{tag}")
    return dh_in, dg


def _local_step(x, tgt, W, xq=None, recv=None):
    G = {}

    def emit(key, arrs):
        G[key] = arrs
        if xq is not None:
            xq.push(arrs, True, lambda res, key=key: recv.__setitem__(key, res))

    seq, D = x.shape
    L = N_META + seq
    LP = -(-L // ROW_ALIGN) * ROW_ALIGN
    pad = LP - L
    H_sb = D // HEAD_DIM
    DI = W["ssm_norm_g"].shape[-1]
    H_ssm = DI // HEAD_DIM
    CONV = DI + 2 * SSM_GROUPS * D_STATE
    ZX = DI + CONV

    h0 = jnp.concatenate([W["meta_tokens"], x, jnp.zeros((pad, D), F32)], axis=0)
    tgt_p = jnp.pad(tgt, ((N_META, pad), (0, 0)))
    ng = W["norm_g"]

    h1, sv_f00 = _ffn_fwd(h0, ng[0, 0], W, 0, 0, xq)
    u0 = _rms_fwd(h1, ng[0, 1], name="ssm_norm")
    w_in = W["ssm_in_proj"][0]
    w_zx = w_in[:, :ZX]
    w_dt = jnp.pad(w_in[:, ZX:], ((0, 0), (0, 128 - H_ssm)))
    zx = _matmul([(u0, w_zx)], name="ssm_in_zx", xq=xq)
    dtr = _matmul([(u0, w_dt)], name="ssm_in_dt")
    conv_w, conv_b = W["ssm_conv_w"][0], W["ssm_conv_b"][0]
    xbc = _conv_fwd(zx, conv_w, conv_b, DI, name="ssm_conv", xq=xq)
    dtrt = dtr[:, :H_ssm].T
    padh = lambda t: jnp.pad(t.reshape(1, H_ssm), ((0, 0), (0, 128 - H_ssm)))
    brow, bcol = padh(W["ssm_dt_bias"][0]), W["ssm_dt_bias"][0].reshape(H_ssm, 1)
    alrow, alcol = padh(W["ssm_a_log"][0]), W["ssm_a_log"][0].reshape(H_ssm, 1)
    dvec = jnp.repeat(W["ssm_d"][0], HEAD_DIM).reshape(1, DI)
    ssm_args = (xbc, dtr, dtrt, brow, bcol, alrow, alcol, dvec)
    y, states = _ssd_fwd(*ssm_args, name="ssd_fwd", xq=xq)
    sng = W["ssm_norm_g"].reshape(1, DI)
    yn = _gate_fwd(y, zx, sng, name="ssm_gate")
    w_out = W["ssm_out_proj"][0]
    h2 = _matmul([(yn, w_out)], res=h1, name="ssm_out", xq=xq)
    h3, sv_f01 = _ffn_fwd(h2, ng[0, 2], W, 0, 1, xq)

    kv_in = _rms_fwd(h3, W["kv_norm_g"], name="kv_norm")
    kraw = _matmul([(kv_in, W["w_k"])], shards=H_sb, name="kv_k")
    vh = _matmul([(kv_in, W["w_v"])], shards=H_sb, out_dtype=BF16, name="kv_v")
    kh = _rms_fwd(kraw.reshape(H_sb * LP, HEAD_DIM), W["k_norm_g"], name="k_headnorm").reshape(H_sb, LP, HEAD_DIM)

    h4, sv_f10 = _ffn_fwd(h3, ng[1, 0], W, 1, 0, xq)
    u1 = _rms_fwd(h4, ng[1, 1], name="sb_norm")
    qraw = _matmul([(u1, W["sb_w_q"][0])], shards=H_sb, name="sb_q")
    scale = HEAD_DIM ** -0.5
    qh = _rms_fwd(qraw.reshape(H_sb * LP, HEAD_DIM), W["sb_q_norm_g"][0], scale=scale,
                  name="q_headnorm").reshape(H_sb, LP, HEAD_DIM)
    zmax = 1.02 * math.sqrt(HEAD_DIM) * jnp.max(jnp.abs(W["sb_q_norm_g"])) * jnp.max(jnp.abs(W["k_norm_g"]))
    o, cmat, kfirst = _sb_fwd(qh, kh, vh, jnp.full((1, LANES), zmax, F32), name="sb_fwd")
    kstart = kfirst[:, :, 0, 0].astype(jnp.int32)
    h5 = _matmul([(o, W["sb_w_o"][0])], res=h4, name="sb_out")
    h6, sv_f11 = _ffn_fwd(h5, ng[1, 2], W, 1, 1, xq)

    dh, lvec = _loss_head(h6, tgt_p, seq, name="loss_head")
    loss = jnp.sum(lvec)
    dng = [[None] * 3 for _ in range(2)]
    shard_rows = _shard_rows

    dh, dng[1][2] = _ffn_bwd(dh, sv_f11, ng[1, 2], W, 1, 1, xq, emit)
    g_wo = shard_rows(_matmul([(o, dh)], trans_a=True, out_dtype=BF16, name="sb_dwo"))
    do = _matmul([(dh, W["sb_w_o"][0])], trans_b=True, shards=H_sb, name="sb_do")
    dq, dk, dv = _sb_bwd(kstart, qh, kh, vh, cmat, do, name="sb_bwd")
    dqraw, dqg = _rms_bwd(qraw.reshape(H_sb * LP, HEAD_DIM), W["sb_q_norm_g"][0], dq.reshape(H_sb * LP, HEAD_DIM),
                          alpha=scale, name="q_headnorm_bwd")
    G["sb_q_norm_g"] = dqg
    dqraw = dqraw.reshape(H_sb, LP, HEAD_DIM)
    g_wq = shard_rows(_matmul([(u1, dqraw)], trans_a=True, out_dtype=BF16, name="sb_dwq"))
    emit("sb", [g_wq, g_wo])
    du1 = _matmul([(dqraw, W["sb_w_q"][0])], trans_b=True, name="sb_du")
    dh, dng[1][1] = _rms_bwd(h4, ng[1, 1], du1, res=dh, name="sb_norm_bwd")
    dh, dng[1][0] = _ffn_bwd(dh, sv_f10, ng[1, 0], W, 1, 0, xq, emit)

    dkraw, dkg = _rms_bwd(kraw.reshape(H_sb * LP, HEAD_DIM), W["k_norm_g"], dk.reshape(H_sb * LP, HEAD_DIM),
                          name="k_headnorm_bwd")
    G["k_norm_g"] = dkg.reshape(-1)
    dkraw = dkraw.reshape(H_sb, LP, HEAD_DIM)
    dvf = dv
    g_wk = shard_rows(_matmul([(kv_in, dkraw)], trans_a=True, out_dtype=BF16, name="kv_dwk"))
    g_wv = shard_rows(_matmul([(kv_in, dvf)], trans_a=True, out_dtype=BF16, name="kv_dwv"))
    emit("kv", [g_wk, g_wv])
    dkv = _matmul([(dkraw, W["w_k"]), (dvf, W["w_v"])], trans_b=True, name="kv_din", xq=xq)
    dh, dkvg = _rms_bwd(h3, W["kv_norm_g"], dkv, res=dh, name="kv_norm_bwd")
    G["kv_norm_g"] = dkvg.reshape(-1)

    dh, dng[0][2] = _ffn_bwd(dh, sv_f01, ng[0, 2], W, 0, 1, xq, emit)
    emit("wout", [shard_rows(_matmul([(yn, dh)], trans_a=True, out_dtype=BF16, name="ssm_dwout", xq=xq))])
    dyn = _matmul([(dh, w_out)], trans_b=True, name="ssm_dyn", xq=xq)
    dy, dz, dsng = _gate_bwd(y, zx, sng, dyn, name="ssm_gate_bwd")
    G["ssm_norm_g"] = dsng
    dxbc, ddtr, dbias, dalog, ddvec = _ssd_bwd(*ssm_args, dy, states, name="ssd_bwd", xq=xq)
    G["ssm_dt_bias"] = dbias[:, :H_ssm]
    G["ssm_a_log"] = dalog[:, :H_ssm]
    G["ssm_d"] = jnp.sum(ddvec.reshape(H_ssm, HEAD_DIM), axis=1).reshape(1, H_ssm)
    gpre, dcw, dcb = _conv_bwd_g(zx, conv_w, conv_b, dxbc, DI, name="ssm_conv_bwd_g", xq=xq)
    G["ssm_conv_w"] = dcw[:D_CONV][None]
    G["ssm_conv_b"] = dcb
    dxbc_pre = _conv_bwd_u(gpre, conv_w, name="ssm_conv_bwd_u", xq=xq)
    emit("win", [_deinterleave(jnp.concatenate([
        _matmul([(u0, dz)], trans_a=True, out_dtype=BF16, name="ssm_dwin_z", xq=xq),
        _matmul([(u0, dxbc_pre)], trans_a=True, out_dtype=BF16, name="ssm_dwin_x", xq=xq),
        _matmul([(u0, ddtr)], trans_a=True, out_dtype=BF16, name="ssm_dwin_dt")[:, :H_ssm]], axis=1),
        out_dtype=BF16, name="ssm_dwin_shards")])
    du0 = _matmul([(dz, w_zx)], trans_b=True, name="ssm_du_z", xq=xq)
    du0 = _matmul([(dxbc_pre, w_zx)], trans_b=True, b_off=DI, res=du0, name="ssm_du_x", xq=xq)
    du0 = _matmul([(ddtr, w_dt)], trans_b=True, res=du0, name="ssm_du_dt")
    dh, dng[0][1] = _rms_bwd(h1, ng[0, 1], du0, res=dh, name="ssm_norm_bwd")
    dh, dng[0][0] = _ffn_bwd(dh, sv_f00, ng[0, 0], W, 0, 0, xq, emit)

    G["norm_g"] = jnp.stack([jnp.concatenate(r, axis=0) for r in dng])
    G["meta_tokens"] = dh[:N_META]
    return loss, dh[N_META:L], G


WEIGHTS = ['meta_tokens', 'norm_g', 'ffn_w1', 'ffn_w3', 'ffn_w2', 'ssm_in_proj', 'ssm_conv_w', 'ssm_conv_b',
           'ssm_dt_bias', 'ssm_a_log', 'ssm_d', 'ssm_norm_g', 'ssm_out_proj', 'kv_norm_g', 'w_k', 'k_norm_g', 'w_v',
           'sb_w_q', 'sb_q_norm_g', 'sb_w_o']
SHARD_AXIS = {'meta_tokens': 1, 'norm_g': 2, 'ffn_w1': 3, 'ffn_w3': 3, 'ffn_w2': 2, 'ssm_in_proj': 2, 'ssm_conv_w': 2,
              'ssm_conv_b': 1, 'ssm_dt_bias': None, 'ssm_a_log': None, 'ssm_d': None, 'ssm_norm_g': 1,
              'ssm_out_proj': 1, 'kv_norm_g': None, 'w_k': 0, 'k_norm_g': None, 'w_v': 0, 'sb_w_q': 1,
              'sb_q_norm_g': None, 'sb_w_o': 1}
MATMUL_WEIGHTS = ('ffn_w1', 'ffn_w3', 'ffn_w2', 'ssm_in_proj', 'ssm_out_proj', 'w_k', 'w_v', 'sb_w_q', 'sb_w_o')
PACK_ROWS = 16


def _pack(arrs, dtype):
    flat = jnp.concatenate([a.reshape(-1).astype(dtype) for a in arrs])
    n = flat.shape[0]
    npad = -(-n // (LANES * PACK_ROWS)) * (LANES * PACK_ROWS)
    return jnp.pad(flat, (0, npad - n)).reshape(npad // LANES, LANES)


def _unpack_gathered(buf, names, shard_shapes, dtype):
    flat = buf.reshape(N_DEV, -1)
    out, off = {}, 0
    for n in names:
        shp = shard_shapes[n]
        size = math.prod(shp)
        t = flat[:, off:off + size].reshape((N_DEV,) + tuple(shp))
        off += size
        ax = SHARD_AXIS[n]
        t = jnp.moveaxis(t, 0, ax)
        full = shp[:ax] + (N_DEV * shp[ax],) + shp[ax + 1:]
        out[n] = t.reshape(full).astype(dtype)
    return out


def _to_shards(g, ax):
    shp = g.shape
    t = g.reshape(shp[:ax] + (N_DEV, shp[ax] // N_DEV) + shp[ax + 1:])
    return jnp.moveaxis(t, ax, 0).reshape(N_DEV, -1)


def kernel(x, meta_tokens, norm_g, ffn_w1, ffn_w3, ffn_w2, ssm_in_proj, ssm_conv_w, ssm_conv_b, ssm_dt_bias, ssm_a_log, ssm_d, ssm_norm_g, ssm_out_proj, kv_norm_g, w_k, k_norm_g, w_v, sb_w_q, sb_q_norm_g, sb_w_o, loss_target, m_meta_tokens, m_norm_g, m_ffn_w1, m_ffn_w3, m_ffn_w2, m_ssm_in_proj, m_ssm_conv_w, m_ssm_conv_b, m_ssm_dt_bias, m_ssm_a_log, m_ssm_d, m_ssm_norm_g, m_ssm_out_proj, m_kv_norm_g, m_w_k, m_k_norm_g, m_w_v, m_sb_w_q, m_sb_q_norm_g, m_sb_w_o, v_meta_tokens, v_norm_g, v_ffn_w1, v_ffn_w3, v_ffn_w2, v_ssm_in_proj, v_ssm_conv_w, v_ssm_conv_b, v_ssm_dt_bias, v_ssm_a_log, v_ssm_d, v_ssm_norm_g, v_ssm_out_proj, v_kv_norm_g, v_w_k, v_k_norm_g, v_w_v, v_sb_w_q, v_sb_q_norm_g, v_sb_w_o):
    shard = dict(meta_tokens=meta_tokens, norm_g=norm_g, ffn_w1=ffn_w1, ffn_w3=ffn_w3, ffn_w2=ffn_w2,
                 ssm_in_proj=ssm_in_proj, ssm_conv_w=ssm_conv_w, ssm_conv_b=ssm_conv_b, ssm_dt_bias=ssm_dt_bias,
                 ssm_a_log=ssm_a_log, ssm_d=ssm_d, ssm_norm_g=ssm_norm_g, ssm_out_proj=ssm_out_proj,
                 kv_norm_g=kv_norm_g, w_k=w_k, k_norm_g=k_norm_g, w_v=w_v, sb_w_q=sb_w_q, sb_q_norm_g=sb_q_norm_g,
                 sb_w_o=sb_w_o)
    mom_m = dict(zip(WEIGHTS, (m_meta_tokens, m_norm_g, m_ffn_w1, m_ffn_w3, m_ffn_w2, m_ssm_in_proj, m_ssm_conv_w,
                               m_ssm_conv_b, m_ssm_dt_bias, m_ssm_a_log, m_ssm_d, m_ssm_norm_g, m_ssm_out_proj,
                               m_kv_norm_g, m_w_k, m_k_norm_g, m_w_v, m_sb_w_q, m_sb_q_norm_g, m_sb_w_o)))
    mom_v = dict(zip(WEIGHTS, (v_meta_tokens, v_norm_g, v_ffn_w1, v_ffn_w3, v_ffn_w2, v_ssm_in_proj, v_ssm_conv_w,
                               v_ssm_conv_b, v_ssm_dt_bias, v_ssm_a_log, v_ssm_d, v_ssm_norm_g, v_ssm_out_proj,
                               v_kv_norm_g, v_w_k, v_k_norm_g, v_w_v, v_sb_w_q, v_sb_q_norm_g, v_sb_w_o)))
    sharded = [n for n in WEIGHTS if SHARD_AXIS[n] is not None]
    replicated = [n for n in WEIGHTS if SHARD_AXIS[n] is None]
    small = [n for n in sharded if n not in MATMUL_WEIGHTS]
    shapes = {n: tuple(shard[n].shape) for n in WEIGHTS}
    D = x.shape[-1]
    bf = lambda t: t.astype(BF16)
    full_rows = lambda t: t.reshape(N_DEV * t.shape[1], t.shape[2])
    cols = lambda l, s: jnp.concatenate([bf(ffn_w1[l, s]), bf(ffn_w3[l, s])], axis=0)
    xq, recv = _Queue(), {}
    W = _Weights(xq)
    W.update({n: shard[n] for n in replicated})

    def have_w13(l, s):
        def done(res):
            t = _interleave(res[0], name=f"weights_w13_{l}{s}")
            W[("w13", l, s)] = (t[:D], t[D:])
        return done

    def have_w2(l, s):
        return lambda res: W.__setitem__(("w2", l, s), full_rows(res[0]))

    def have_win(res):
        W["ssm_in_proj"] = _interleave(res[0], name="weights_win")[None]

    def have_rows(res):
        W["ssm_out_proj"] = full_rows(res[0])[None]
        W["w_k"], W["w_v"] = full_rows(res[1]), full_rows(res[2])
        W["sb_w_q"], W["sb_w_o"] = full_rows(res[3])[None], full_rows(res[4])[None]

    first = _exchange([cols(0, 0), bf(ffn_w2[0, 0]), _pack([shard[n] for n in small], F32)], scatter=False,
                      name="gather_first")
    have_w13(0, 0)(first[:1])
    have_w2(0, 0)(first[1:2])
    W.update(_unpack_gathered(first[2], small, shapes, F32))
    xq.push([bf(ssm_in_proj[0])], False, have_win)
    xq.push([bf(ffn_w2[0, 1])], False, have_w2(0, 1))
    xq.push([cols(0, 1)], False, have_w13(0, 1))
    xq.push([bf(ssm_out_proj[0]), bf(w_k), bf(w_v), bf(sb_w_q[0]), bf(sb_w_o[0])], False, have_rows)
    xq.push([cols(1, 0)], False, have_w13(1, 0))
    xq.push([bf(ffn_w2[1, 0])], False, have_w2(1, 0))
    xq.skip()
    xq.push([cols(1, 1)], False, have_w13(1, 1))
    xq.skip()
    xq.push([bf(ffn_w2[1, 1])], False, have_w2(1, 1))

    loss, dx, G = _local_step(x[0], loss_target[0], W, xq, recv)
    loss = lax.psum(loss, ("x", "y", "c"))

    send = jnp.concatenate([_to_shards(G[n], SHARD_AXIS[n]) for n in small], axis=1)
    n_el = send.shape[1]
    npad = -(-n_el // (LANES * PACK_ROWS)) * (LANES * PACK_ROWS)
    send = jnp.pad(send, ((0, 0), (0, npad - n_el))).reshape(N_DEV, npad // LANES, LANES)
    xq.push([send], True, lambda res: recv.__setitem__("small", res))
    xq.flush("scatter_last")
    rep = _pack([G[n] for n in replicated], F32)
    rep_sum = _sum_rows(_exchange([rep], scatter=False, name="gather_small_grads")[0], name="sum_small_grads").reshape(-1)
    summed = {k: [_sum_rows(t, name=f"sum_{'_'.join(map(str, k)) if isinstance(k, tuple) else k}_{i}")
                  for i, t in enumerate(v)] for k, v in recv.items()}

    grads, off = {}, 0
    small_sum = summed["small"][0].reshape(-1)
    for n in small:
        size = math.prod(shapes[n])
        grads[n] = small_sum[off:off + size].reshape(shapes[n])
        off += size
    off = 0
    for n in replicated:
        size = math.prod(shapes[n])
        grads[n] = rep_sum[off:off + size].reshape(shapes[n])
        off += size
    ls = [(l, s) for l in range(2) for s in range(2)]
    grads["ffn_w1"] = jnp.stack([summed[("w13", l, s)][0] for l, s in ls]).reshape(shapes["ffn_w1"])
    grads["ffn_w3"] = jnp.stack([summed[("w13", l, s)][1] for l, s in ls]).reshape(shapes["ffn_w3"])
    grads["ffn_w2"] = jnp.stack([summed[("w2", l, s)][0] for l, s in ls]).reshape(shapes["ffn_w2"])
    grads["ssm_in_proj"] = summed["win"][0][None]
    grads["ssm_out_proj"] = summed["wout"][0][None]
    grads["w_k"], grads["w_v"] = summed["kv"]
    grads["sb_w_q"], grads["sb_w_o"] = summed["sb"][0][None], summed["sb"][1][None]

    delta, new_m, new_v = {}, {}, {}
    for n in WEIGHTS:
        w2 = shard[n].reshape(1, -1) if shard[n].ndim == 1 else shard[n]
        r2 = lambda t: t.reshape(w2.shape)
        d, nm, nv = _adamw(w2, r2(grads[n]), r2(mom_m[n]), r2(mom_v[n]), name=f"adamw_{n}")
        delta[n], new_m[n], new_v[n] = (t.reshape(shapes[n]) for t in (d, nm, nv))

    return (loss, dx[None], *[grads[n] for n in WEIGHTS], *[delta[n] for n in WEIGHTS],
            *[new_m[n] for n in WEIGHTS], *[new_v[n] for n in WEIGHTS])
```

```python
import functools
import math

import jax
import jax.numpy as jnp
from jax import lax
from jax.experimental import pallas as pl
from jax.experimental.pallas import tpu as pltpu

F32 = jnp.float32
BF16 = jnp.bfloat16
RMS_EPS = 1e-6
N_META = 16
HEAD_DIM = 64
SSM_GROUPS = 8
D_STATE = 128
D_CONV = 4
FFN_RES = 0.5
ADAM_LR, ADAM_B1, ADAM_B2, ADAM_EPS, ADAM_WD, ADAM_STEP = 0.001, 0.9, 0.999, 1e-08, 0.01, 10
N_DEV = 8
SSD_CHUNK = 128
ATT_BLOCK = 256
ROW_ALIGN = 768
VMEM_LIMIT_V7X = 48 * 1024 * 1024
MATMUL_VMEM_BUDGET_V7X = 30 * 1024 * 1024
MESH = pl.DeviceIdType.MESH
LANES = 128


def _cp(*sem):
    return pltpu.CompilerParams(dimension_semantics=sem if sem else None, vmem_limit_bytes=VMEM_LIMIT_V7X)


def _tile(n, cands):
    for c in cands:
        if n % c == 0:
            return c
    return n


def _softplus(x):
    return jnp.maximum(x, 0.0) + jnp.log(1.0 + jnp.exp(-jnp.abs(x)))


def _sigmoid(x):
    return 1.0 / (1.0 + jnp.exp(-x))


def _split3(x):
    hi = x.astype(BF16)
    r1 = x - hi.astype(F32)
    mid = r1.astype(BF16)
    lo = (r1 - mid.astype(F32)).astype(BF16)
    return hi, mid, lo


def _dot(a, b):
    return jnp.dot(a, b, preferred_element_type=F32)


def _dot_nt(a, b):
    return lax.dot_general(a, b, (((1,), (1,)), ((), ())), preferred_element_type=F32)


def _dot_tn(a, b):
    return lax.dot_general(a, b, (((0,), (0,)), ((), ())), preferred_element_type=F32)


def _dot3_left(t_bf16, x):
    hi, mid, lo = _split3(x)
    return _dot(t_bf16, hi) + _dot(t_bf16, mid) + _dot(t_bf16, lo)


def _dot3_right(x, t_bf16):
    hi, mid, lo = _split3(x)
    return _dot(hi, t_bf16) + _dot(mid, t_bf16) + _dot(lo, t_bf16)


CARRIER_MIN_FLOP = 4e10


EXCHANGE_US_PER_MB = 94.0
MATMUL_TFLOPS = 650.0


class _Carry:
    def __init__(self, arrs, scatter, done):
        self.arrs, self.scatter, self.done = list(arrs), scatter, done
        per_peer = sum(math.prod(a.shape[1:] if scatter else a.shape) * a.dtype.itemsize for a in self.arrs)
        self.us = EXCHANGE_US_PER_MB * per_peer / 2 ** 20


class _Queue:
    def __init__(self):
        self.items = []

    def push(self, arrs, scatter, done):
        self.items.append(_Carry(arrs, scatter, done))

    def skip(self):
        self.items.append(None)

    def pop(self, kernel_us):
        if self.items and self.items[0] is None:
            self.items.pop(0)
            return None
        for k, it in enumerate(self.items):
            if it is None:
                break
            if not it.scatter or it.us <= 1.15 * kernel_us:
                return self.items.pop(k)
            if k == 0 and len(self.items) > 4:
                return self.items.pop(0)
        return None

    def flush(self, name):
        for k, it in enumerate([it for it in self.items if it is not None]):
            it.done(_exchange(it.arrs, scatter=it.scatter, name=f"{name}_{k}"))
        self.items = []


class _Weights(dict):
    def __init__(self, xq):
        super().__init__()
        self.xq, self.fetched = xq, 0

    def __missing__(self, key):
        while not dict.__contains__(self, key) and self.xq.items:
            it = self.xq.items.pop(0)
            if it is not None:
                it.done(_exchange(it.arrs, scatter=it.scatter, name=f"gather_now_{self.fetched}"))
                self.fetched += 1
        return dict.__getitem__(self, key)


ELEMENTWISE_US_PER_MB = 1.6


def _pop_for_rows(xq, rows, cols, us_per_mb=ELEMENTWISE_US_PER_MB):
    return xq.pop(us_per_mb * rows * cols * 4 / 2 ** 20) if xq is not None else None


def _xchg_shapes(arrs, scatter):
    return [jax.ShapeDtypeStruct((N_DEV,) + tuple(a.shape[1:] if scatter else a.shape), a.dtype) for a in arrs]


def _xchg_scratch(n):
    return [pltpu.SemaphoreType.DMA((n, N_DEV - 1)), pltpu.SemaphoreType.DMA((n, N_DEV - 1)),
            pltpu.SemaphoreType.DMA((n,))]


def _xchg_copies(srcs, dsts, send_sems, recv_sems, local_sems, scatter, with_recv):
    x, y, c, me = _mesh_pos()
    own, sends, recvs = [], [], []
    for a, (s, d) in enumerate(zip(srcs, dsts)):
        own.append(pltpu.make_async_copy(s.at[me] if scatter else s, d.at[me], local_sems.at[a]))
        for f in range(1, N_DEV):
            peer, pid = _peer(x, y, c, f)
            for row, lst in ((me, sends), (pid, recvs)) if with_recv else ((me, sends),):
                lst.append(pltpu.make_async_remote_copy(
                    src_ref=s.at[pid] if scatter else s, dst_ref=d.at[row], send_sem=send_sems.at[a, f - 1],
                    recv_sem=recv_sems.at[a, f - 1], device_id=peer, device_id_type=MESH))
    return own, sends, recvs


def _xchg_start(*a):
    own, sends, _ = _xchg_copies(*a, with_recv=False)
    for cp in own + sends:
        cp.start()


def _xchg_wait(*a):
    own, sends, recvs = _xchg_copies(*a, with_recv=True)
    for snd, rcv in zip(sends, recvs):
        snd.wait_send()
        rcv.wait_recv()
    for cp in own:
        cp.wait()


def _call(body, *, name, grid, in_specs, out_specs, out_shape, scratch_shapes=(), sem, args, carry=None):
    n_in, n_out, n_scr = len(in_specs), len(out_specs), len(scratch_shapes)
    if carry is None:
        return pl.pallas_call(
            body, name=name, grid=grid, in_specs=list(in_specs), out_specs=list(out_specs), out_shape=list(out_shape),
            scratch_shapes=list(scratch_shapes), compiler_params=_cp(*sem))(*args)
    n = len(carry.arrs)
    hbm = pl.BlockSpec(memory_space=pl.ANY)

    def wrapped(*refs):
        ins, csrc = refs[:n_in], refs[n_in:n_in + n]
        outs, cdst = refs[n_in + n:n_in + n + n_out], refs[n_in + n + n_out:n_in + 2 * n + n_out]
        scr = refs[n_in + 2 * n + n_out:]
        xa = (csrc, cdst, *scr[n_scr:], carry.scatter)
        pid = [pl.program_id(a) for a in range(len(grid))]
        first = functools.reduce(jnp.logical_and, [p == 0 for p in pid])
        last = functools.reduce(jnp.logical_and, [p == g - 1 for p, g in zip(pid, grid)])

        @pl.when(first)
        def _():
            _xchg_start(*xa)

        body(*ins, *outs, *scr[:n_scr])

        @pl.when(last)
        def _():
            _xchg_wait(*xa)

    res = pl.pallas_call(
        wrapped, name=name, grid=grid, in_specs=list(in_specs) + [hbm] * n, out_specs=list(out_specs) + [hbm] * n,
        out_shape=list(out_shape) + _xchg_shapes(carry.arrs, carry.scatter),
        scratch_shapes=list(scratch_shapes) + _xchg_scratch(n), compiler_params=_cp(*["arbitrary"] * len(grid)),
    )(*args, *carry.arrs)
    carry.done(list(res[n_out:]))
    return list(res[:n_out])


def _matmul(pairs, *, name, out_dtype=F32, trans_a=False, trans_b=False, b_off=0, res=None, alpha=1.0, shards=None,
            xq=None, tm=None, tn=None, tk=None):
    a0, b0 = pairs[0]
    ha, hb = a0.ndim == 3, b0.ndim == 3
    shape2 = lambda t: (t.shape[1], t.shape[0] * t.shape[2]) if t.ndim == 3 else t.shape
    if trans_a:
        K, M = shape2(a0)
    else:
        M, K = shape2(a0)
    N = b0.shape[0] if trans_b else shape2(b0)[1]
    npair = len(pairs)
    has_res = res is not None
    tm = tm or (M if (ha and trans_a) else _tile(M, (768, 512, 1408, 384, 256, 128)))
    tk = tk or (K if (ha and not trans_a) else _tile(K, (1024, 1408, 768, 512, 256, 128)))
    if tn is None:
        sa, sb, so = a0.dtype.itemsize, b0.dtype.itemsize, jnp.dtype(out_dtype).itemsize
        for tn in ([N] if (shards or hb) else
                   [c for c in (2048, 1536, 1408, 1024, 512, 384, 256, 128) if N % c == 0] or [N]):
            if (2 * npair * tk * (tm * sa + tn * sb) + tm * tn * (4 + 2 * so + (8 if has_res else 0))
                    <= MATMUL_VMEM_BUDGET_V7X):
                break
    nk = K // tk
    cs = N // shards if shards else None
    n_in = 2 * npair + (1 if has_res else 0)
    a_blk = (tk, tm) if trans_a else (tm, tk)

    def body(*refs):
        o_ref, acc, flat = refs[n_in], refs[n_in + 1], list(refs[n_in + 2:])
        k = pl.program_id(2)

        @pl.when(k == 0)
        def _():
            acc[...] = jnp.zeros_like(acc)

        def operand(ref, heads):
            if not heads:
                return ref[...].astype(BF16)
            buf = flat.pop(0)
            for h in range(ref.shape[0]):
                buf[:, h * HEAD_DIM:(h + 1) * HEAD_DIM] = ref[h].astype(BF16)
            return buf[...]

        part = None
        for p in range(npair):
            a = operand(refs[2 * p], ha)
            b = operand(refs[2 * p + 1], hb)
            d = _dot_tn(a, b) if trans_a else _dot_nt(a, b) if trans_b else _dot(a, b)
            part = d if part is None else part + d
        acc[...] += part

        @pl.when(k == nk - 1)
        def _():
            if shards:
                for d in range(shards):
                    v = acc[:, d * cs:(d + 1) * cs]
                    o_ref[d] = (v * alpha if alpha != 1.0 else v).astype(out_dtype)
                return
            v = acc[...]
            if alpha != 1.0:
                v = v * alpha
            if has_res:
                v = refs[2 * npair][...] + v
            o_ref[...] = v.astype(out_dtype)

    if ha:
        assert (tm == M) if trans_a else (tk == K)
        a_spec = pl.BlockSpec((a0.shape[0], a_blk[0], HEAD_DIM),
                              (lambda i, j, k: (0, k, 0)) if trans_a else (lambda i, j, k: (0, i, 0)))
    elif trans_a:
        a_spec = pl.BlockSpec((tk, tm), lambda i, j, k: (k, i))
    else:
        a_spec = pl.BlockSpec((tm, tk), lambda i, j, k: (i, k))
    if hb:
        assert not trans_b and tn == N
        b_spec = pl.BlockSpec((b0.shape[0], tk, HEAD_DIM), lambda i, j, k: (0, k, 0))
    elif trans_b:
        assert not trans_a and b_off % tk == 0
        b_spec = pl.BlockSpec((tn, tk), lambda i, j, k: (j, k + b_off // tk))
    else:
        b_spec = pl.BlockSpec((tk, tn), lambda i, j, k: (k, j))
    flat_scratch = [pltpu.VMEM(a_blk, BF16)] * (npair if ha else 0) + [pltpu.VMEM((tk, tn), BF16)] * (npair if hb else 0)
    if shards:
        assert not has_res and tn == N
        o_spec = pl.BlockSpec((shards, tm, cs), lambda i, j, k: (0, i, 0))
        out_shape = jax.ShapeDtypeStruct((shards, M, cs), out_dtype)
    else:
        o_spec = pl.BlockSpec((tm, tn), lambda i, j, k: (i, j))
        out_shape = jax.ShapeDtypeStruct((M, N), out_dtype)
    in_specs, args = [], []
    for a, b in pairs:
        in_specs += [a_spec, b_spec]
        args += [a, b]
    if has_res:
        in_specs.append(o_spec)
        args.append(res)
    flop = 2.0 * npair * M * N * K
    carry = xq.pop(flop / MATMUL_TFLOPS / 1e6) if (xq is not None and flop >= CARRIER_MIN_FLOP) else None
    return _call(body, name=name, grid=(M // tm, N // tn, nk), in_specs=in_specs, out_specs=[o_spec],
                 out_shape=[out_shape], scratch_shapes=[pltpu.VMEM((tm, tn), F32)] + flat_scratch,
                 sem=("parallel", "parallel", "arbitrary"), args=args, carry=carry)[0]


def _rms_fwd(h, g, *, name, scale=1.0):
    R, D = h.shape
    tr = _tile(R, (2048, 1024, 768, 512, 256, 128)) if D <= 128 else _tile(R, (384, 256, 128))

    def body(h_ref, g_ref, o_ref):
        x = h_ref[...]
        r = lax.rsqrt(jnp.mean(x * x, axis=1, keepdims=True) + RMS_EPS)
        y = x * r * g_ref[...]
        if scale != 1.0:
            y = y * scale
        o_ref[...] = y.astype(BF16)

    return pl.pallas_call(
        body, name=name, grid=(R // tr,),
        in_specs=[pl.BlockSpec((tr, D), lambda i: (i, 0)), pl.BlockSpec((1, D), lambda i: (0, 0))],
        out_specs=pl.BlockSpec((tr, D), lambda i: (i, 0)),
        out_shape=jax.ShapeDtypeStruct((R, D), BF16), compiler_params=_cp("parallel"),
    )(h, g.reshape(1, D))


def _rms_bwd(h, g, dn, res=None, *, name, alpha=1.0):
    R, D = h.shape
    tr = _tile(R, (2048, 1024, 768, 512, 256, 128)) if D <= 128 else _tile(R, (384, 256, 128))
    has_res = res is not None

    def body(*refs):
        h_ref, g_ref, dn_ref = refs[:3]
        dh_ref, dg_ref = refs[-2], refs[-1]
        i = pl.program_id(0)

        @pl.when(i == 0)
        def _():
            dg_ref[...] = jnp.zeros_like(dg_ref)

        x = h_ref[...]
        r = lax.rsqrt(jnp.mean(x * x, axis=1, keepdims=True) + RMS_EPS)
        xh = x * r
        d = dn_ref[...].astype(F32)
        if alpha != 1.0:
            d = d * alpha
        dng = d * g_ref[...]
        m = jnp.mean(dng * xh, axis=1, keepdims=True)
        dh = r * (dng - xh * m)
        if has_res:
            dh = dh + refs[3][...]
        dh_ref[...] = dh
        dg_ref[...] += jnp.sum(d * xh, axis=0, keepdims=True)

    row = pl.BlockSpec((tr, D), lambda i: (i, 0))
    vec = pl.BlockSpec((1, D), lambda i: (0, 0))
    in_specs = [row, vec, row] + ([row] if has_res else [])
    args = [h, g.reshape(1, D), dn] + ([res] if has_res else [])
    return pl.pallas_call(
        body, name=name, grid=(R // tr,), in_specs=in_specs, out_specs=[row, vec],
        out_shape=[jax.ShapeDtypeStruct((R, D), F32), jax.ShapeDtypeStruct((1, D), F32)],
        compiler_params=_cp("arbitrary"),
    )(*args)


def _ffn_up(n, w1, w3, *, name, xq=None):
    M, K = n.shape
    N = w1.shape[1]
    tm = _tile(M, (384, 256, 128))
    tn = _tile(N, (1408, 512, 256, 128))

    def body(n_ref, w1_ref, w3_ref, da_ref, db_ref, s_ref):
        x = n_ref[...]
        a = _dot(x, w1_ref[...])
        b = _dot(x, w3_ref[...])
        sg = _sigmoid(a)
        silu = a * sg
        da_ref[...] = (b * sg * (1.0 + a * (1.0 - sg))).astype(BF16)
        db_ref[...] = silu.astype(BF16)
        s_ref[...] = (silu * b).astype(BF16)

    o_spec = pl.BlockSpec((tm, tn), lambda j, i: (i, j))
    w_spec = pl.BlockSpec((K, tn), lambda j, i: (0, j))
    sh = jax.ShapeDtypeStruct((M, N), BF16)
    return _call(body, name=name, grid=(N // tn, M // tm),
                 in_specs=[pl.BlockSpec((tm, K), lambda j, i: (i, 0)), w_spec, w_spec],
                 out_specs=[o_spec, o_spec, o_spec], out_shape=[sh, sh, sh], sem=("parallel", "parallel"),
                 args=(n, w1, w3), carry=xq.pop(4.0 * M * N * K / MATMUL_TFLOPS / 1e6) if xq is not None else None)


def _ffn_mid_bwd(dh, w2, s_a, s_b, *, name, xq=None):
    M, K = dh.shape
    N = w2.shape[0]
    tm = _tile(M, (384, 256, 128))
    tn = _tile(N, (1408, 512, 256, 128))

    def body(dh_ref, w_ref, a_ref, b_ref, da_ref, db_ref):
        ds = _dot_nt(dh_ref[...].astype(BF16), w_ref[...]) * FFN_RES
        da_ref[...] = (ds * a_ref[...].astype(F32)).astype(BF16)
        db_ref[...] = (ds * b_ref[...].astype(F32)).astype(BF16)

    o_spec = pl.BlockSpec((tm, tn), lambda j, i: (i, j))
    sh = jax.ShapeDtypeStruct((M, N), BF16)
    return _call(body, name=name, grid=(N // tn, M // tm),
                 in_specs=[pl.BlockSpec((tm, K), lambda j, i: (i, 0)), pl.BlockSpec((tn, K), lambda j, i: (j, 0)),
                           o_spec, o_spec],
                 out_specs=[o_spec, o_spec], out_shape=[sh, sh], sem=("parallel", "parallel"),
                 args=(dh, w2, s_a, s_b), carry=xq.pop(2.0 * M * N * K / MATMUL_TFLOPS / 1e6) if xq is not None else None)


def _conv_pre(xx, w_ref, b_ref, tr):
    acc = None
    for k in range(D_CONV):
        sh = D_CONV - 1 - k
        v = (pltpu.roll(xx, sh, 0) if sh else xx)[8:8 + tr]
        t = w_ref[k:k + 1, :] * v
        acc = t if acc is None else acc + t
    return acc + b_ref[...]


def _conv_fwd(zx, w, b, col_off, *, name, xq=None):
    LP = zx.shape[0]
    C = w.shape[1]
    tr = _tile(LP, (256, 128))
    tc = _tile(C, (512, 256, 128))
    co = col_off // tc

    def body(cur_ref, prev_ref, w_ref, b_ref, o_ref):
        i = pl.program_id(0)
        prev = jnp.where(i == 0, 0.0, prev_ref[...])
        pre = _conv_pre(jnp.concatenate([prev, cur_ref[...]], axis=0), w_ref, b_ref, tr)
        o_ref[...] = pre * _sigmoid(pre)

    return _call(
        body, name=name, grid=(LP // tr, C // tc),
        in_specs=[pl.BlockSpec((tr, tc), lambda i, j: (i, j + co)),
                  pl.BlockSpec((8, tc), lambda i, j: (jnp.maximum(i * (tr // 8) - 1, 0), j + co)),
                  pl.BlockSpec((D_CONV, tc), lambda i, j: (0, j)), pl.BlockSpec((1, tc), lambda i, j: (0, j))],
        out_specs=[pl.BlockSpec((tr, tc), lambda i, j: (i, j))],
        out_shape=[jax.ShapeDtypeStruct((LP, C), F32)], sem=("parallel", "parallel"),
        args=(zx, zx, w, b.reshape(1, C)), carry=_pop_for_rows(xq, LP, C))[0]


def _conv_bwd_g(zx, w, b, dact, col_off, *, name, xq=None):
    LP = zx.shape[0]
    C = w.shape[1]
    tr = _tile(LP, (256, 128))
    tc = _tile(C, (512, 256, 128))
    co = col_off // tc

    def body(cur_ref, prev_ref, w_ref, b_ref, d_ref, g_ref, dw_ref, db_ref):
        i = pl.program_id(1)

        @pl.when(i == 0)
        def _():
            dw_ref[...] = jnp.zeros_like(dw_ref)
            db_ref[...] = jnp.zeros_like(db_ref)

        prev = jnp.where(i == 0, 0.0, prev_ref[...])
        xx = jnp.concatenate([prev, cur_ref[...]], axis=0)
        pre = _conv_pre(xx, w_ref, b_ref, tr)
        sg = _sigmoid(pre)
        g = d_ref[...] * sg * (1.0 + pre * (1.0 - sg))
        g_ref[...] = g
        db_ref[...] += jnp.sum(g, axis=0, keepdims=True)
        rows = []
        for k in range(D_CONV):
            sh = D_CONV - 1 - k
            v = (pltpu.roll(xx, sh, 0) if sh else xx)[8:8 + tr]
            rows.append(jnp.sum(g * v, axis=0, keepdims=True))
        rows.append(jnp.zeros((8 - D_CONV, tc), F32))
        dw_ref[...] += jnp.concatenate(rows, axis=0)

    return _call(
        body, name=name, grid=(C // tc, LP // tr),
        in_specs=[pl.BlockSpec((tr, tc), lambda j, i: (i, j + co)),
                  pl.BlockSpec((8, tc), lambda j, i: (jnp.maximum(i * (tr // 8) - 1, 0), j + co)),
                  pl.BlockSpec((D_CONV, tc), lambda j, i: (0, j)), pl.BlockSpec((1, tc), lambda j, i: (0, j)),
                  pl.BlockSpec((tr, tc), lambda j, i: (i, j))],
        out_specs=[pl.BlockSpec((tr, tc), lambda j, i: (i, j)), pl.BlockSpec((8, tc), lambda j, i: (0, j)),
                   pl.BlockSpec((1, tc), lambda j, i: (0, j))],
        out_shape=[jax.ShapeDtypeStruct((LP, C), F32), jax.ShapeDtypeStruct((8, C), F32), jax.ShapeDtypeStruct((1, C), F32)],
        sem=("parallel", "arbitrary"), args=(zx, zx, w, b.reshape(1, C), dact), carry=_pop_for_rows(xq, LP, C, 2.0))


def _conv_bwd_u(g, w, *, name, xq=None):
    LP, C = g.shape
    tr = _tile(LP, (256, 128))
    tc = _tile(C, (512, 256, 128))
    nb = LP // tr

    def body(cur_ref, nxt_ref, w_ref, o_ref):
        i = pl.program_id(0)
        nxt = jnp.where(i == nb - 1, 0.0, nxt_ref[...])
        xx = jnp.concatenate([cur_ref[...], nxt], axis=0)
        acc = None
        for k in range(D_CONV):
            sh = D_CONV - 1 - k
            v = (pltpu.roll(xx, tr + 8 - sh, 0) if sh else xx)[:tr]
            t = w_ref[k:k + 1, :] * v
            acc = t if acc is None else acc + t
        o_ref[...] = acc

    return _call(
        body, name=name, grid=(nb, C // tc),
        in_specs=[pl.BlockSpec((tr, tc), lambda i, j: (i, j)),
                  pl.BlockSpec((8, tc), lambda i, j: (jnp.minimum((i + 1) * (tr // 8), LP // 8 - 1), j)),
                  pl.BlockSpec((D_CONV, tc), lambda i, j: (0, j))],
        out_specs=[pl.BlockSpec((tr, tc), lambda i, j: (i, j))],
        out_shape=[jax.ShapeDtypeStruct((LP, C), F32)], sem=("parallel", "parallel"),
        args=(g, g, w), carry=_pop_for_rows(xq, LP, C))[0]


def _ssd_prelude(dtr_ref, dtrt_ref, brow_ref, bcol_ref, alrow_ref, alcol_ref, Q):
    ii = lax.broadcasted_iota(jnp.int32, (Q, Q), 0)
    jj = lax.broadcasted_iota(jnp.int32, (Q, Q), 1)
    tril = ii >= jj
    dt_col = _softplus(dtr_ref[...] + brow_ref[...])
    a_row_p = -jnp.exp(alrow_ref[...])
    dt_row = _softplus(dtrt_ref[...] + bcol_ref[...])
    a_col_p = -jnp.exp(alcol_ref[...])
    cum_col = _dot3_left(tril.astype(BF16), dt_col * a_row_p)
    cum_row = _dot3_right(dt_row * a_col_p, (ii <= jj).astype(BF16))
    return ii, jj, tril, dt_col, dt_row, a_row_p, cum_col, cum_row


def _col_of(mat, lane_idx, h):
    return jnp.sum(jnp.where(lane_idx == h, mat, 0.0), axis=1, keepdims=True)


def _ssd_fwd(xbc, dtr, dtrt, brow, bcol, alrow, alcol, dvec, *, name, xq=None):
    LP = xbc.shape[0]
    Q = SSD_CHUNK
    nc = LP // Q
    G = SSM_GROUPS
    DI = dvec.shape[1]
    gw = DI // G
    hpg = gw // HEAD_DIM
    H = G * hpg
    boff, coff = DI, DI + G * D_STATE

    def body(xbc_ref, dtr_ref, dtrt_ref, brow_ref, bcol_ref, alrow_ref, alcol_ref, dvec_ref, y_ref, st_ref, state):
        c = pl.program_id(0)

        @pl.when(c == 0)
        def _():
            state[...] = jnp.zeros_like(state)

        st_ref[...] = state[...]
        ii, jj, tril, dt_col, dt_row, _, cum_col, cum_row = _ssd_prelude(
            dtr_ref, dtrt_ref, brow_ref, bcol_ref, alrow_ref, alcol_ref, Q)
        lane_h = lax.broadcasted_iota(jnp.int32, (Q, 128), 1)
        lane_g = lax.broadcasted_iota(jnp.int32, (Q, gw), 1) // HEAD_DIM
        for g in range(G):
            xg = xbc_ref[:, g * gw:(g + 1) * gw]
            bb = xbc_ref[:, boff + g * D_STATE: boff + (g + 1) * D_STATE].astype(BF16)
            cb = xbc_ref[:, coff + g * D_STATE: coff + (g + 1) * D_STATE].astype(BF16)
            gm = _dot_nt(cb, bb)
            sg = state[g]
            yoff = _dot(cb, sg.astype(BF16))
            ydiag = jnp.zeros((Q, gw), F32)
            esc = jnp.zeros((Q, gw), F32)
            wsc = jnp.zeros((Q, gw), F32)
            lam = jnp.zeros((1, gw), F32)
            for j in range(hpg):
                h = g * hpg + j
                ccol = _col_of(cum_col, lane_h, h)
                dcol = _col_of(dt_col, lane_h, h)
                seg = ccol - cum_row[h:h + 1, :]
                decay = jnp.exp(jnp.where(tril, seg, -jnp.inf))
                mh = gm * decay * dt_row[h:h + 1, :]
                hm = lane_g == j
                ydiag = ydiag + _dot(mh.astype(BF16), jnp.where(hm, xg, 0.0).astype(BF16))
                tot = ccol[Q - 1:Q, :]
                esc = jnp.where(hm, jnp.exp(ccol), esc)
                wsc = jnp.where(hm, jnp.exp(tot - ccol) * dcol, wsc)
                lam = jnp.where(hm[0:1], jnp.exp(tot), lam)
            y_ref[:, g * gw:(g + 1) * gw] = ydiag + yoff * esc + dvec_ref[:, g * gw:(g + 1) * gw] * xg
            state[g] = sg * lam + _dot_tn(bb, (xg * wsc).astype(BF16))

    W = xbc.shape[1]
    full = lambda shape: pl.BlockSpec(shape, lambda c: (0,) * len(shape))
    return _call(
        body, name=name, grid=(nc,),
        in_specs=[pl.BlockSpec((Q, W), lambda c: (c, 0)), pl.BlockSpec((Q, 128), lambda c: (c, 0)),
                  pl.BlockSpec((H, Q), lambda c: (0, c)), full((1, 128)), full((H, 1)), full((1, 128)), full((H, 1)),
                  full((1, DI))],
        out_specs=[pl.BlockSpec((Q, DI), lambda c: (c, 0)), pl.BlockSpec((None, G, D_STATE, gw), lambda c: (c, 0, 0, 0))],
        out_shape=[jax.ShapeDtypeStruct((LP, DI), F32), jax.ShapeDtypeStruct((nc, G, D_STATE, gw), F32)],
        scratch_shapes=[pltpu.VMEM((G, D_STATE, gw), F32)], sem=("arbitrary",),
        args=(xbc, dtr, dtrt, brow, bcol, alrow, alcol, dvec), carry=_pop_for_rows(xq, LP, W, 1.25))


def _ssd_bwd(xbc, dtr, dtrt, brow, bcol, alrow, alcol, dvec, dy, states, *, name, xq=None):
    LP = xbc.shape[0]
    Q = SSD_CHUNK
    nc = LP // Q
    G = SSM_GROUPS
    DI = dvec.shape[1]
    gw = DI // G
    hpg = gw // HEAD_DIM
    H = G * hpg
    boff, coff = DI, DI + G * D_STATE
    W = xbc.shape[1]

    def body(xbc_ref, dtr_ref, dtrt_ref, brow_ref, bcol_ref, alrow_ref, alcol_ref, dvec_ref, dy_ref, st_ref,
             dxbc_ref, ddtr_ref, dbias_ref, dalog_ref, ddvec_ref, dstate):
        c = pl.program_id(0)

        @pl.when(c == 0)
        def _():
            dstate[...] = jnp.zeros_like(dstate)
            dbias_ref[...] = jnp.zeros_like(dbias_ref)
            dalog_ref[...] = jnp.zeros_like(dalog_ref)
            ddvec_ref[...] = jnp.zeros_like(ddvec_ref)

        ii, jj, tril, dt_col, dt_row, a_row_p, cum_col, cum_row = _ssd_prelude(
            dtr_ref, dtrt_ref, brow_ref, bcol_ref, alrow_ref, alcol_ref, Q)
        eye = ii == jj
        lane_h = lax.broadcasted_iota(jnp.int32, (Q, 128), 1)
        row_h = lax.broadcasted_iota(jnp.int32, (Q, 128), 0)
        lane_g = lax.broadcasted_iota(jnp.int32, (Q, gw), 1) // HEAD_DIM
        lane_s = lax.broadcasted_iota(jnp.int32, (D_STATE, gw), 1) // HEAD_DIM
        dcum_mat = jnp.zeros((Q, 128), F32)
        ddt_mat = jnp.zeros((Q, 128), F32)
        dtot_row = jnp.zeros((1, 128), F32)
        for g in range(G):
            xg = xbc_ref[:, g * gw:(g + 1) * gw]
            dyg = dy_ref[:, g * gw:(g + 1) * gw]
            bb = xbc_ref[:, boff + g * D_STATE: boff + (g + 1) * D_STATE].astype(BF16)
            cb = xbc_ref[:, coff + g * D_STATE: coff + (g + 1) * D_STATE].astype(BF16)
            sg = st_ref[g]
            dsg = dstate[g]
            sb = sg.astype(BF16)
            dsb = dsg.astype(BF16)
            xb = xg.astype(BF16)
            gm = _dot_nt(cb, bb)
            cs = _dot(cb, sb)
            bds = _dot(bb, dsb)
            dxg = dvec_ref[:, g * gw:(g + 1) * gw] * dyg
            dgm = jnp.zeros((Q, Q), F32)
            esc = jnp.zeros((Q, gw), F32)
            wsc = jnp.zeros((Q, gw), F32)
            lam = jnp.zeros((1, gw), F32)
            dycs = dyg * cs
            xbds = xg * bds
            dss = dsg * sg
            for j in range(hpg):
                h = g * hpg + j
                ccol = _col_of(cum_col, lane_h, h)
                dcol = _col_of(dt_col, lane_h, h)
                drow = dt_row[h:h + 1, :]
                seg = ccol - cum_row[h:h + 1, :]
                decay = jnp.exp(jnp.where(tril, seg, -jnp.inf))
                hm = lane_g == j
                dyh = jnp.where(hm, dyg, 0.0).astype(BF16)
                gl = gm * decay
                mh = gl * drow
                dmf = _dot_nt(dyh, xb)
                dxg = dxg + _dot_tn(mh.astype(BF16), dyh)
                dgm = dgm + dmf * decay * drow
                n_p = dmf * gl
                n_m = n_p * drow
                rowsum_n = jnp.sum(n_m, axis=1, keepdims=True)
                colsum_n = jnp.sum(jnp.where(eye, jnp.sum(n_m, axis=0, keepdims=True), 0.0), axis=1, keepdims=True)
                colsum_np = jnp.sum(jnp.where(eye, jnp.sum(n_p, axis=0, keepdims=True), 0.0), axis=1, keepdims=True)
                tot = ccol[Q - 1:Q, :]
                e = jnp.exp(ccol)
                wexp = jnp.exp(tot - ccol)
                wcol = wexp * dcol
                lamh = jnp.exp(tot)
                yoff_t = jnp.sum(jnp.where(hm, dycs, 0.0), axis=1, keepdims=True) * e
                e_s = jnp.sum(jnp.where(hm, xbds, 0.0), axis=1, keepdims=True)
                ew = e_s * wcol
                dtot = jnp.sum(ew, axis=0, keepdims=True) + lamh * jnp.sum(
                    jnp.sum(jnp.where(lane_s == j, dss, 0.0), axis=1, keepdims=True), axis=0, keepdims=True)
                dcum_h = rowsum_n + yoff_t - colsum_n - ew
                ddt_h = colsum_np + e_s * wexp
                onehot = lane_h == h
                dcum_mat = jnp.where(onehot, dcum_h, dcum_mat)
                ddt_mat = jnp.where(onehot, ddt_h, ddt_mat)
                dtot_row = jnp.where(onehot[0:1], dtot, dtot_row)
                esc = jnp.where(hm, e, esc)
                wsc = jnp.where(hm, wcol, wsc)
                lam = jnp.where(hm[0:1], lamh, lam)
            dgb = dgm.astype(BF16)
            dye = (dyg * esc).astype(BF16)
            xw = (xg * wsc).astype(BF16)
            dxbc_ref[:, g * gw:(g + 1) * gw] = dxg + bds * wsc
            dxbc_ref[:, boff + g * D_STATE: boff + (g + 1) * D_STATE] = _dot_tn(dgb, cb) + _dot_nt(xw, dsb)
            dxbc_ref[:, coff + g * D_STATE: coff + (g + 1) * D_STATE] = _dot(dgb, bb) + _dot_nt(dye, sb)
            dstate[g] = dsg * lam + _dot_tn(cb, dye)
            ddvec_ref[:, g * gw:(g + 1) * gw] += jnp.sum(dyg * xg, axis=0, keepdims=True)
        dcum_mat = dcum_mat + jnp.where(row_h == Q - 1, dtot_row, 0.0)
        da = _dot3_left((ii <= jj).astype(BF16), dcum_mat)
        ddt = ddt_mat + da * a_row_p
        dalog_ref[...] += jnp.sum(da * dt_col, axis=0, keepdims=True) * a_row_p
        ddtr = ddt * _sigmoid(dtr_ref[...] + brow_ref[...])
        ddtr_ref[...] = ddtr
        dbias_ref[...] += jnp.sum(ddtr, axis=0, keepdims=True)

    full = lambda shape: pl.BlockSpec(shape, lambda c: (0,) * len(shape))
    rc = lambda c: nc - 1 - c
    return _call(
        body, name=name, grid=(nc,),
        in_specs=[pl.BlockSpec((Q, W), lambda c: (rc(c), 0)), pl.BlockSpec((Q, 128), lambda c: (rc(c), 0)),
                  pl.BlockSpec((H, Q), lambda c: (0, rc(c))), full((1, 128)), full((H, 1)), full((1, 128)), full((H, 1)),
                  full((1, DI)), pl.BlockSpec((Q, DI), lambda c: (rc(c), 0)),
                  pl.BlockSpec((None, G, D_STATE, gw), lambda c: (rc(c), 0, 0, 0))],
        out_specs=[pl.BlockSpec((Q, W), lambda c: (rc(c), 0)), pl.BlockSpec((Q, 128), lambda c: (rc(c), 0)),
                   full((1, 128)), full((1, 128)), full((1, DI))],
        out_shape=[jax.ShapeDtypeStruct((LP, W), F32), jax.ShapeDtypeStruct((LP, 128), F32),
                   jax.ShapeDtypeStruct((1, 128), F32), jax.ShapeDtypeStruct((1, 128), F32),
                   jax.ShapeDtypeStruct((1, DI), F32)],
        scratch_shapes=[pltpu.VMEM((G, D_STATE, gw), F32)], sem=("arbitrary",),
        args=(xbc, dtr, dtrt, brow, bcol, alrow, alcol, dvec, dy, states), carry=_pop_for_rows(xq, LP, W, 4.5))


def _gate_fwd(y, zx, g, *, name):
    LP, DI = y.shape
    gw = DI // SSM_GROUPS
    tr = _tile(LP, (256, 128))

    def body(y_ref, z_ref, g_ref, o_ref):
        for k in range(SSM_GROUPS):
            sl = slice(k * gw, (k + 1) * gw)
            z = z_ref[:, sl]
            t = y_ref[:, sl] * (z * _sigmoid(z))
            r = lax.rsqrt(jnp.mean(t * t, axis=1, keepdims=True) + RMS_EPS)
            o_ref[:, sl] = (t * r * g_ref[:, sl]).astype(BF16)

    row = pl.BlockSpec((tr, DI), lambda i: (i, 0))
    return pl.pallas_call(
        body, name=name, grid=(LP // tr,), in_specs=[row, row, pl.BlockSpec((1, DI), lambda i: (0, 0))],
        out_specs=row, out_shape=jax.ShapeDtypeStruct((LP, DI), BF16), compiler_params=_cp("parallel"),
    )(y, zx, g)


def _gate_bwd(y, zx, g, dyn, *, name):
    LP, DI = y.shape
    gw = DI // SSM_GROUPS
    tr = _tile(LP, (256, 128))

    def body(y_ref, z_ref, g_ref, d_ref, dy_ref, dz_ref, dg_ref):
        i = pl.program_id(0)

        @pl.when(i == 0)
        def _():
            dg_ref[...] = jnp.zeros_like(dg_ref)

        for k in range(SSM_GROUPS):
            sl = slice(k * gw, (k + 1) * gw)
            z = z_ref[:, sl]
            yv = y_ref[:, sl]
            sg = _sigmoid(z)
            sz = z * sg
            t = yv * sz
            r = lax.rsqrt(jnp.mean(t * t, axis=1, keepdims=True) + RMS_EPS)
            th = t * r
            d = d_ref[:, sl]
            dtn = d * g_ref[:, sl]
            dt_ = r * (dtn - th * jnp.mean(dtn * th, axis=1, keepdims=True))
            dg_ref[:, sl] += jnp.sum(d * th, axis=0, keepdims=True)
            dy_ref[:, sl] = dt_ * sz
            dz_ref[:, sl] = dt_ * yv * sg * (1.0 + z * (1.0 - sg))

    row = pl.BlockSpec((tr, DI), lambda i: (i, 0))
    vec = pl.BlockSpec((1, DI), lambda i: (0, 0))
    return pl.pallas_call(
        body, name=name, grid=(LP // tr,), in_specs=[row, row, vec, row], out_specs=[row, row, vec],
        out_shape=[jax.ShapeDtypeStruct((LP, DI), F32), jax.ShapeDtypeStruct((LP, DI), F32),
                   jax.ShapeDtypeStruct((1, DI), F32)],
        compiler_params=_cp("arbitrary"),
    )(y, zx, g, dyn)


EXP_ZERO = -104.0
LOG2E = 1.4426950408889634


def _dot2_right(x, t2_bf16):
    hi = x.astype(BF16)
    lo = (x - hi.astype(F32)).astype(BF16)
    return _dot(jnp.concatenate([hi, lo], axis=1), t2_bf16)


def _tri2(T, upper):
    r = lax.broadcasted_iota(jnp.int32, (2 * T, T), 0) % T
    c = lax.broadcasted_iota(jnp.int32, (2 * T, T), 1)
    return (r <= c if upper else r >= c).astype(BF16)


def _sb_tile(q, k_blk, lower2, valid=None):
    z = _dot_nt(q, k_blk)
    sp = jnp.maximum(z, 0.0) + jnp.log(1.0 + jnp.exp2(jnp.abs(z) * (-LOG2E)))
    if valid is not None:
        sp = jnp.where(valid, sp, 0.0)
    return z, sp, z - _dot2_right(sp, lower2)


def _sb_weights(zr, c, valid=None):
    w = jnp.exp(zr + c)
    return w if valid is None else jnp.where(valid, w, 0.0)


def _sb_fwd(q, k, v, zmax, *, name):
    H, LP, dh = q.shape
    T = ATT_BLOCK
    nq = LP // T
    assert nq < LANES

    def body(q_ref, k_ref, v_ref, zb_ref, o_ref, c_ref, kf_ref):
        i = pl.program_id(1)
        ii = lax.broadcasted_iota(jnp.int32, (T, T), 0)
        jj = lax.broadcasted_iota(jnp.int32, (T, T), 1)
        lane = lax.broadcasted_iota(jnp.int32, (T, LANES), 1)
        lower2 = _tri2(T, upper=False)
        qv = q_ref[...]
        zb = zb_ref[0:1, 0:1]

        def kv(kb):
            ks = pl.multiple_of(kb * T, T)
            return k_ref[pl.ds(ks, T), :], v_ref[pl.ds(ks, T), :]

        kd, vd = kv(i)
        k1, v1 = kv(jnp.maximum(i - 1, 0))
        diag = jj < ii
        prev = jnp.full((T, T), i > 0)
        _, sp0, zr0 = _sb_tile(qv, kd, lower2, diag)
        _, sp1, zr1 = _sb_tile(qv, k1, lower2, prev)
        c0 = -jnp.sum(sp0, axis=1, keepdims=True)
        acc = (_dot(_sb_weights(zr0, 0.0, diag).astype(BF16), vd)
               + _dot(_sb_weights(zr1, c0, prev).astype(BF16), v1))
        c = c0 - jnp.sum(sp1, axis=1, keepdims=True)
        c_ref[...] = jnp.where(lane == i - 1, c0, 0.0)

        def alive(c):
            return jnp.max(c + zb) > EXP_ZERO

        def cond(carry):
            kb, _, _, live = carry
            return (kb >= 0) & live

        def step(carry):
            kb, c, acc, _ = carry
            kt, vt = kv(kb)
            _, sp, zr = _sb_tile(qv, kt, lower2)
            acc = acc + _dot(_sb_weights(zr, c).astype(BF16), vt)
            c_ref[...] = jnp.where(lane == kb, c, c_ref[...])
            c = c - jnp.sum(sp, axis=1, keepdims=True)
            return kb - 1, c, acc, alive(c)

        kb, _, acc, _ = lax.while_loop(cond, step, (i - 2, c, acc, alive(c)))
        o_ref[...] = acc
        kf_ref[...] = jnp.zeros_like(kf_ref) + (kb + 1).astype(F32)

    blk = pl.BlockSpec((None, T, dh), lambda h, i: (h, i, 0))
    cblk = pl.BlockSpec((None, T, LANES), lambda h, i: (h, i, 0))
    whole = pl.BlockSpec((None, LP, dh), lambda h, i: (h, 0, 0))
    return pl.pallas_call(
        body, name=name, grid=(H, nq), in_specs=[blk, whole, whole, pl.BlockSpec((1, LANES), lambda h, i: (0, 0))],
        out_specs=[blk, cblk, pl.BlockSpec((None, None, 8, LANES), lambda h, i: (h, i, 0, 0))],
        out_shape=[jax.ShapeDtypeStruct((H, LP, dh), F32), jax.ShapeDtypeStruct((H, LP, LANES), F32),
                   jax.ShapeDtypeStruct((H, nq, 8, LANES), F32)],
        compiler_params=_cp("parallel", "parallel"),
    )(q, k, v, zmax)


def _sb_bwd(kstart, q, k, v, cmat, do, *, name):
    H, LP, dh = q.shape
    T = ATT_BLOCK
    nq = LP // T

    def body(ks_ref, q_ref, k_ref, v_ref, c_ref, do_ref, dq_ref, dk_ref, dv_ref):
        h = pl.program_id(0)
        i = pl.program_id(1)

        @pl.when(i == 0)
        def _():
            dk_ref[...] = jnp.zeros_like(dk_ref)
            dv_ref[...] = jnp.zeros_like(dv_ref)

        ii = lax.broadcasted_iota(jnp.int32, (T, T), 0)
        jj = lax.broadcasted_iota(jnp.int32, (T, T), 1)
        lane = lax.broadcasted_iota(jnp.int32, (T, LANES), 1)
        lower2 = _tri2(T, upper=False)
        upper2 = _tri2(T, upper=True)
        qv = q_ref[...]
        dob = do_ref[...].astype(BF16)
        cm = c_ref[...]

        def front(kb, valid=None):
            ks = pl.multiple_of(kb * T, T)
            k_blk = k_ref[pl.ds(ks, T), :]
            c = jnp.sum(jnp.where(lane == kb, cm, 0.0), axis=1, keepdims=True)
            z, sp, zr = _sb_tile(qv, k_blk, lower2, valid)
            w = _sb_weights(zr, c, valid)
            gw_ = w * _dot_nt(dob, v_ref[pl.ds(ks, T), :])
            gin = _dot2_right(gw_, upper2)
            return ks, k_blk, w, gw_, gin, jnp.exp(z - sp)

        def back(t, cg, dq, valid=None):
            ks, k_blk, w, gw_, gin, sig = t
            dz = gw_ - sig * (cg + gin)
            if valid is not None:
                dz = jnp.where(valid, dz, 0.0)
            dz = dz.astype(BF16)
            dk_ref[pl.ds(ks, T), :] += _dot_tn(dz, qv)
            dv_ref[pl.ds(ks, T), :] += _dot_tn(w.astype(BF16), dob)
            return cg + jnp.sum(gw_, axis=1, keepdims=True), dq + _dot(dz, k_blk)

        cg, dq = lax.fori_loop(ks_ref[h, i], i - 1, lambda kb, cr: back(front(kb), *cr),
                               (jnp.zeros((T, 1), F32), jnp.zeros((T, dh), F32)))
        diag = jj < ii
        prev = jnp.full((T, T), i > 0)
        t1 = front(jnp.maximum(i - 1, 0), prev)
        t0 = front(i, diag)
        cg, dq = back(t1, cg, dq, prev)
        _, dq = back(t0, cg, dq, diag)
        dq_ref[...] = dq

    blk = pl.BlockSpec((None, T, dh), lambda h, i, ks: (h, i, 0))
    cblk = pl.BlockSpec((None, T, LANES), lambda h, i, ks: (h, i, 0))
    whole = pl.BlockSpec((None, LP, dh), lambda h, i, ks: (h, 0, 0))
    sh = jax.ShapeDtypeStruct((H, LP, dh), F32)
    return pl.pallas_call(
        body, name=name,
        grid_spec=pltpu.PrefetchScalarGridSpec(
            num_scalar_prefetch=1, grid=(H, nq), in_specs=[blk, whole, whole, cblk, blk], out_specs=[blk, whole, whole]),
        out_shape=[sh, sh, sh], compiler_params=_cp("parallel", "arbitrary"),
    )(kstart, q, k, v, cmat, do)


def _loss_head(h, tgt, seq, *, name):
    LP, D = h.shape
    tr = _tile(LP, (384, 256, 128))

    def body(h_ref, t_ref, dh_ref, l_ref):
        i = pl.program_id(0)

        @pl.when(i == 0)
        def _():
            l_ref[...] = jnp.zeros_like(l_ref)

        row = lax.broadcasted_iota(jnp.int32, (tr, D), 0) + i * tr
        e = jnp.where((row >= N_META) & (row < N_META + seq), h_ref[...] - t_ref[...], 0.0)
        dh_ref[...] = e * (1.0 / D)
        l_ref[...] += jnp.sum(e * e, axis=0, keepdims=True) * (0.5 / D)

    row = pl.BlockSpec((tr, D), lambda i: (i, 0))
    return pl.pallas_call(
        body, name=name, grid=(LP // tr,), in_specs=[row, row], out_specs=[row, pl.BlockSpec((1, D), lambda i: (0, 0))],
        out_shape=[jax.ShapeDtypeStruct((LP, D), F32), jax.ShapeDtypeStruct((1, D), F32)],
        compiler_params=_cp("arbitrary"),
    )(h, tgt)


def _adamw(w, g, m, v, *, name):
    shape = w.shape
    C = shape[-1]
    R = math.prod(shape) // C
    tr = _tile(R, (512, 256, 128, 64, 32, 16, 8))
    c1 = 1.0 / (1.0 - ADAM_B1 ** ADAM_STEP)
    c2 = 1.0 / (1.0 - ADAM_B2 ** ADAM_STEP)

    def body(w_ref, g_ref, m_ref, v_ref, d_ref, nm_ref, nv_ref):
        gv = g_ref[...]
        nm = ADAM_B1 * m_ref[...] + (1.0 - ADAM_B1) * gv
        nv = ADAM_B2 * v_ref[...] + (1.0 - ADAM_B2) * (gv * gv)
        d_ref[...] = -ADAM_LR * ((nm * c1) / (jnp.sqrt(nv * c2) + ADAM_EPS) + ADAM_WD * w_ref[...])
        nm_ref[...] = nm
        nv_ref[...] = nv

    blk = pl.BlockSpec((tr, C), lambda i: (i, 0))
    sh = jax.ShapeDtypeStruct((R, C), F32)
    d, nm, nv = pl.pallas_call(
        body, name=name, grid=(R // tr,), in_specs=[blk] * 4, out_specs=[blk] * 3, out_shape=[sh] * 3,
        compiler_params=_cp("parallel"),
    )(w.reshape(R, C), g.reshape(R, C), m.reshape(R, C), v.reshape(R, C))
    return d.reshape(shape), nm.reshape(shape), nv.reshape(shape)


def _sum_rows(buf, *, name):
    n, R, C = buf.shape
    tr = _tile(R, (512, 256, 128, 64, 32, 16, 8))

    def body(b_ref, o_ref):
        acc = b_ref[0].astype(F32)
        for k in range(1, n):
            acc = acc + b_ref[k].astype(F32)
        o_ref[...] = acc

    return pl.pallas_call(
        body, name=name, grid=(R // tr,), in_specs=[pl.BlockSpec((n, tr, C), lambda i: (0, i, 0))],
        out_specs=pl.BlockSpec((tr, C), lambda i: (i, 0)), out_shape=jax.ShapeDtypeStruct((R, C), F32),
        compiler_params=_cp("parallel"),
    )(buf)


def _interleave(buf, *, name):
    n, R, C = buf.shape
    tr = _tile(R, (256, 128, 64, 32, 16))

    def body(b_ref, o_ref):
        for d in range(n):
            o_ref[:, d * C:(d + 1) * C] = b_ref[d]

    return pl.pallas_call(
        body, name=name, grid=(R // tr,), in_specs=[pl.BlockSpec((n, tr, C), lambda i: (0, i, 0))],
        out_specs=pl.BlockSpec((tr, n * C), lambda i: (i, 0)), out_shape=jax.ShapeDtypeStruct((R, n * C), buf.dtype),
        compiler_params=_cp("parallel"),
    )(buf)


def _deinterleave(x, *, out_dtype, name):
    R, NC = x.shape
    C = NC // N_DEV
    tr = _tile(R, (256, 128, 64, 32, 16))

    def body(x_ref, o_ref):
        for d in range(N_DEV):
            o_ref[d] = x_ref[:, d * C:(d + 1) * C].astype(out_dtype)

    return pl.pallas_call(
        body, name=name, grid=(R // tr,), in_specs=[pl.BlockSpec((tr, NC), lambda i: (i, 0))],
        out_specs=pl.BlockSpec((N_DEV, tr, C), lambda i: (0, i, 0)),
        out_shape=jax.ShapeDtypeStruct((N_DEV, R, C), out_dtype), compiler_params=_cp("parallel"),
    )(x)


def _mesh_pos():
    x, y, c = lax.axis_index("x"), lax.axis_index("y"), lax.axis_index("c")
    return x, y, c, 4 * x + 2 * y + c


def _peer(x, y, c, f):
    px, py, pc = (x + ((f >> 2) & 1)) % 2, (y + ((f >> 1) & 1)) % 2, (c + (f & 1)) % 2
    return (px, py, pc), 4 * px + 2 * py + pc


def _gather_two_level(arrs, *, name):
    n = len(arrs)
    hbm = pl.BlockSpec(memory_space=pl.ANY)

    def body(*refs):
        srcs, dsts, send_sems, recv_sems, local_sems = refs[:n], refs[n:2 * n], refs[2 * n], refs[2 * n + 1], refs[2 * n + 2]
        x, y, c = lax.axis_index("x"), lax.axis_index("y"), lax.axis_index("c")
        sibling = (x, y, 1 - c)
        chips = [(1 - x, y), (x, 1 - y), (1 - x, 1 - y)]
        row = lambda px, py, pc: 4 * px + 2 * py + pc
        me = row(x, y, c)

        def copy(a, k, block, to, src=None):
            return pltpu.make_async_remote_copy(
                src_ref=dsts[a].at[block] if src is None else src, dst_ref=dsts[a].at[block],
                send_sem=send_sems.at[a, k], recv_sem=recv_sems.at[a, k], device_id=to, device_id_type=MESH)

        own = [pltpu.make_async_copy(srcs[a], dsts[a].at[me], local_sems.at[a]) for a in range(n)]
        first = []
        for a in range(n):
            first.append(copy(a, 0, me, sibling, src=srcs[a]))
            first += [copy(a, 1 + j, me, (*chip, c), src=srcs[a]) for j, chip in enumerate(chips)]
        for cp in own + first:
            cp.start()
        passed = []
        for a in range(n):
            for j, chip in enumerate(chips):
                copy(a, 1 + j, row(*chip, c), (x, y, c)).wait_recv()
                passed.append(copy(a, 4 + j, row(*chip, c), sibling))
                passed[-1].start()
        for a in range(n):
            copy(a, 0, row(x, y, 1 - c), (x, y, c)).wait_recv()
            for j, chip in enumerate(chips):
                copy(a, 4 + j, row(*chip, 1 - c), (x, y, c)).wait_recv()
        for cp in first + passed:
            cp.wait_send()
        for cp in own:
            cp.wait()

    return list(pl.pallas_call(
        body, name=name, in_specs=[hbm] * n, out_specs=[hbm] * n, out_shape=_xchg_shapes(arrs, False),
        scratch_shapes=_xchg_scratch(n), compiler_params=pltpu.CompilerParams(has_side_effects=True),
    )(*arrs))


def _exchange(arrs, *, scatter, name):
    n = len(arrs)
    hbm = pl.BlockSpec(memory_space=pl.ANY)

    def body(*refs):
        xa = (refs[:n], refs[n:2 * n], *refs[2 * n:], scatter)
        _xchg_start(*xa)
        _xchg_wait(*xa)

    return list(pl.pallas_call(
        body, name=name, in_specs=[hbm] * n, out_specs=[hbm] * n, out_shape=_xchg_shapes(arrs, scatter),
        scratch_shapes=_xchg_scratch(n), compiler_params=pltpu.CompilerParams(has_side_effects=True),
    )(*arrs))


def _shard_rows(t):
    return t.reshape(N_DEV, t.shape[0] // N_DEV, t.shape[1])


def _ffn_fwd(h, g, W, l, s, xq):
    tag = f"{l}{s}"
    n = _rms_fwd(h, g, name=f"ffn_norm_{tag}")
    w1, w3 = W[("w13", l, s)]
    a, b, sw = _ffn_up(n, w1, w3, name=f"ffn_up_{tag}", xq=xq)
    h2 = _matmul([(sw, W[("w2", l, s)])], res=h, alpha=FFN_RES, name=f"ffn_down_{tag}", xq=xq)
    return h2, (h, n, a, b, sw)


def _ffn_bwd(dh, saved, g, W, l, s, xq, emit):
    tag = f"{l}{s}"
    h, n, a, b, sw = saved
    (w1, w3), w2 = W[("w13", l, s)], W[("w2", l, s)]
    da, db = _ffn_mid_bwd(dh, w2, a, b, name=f"ffn_mid_bwd_{tag}", xq=xq)
    dw2 = _matmul([(sw, dh)], trans_a=True, alpha=FFN_RES, out_dtype=BF16, name=f"ffn_dw2_{tag}", xq=xq)
    emit(("w2", l, s), [_shard_rows(dw2)])
    dn = _matmul([(da, w1), (db, w3)], trans_b=True, name=f"ffn_dn_{tag}", xq=xq)
    dw1 = _matmul([(n, da)], trans_a=True, shards=N_DEV, out_dtype=BF16, name=f"ffn_dw1_{tag}", xq=xq)
    dw3 = _matmul([(n, db)], trans_a=True, shards=N_DEV, out_dtype=BF16, name=f"ffn_dw3_{tag}", xq=xq)
    emit(("w13", l, s), [dw1, dw3])
    dh_in, dg = _rms_bwd(h, g, dn, res=dh, name=f"ffn_norm_bwd_{tag}")
    return dh_in, dg


def _local_step(x, tgt, W, xq=None, recv=None):
    G = {}

    def emit(key, arrs):
        G[key] = arrs
        if xq is not None:
            xq.push(arrs, True, lambda res, key=key: recv.__setitem__(key, res))

    seq, D = x.shape
    L = N_META + seq
    LP = -(-L // ROW_ALIGN) * ROW_ALIGN
    pad = LP - L
    H_sb = D // HEAD_DIM
    DI = W["ssm_norm_g"].shape[-1]
    H_ssm = DI // HEAD_DIM
    CONV = DI + 2 * SSM_GROUPS * D_STATE
    ZX = DI + CONV

    h0 = jnp.concatenate([W["meta_tokens"], x, jnp.zeros((pad, D), F32)], axis=0)
    tgt_p = jnp.pad(tgt, ((N_META, pad), (0, 0)))
    ng = W["norm_g"]

    h1, sv_f00 = _ffn_fwd(h0, ng[0, 0], W, 0, 0, xq)
    u0 = _rms_fwd(h1, ng[0, 1], name="ssm_norm")
    w_in = W["ssm_in_proj"][0]
    w_zx = w_in[:, :ZX]
    w_dt = jnp.pad(w_in[:, ZX:], ((0, 0), (0, 128 - H_ssm)))
    zx = _matmul([(u0, w_zx)], name="ssm_in_zx", xq=xq)
    dtr = _matmul([(u0, w_dt)], name="ssm_in_dt")
    conv_w, conv_b = W["ssm_conv_w"][0], W["ssm_conv_b"][0]
    xbc = _conv_fwd(zx, conv_w, conv_b, DI, name="ssm_conv", xq=xq)
    dtrt = dtr[:, :H_ssm].T
    padh = lambda t: jnp.pad(t.reshape(1, H_ssm), ((0, 0), (0, 128 - H_ssm)))
    brow, bcol = padh(W["ssm_dt_bias"][0]), W["ssm_dt_bias"][0].reshape(H_ssm, 1)
    alrow, alcol = padh(W["ssm_a_log"][0]), W["ssm_a_log"][0].reshape(H_ssm, 1)
    dvec = jnp.repeat(W["ssm_d"][0], HEAD_DIM).reshape(1, DI)
    ssm_args = (xbc, dtr, dtrt, brow, bcol, alrow, alcol, dvec)
    y, states = _ssd_fwd(*ssm_args, name="ssd_fwd", xq=xq)
    sng = W["ssm_norm_g"].reshape(1, DI)
    yn = _gate_fwd(y, zx, sng, name="ssm_gate")
    w_out = W["ssm_out_proj"][0]
    h2 = _matmul([(yn, w_out)], res=h1, name="ssm_out", xq=xq)
    h3, sv_f01 = _ffn_fwd(h2, ng[0, 2], W, 0, 1, xq)

    kv_in = _rms_fwd(h3, W["kv_norm_g"], name="kv_norm")
    kraw = _matmul([(kv_in, W["w_k"])], shards=H_sb, name="kv_k")
    vh = _matmul([(kv_in, W["w_v"])], shards=H_sb, out_dtype=BF16, name="kv_v")
    kh = _rms_fwd(kraw.reshape(H_sb * LP, HEAD_DIM), W["k_norm_g"], name="k_headnorm").reshape(H_sb, LP, HEAD_DIM)

    h4, sv_f10 = _ffn_fwd(h3, ng[1, 0], W, 1, 0, xq)
    u1 = _rms_fwd(h4, ng[1, 1], name="sb_norm")
    qraw = _matmul([(u1, W["sb_w_q"][0])], shards=H_sb, name="sb_q")
    scale = HEAD_DIM ** -0.5
    qh = _rms_fwd(qraw.reshape(H_sb * LP, HEAD_DIM), W["sb_q_norm_g"][0], scale=scale,
                  name="q_headnorm").reshape(H_sb, LP, HEAD_DIM)
    zmax = 1.02 * math.sqrt(HEAD_DIM) * jnp.max(jnp.abs(W["sb_q_norm_g"])) * jnp.max(jnp.abs(W["k_norm_g"]))
    o, cmat, kfirst = _sb_fwd(qh, kh, vh, jnp.full((1, LANES), zmax, F32), name="sb_fwd")
    kstart = kfirst[:, :, 0, 0].astype(jnp.int32)
    h5 = _matmul([(o, W["sb_w_o"][0])], res=h4, name="sb_out")
    h6, sv_f11 = _ffn_fwd(h5, ng[1, 2], W, 1, 1, xq)

    dh, lvec = _loss_head(h6, tgt_p, seq, name="loss_head")
    loss = jnp.sum(lvec)
    dng = [[None] * 3 for _ in range(2)]
    shard_rows = _shard_rows

    dh, dng[1][2] = _ffn_bwd(dh, sv_f11, ng[1, 2], W, 1, 1, xq, emit)
    g_wo = shard_rows(_matmul([(o, dh)], trans_a=True, out_dtype=BF16, name="sb_dwo"))
    do = _matmul([(dh, W["sb_w_o"][0])], trans_b=True, shards=H_sb, name="sb_do")
    dq, dk, dv = _sb_bwd(kstart, qh, kh, vh, cmat, do, name="sb_bwd")
    dqraw, dqg = _rms_bwd(qraw.reshape(H_sb * LP, HEAD_DIM), W["sb_q_norm_g"][0], dq.reshape(H_sb * LP, HEAD_DIM),
                          alpha=scale, name="q_headnorm_bwd")
    G["sb_q_norm_g"] = dqg
    dqraw = dqraw.reshape(H_sb, LP, HEAD_DIM)
    g_wq = shard_rows(_matmul([(u1, dqraw)], trans_a=True, out_dtype=BF16, name="sb_dwq"))
    emit("sb", [g_wq, g_wo])
    du1 = _matmul([(dqraw, W["sb_w_q"][0])], trans_b=True, name="sb_du")
    dh, dng[1][1] = _rms_bwd(h4, ng[1, 1], du1, res=dh, name="sb_norm_bwd")
    dh, dng[1][0] = _ffn_bwd(dh, sv_f10, ng[1, 0], W, 1, 0, xq, emit)

    dkraw, dkg = _rms_bwd(kraw.reshape(H_sb * LP, HEAD_DIM), W["k_norm_g"], dk.reshape(H_sb * LP, HEAD_DIM),
                          name="k_headnorm_bwd")
    G["k_norm_g"] = dkg.reshape(-1)
    dkraw = dkraw.reshape(H_sb, LP, HEAD_DIM)
    dvf = dv
    g_wk = shard_rows(_matmul([(kv_in, dkraw)], trans_a=True, out_dtype=BF16, name="kv_dwk"))
    g_wv = shard_rows(_matmul([(kv_in, dvf)], trans_a=True, out_dtype=BF16, name="kv_dwv"))
    emit("kv", [g_wk, g_wv])
    dkv = _matmul([(dkraw, W["w_k"]), (dvf, W["w_v"])], trans_b=True, name="kv_din", xq=xq)
    dh, dkvg = _rms_bwd(h3, W["kv_norm_g"], dkv, res=dh, name="kv_norm_bwd")
    G["kv_norm_g"] = dkvg.reshape(-1)

    dh, dng[0][2] = _ffn_bwd(dh, sv_f01, ng[0, 2], W, 0, 1, xq, emit)
    emit("wout", [shard_rows(_matmul([(yn, dh)], trans_a=True, out_dtype=BF16, name="ssm_dwout", xq=xq))])
    dyn = _matmul([(dh, w_out)], trans_b=True, name="ssm_dyn", xq=xq)
    dy, dz, dsng = _gate_bwd(y, zx, sng, dyn, name="ssm_gate_bwd")
    G["ssm_norm_g"] = dsng
    dxbc, ddtr, dbias, dalog, ddvec = _ssd_bwd(*ssm_args, dy, states, name="ssd_bwd", xq=xq)
    G["ssm_dt_bias"] = dbias[:, :H_ssm]
    G["ssm_a_log"] = dalog[:, :H_ssm]
    G["ssm_d"] = jnp.sum(ddvec.reshape(H_ssm, HEAD_DIM), axis=1).reshape(1, H_ssm)
    gpre, dcw, dcb = _conv_bwd_g(zx, conv_w, conv_b, dxbc, DI, name="ssm_conv_bwd_g", xq=xq)
    G["ssm_conv_w"] = dcw[:D_CONV][None]
    G["ssm_conv_b"] = dcb
    dxbc_pre = _conv_bwd_u(gpre, conv_w, name="ssm_conv_bwd_u", xq=xq)
    emit("win", [_deinterleave(jnp.concatenate([
        _matmul([(u0, dz)], trans_a=True, out_dtype=BF16, name="ssm_dwin_z", xq=xq),
        _matmul([(u0, dxbc_pre)], trans_a=True, out_dtype=BF16, name="ssm_dwin_x", xq=xq),
        _matmul([(u0, ddtr)], trans_a=True, out_dtype=BF16, name="ssm_dwin_dt")[:, :H_ssm]], axis=1),
        out_dtype=BF16, name="ssm_dwin_shards")])
    du0 = _matmul([(dz, w_zx)], trans_b=True, name="ssm_du_z", xq=xq)
    du0 = _matmul([(dxbc_pre, w_zx)], trans_b=True, b_off=DI, res=du0, name="ssm_du_x", xq=xq)
    du0 = _matmul([(ddtr, w_dt)], trans_b=True, res=du0, name="ssm_du_dt")
    dh, dng[0][1] = _rms_bwd(h1, ng[0, 1], du0, res=dh, name="ssm_norm_bwd")
    dh, dng[0][0] = _ffn_bwd(dh, sv_f00, ng[0, 0], W, 0, 0, xq, emit)

    G["norm_g"] = jnp.stack([jnp.concatenate(r, axis=0) for r in dng])
    G["meta_tokens"] = dh[:N_META]
    return loss, dh[N_META:L], G


WEIGHTS = ['meta_tokens', 'norm_g', 'ffn_w1', 'ffn_w3', 'ffn_w2', 'ssm_in_proj', 'ssm_conv_w', 'ssm_conv_b',
           'ssm_dt_bias', 'ssm_a_log', 'ssm_d', 'ssm_norm_g', 'ssm_out_proj', 'kv_norm_g', 'w_k', 'k_norm_g', 'w_v',
           'sb_w_q', 'sb_q_norm_g', 'sb_w_o']
SHARD_AXIS = {'meta_tokens': 1, 'norm_g': 2, 'ffn_w1': 3, 'ffn_w3': 3, 'ffn_w2': 2, 'ssm_in_proj': 2, 'ssm_conv_w': 2,
              'ssm_conv_b': 1, 'ssm_dt_bias': None, 'ssm_a_log': None, 'ssm_d': None, 'ssm_norm_g': 1,
              'ssm_out_proj': 1, 'kv_norm_g': None, 'w_k': 0, 'k_norm_g': None, 'w_v': 0, 'sb_w_q': 1,
              'sb_q_norm_g': None, 'sb_w_o': 1}
MATMUL_WEIGHTS = ('ffn_w1', 'ffn_w3', 'ffn_w2', 'ssm_in_proj', 'ssm_out_proj', 'w_k', 'w_v', 'sb_w_q', 'sb_w_o')
PACK_ROWS = 16


def _pack(arrs, dtype):
    flat = jnp.concatenate([a.reshape(-1).astype(dtype) for a in arrs])
    n = flat.shape[0]
    npad = -(-n // (LANES * PACK_ROWS)) * (LANES * PACK_ROWS)
    return jnp.pad(flat, (0, npad - n)).reshape(npad // LANES, LANES)


def _unpack_gathered(buf, names, shard_shapes, dtype):
    flat = buf.reshape(N_DEV, -1)
    out, off = {}, 0
    for n in names:
        shp = shard_shapes[n]
        size = math.prod(shp)
        t = flat[:, off:off + size].reshape((N_DEV,) + tuple(shp))
        off += size
        ax = SHARD_AXIS[n]
        t = jnp.moveaxis(t, 0, ax)
        full = shp[:ax] + (N_DEV * shp[ax],) + shp[ax + 1:]
        out[n] = t.reshape(full).astype(dtype)
    return out


def _to_shards(g, ax):
    shp = g.shape
    t = g.reshape(shp[:ax] + (N_DEV, shp[ax] // N_DEV) + shp[ax + 1:])
    return jnp.moveaxis(t, ax, 0).reshape(N_DEV, -1)


def kernel(x, meta_tokens, norm_g, ffn_w1, ffn_w3, ffn_w2, ssm_in_proj, ssm_conv_w, ssm_conv_b, ssm_dt_bias, ssm_a_log, ssm_d, ssm_norm_g, ssm_out_proj, kv_norm_g, w_k, k_norm_g, w_v, sb_w_q, sb_q_norm_g, sb_w_o, loss_target, m_meta_tokens, m_norm_g, m_ffn_w1, m_ffn_w3, m_ffn_w2, m_ssm_in_proj, m_ssm_conv_w, m_ssm_conv_b, m_ssm_dt_bias, m_ssm_a_log, m_ssm_d, m_ssm_norm_g, m_ssm_out_proj, m_kv_norm_g, m_w_k, m_k_norm_g, m_w_v, m_sb_w_q, m_sb_q_norm_g, m_sb_w_o, v_meta_tokens, v_norm_g, v_ffn_w1, v_ffn_w3, v_ffn_w2, v_ssm_in_proj, v_ssm_conv_w, v_ssm_conv_b, v_ssm_dt_bias, v_ssm_a_log, v_ssm_d, v_ssm_norm_g, v_ssm_out_proj, v_kv_norm_g, v_w_k, v_k_norm_g, v_w_v, v_sb_w_q, v_sb_q_norm_g, v_sb_w_o):
    shard = dict(meta_tokens=meta_tokens, norm_g=norm_g, ffn_w1=ffn_w1, ffn_w3=ffn_w3, ffn_w2=ffn_w2,
                 ssm_in_proj=ssm_in_proj, ssm_conv_w=ssm_conv_w, ssm_conv_b=ssm_conv_b, ssm_dt_bias=ssm_dt_bias,
                 ssm_a_log=ssm_a_log, ssm_d=ssm_d, ssm_norm_g=ssm_norm_g, ssm_out_proj=ssm_out_proj,
                 kv_norm_g=kv_norm_g, w_k=w_k, k_norm_g=k_norm_g, w_v=w_v, sb_w_q=sb_w_q, sb_q_norm_g=sb_q_norm_g,
                 sb_w_o=sb_w_o)
    mom_m = dict(zip(WEIGHTS, (m_meta_tokens, m_norm_g, m_ffn_w1, m_ffn_w3, m_ffn_w2, m_ssm_in_proj, m_ssm_conv_w,
                               m_ssm_conv_b, m_ssm_dt_bias, m_ssm_a_log, m_ssm_d, m_ssm_norm_g, m_ssm_out_proj,
                               m_kv_norm_g, m_w_k, m_k_norm_g, m_w_v, m_sb_w_q, m_sb_q_norm_g, m_sb_w_o)))
    mom_v = dict(zip(WEIGHTS, (v_meta_tokens, v_norm_g, v_ffn_w1, v_ffn_w3, v_ffn_w2, v_ssm_in_proj, v_ssm_conv_w,
                               v_ssm_conv_b, v_ssm_dt_bias, v_ssm_a_log, v_ssm_d, v_ssm_norm_g, v_ssm_out_proj,
                               v_kv_norm_g, v_w_k, v_k_norm_g, v_w_v, v_sb_w_q, v_sb_q_norm_g, v_sb_w_o)))
    sharded = [n for n in WEIGHTS if SHARD_AXIS[n] is not None]
    replicated = [n for n in WEIGHTS if SHARD_AXIS[n] is None]
    small = [n for n in sharded if n not in MATMUL_WEIGHTS]
    shapes = {n: tuple(shard[n].shape) for n in WEIGHTS}
    D = x.shape[-1]
    bf = lambda t: t.astype(BF16)
    full_rows = lambda t: t.reshape(N_DEV * t.shape[1], t.shape[2])
    cols = lambda l, s: jnp.concatenate([bf(ffn_w1[l, s]), bf(ffn_w3[l, s])], axis=0)
    xq, recv = _Queue(), {}
    W = _Weights(xq)
    W.update({n: shard[n] for n in replicated})

    def have_w13(l, s):
        def done(res):
            t = _interleave(res[0], name=f"weights_w13_{l}{s}")
            W[("w13", l, s)] = (t[:D], t[D:])
        return done

    def have_w2(l, s):
        return lambda res: W.__setitem__(("w2", l, s), full_rows(res[0]))

    def have_win(res):
        W["ssm_in_proj"] = _interleave(res[0], name="weights_win")[None]

    def have_rows(res):
        W["ssm_out_proj"] = full_rows(res[0])[None]
        W["w_k"], W["w_v"] = full_rows(res[1]), full_rows(res[2])
        W["sb_w_q"], W["sb_w_o"] = full_rows(res[3])[None], full_rows(res[4])[None]

    first = _gather_two_level([cols(0, 0), bf(ffn_w2[0, 0]), _pack([shard[n] for n in small], F32)], name="gather_first")
    have_w13(0, 0)(first[:1])
    have_w2(0, 0)(first[1:2])
    W.update(_unpack_gathered(first[2], small, shapes, F32))
    xq.push([bf(ssm_in_proj[0])], False, have_win)
    xq.push([bf(ffn_w2[0, 1])], False, have_w2(0, 1))
    xq.push([cols(0, 1)], False, have_w13(0, 1))
    xq.push([bf(ssm_out_proj[0]), bf(w_k), bf(w_v), bf(sb_w_q[0]), bf(sb_w_o[0])], False, have_rows)
    xq.push([cols(1, 0)], False, have_w13(1, 0))
    xq.push([bf(ffn_w2[1, 0])], False, have_w2(1, 0))
    xq.skip()
    xq.push([cols(1, 1)], False, have_w13(1, 1))
    xq.skip()
    xq.push([bf(ffn_w2[1, 1])], False, have_w2(1, 1))

    loss, dx, G = _local_step(x[0], loss_target[0], W, xq, recv)
    loss = lax.psum(loss, ("x", "y", "c"))

    send = jnp.concatenate([_to_shards(G[n], SHARD_AXIS[n]) for n in small], axis=1)
    n_el = send.shape[1]
    npad = -(-n_el // (LANES * PACK_ROWS)) * (LANES * PACK_ROWS)
    send = jnp.pad(send, ((0, 0), (0, npad - n_el))).reshape(N_DEV, npad // LANES, LANES)
    xq.push([send], True, lambda res: recv.__setitem__("small", res))
    xq.flush("scatter_last")
    rep = _pack([G[n] for n in replicated], F32)
    rep_sum = _sum_rows(_exchange([rep], scatter=False, name="gather_small_grads")[0], name="sum_small_grads").reshape(-1)
    summed = {k: [_sum_rows(t, name=f"sum_{'_'.join(map(str, k)) if isinstance(k, tuple) else k}_{i}")
                  for i, t in enumerate(v)] for k, v in recv.items()}

    grads, off = {}, 0
    small_sum = summed["small"][0].reshape(-1)
    for n in small:
        size = math.prod(shapes[n])
        grads[n] = small_sum[off:off + size].reshape(shapes[n])
        off += size
    off = 0
    for n in replicated:
        size = math.prod(shapes[n])
        grads[n] = rep_sum[off:off + size].reshape(shapes[n])
        off += size
    ls = [(l, s) for l in range(2) for s in range(2)]
    grads["ffn_w1"] = jnp.stack([summed[("w13", l, s)][0] for l, s in ls]).reshape(shapes["ffn_w1"])
    grads["ffn_w3"] = jnp.stack([summed[("w13", l, s)][1] for l, s in ls]).reshape(shapes["ffn_w3"])
    grads["ffn_w2"] = jnp.stack([summed[("w2", l, s)][0] for l, s in ls]).reshape(shapes["ffn_w2"])
    grads["ssm_in_proj"] = summed["win"][0][None]
    grads["ssm_out_proj"] = summed["wout"][0][None]
    grads["w_k"], grads["w_v"] = summed["kv"]
    grads["sb_w_q"], grads["sb_w_o"] = summed["sb"][0][None], summed["sb"][1][None]

    delta, new_m, new_v = {}, {}, {}
    for n in WEIGHTS:
        w2 = shard[n].reshape(1, -1) if shard[n].ndim == 1 else shard[n]
        r2 = lambda t: t.reshape(w2.shape)
        d, nm, nv = _adamw(w2, r2(grads[n]), r2(mom_m[n]), r2(mom_v[n]), name=f"adamw_{n}")
        delta[n], new_m[n], new_v[n] = (t.reshape(shapes[n]) for t in (d, nm, nv))

    return (loss, dx[None], *[grads[n] for n in WEIGHTS], *[delta[n] for n in WEIGHTS],
            *[new_m[n] for n in WEIGHTS], *[new_v[n] for n in WEIGHTS])
```
